```python
import math
import jax, jax.numpy as jnp
from jax import lax
import numpy as np

D_MODEL = 1024
BATCH = 2
SEQ = 8192
DEPTH = 1

CHUNK = 64
A_HEADS = 4
A_HEAD_DIM = 64
A_VDIM = 2 * A_HEAD_DIM
B_HEADS = 8
B_HEAD_DIM = 64
B_LEFT_CHUNKS = 8
B_MAX_REL = 128
T5_BUCKETS = 32
T5_MAX_DIST = 128
N_GROUPS = 4
EXPERTS_PER_GROUP = 8
N_EXPERTS = N_GROUPS * EXPERTS_PER_GROUP
TOP_K = 2
D_EXPERT = 512
MOE_BLOCK = 128
Q_BLOCK = 128
EPS = 1e-6

A_QK_W = A_HEADS * A_HEAD_DIM
A_V_W = A_HEADS * A_VDIM
B_W = B_HEADS * B_HEAD_DIM
PROJ_SIZES = [A_QK_W] * 4 + [A_V_W] + [B_W] * 3 + [D_MODEL] * 2
PROJ_W = sum(PROJ_SIZES)

kernel_name = "hybrid_diffattn_bandattn_hiermoe"


def rmsnorm(x, g):
    xf = x.astype(jnp.float32)
    y = xf * lax.rsqrt(jnp.mean(xf * xf, axis=-1, keepdims=True) + EPS)
    return (y * g.astype(jnp.float32)).astype(x.dtype)


def t5_bucket(rel):
    nb = T5_BUCKETS // 2
    max_exact = nb // 2
    side = jnp.where(rel > 0, nb, 0)
    n = jnp.abs(rel)
    nf = jnp.maximum(n, 1).astype(jnp.float32)
    large = max_exact + (jnp.log(nf / max_exact) / math.log(T5_MAX_DIST / max_exact)
                         * (nb - max_exact)).astype(jnp.int32)
    large = jnp.minimum(large, nb - 1)
    return side + jnp.where(n < max_exact, n, large)


def diff_attention(q1, q2, k1, k2, v, t5_table, lam):
    bsz, s_len, h, d = q1.shape
    nqb = s_len // Q_BLOCK
    scale = d ** -0.5
    k_pos = jnp.arange(s_len)

    def to_blocks(t):
        return t.reshape(bsz, nqb, Q_BLOCK, h, d).transpose(1, 0, 2, 3, 4)

    def block(args):
        qb1, qb2, bi = args
        q_pos = bi * Q_BLOCK + jnp.arange(Q_BLOCK)
        bias = t5_table[t5_bucket(k_pos[None, :] - q_pos[:, None])]
        bias = jnp.transpose(bias, (2, 0, 1)).astype(jnp.float32)
        allowed = (k_pos[None, :] // CHUNK) <= (q_pos[:, None] // CHUNK)

        def probs(qb, kk):
            s = jnp.einsum('bqhd,bkhd->bhqk', qb, kk).astype(jnp.float32) * scale + bias
            s = jnp.where(allowed, s, -1e30)
            return jax.nn.softmax(s, axis=-1)

        attn = probs(qb1, k1) - lam * probs(qb2, k2)
        return jnp.einsum('bhqk,bkhe->bqhe', attn.astype(v.dtype), v)

    out = lax.map(block, (to_blocks(q1), to_blocks(q2), jnp.arange(nqb)))
    return out.transpose(1, 0, 2, 3, 4).reshape(bsz, s_len, h, v.shape[-1])


def chunk_band_attention(q, k, v, rel_table):
    bsz, s_len, h, d = q.shape
    nc = s_len // CHUNK
    w = B_LEFT_CHUNKS
    band_len = (w + 1) * CHUNK
    qc = q.reshape(bsz, nc, CHUNK, h, d)

    def band(t):
        tp = jnp.pad(t, ((0, 0), (w * CHUNK, 0), (0, 0), (0, 0)))
        tp = tp.reshape(bsz, nc + w, CHUNK, h, t.shape[-1])
        return jnp.concatenate([tp[:, j:j + nc] for j in range(w + 1)], axis=2)

    kb, vb = band(k), band(v)
    qi = jnp.arange(CHUNK)
    kj = jnp.arange(band_len)
    rel = kj[None, :] - w * CHUNK - qi[:, None]
    rel_idx = jnp.clip(rel, -B_MAX_REL, B_MAX_REL) + B_MAX_REL
    bias = jnp.transpose(rel_table[rel_idx], (2, 0, 1)).astype(jnp.float32)
    valid = (jnp.arange(nc)[:, None] - w + kj[None, :] // CHUNK) >= 0
    s = jnp.einsum('bcqhd,bckhd->bchqk', qc, kb).astype(jnp.float32) * (d ** -0.5) + bias
    s = jnp.where(valid[None, :, None, None, :], s, -1e30)
    p = jax.nn.softmax(s, axis=-1)
    o = jnp.einsum('bchqk,bckhe->bcqhe', p.astype(v.dtype), vb)
    return o.reshape(bsz, s_len, h, d)


def hier_moe(x, w_rg, b_rg, w_re, b_re, w_gate, w_up, w_down):
    bsz, s_len, dm = x.shape
    n_tok = bsz * s_len
    xf = x.reshape(n_tok, dm)
    g_prob = jax.nn.softmax((xf @ w_rg).astype(jnp.float32) + b_rg.astype(jnp.float32), axis=-1)
    p_g, g_idx = lax.top_k(g_prob, 1)
    e_logits = ((xf @ w_re).astype(jnp.float32) + b_re.astype(jnp.float32)).reshape(n_tok, N_GROUPS, EXPERTS_PER_GROUP)
    e_in_group = jnp.take_along_axis(e_logits, g_idx[:, :, None], axis=1)[:, 0]
    top_l, top_j = lax.top_k(e_in_group, TOP_K)
    comb = p_g * jax.nn.softmax(top_l, axis=-1)
    expert_id = g_idx * EXPERTS_PER_GROUP + top_j

    n_assign = n_tok * TOP_K
    e_flat = expert_id.reshape(n_assign)
    tok_flat = jnp.repeat(jnp.arange(n_tok, dtype=jnp.int32), TOP_K)
    w_flat = comb.reshape(n_assign)
    order = jnp.argsort(e_flat)
    se, stok, sw = e_flat[order], tok_flat[order], w_flat[order]
    counts = jax.ops.segment_sum(jnp.ones((n_assign,), jnp.int32), e_flat, num_segments=N_EXPERTS)
    padded = ((counts + MOE_BLOCK - 1) // MOE_BLOCK) * MOE_BLOCK
    start = jnp.cumsum(counts) - counts
    pend = jnp.cumsum(padded)
    pstart = pend - padded
    dest = pstart[se] + (jnp.arange(n_assign) - start[se])
    n_blocks = (n_assign + MOE_BLOCK - 1) // MOE_BLOCK + N_EXPERTS
    n_rows = n_blocks * MOE_BLOCK
    buf_tok = jnp.zeros((n_rows,), jnp.int32).at[dest].set(stok)
    buf_w = jnp.zeros((n_rows,), jnp.float32).at[dest].set(sw)
    block_expert = jnp.minimum(jnp.searchsorted(pend, jnp.arange(n_blocks) * MOE_BLOCK, side='right'),
                               N_EXPERTS - 1)

    def run_block(args):
        tok, e = args
        xb = xf[tok]
        hb = jax.nn.silu(xb @ w_gate[e]) * (xb @ w_up[e])
        return hb @ w_down[e]

    yb = lax.map(run_block, (buf_tok.reshape(n_blocks, MOE_BLOCK), block_expert))
    yb = yb.reshape(n_rows, dm) * buf_w[:, None].astype(yb.dtype)
    y = jnp.zeros((n_tok, dm), x.dtype).at[buf_tok].add(yb.astype(x.dtype))
    return y.reshape(bsz, s_len, dm)


def setup_inputs(seed: int = 0) -> dict:
    key = jax.random.key(seed)
    ks = jax.random.split(key, 24)
    f = jnp.float32

    def nrm(k, shape, scale):
        return jax.random.normal(k, shape, f) * scale

    def gain(k, shape):
        return 1.0 + 0.05 * jax.random.normal(k, shape, f)

    L = DEPTH
    return {
        "x": nrm(ks[0], (BATCH, SEQ, D_MODEL), 1.0),
        "norm1_g": gain(ks[1], (L, D_MODEL)),
        "w_in": nrm(ks[2], (L, D_MODEL, PROJ_W), D_MODEL ** -0.5),
        "a_qnorm_g": gain(ks[3], (L, A_HEAD_DIM)),
        "a_knorm_g": gain(ks[4], (L, A_HEAD_DIM)),
        "a_lambda": nrm(ks[5], (L, 4, A_HEAD_DIM), 0.1),
        "a_subln_g": gain(ks[6], (L, A_VDIM)),
        "t5_table": nrm(ks[7], (T5_BUCKETS, A_HEADS), 0.5),
        "b_qnorm_g": gain(ks[8], (L, B_HEAD_DIM)),
        "b_knorm_g": gain(ks[9], (L, B_HEAD_DIM)),
        "b_rel_table": nrm(ks[10], (L, 2 * B_MAX_REL + 1, B_HEADS), 0.5),
        "w_branch_a": nrm(ks[11], (L, A_V_W, D_MODEL), A_V_W ** -0.5),
        "w_branch_b": nrm(ks[12], (L, B_W, D_MODEL), B_W ** -0.5),
        "w_out": nrm(ks[13], (L, D_MODEL, D_MODEL), D_MODEL ** -0.5),
        "norm2_g": gain(ks[14], (L, D_MODEL)),
        "w_router_group": nrm(ks[15], (L, D_MODEL, N_GROUPS), D_MODEL ** -0.5),
        "b_router_group": nrm(ks[16], (L, N_GROUPS), 0.01),
        "w_router_expert": nrm(ks[17], (L, D_MODEL, N_EXPERTS), D_MODEL ** -0.5),
        "b_router_expert": nrm(ks[18], (L, N_EXPERTS), 0.01),
        "w_gate": nrm(ks[19], (L, N_EXPERTS, D_MODEL, D_EXPERT), D_MODEL ** -0.5),
        "w_up": nrm(ks[20], (L, N_EXPERTS, D_MODEL, D_EXPERT), D_MODEL ** -0.5),
        "w_down": nrm(ks[21], (L, N_EXPERTS, D_EXPERT, D_MODEL), D_EXPERT ** -0.5),
    }


def reference(x, norm1_g, w_in, a_qnorm_g, a_knorm_g, a_lambda, a_subln_g, t5_table,
              b_qnorm_g, b_knorm_g, b_rel_table, w_branch_a, w_branch_b, w_out, norm2_g,
              w_router_group, b_router_group, w_router_expert, b_router_expert,
              w_gate, w_up, w_down):
    bsz, s_len, _ = x.shape
    split_idx = np.cumsum(PROJ_SIZES)[:-1].tolist()

    def heads(t, h):
        return t.reshape(bsz, s_len, h, -1)

    for l in range(DEPTH):
        xn = rmsnorm(x, norm1_g[l])
        proj = xn @ w_in[l]
        q1, q2, k1, k2, va, qb, kb, vb, gate_a, gate_b = jnp.split(proj, split_idx, axis=-1)

        q1 = rmsnorm(heads(q1, A_HEADS), a_qnorm_g[l])
        q2 = rmsnorm(heads(q2, A_HEADS), a_qnorm_g[l])
        k1 = rmsnorm(heads(k1, A_HEADS), a_knorm_g[l])
        k2 = rmsnorm(heads(k2, A_HEADS), a_knorm_g[l])
        va = heads(va, A_HEADS)
        lam_init = 0.8 - 0.6 * math.exp(-0.3 * l)
        lp = a_lambda[l].astype(jnp.float32)
        lam = jnp.exp(jnp.sum(lp[0] * lp[1])) - jnp.exp(jnp.sum(lp[2] * lp[3])) + lam_init
        oa = diff_attention(q1, q2, k1, k2, va, t5_table, lam)
        oa = rmsnorm(oa, a_subln_g[l]) * (1.0 - lam_init)
        ya = oa.reshape(bsz, s_len, A_V_W) @ w_branch_a[l]

        qb = rmsnorm(heads(qb, B_HEADS), b_qnorm_g[l])
        kb = rmsnorm(heads(kb, B_HEADS), b_knorm_g[l])
        vb = heads(vb, B_HEADS)
        ob = chunk_band_attention(qb, kb, vb, b_rel_table[l])
        yb = ob.reshape(bsz, s_len, B_W) @ w_branch_b[l]

        mixed = jax.nn.sigmoid(gate_a) * ya + jax.nn.sigmoid(gate_b) * yb
        x = x + mixed @ w_out[l]

        hn = rmsnorm(x, norm2_g[l])
        x = x + hier_moe(hn, w_router_group[l], b_router_group[l], w_router_expert[l],
                         b_router_expert[l], w_gate[l], w_up[l], w_down[l])
    return x
```

```python
import functools
import math

import jax
import jax.numpy as jnp
import numpy as np
from jax import lax
from jax.experimental import pallas as pl
from jax.experimental.pallas import tpu as pltpu

D_MODEL = 1024
CHUNK = 64
A_HEADS = 4
A_HEAD_DIM = 64
A_VDIM = 2 * A_HEAD_DIM
B_HEADS = 8
B_HEAD_DIM = 64
B_LEFT_CHUNKS = 8
B_MAX_REL = 128
T5_BUCKETS = 32
T5_MAX_DIST = 128
N_GROUPS = 4
EXPERTS_PER_GROUP = 8
N_EXPERTS = N_GROUPS * EXPERTS_PER_GROUP
TOP_K = 2
D_EXPERT = 512
EPS = 1e-6
NEG = -1e30

LANES = 128
A_W = A_HEADS * 2 * A_HEAD_DIM
B_W = B_HEADS * B_HEAD_DIM
PROJ_W = 4 * 256 + 4 * 512 + 2 * D_MODEL

TM_PROJ = 512
TQ = 256
TK = 256
TM_POST = 512
TM_RANK = 512
TM_ROWS = 256
FFN_BLK = 256
VMEM_LIMIT = 56 * 1024 * 1024


def _cparams(sem):
    return pltpu.CompilerParams(dimension_semantics=sem, vmem_limit_bytes=VMEM_LIMIT)


def _proj_kernel(x_ref, g1_ref, w_ref, gn_ref, gmat_ref,
                 qa_ref, ka_ref, va_ref, qb_ref, kb_ref, vb_ref, ga_ref, gb_ref):
    x = x_ref[...]
    xn = x * lax.rsqrt(jnp.mean(x * x, axis=-1, keepdims=True) + EPS) * g1_ref[...]
    xn = xn.astype(jnp.bfloat16)

    def slab(c0, width):
        return jnp.dot(xn, w_ref[:, c0:c0 + width], preferred_element_type=jnp.float32)

    def headnorm(y, gi):
        ss = jnp.dot((y * y).astype(jnp.bfloat16), gmat_ref[...],
                     preferred_element_type=jnp.float32)
        return y * lax.rsqrt(ss * (1.0 / A_HEAD_DIM) + EPS) * gn_ref[gi:gi + 1, :]

    qa_ref[...] = headnorm(slab(0, 512), 0).astype(jnp.bfloat16)
    ka_ref[...] = headnorm(slab(512, 512), 1).astype(jnp.bfloat16)
    va_ref[...] = slab(1024, 512).astype(jnp.bfloat16)
    qb_ref[...] = headnorm(slab(1536, 512), 2).astype(jnp.bfloat16)
    kb_ref[...] = headnorm(slab(2048, 512), 3).astype(jnp.bfloat16)
    vb_ref[...] = slab(2560, 512).astype(jnp.bfloat16)
    for j in range(2):
        ga_ref[:, j * 512:(j + 1) * 512] = jax.nn.sigmoid(slab(3072 + j * 512, 512)).astype(jnp.bfloat16)
        gb_ref[:, j * 512:(j + 1) * 512] = jax.nn.sigmoid(slab(4096 + j * 512, 512)).astype(jnp.bfloat16)


def _proj(x2, g1, w_perm, gn, gmat):
    t = x2.shape[0]
    n = t // TM_PROJ
    row = lambda w: pl.BlockSpec((TM_PROJ, w), lambda i: (i, 0))
    full = lambda a: pl.BlockSpec(a.shape, lambda i: (0,) * a.ndim)
    outs = [jax.ShapeDtypeStruct((t, 512), jnp.bfloat16)] * 6 + [jax.ShapeDtypeStruct((t, D_MODEL), jnp.bfloat16)] * 2
    return pl.pallas_call(
        _proj_kernel,
        grid=(n,),
        in_specs=[row(D_MODEL), full(g1), full(w_perm), full(gn), full(gmat)],
        out_specs=[row(512)] * 6 + [row(D_MODEL)] * 2,
        out_shape=outs,
        compiler_params=_cparams(("arbitrary",)),
        name="proj",
    )(x2, g1, w_perm, gn, gmat)


def _attn_kernel(scal_ref, q_ref, k_ref, v_ref, bias_ref, gsub_ref, o_ref,
                 m_sc, l_sc, acc_sc, mf_sc, lf_sc, accf_sc, *, mode, n_near):
    h = pl.program_id(1)
    qi = pl.program_id(2)
    q = q_ref[...]
    lane = lax.broadcasted_iota(jnp.int32, q.shape, 1)
    zero = jnp.zeros_like(q)
    qz = jnp.concatenate([jnp.where(lane < 64, q, zero), jnp.where(lane >= 64, q, zero)], axis=0)

    def scores(kblk):
        k = k_ref[pl.ds(pl.multiple_of(kblk * TK, TK), TK), :]
        return lax.dot_general(qz, k, (((1,), (1,)), ((), ())), preferred_element_type=jnp.float32)

    def update(s, kblk, m_ref, l_ref, a_ref):
        m_old = m_ref[...]
        m_new = jnp.maximum(m_old, jnp.max(s, axis=-1, keepdims=True))
        alpha = jnp.exp(m_old - m_new)
        p = jnp.exp(s - m_new)
        l_ref[...] = alpha * l_ref[...] + jnp.sum(p, axis=-1, keepdims=True)
        v = v_ref[pl.ds(pl.multiple_of(kblk * TK, TK), TK), :]
        a_ref[...] = alpha * a_ref[...] + jnp.dot(p.astype(jnp.bfloat16), v, preferred_element_type=jnp.float32)
        m_ref[...] = m_new

    m_sc[...] = jnp.full(m_sc.shape, NEG, jnp.float32)
    l_sc[...] = jnp.zeros(l_sc.shape, jnp.float32)
    acc_sc[...] = jnp.zeros(acc_sc.shape, jnp.float32)

    for j in range(n_near):
        def near(j=j):
            update(scores(qi - j) + bias_ref[0, j], qi - j, m_sc, l_sc, acc_sc)
        if j == 0:
            near()
        else:
            pl.when(qi >= j)(near)

    if mode == "diff":
        mf_sc[...] = jnp.full(mf_sc.shape, NEG, jnp.float32)
        lf_sc[...] = jnp.zeros(lf_sc.shape, jnp.float32)
        accf_sc[...] = jnp.zeros(accf_sc.shape, jnp.float32)

        def far(kblk, carry):
            update(scores(kblk), kblk, mf_sc, lf_sc, accf_sc)
            return carry

        lax.fori_loop(0, jnp.maximum(qi - (n_near - 1), 0), far, 0)

        cfar = scal_ref[1 + h]
        mf = mf_sc[...] + cfar
        mn = m_sc[...]
        m = jnp.maximum(mf, mn)
        wf = jnp.exp(mf - m)
        wn = jnp.exp(mn - m)
        l = wf * lf_sc[...] + wn * l_sc[...]
        o = (wf * accf_sc[...] + wn * acc_sc[...]) / l
        lam = scal_ref[0]
        od = o[:TQ] - lam * o[TQ:]
        od = od * lax.rsqrt(jnp.mean(od * od, axis=-1, keepdims=True) + EPS) * gsub_ref[...]
        o_ref[...] = od.astype(o_ref.dtype)
    else:
        o = acc_sc[...] / l_sc[...]
        o_ref[...] = jnp.where(lane < 64, o[:TQ], o[TQ:]).astype(o_ref.dtype)


def _attention(scal, q, k, v, bias, gsub, *, bsz, s_len, mode):
    n_blk = q.shape[1] // LANES
    nq = s_len // TQ
    n_near = bias.shape[1]
    kern = functools.partial(_attn_kernel, mode=mode, n_near=n_near)
    acc = lambda w: pltpu.VMEM((2 * TQ, w), jnp.float32)
    return pl.pallas_call(
        kern,
        grid=(bsz, n_blk, nq),
        in_specs=[
            pl.BlockSpec(memory_space=pltpu.SMEM),
            pl.BlockSpec((TQ, LANES), lambda b, h, i: (b * nq + i, h)),
            pl.BlockSpec((s_len, LANES), lambda b, h, i: (b, h)),
            pl.BlockSpec((s_len, LANES), lambda b, h, i: (b, h)),
            pl.BlockSpec((1, n_near, 2 * TQ, TK), lambda b, h, i: (h, 0, 0, 0)),
            pl.BlockSpec((1, LANES), lambda b, h, i: (0, 0)),
        ],
        out_specs=pl.BlockSpec((TQ, LANES), lambda b, h, i: (b * nq + i, h)),
        out_shape=jax.ShapeDtypeStruct(q.shape, jnp.bfloat16),
        scratch_shapes=[acc(1), acc(1), acc(LANES), acc(1), acc(1), acc(LANES)],
        compiler_params=_cparams(("arbitrary", "arbitrary", "arbitrary")),
        name="attn_" + mode,
    )(scal, q, k, v, bias, gsub)


def _post_kernel(oa_ref, ob_ref, ga_ref, gb_ref, x_ref, wa_ref, wb_ref, wo_ref, g2_ref,
                 wrh_ref, wrl_ref, br_ref, x1_ref, hn_ref, route_ref):
    f32 = jnp.float32
    ya = jnp.dot(oa_ref[...], wa_ref[...], preferred_element_type=f32)
    yb = jnp.dot(ob_ref[...], wb_ref[...], preferred_element_type=f32)
    mixed = ga_ref[...].astype(f32) * ya + gb_ref[...].astype(f32) * yb
    x1 = x_ref[...] + jnp.dot(mixed.astype(jnp.bfloat16), wo_ref[...], preferred_element_type=f32)
    x1_ref[...] = x1
    hn = x1 * lax.rsqrt(jnp.mean(x1 * x1, axis=-1, keepdims=True) + EPS) * g2_ref[...]
    hn_ref[...] = hn

    hh = hn.astype(jnp.bfloat16)
    hl = (hn - hh.astype(f32)).astype(jnp.bfloat16)
    lg = (jnp.dot(hh, wrh_ref[...], preferred_element_type=f32)
          + jnp.dot(hh, wrl_ref[...], preferred_element_type=f32)
          + jnp.dot(hl, wrh_ref[...], preferred_element_type=f32)) + br_ref[...]

    lanei = lax.broadcasted_iota(jnp.int32, lg.shape, 1)
    lanef = lanei.astype(f32)
    big = 999.0
    gmask = lanei < N_GROUPS
    gl = jnp.where(gmask, lg, NEG)
    gm = jnp.max(gl, axis=-1, keepdims=True)
    ge = jnp.where(gmask, jnp.exp(gl - gm), 0.0)
    gp = ge / jnp.sum(ge, axis=-1, keepdims=True)
    p_g = jnp.max(gp, axis=-1, keepdims=True)
    gidx = jnp.min(jnp.where(gmask & (gp == p_g), lanef, big), axis=-1, keepdims=True)
    egrp = lax.shift_right_arithmetic(lanei - N_GROUPS, 3).astype(f32)
    emask = (lanei >= N_GROUPS) & (lanei < N_GROUPS + N_EXPERTS) & (egrp == gidx)
    el = jnp.where(emask, lg, NEG)
    v1 = jnp.max(el, axis=-1, keepdims=True)
    i1 = jnp.min(jnp.where(emask & (el == v1), lanef, big), axis=-1, keepdims=True)
    emask2 = emask & (lanef != i1)
    el2 = jnp.where(emask2, lg, NEG)
    v2 = jnp.max(el2, axis=-1, keepdims=True)
    i2 = jnp.min(jnp.where(emask2 & (el2 == v2), lanef, big), axis=-1, keepdims=True)
    t = jnp.exp(v2 - v1)
    den = 1.0 + t
    w1 = p_g * (1.0 / den)
    w2 = p_g * (t / den)
    route = jnp.where(lanei == 0, i1 - N_GROUPS,
                      jnp.where(lanei == 1, i2 - N_GROUPS,
                                jnp.where(lanei == 2, w1, jnp.where(lanei == 3, w2, 0.0))))
    route_ref[...] = route


def _post(oa, ob, ga, gb, x2, wa, wb, wo, g2, wrh, wrl, br):
    t = x2.shape[0]
    n = t // TM_POST
    row = lambda w: pl.BlockSpec((TM_POST, w), lambda i: (i, 0))
    full = lambda a: pl.BlockSpec(a.shape, lambda i: (0,) * a.ndim)
    return pl.pallas_call(
        _post_kernel,
        grid=(n,),
        in_specs=[row(512), row(512), row(D_MODEL), row(D_MODEL), row(D_MODEL),
                  full(wa), full(wb), full(wo), full(g2), full(wrh), full(wrl), full(br)],
        out_specs=[row(D_MODEL), row(D_MODEL), row(LANES)],
        out_shape=[jax.ShapeDtypeStruct((t, D_MODEL), jnp.float32),
                   jax.ShapeDtypeStruct((t, D_MODEL), jnp.float32),
                   jax.ShapeDtypeStruct((t, LANES), jnp.float32)],
        compiler_params=_cparams(("arbitrary",)),
        name="post",
    )(oa, ob, ga, gb, x2, wa, wb, wo, g2, wrh, wrl, br)


def _rank_kernel(route_ref, ltri_ref, utri_ref, dest_ref, cnt_ref, cnt_sc, pstart_sc, base_sc):
    f32 = jnp.float32
    p = pl.program_id(0)
    i = pl.program_id(1)
    route = route_ref[...]
    lanef = lax.broadcasted_iota(jnp.int32, route.shape, 1).astype(f32)
    oh1 = (lanef == route[:, 0:1]).astype(f32)
    oh2 = (lanef == route[:, 1:2]).astype(f32)
    both = oh1 + oh2
    colsum = jnp.sum(both, axis=0, keepdims=True)

    @pl.when((p == 0) & (i == 0))
    def _():
        cnt_sc[...] = jnp.zeros(cnt_sc.shape, f32)

    @pl.when(p == 0)
    def _():
        cnt_sc[...] += colsum
        dest_ref[...] = jnp.zeros(dest_ref.shape, f32)
        cnt_ref[...] = jnp.zeros(cnt_ref.shape, f32)

    @pl.when((p == 1) & (i == 0))
    def _():
        cnt = cnt_sc[...]
        chi = jnp.floor(cnt * (1.0 / 256.0))
        clo = cnt - chi * 256.0
        split = jnp.concatenate([jnp.broadcast_to(chi, (8, LANES)), jnp.broadcast_to(clo, (8, LANES))], axis=0)
        excl = jnp.dot(split.astype(jnp.bfloat16), utri_ref[...], preferred_element_type=f32)
        pstart_sc[...] = excl[0:1] * 256.0 + excl[8:9]
        base_sc[...] = jnp.zeros(base_sc.shape, f32)

    @pl.when(p == 1)
    def _():
        prior = jnp.dot(ltri_ref[...], both.astype(jnp.bfloat16), preferred_element_type=f32)
        slot = prior + base_sc[...] + pstart_sc[...]
        d1 = jnp.sum(oh1 * slot, axis=-1, keepdims=True)
        d2 = jnp.sum(oh2 * slot, axis=-1, keepdims=True)
        dest_ref[...] = jnp.where(lanef == 0.0, d1, jnp.where(lanef == 1.0, d2, 0.0))
        base_sc[...] += colsum
        cnt_ref[...] = jnp.broadcast_to(cnt_sc[...], cnt_ref.shape)


def _rank(route, ltri, utri):
    t = route.shape[0]
    n = t // TM_RANK
    full = lambda a: pl.BlockSpec(a.shape, lambda p, i: (0,) * a.ndim)
    row1 = lambda: pltpu.VMEM((1, LANES), jnp.float32)
    return pl.pallas_call(
        _rank_kernel,
        grid=(2, n),
        in_specs=[pl.BlockSpec((TM_RANK, LANES), lambda p, i: (i, 0)), full(ltri), full(utri)],
        out_specs=[pl.BlockSpec((TM_RANK, LANES), lambda p, i: (i * p, 0)),
                   pl.BlockSpec((8, LANES), lambda p, i: (0, 0))],
        out_shape=[jax.ShapeDtypeStruct((t, LANES), jnp.float32),
                   jax.ShapeDtypeStruct((8, LANES), jnp.float32)],
        scratch_shapes=[row1(), row1(), row1()],
        compiler_params=_cparams(("arbitrary", "arbitrary")),
        name="rank",
    )(route, ltri, utri)


def _dispatch_kernel(dest_ref, hn_ref, xs_ref, sem):
    def row_copy(r, k):
        d = dest_ref[0, 0, 2 * r + k]
        return pltpu.make_async_copy(hn_ref.at[pl.ds(r, 1), :], xs_ref.at[pl.ds(d, 1), :], sem)

    def issue(r, c):
        row_copy(r, 0).start()
        row_copy(r, 1).start()
        return c

    def drain(r, c):
        row_copy(r, 0).wait()
        row_copy(r, 1).wait()
        return c

    lax.fori_loop(0, TM_ROWS, issue, 0)
    lax.fori_loop(0, TM_ROWS, drain, 0)


def _dispatch(dest3, hn, n_rows):
    t = hn.shape[0]
    n = t // TM_ROWS
    return pl.pallas_call(
        _dispatch_kernel,
        grid=(n,),
        in_specs=[pl.BlockSpec((1, 1, 2 * TM_ROWS), lambda i: (i, 0, 0), memory_space=pltpu.SMEM),
                  pl.BlockSpec((TM_ROWS, D_MODEL), lambda i: (i, 0))],
        out_specs=pl.BlockSpec(memory_space=pl.ANY),
        out_shape=jax.ShapeDtypeStruct((n_rows, D_MODEL), jnp.float32),
        scratch_shapes=[pltpu.SemaphoreType.DMA],
        compiler_params=_cparams(("arbitrary",)),
        name="dispatch",
    )(dest3, hn)


def _ffn_kernel(tile_ref, exp_ref, lo_ref, hi_ref, cast_ref, init_ref,
                xs_ref, wg_ref, wu_ref, wd_ref, ys_ref, wg_sc, wu_sc, wd_sc):
    v = pl.program_id(0)
    lo = lo_ref[v]
    hi = hi_ref[v]

    @pl.when(init_ref[v] == 1)
    def _():
        ys_ref[...] = jnp.zeros(ys_ref.shape, ys_ref.dtype)

    @pl.when(hi > lo)
    def _():
        @pl.when(cast_ref[v] == 1)
        def _():
            wg_sc[...] = wg_ref[0].astype(jnp.bfloat16)
            wu_sc[...] = wu_ref[0].astype(jnp.bfloat16)
            wd_sc[...] = wd_ref[0].astype(jnp.bfloat16)

        x = xs_ref[...].astype(jnp.bfloat16)
        g = jnp.dot(x, wg_sc[...], preferred_element_type=jnp.float32)
        u = jnp.dot(x, wu_sc[...], preferred_element_type=jnp.float32)
        hb = (g * jax.nn.sigmoid(g) * u).astype(jnp.bfloat16)
        y = jnp.dot(hb, wd_sc[...], preferred_element_type=jnp.float32)
        rows = tile_ref[v] * FFN_BLK + lax.broadcasted_iota(jnp.int32, y.shape, 0)
        ys_ref[...] = jnp.where((rows >= lo) & (rows < hi), y, ys_ref[...])


def _ffn(seg, xs, w_gate, w_up, w_down):
    n_rows = xs.shape[0]
    n_seg = seg[0].shape[0]
    grid_spec = pltpu.PrefetchScalarGridSpec(
        num_scalar_prefetch=6,
        grid=(n_seg,),
        in_specs=[
            pl.BlockSpec((FFN_BLK, D_MODEL), lambda v, t, e, *_: (t[v], 0)),
            pl.BlockSpec((1, D_MODEL, D_EXPERT), lambda v, t, e, *_: (e[v], 0, 0)),
            pl.BlockSpec((1, D_MODEL, D_EXPERT), lambda v, t, e, *_: (e[v], 0, 0)),
            pl.BlockSpec((1, D_EXPERT, D_MODEL), lambda v, t, e, *_: (e[v], 0, 0)),
        ],
        out_specs=pl.BlockSpec((FFN_BLK, D_MODEL), lambda v, t, e, *_: (t[v], 0)),
        scratch_shapes=[pltpu.VMEM((D_MODEL, D_EXPERT), jnp.bfloat16),
                        pltpu.VMEM((D_MODEL, D_EXPERT), jnp.bfloat16),
                        pltpu.VMEM((D_EXPERT, D_MODEL), jnp.bfloat16)],
    )
    return pl.pallas_call(
        _ffn_kernel,
        grid_spec=grid_spec,
        out_shape=jax.ShapeDtypeStruct((n_rows, D_MODEL), jnp.float32),
        compiler_params=_cparams(("arbitrary",)),
        name="ffn",
    )(*seg, xs, w_gate, w_up, w_down)


def _segments(counts, n_rows):
    i32 = jnp.int32
    n_tiles = n_rows // FFN_BLK
    ends = jnp.cumsum(counts)
    starts = ends - counts
    cuts = jnp.sort(jnp.concatenate([jnp.arange(n_tiles, dtype=i32) * FFN_BLK, starts.astype(i32)]))
    lo = cuts
    hi = jnp.concatenate([cuts[1:], jnp.array([n_rows], i32)])
    valid = hi > lo
    tile = jnp.minimum(lo // FFN_BLK, n_tiles - 1)
    expert = jnp.minimum(jnp.searchsorted(ends, lo, side="right"), N_EXPERTS - 1).astype(i32)
    expert = lax.cummax(jnp.where(valid, expert, 0))
    prev_expert = jnp.concatenate([jnp.array([-1], i32), expert[:-1]])
    first_valid = valid & (jnp.cumsum(valid.astype(i32)) == 1)
    cast = valid & ((expert != prev_expert) | first_valid)
    prev_tile = jnp.concatenate([jnp.array([-1], i32), tile[:-1]])
    init = tile != prev_tile
    return (tile.astype(i32), expert, lo.astype(i32), hi.astype(i32), cast.astype(i32), init.astype(i32))


def _combine_kernel(dest_ref, route_ref, x1_ref, ys_ref, out_ref, y0_sc, y1_sc, sem):
    def row_copy(r, k):
        d = dest_ref[0, 0, 2 * r + k]
        dst = y0_sc if k == 0 else y1_sc
        return pltpu.make_async_copy(ys_ref.at[pl.ds(d, 1), :], dst.at[pl.ds(r, 1), :], sem)

    def issue(r, c):
        row_copy(r, 0).start()
        row_copy(r, 1).start()
        return c

    def drain(r, c):
        row_copy(r, 0).wait()
        row_copy(r, 1).wait()
        return c

    lax.fori_loop(0, TM_ROWS, issue, 0)
    lax.fori_loop(0, TM_ROWS, drain, 0)
    route = route_ref[...]
    out_ref[...] = x1_ref[...] + (route[:, 2:3] * y0_sc[...] + route[:, 3:4] * y1_sc[...])


def _combine(dest3, route, x1, ys):
    t = x1.shape[0]
    n = t // TM_ROWS
    return pl.pallas_call(
        _combine_kernel,
        grid=(n,),
        in_specs=[pl.BlockSpec((1, 1, 2 * TM_ROWS), lambda i: (i, 0, 0), memory_space=pltpu.SMEM),
                  pl.BlockSpec((TM_ROWS, LANES), lambda i: (i, 0)),
                  pl.BlockSpec((TM_ROWS, D_MODEL), lambda i: (i, 0)),
                  pl.BlockSpec(memory_space=pl.ANY)],
        out_specs=pl.BlockSpec((TM_ROWS, D_MODEL), lambda i: (i, 0)),
        out_shape=jax.ShapeDtypeStruct((t, D_MODEL), jnp.float32),
        scratch_shapes=[pltpu.VMEM((TM_ROWS, D_MODEL), jnp.float32),
                        pltpu.VMEM((TM_ROWS, D_MODEL), jnp.float32),
                        pltpu.SemaphoreType.DMA],
        compiler_params=_cparams(("arbitrary",)),
        name="combine",
    )(dest3, route, x1, ys)


def _t5_bucket(rel):
    nb = T5_BUCKETS // 2
    max_exact = nb // 2
    side = jnp.where(rel > 0, nb, 0)
    n = jnp.abs(rel)
    nf = jnp.maximum(n, 1).astype(jnp.float32)
    large = max_exact + (jnp.log(nf / max_exact) / math.log(T5_MAX_DIST / max_exact)
                         * (nb - max_exact)).astype(jnp.int32)
    large = jnp.minimum(large, nb - 1)
    return side + jnp.where(n < max_exact, n, large)


def _diff_bias_tiles(t5_table):
    qi = jnp.arange(TQ)[:, None]
    tiles = []
    for j in range(2):
        kj = jnp.arange(TK)[None, :] - j * TK
        bias = jnp.transpose(t5_table[_t5_bucket(kj - qi)], (2, 0, 1)).astype(jnp.float32)
        allowed = (kj // CHUNK) <= (qi // CHUNK)
        tiles.append(jnp.where(allowed[None], bias, NEG))
    near = jnp.stack(tiles, axis=1)
    near = jnp.concatenate([near, near], axis=2)
    far = t5_table[_t5_bucket(jnp.array(-(TK + 1)))].astype(jnp.float32)
    return near, far


def _band_bias_tiles(rel_table):
    qi = jnp.arange(TQ)[:, None]
    tiles = []
    for j in range(3):
        kj = jnp.arange(TK)[None, :] - j * TK
        idx = jnp.clip(kj - qi, -B_MAX_REL, B_MAX_REL) + B_MAX_REL
        bias = jnp.transpose(rel_table[idx], (2, 0, 1)).astype(jnp.float32)
        dchunk = qi // CHUNK - kj // CHUNK
        allowed = (dchunk >= 0) & (dchunk <= B_LEFT_CHUNKS)
        tiles.append(jnp.where(allowed[None], bias, NEG))
    t = jnp.stack(tiles, axis=1)
    return t.reshape(B_HEADS // 2, 2, 3, TQ, TK).transpose(0, 2, 1, 3, 4).reshape(B_HEADS // 2, 3, 2 * TQ, TK)


def kernel(x, norm1_g, w_in, a_qnorm_g, a_knorm_g, a_lambda, a_subln_g, t5_table, b_qnorm_g, b_knorm_g,
           b_rel_table, w_branch_a, w_branch_b, w_out, norm2_g, w_router_group, b_router_group,
           w_router_expert, b_router_expert, w_gate, w_up, w_down):
    bsz, s_len, _ = x.shape
    n_tok = bsz * s_len
    f32, bf16 = jnp.float32, jnp.bfloat16
    assert s_len % TQ == 0 and TQ == TK and TQ >= 2 * T5_MAX_DIST and n_tok % TM_PROJ == 0
    l = 0
    x2 = x.reshape(n_tok, D_MODEL)

    w = w_in[l]
    qk = w[:, :1024].reshape(D_MODEL, 2, 2, A_HEADS, A_HEAD_DIM)
    qk = qk.transpose(0, 1, 3, 2, 4).reshape(D_MODEL, 1024)
    w_perm = jnp.concatenate([qk, w[:, 1024:]], axis=1).astype(bf16)
    scale = A_HEAD_DIM ** -0.5
    gn = jnp.stack([jnp.tile(a_qnorm_g[l] * scale, 8), jnp.tile(a_knorm_g[l], 8),
                    jnp.tile(b_qnorm_g[l] * (B_HEAD_DIM ** -0.5), 8), jnp.tile(b_knorm_g[l], 8)]).astype(f32)
    gmat = jnp.asarray(np.kron(np.eye(8), np.ones((64, 64))), dtype=bf16)

    qa, ka, va, qb, kb, vb, ga, gb = _proj(x2, norm1_g[l][None].astype(f32), w_perm, gn, gmat)

    lam_init = 0.8 - 0.6 * math.exp(-0.3 * l)
    lp = a_lambda[l].astype(f32)
    lam = jnp.exp(jnp.sum(lp[0] * lp[1])) - jnp.exp(jnp.sum(lp[2] * lp[3])) + lam_init
    bias_a, far_a = _diff_bias_tiles(t5_table)
    scal_a = jnp.concatenate([lam[None], far_a]).astype(f32)
    gsub = (a_subln_g[l] * (1.0 - lam_init))[None].astype(f32)
    oa = _attention(scal_a, qa, ka, va, bias_a, gsub, bsz=bsz, s_len=s_len, mode="diff")

    bias_b = _band_bias_tiles(b_rel_table[l])
    ob = _attention(jnp.zeros((1,), f32), qb, kb, vb, bias_b, gsub, bsz=bsz, s_len=s_len, mode="band")

    wr = jnp.zeros((D_MODEL, LANES), f32)
    wr = wr.at[:, :N_GROUPS].set(w_router_group[l]).at[:, N_GROUPS:N_GROUPS + N_EXPERTS].set(w_router_expert[l])
    wrh = wr.astype(bf16)
    wrl = (wr - wrh.astype(f32)).astype(bf16)
    br = jnp.zeros((1, LANES), f32)
    br = br.at[0, :N_GROUPS].set(b_router_group[l]).at[0, N_GROUPS:N_GROUPS + N_EXPERTS].set(b_router_expert[l])
    x1, hn, route = _post(oa, ob, ga, gb, x2, w_branch_a[l].astype(bf16), w_branch_b[l].astype(bf16),
                          w_out[l].astype(bf16), norm2_g[l][None].astype(f32), wrh, wrl, br)

    ltri = jnp.asarray(np.tril(np.ones((TM_RANK, TM_RANK)), -1), dtype=bf16)
    utri = jnp.asarray(np.triu(np.ones((LANES, LANES)), 1), dtype=bf16)
    dest, cnt = _rank(route, ltri, utri)

    counts = cnt[0, :N_EXPERTS].astype(jnp.int32)
    n_rows = n_tok * TOP_K
    seg = _segments(counts, n_rows)

    dest3 = dest[:, :TOP_K].astype(jnp.int32).reshape(n_tok // TM_ROWS, 1, TOP_K * TM_ROWS)
    xs = _dispatch(dest3, hn, n_rows)
    ys = _ffn(seg, xs, w_gate[l], w_up[l], w_down[l])
    out = _combine(dest3, route, x1, ys)
    return out.reshape(bsz, s_len, D_MODEL)
```

```python
import functools
import math

import jax
import jax.numpy as jnp
import numpy as np
from jax import lax
from jax.experimental import pallas as pl
from jax.experimental.pallas import tpu as pltpu

D_MODEL = 1024
CHUNK = 64
A_HEADS = 4
A_HEAD_DIM = 64
A_VDIM = 2 * A_HEAD_DIM
B_HEADS = 8
B_HEAD_DIM = 64
B_LEFT_CHUNKS = 8
B_MAX_REL = 128
T5_BUCKETS = 32
T5_MAX_DIST = 128
N_GROUPS = 4
EXPERTS_PER_GROUP = 8
N_EXPERTS = N_GROUPS * EXPERTS_PER_GROUP
TOP_K = 2
D_EXPERT = 512
EPS = 1e-6
NEG = -1e30

LANES = 128
A_W = A_HEADS * 2 * A_HEAD_DIM
B_W = B_HEADS * B_HEAD_DIM
PROJ_W = 4 * 256 + 4 * 512 + 2 * D_MODEL

TM_PROJ = 512
TQ = 512
TK = 512
TM_POST = 512
TM_RANK = 512
TM_ROWS = 256
FFN_BLK = 256
VMEM_LIMIT = 56 * 1024 * 1024


def _cparams(sem):
    return pltpu.CompilerParams(dimension_semantics=sem, vmem_limit_bytes=VMEM_LIMIT)


def _proj_kernel(x_ref, g1_ref, w_ref, gn_ref, gmat_ref,
                 qa_ref, ka_ref, va_ref, qb_ref, kb_ref, vb_ref, ga_ref, gb_ref):
    x = x_ref[...]
    xn = x * lax.rsqrt(jnp.mean(x * x, axis=-1, keepdims=True) + EPS) * g1_ref[...]
    xn = xn.astype(jnp.bfloat16)

    def slab(c0, width):
        return jnp.dot(xn, w_ref[:, c0:c0 + width], preferred_element_type=jnp.float32)

    def headnorm(y, gi):
        ss = jnp.dot((y * y).astype(jnp.bfloat16), gmat_ref[...],
                     preferred_element_type=jnp.float32)
        return y * lax.rsqrt(ss * (1.0 / A_HEAD_DIM) + EPS) * gn_ref[gi:gi + 1, :]

    qa_ref[...] = headnorm(slab(0, 512), 0).astype(jnp.bfloat16)
    ka_ref[...] = headnorm(slab(512, 512), 1).astype(jnp.bfloat16)
    va_ref[...] = slab(1024, 512).astype(jnp.bfloat16)
    qb_ref[...] = headnorm(slab(1536, 512), 2).astype(jnp.bfloat16)
    kb_ref[...] = headnorm(slab(2048, 512), 3).astype(jnp.bfloat16)
    vb_ref[...] = slab(2560, 512).astype(jnp.bfloat16)
    for j in range(2):
        ga_ref[:, j * 512:(j + 1) * 512] = jax.nn.sigmoid(slab(3072 + j * 512, 512)).astype(jnp.bfloat16)
        gb_ref[:, j * 512:(j + 1) * 512] = jax.nn.sigmoid(slab(4096 + j * 512, 512)).astype(jnp.bfloat16)


def _proj(x2, g1, w_perm, gn, gmat):
    t = x2.shape[0]
    n = t // TM_PROJ
    row = lambda w: pl.BlockSpec((TM_PROJ, w), lambda i: (i, 0))
    full = lambda a: pl.BlockSpec(a.shape, lambda i: (0,) * a.ndim)
    outs = [jax.ShapeDtypeStruct((t, 512), jnp.bfloat16)] * 6 + [jax.ShapeDtypeStruct((t, D_MODEL), jnp.bfloat16)] * 2
    return pl.pallas_call(
        _proj_kernel,
        grid=(n,),
        in_specs=[row(D_MODEL), full(g1), full(w_perm), full(gn), full(gmat)],
        out_specs=[row(512)] * 6 + [row(D_MODEL)] * 2,
        out_shape=outs,
        compiler_params=_cparams(("arbitrary",)),
        name="proj",
    )(x2, g1, w_perm, gn, gmat)


def _attn_kernel(scal_ref, q_ref, k_ref, v_ref, bias_ref, gsub_ref, o_ref,
                 qz_sc, m_sc, l_sc, acc_sc, mf_sc, lf_sc, accf_sc, *, mode, n_near):
    h = pl.program_id(1)
    qi = pl.program_id(2)
    nc = TK // LANES
    q = q_ref[...]
    lane = lax.broadcasted_iota(jnp.int32, q.shape, 1)
    zero = jnp.zeros_like(q)
    qz_sc[0:TQ, :] = jnp.where(lane < 64, q, zero)
    qz_sc[TQ:2 * TQ, :] = jnp.where(lane >= 64, q, zero)

    def scores(kblk):
        k = k_ref[pl.ds(pl.multiple_of(kblk * TK, TK), TK), :]
        return lax.dot_general(qz_sc[...], k, (((1,), (1,)), ((), ())), preferred_element_type=jnp.float32)

    def update(s, kblk, m_ref, l_ref, a_ref):
        cols = [s[:, c * LANES:(c + 1) * LANES] for c in range(nc)]
        m_old = m_ref[...]
        m_new = jnp.maximum(m_old, jnp.max(functools.reduce(jnp.maximum, cols), axis=-1, keepdims=True))
        alpha = jnp.exp(m_old - m_new)
        ps = [jnp.exp(c - m_new) for c in cols]
        l_ref[...] = alpha * l_ref[...] + functools.reduce(jnp.add, ps)
        p = jnp.concatenate([x.astype(jnp.bfloat16) for x in ps], axis=1)
        v = v_ref[pl.ds(pl.multiple_of(kblk * TK, TK), TK), :]
        a_ref[...] = alpha * a_ref[...] + jnp.dot(p, v, preferred_element_type=jnp.float32)
        m_ref[...] = m_new

    m_sc[...] = jnp.full(m_sc.shape, NEG, jnp.float32)
    l_sc[...] = jnp.zeros(l_sc.shape, jnp.float32)
    acc_sc[...] = jnp.zeros(acc_sc.shape, jnp.float32)

    for j in range(n_near):
        def near(j=j):
            update(scores(qi - j) + bias_ref[0, j], qi - j, m_sc, l_sc, acc_sc)
        if j == 0:
            near()
        else:
            pl.when(qi >= j)(near)

    if mode == "diff":
        mf_sc[...] = jnp.full(mf_sc.shape, NEG, jnp.float32)
        lf_sc[...] = jnp.zeros(lf_sc.shape, jnp.float32)
        accf_sc[...] = jnp.zeros(accf_sc.shape, jnp.float32)

        def far(kblk, carry):
            update(scores(kblk), kblk, mf_sc, lf_sc, accf_sc)
            return carry

        lax.fori_loop(0, jnp.maximum(qi - (n_near - 1), 0), far, 0)

        cfar = scal_ref[1 + h]
        mf = mf_sc[...] + cfar
        mn = m_sc[...]
        m = jnp.maximum(mf, mn)
        wf = jnp.exp(mf - m)
        wn = jnp.exp(mn - m)
        l = jnp.sum(wf * lf_sc[...] + wn * l_sc[...], axis=-1, keepdims=True)
        o = (wf * accf_sc[...] + wn * acc_sc[...]) / l
        lam = scal_ref[0]
        od = o[:TQ] - lam * o[TQ:]
        od = od * lax.rsqrt(jnp.mean(od * od, axis=-1, keepdims=True) + EPS) * gsub_ref[...]
        o_ref[...] = od.astype(o_ref.dtype)
    else:
        o = acc_sc[...] / jnp.sum(l_sc[...], axis=-1, keepdims=True)
        o_ref[...] = jnp.where(lane < 64, o[:TQ], o[TQ:]).astype(o_ref.dtype)


def _attention(scal, q, k, v, bias, gsub, *, bsz, s_len, mode):
    n_blk = q.shape[1] // LANES
    nq = s_len // TQ
    n_near = bias.shape[1]
    kern = functools.partial(_attn_kernel, mode=mode, n_near=n_near)
    acc = lambda: pltpu.VMEM((2 * TQ, LANES), jnp.float32)
    return pl.pallas_call(
        kern,
        grid=(bsz, n_blk, nq),
        in_specs=[
            pl.BlockSpec(memory_space=pltpu.SMEM),
            pl.BlockSpec((TQ, LANES), lambda b, h, i: (b * nq + i, h)),
            pl.BlockSpec((s_len, LANES), lambda b, h, i: (b, h)),
            pl.BlockSpec((s_len, LANES), lambda b, h, i: (b, h)),
            pl.BlockSpec((1, n_near, 2 * TQ, TK), lambda b, h, i: (h, 0, 0, 0)),
            pl.BlockSpec((1, LANES), lambda b, h, i: (0, 0)),
        ],
        out_specs=pl.BlockSpec((TQ, LANES), lambda b, h, i: (b * nq + i, h)),
        out_shape=jax.ShapeDtypeStruct(q.shape, jnp.bfloat16),
        scratch_shapes=[pltpu.VMEM((2 * TQ, LANES), jnp.bfloat16), acc(), acc(), acc(), acc(), acc(), acc()],
        compiler_params=_cparams(("arbitrary", "arbitrary", "arbitrary")),
        name="attn_" + mode,
    )(scal, q, k, v, bias, gsub)


def _post_kernel(oa_ref, ob_ref, ga_ref, gb_ref, x_ref, wa_ref, wb_ref, wo_ref, g2_ref,
                 wrh_ref, wrl_ref, br_ref, x1_ref, hn_ref, route_ref):
    f32 = jnp.float32
    ya = jnp.dot(oa_ref[...], wa_ref[...], preferred_element_type=f32)
    yb = jnp.dot(ob_ref[...], wb_ref[...], preferred_element_type=f32)
    mixed = ga_ref[...].astype(f32) * ya + gb_ref[...].astype(f32) * yb
    x1 = x_ref[...] + jnp.dot(mixed.astype(jnp.bfloat16), wo_ref[...], preferred_element_type=f32)
    x1_ref[...] = x1
    hn = x1 * lax.rsqrt(jnp.mean(x1 * x1, axis=-1, keepdims=True) + EPS) * g2_ref[...]
    hn_ref[...] = hn

    hh = hn.astype(jnp.bfloat16)
    hl = (hn - hh.astype(f32)).astype(jnp.bfloat16)
    lg = (jnp.dot(hh, wrh_ref[...], preferred_element_type=f32)
          + jnp.dot(hh, wrl_ref[...], preferred_element_type=f32)
          + jnp.dot(hl, wrh_ref[...], preferred_element_type=f32)) + br_ref[...]

    lanei = lax.broadcasted_iota(jnp.int32, lg.shape, 1)
    lanef = lanei.astype(f32)
    big = 999.0
    gmask = lanei < N_GROUPS
    gl = jnp.where(gmask, lg, NEG)
    gm = jnp.max(gl, axis=-1, keepdims=True)
    ge = jnp.where(gmask, jnp.exp(gl - gm), 0.0)
    gp = ge / jnp.sum(ge, axis=-1, keepdims=True)
    p_g = jnp.max(gp, axis=-1, keepdims=True)
    gidx = jnp.min(jnp.where(gmask & (gp == p_g), lanef, big), axis=-1, keepdims=True)
    egrp = lax.shift_right_arithmetic(lanei - N_GROUPS, 3).astype(f32)
    emask = (lanei >= N_GROUPS) & (lanei < N_GROUPS + N_EXPERTS) & (egrp == gidx)
    el = jnp.where(emask, lg, NEG)
    v1 = jnp.max(el, axis=-1, keepdims=True)
    i1 = jnp.min(jnp.where(emask & (el == v1), lanef, big), axis=-1, keepdims=True)
    emask2 = emask & (lanef != i1)
    el2 = jnp.where(emask2, lg, NEG)
    v2 = jnp.max(el2, axis=-1, keepdims=True)
    i2 = jnp.min(jnp.where(emask2 & (el2 == v2), lanef, big), axis=-1, keepdims=True)
    t = jnp.exp(v2 - v1)
    den = 1.0 + t
    w1 = p_g * (1.0 / den)
    w2 = p_g * (t / den)
    route = jnp.where(lanei == 0, i1 - N_GROUPS,
                      jnp.where(lanei == 1, i2 - N_GROUPS,
                                jnp.where(lanei == 2, w1, jnp.where(lanei == 3, w2, 0.0))))
    route_ref[...] = route


def _post(oa, ob, ga, gb, x2, wa, wb, wo, g2, wrh, wrl, br):
    t = x2.shape[0]
    n = t // TM_POST
    row = lambda w: pl.BlockSpec((TM_POST, w), lambda i: (i, 0))
    full = lambda a: pl.BlockSpec(a.shape, lambda i: (0,) * a.ndim)
    return pl.pallas_call(
        _post_kernel,
        grid=(n,),
        in_specs=[row(512), row(512), row(D_MODEL), row(D_MODEL), row(D_MODEL),
                  full(wa), full(wb), full(wo), full(g2), full(wrh), full(wrl), full(br)],
        out_specs=[row(D_MODEL), row(D_MODEL), row(LANES)],
        out_shape=[jax.ShapeDtypeStruct((t, D_MODEL), jnp.float32),
                   jax.ShapeDtypeStruct((t, D_MODEL), jnp.float32),
                   jax.ShapeDtypeStruct((t, LANES), jnp.float32)],
        compiler_params=_cparams(("arbitrary",)),
        name="post",
    )(oa, ob, ga, gb, x2, wa, wb, wo, g2, wrh, wrl, br)


def _rank_kernel(route_ref, ltri_ref, utri_ref, dest_ref, cnt_ref, cnt_sc, pstart_sc, base_sc):
    f32 = jnp.float32
    p = pl.program_id(0)
    i = pl.program_id(1)
    route = route_ref[...]
    lanef = lax.broadcasted_iota(jnp.int32, route.shape, 1).astype(f32)
    oh1 = (lanef == route[:, 0:1]).astype(f32)
    oh2 = (lanef == route[:, 1:2]).astype(f32)
    both = oh1 + oh2
    colsum = jnp.sum(both, axis=0, keepdims=True)

    @pl.when((p == 0) & (i == 0))
    def _():
        cnt_sc[...] = jnp.zeros(cnt_sc.shape, f32)

    @pl.when(p == 0)
    def _():
        cnt_sc[...] += colsum
        dest_ref[...] = jnp.zeros(dest_ref.shape, f32)
        cnt_ref[...] = jnp.zeros(cnt_ref.shape, f32)

    @pl.when((p == 1) & (i == 0))
    def _():
        cnt = cnt_sc[...]
        chi = jnp.floor(cnt * (1.0 / 256.0))
        clo = cnt - chi * 256.0
        split = jnp.concatenate([jnp.broadcast_to(chi, (8, LANES)), jnp.broadcast_to(clo, (8, LANES))], axis=0)
        excl = jnp.dot(split.astype(jnp.bfloat16), utri_ref[...], preferred_element_type=f32)
        pstart_sc[...] = excl[0:1] * 256.0 + excl[8:9]
        base_sc[...] = jnp.zeros(base_sc.shape, f32)

    @pl.when(p == 1)
    def _():
        prior = jnp.dot(ltri_ref[...], both.astype(jnp.bfloat16), preferred_element_type=f32)
        slot = prior + base_sc[...] + pstart_sc[...]
        d1 = jnp.sum(oh1 * slot, axis=-1, keepdims=True)
        d2 = jnp.sum(oh2 * slot, axis=-1, keepdims=True)
        dest_ref[...] = jnp.where(lanef == 0.0, d1, jnp.where(lanef == 1.0, d2, 0.0))
        base_sc[...] += colsum
        cnt_ref[...] = jnp.broadcast_to(cnt_sc[...], cnt_ref.shape)


def _rank(route, ltri, utri):
    t = route.shape[0]
    n = t // TM_RANK
    full = lambda a: pl.BlockSpec(a.shape, lambda p, i: (0,) * a.ndim)
    row1 = lambda: pltpu.VMEM((1, LANES), jnp.float32)
    return pl.pallas_call(
        _rank_kernel,
        grid=(2, n),
        in_specs=[pl.BlockSpec((TM_RANK, LANES), lambda p, i: (i, 0)), full(ltri), full(utri)],
        out_specs=[pl.BlockSpec((TM_RANK, LANES), lambda p, i: (i * p, 0)),
                   pl.BlockSpec((8, LANES), lambda p, i: (0, 0))],
        out_shape=[jax.ShapeDtypeStruct((t, LANES), jnp.float32),
                   jax.ShapeDtypeStruct((8, LANES), jnp.float32)],
        scratch_shapes=[row1(), row1(), row1()],
        compiler_params=_cparams(("arbitrary", "arbitrary")),
        name="rank",
    )(route, ltri, utri)


def _dispatch_kernel(dest_ref, hn_ref, xs_ref, sem):
    def row_copy(r, k):
        d = dest_ref[0, 0, 2 * r + k]
        return pltpu.make_async_copy(hn_ref.at[pl.ds(r, 1), :], xs_ref.at[pl.ds(d, 1), :], sem)

    def issue(r, c):
        row_copy(r, 0).start()
        row_copy(r, 1).start()
        return c

    def drain(r, c):
        row_copy(r, 0).wait()
        row_copy(r, 1).wait()
        return c

    lax.fori_loop(0, TM_ROWS, issue, 0)
    lax.fori_loop(0, TM_ROWS, drain, 0)


def _dispatch(dest3, hn, n_rows):
    t = hn.shape[0]
    n = t // TM_ROWS
    return pl.pallas_call(
        _dispatch_kernel,
        grid=(n,),
        in_specs=[pl.BlockSpec((1, 1, 2 * TM_ROWS), lambda i: (i, 0, 0), memory_space=pltpu.SMEM),
                  pl.BlockSpec((TM_ROWS, D_MODEL), lambda i: (i, 0))],
        out_specs=pl.BlockSpec(memory_space=pl.ANY),
        out_shape=jax.ShapeDtypeStruct((n_rows, D_MODEL), jnp.float32),
        scratch_shapes=[pltpu.SemaphoreType.DMA],
        compiler_params=_cparams(("arbitrary",)),
        name="dispatch",
    )(dest3, hn)


def _ffn_kernel(tile_ref, exp_ref, lo_ref, hi_ref, cast_ref, init_ref,
                xs_ref, wg_ref, wu_ref, wd_ref, ys_ref, wg_sc, wu_sc, wd_sc):
    v = pl.program_id(0)
    lo = lo_ref[v]
    hi = hi_ref[v]

    @pl.when(init_ref[v] == 1)
    def _():
        ys_ref[...] = jnp.zeros(ys_ref.shape, ys_ref.dtype)

    @pl.when(hi > lo)
    def _():
        @pl.when(cast_ref[v] == 1)
        def _():
            wg_sc[...] = wg_ref[0].astype(jnp.bfloat16)
            wu_sc[...] = wu_ref[0].astype(jnp.bfloat16)
            wd_sc[...] = wd_ref[0].astype(jnp.bfloat16)

        x = xs_ref[...].astype(jnp.bfloat16)
        g = jnp.dot(x, wg_sc[...], preferred_element_type=jnp.float32)
        u = jnp.dot(x, wu_sc[...], preferred_element_type=jnp.float32)
        hb = (g * jax.nn.sigmoid(g) * u).astype(jnp.bfloat16)
        y = jnp.dot(hb, wd_sc[...], preferred_element_type=jnp.float32)
        rows = tile_ref[v] * FFN_BLK + lax.broadcasted_iota(jnp.int32, y.shape, 0)
        ys_ref[...] = jnp.where((rows >= lo) & (rows < hi), y, ys_ref[...])


def _ffn(seg, xs, w_gate, w_up, w_down):
    n_rows = xs.shape[0]
    n_seg = seg[0].shape[0]
    grid_spec = pltpu.PrefetchScalarGridSpec(
        num_scalar_prefetch=6,
        grid=(n_seg,),
        in_specs=[
            pl.BlockSpec((FFN_BLK, D_MODEL), lambda v, t, e, *_: (t[v], 0)),
            pl.BlockSpec((1, D_MODEL, D_EXPERT), lambda v, t, e, *_: (e[v], 0, 0)),
            pl.BlockSpec((1, D_MODEL, D_EXPERT), lambda v, t, e, *_: (e[v], 0, 0)),
            pl.BlockSpec((1, D_EXPERT, D_MODEL), lambda v, t, e, *_: (e[v], 0, 0)),
        ],
        out_specs=pl.BlockSpec((FFN_BLK, D_MODEL), lambda v, t, e, *_: (t[v], 0)),
        scratch_shapes=[pltpu.VMEM((D_MODEL, D_EXPERT), jnp.bfloat16),
                        pltpu.VMEM((D_MODEL, D_EXPERT), jnp.bfloat16),
                        pltpu.VMEM((D_EXPERT, D_MODEL), jnp.bfloat16)],
    )
    return pl.pallas_call(
        _ffn_kernel,
        grid_spec=grid_spec,
        out_shape=jax.ShapeDtypeStruct((n_rows, D_MODEL), jnp.float32),
        compiler_params=_cparams(("arbitrary",)),
        name="ffn",
    )(*seg, xs, w_gate, w_up, w_down)


def _segments(counts, n_rows):
    i32 = jnp.int32
    n_tiles = n_rows // FFN_BLK
    n_seg = n_tiles + N_EXPERTS
    tri = jnp.tril(jnp.ones((N_EXPERTS, N_EXPERTS), i32))
    ends = jnp.sum(tri * counts[None, :], axis=1)
    starts = ends - counts
    edges = jnp.arange(n_tiles, dtype=i32) * FFN_BLK
    rank_e = jnp.arange(n_tiles, dtype=i32) + jnp.sum(starts[None, :] <= edges[:, None], axis=1)
    rank_s = jnp.arange(N_EXPERTS, dtype=i32) + jnp.sum(edges[None, :] < starts[:, None], axis=1)
    seg = jnp.arange(n_seg, dtype=i32)
    lo = (jnp.sum(jnp.where(rank_e[None, :] == seg[:, None], edges[None, :], 0), axis=1)
          + jnp.sum(jnp.where(rank_s[None, :] == seg[:, None], starts[None, :], 0), axis=1))
    hi = jnp.concatenate([lo[1:], jnp.array([n_rows], i32)])
    valid = hi > lo
    tile = jnp.minimum(lo // FFN_BLK, n_tiles - 1)
    expert = jnp.minimum(jnp.sum(ends[None, :] <= lo[:, None], axis=1), N_EXPERTS - 1).astype(i32)
    upto = seg[None, :] <= seg[:, None]
    expert = jnp.max(jnp.where(upto & valid[None, :], expert[None, :], 0), axis=1)
    prev_expert = jnp.concatenate([jnp.array([-1], i32), expert[:-1]])
    first_valid = valid & (jnp.sum(jnp.where(upto & valid[None, :], 1, 0), axis=1) == 1)
    cast = valid & ((expert != prev_expert) | first_valid)
    prev_tile = jnp.concatenate([jnp.array([-1], i32), tile[:-1]])
    init = tile != prev_tile
    return (tile.astype(i32), expert, lo.astype(i32), hi.astype(i32), cast.astype(i32), init.astype(i32))


def _combine_kernel(dest_ref, route_ref, x1_ref, ys_ref, out_ref, y0_sc, y1_sc, sem):
    def row_copy(r, k):
        d = dest_ref[0, 0, 2 * r + k]
        dst = y0_sc if k == 0 else y1_sc
        return pltpu.make_async_copy(ys_ref.at[pl.ds(d, 1), :], dst.at[pl.ds(r, 1), :], sem)

    def issue(r, c):
        row_copy(r, 0).start()
        row_copy(r, 1).start()
        return c

    def drain(r, c):
        row_copy(r, 0).wait()
        row_copy(r, 1).wait()
        return c

    lax.fori_loop(0, TM_ROWS, issue, 0)
    lax.fori_loop(0, TM_ROWS, drain, 0)
    route = route_ref[...]
    out_ref[...] = x1_ref[...] + (route[:, 2:3] * y0_sc[...] + route[:, 3:4] * y1_sc[...])


def _combine(dest3, route, x1, ys):
    t = x1.shape[0]
    n = t // TM_ROWS
    return pl.pallas_call(
        _combine_kernel,
        grid=(n,),
        in_specs=[pl.BlockSpec((1, 1, 2 * TM_ROWS), lambda i: (i, 0, 0), memory_space=pltpu.SMEM),
                  pl.BlockSpec((TM_ROWS, LANES), lambda i: (i, 0)),
                  pl.BlockSpec((TM_ROWS, D_MODEL), lambda i: (i, 0)),
                  pl.BlockSpec(memory_space=pl.ANY)],
        out_specs=pl.BlockSpec((TM_ROWS, D_MODEL), lambda i: (i, 0)),
        out_shape=jax.ShapeDtypeStruct((t, D_MODEL), jnp.float32),
        scratch_shapes=[pltpu.VMEM((TM_ROWS, D_MODEL), jnp.float32),
                        pltpu.VMEM((TM_ROWS, D_MODEL), jnp.float32),
                        pltpu.SemaphoreType.DMA],
        compiler_params=_cparams(("arbitrary",)),
        name="combine",
    )(dest3, route, x1, ys)


def _t5_bucket(rel):
    nb = T5_BUCKETS // 2
    max_exact = nb // 2
    side = jnp.where(rel > 0, nb, 0)
    n = jnp.abs(rel)
    nf = jnp.maximum(n, 1).astype(jnp.float32)
    large = max_exact + (jnp.log(nf / max_exact) / math.log(T5_MAX_DIST / max_exact)
                         * (nb - max_exact)).astype(jnp.int32)
    large = jnp.minimum(large, nb - 1)
    return side + jnp.where(n < max_exact, n, large)


def _toeplitz(vec, nq, nk):
    p = nq + nk
    lead = vec.shape[:-1]
    padded = jnp.concatenate([vec, jnp.zeros(lead + (1,), vec.dtype)], axis=-1)
    flat = jnp.tile(padded, (1,) * len(lead) + (nq,))
    skew = flat[..., :nq * (p - 1)].reshape(lead + (nq, p - 1))
    return skew[..., nq - 1:nq - 1 + nk]


def _diff_bias_tiles(t5_table):
    i = jnp.arange(TQ + TK - 1)
    qi = jnp.arange(TQ)[:, None]
    tiles = []
    for j in range(2):
        vec = t5_table[_t5_bucket(i - (TQ - 1) - j * TK)].astype(jnp.float32).T
        kj = jnp.arange(TK)[None, :] - j * TK
        allowed = (kj // CHUNK) <= (qi // CHUNK)
        tiles.append(jnp.where(allowed[None], _toeplitz(vec, TQ, TK), NEG))
    near = jnp.stack(tiles, axis=1)
    near = jnp.concatenate([near, near], axis=2)
    far = t5_table[_t5_bucket(jnp.array(-(TK + 1)))].astype(jnp.float32)
    return near, far


def _band_bias_tiles(rel_table):
    i = jnp.arange(TQ + TK - 1)
    qi = jnp.arange(TQ)[:, None]
    tiles = []
    for j in range(2):
        rel = i - (TQ - 1) - j * TK
        vec = rel_table[jnp.clip(rel, -B_MAX_REL, B_MAX_REL) + B_MAX_REL].astype(jnp.float32).T
        kj = jnp.arange(TK)[None, :] - j * TK
        dchunk = qi // CHUNK - kj // CHUNK
        allowed = (dchunk >= 0) & (dchunk <= B_LEFT_CHUNKS)
        tiles.append(jnp.where(allowed[None], _toeplitz(vec, TQ, TK), NEG))
    t = jnp.stack(tiles, axis=1)
    return t.reshape(B_HEADS // 2, 2, 2, TQ, TK).transpose(0, 2, 1, 3, 4).reshape(B_HEADS // 2, 2, 2 * TQ, TK)


def kernel(x, norm1_g, w_in, a_qnorm_g, a_knorm_g, a_lambda, a_subln_g, t5_table, b_qnorm_g, b_knorm_g,
           b_rel_table, w_branch_a, w_branch_b, w_out, norm2_g, w_router_group, b_router_group,
           w_router_expert, b_router_expert, w_gate, w_up, w_down):
    bsz, s_len, _ = x.shape
    n_tok = bsz * s_len
    f32, bf16 = jnp.float32, jnp.bfloat16
    assert s_len % TQ == 0 and TQ == TK and TQ % CHUNK == 0 and n_tok % TM_PROJ == 0
    assert TK >= T5_MAX_DIST and TK >= B_LEFT_CHUNKS * CHUNK
    l = 0
    x2 = x.reshape(n_tok, D_MODEL)

    w = w_in[l]
    qk = w[:, :1024].reshape(D_MODEL, 2, 2, A_HEADS, A_HEAD_DIM)
    qk = qk.transpose(0, 1, 3, 2, 4).reshape(D_MODEL, 1024)
    w_perm = jnp.concatenate([qk, w[:, 1024:]], axis=1).astype(bf16)
    scale = A_HEAD_DIM ** -0.5
    gn = jnp.stack([jnp.tile(a_qnorm_g[l] * scale, 8), jnp.tile(a_knorm_g[l], 8),
                    jnp.tile(b_qnorm_g[l] * (B_HEAD_DIM ** -0.5), 8), jnp.tile(b_knorm_g[l], 8)]).astype(f32)
    gmat = jnp.asarray(np.kron(np.eye(8), np.ones((64, 64))), dtype=bf16)

    qa, ka, va, qb, kb, vb, ga, gb = _proj(x2, norm1_g[l][None].astype(f32), w_perm, gn, gmat)

    lam_init = 0.8 - 0.6 * math.exp(-0.3 * l)
    lp = a_lambda[l].astype(f32)
    lam = jnp.exp(jnp.sum(lp[0] * lp[1])) - jnp.exp(jnp.sum(lp[2] * lp[3])) + lam_init
    bias_a, far_a = _diff_bias_tiles(t5_table)
    scal_a = jnp.concatenate([lam[None], far_a]).astype(f32)
    gsub = (a_subln_g[l] * (1.0 - lam_init))[None].astype(f32)
    oa = _attention(scal_a, qa, ka, va, bias_a, gsub, bsz=bsz, s_len=s_len, mode="diff")

    bias_b = _band_bias_tiles(b_rel_table[l])
    ob = _attention(jnp.zeros((1,), f32), qb, kb, vb, bias_b, gsub, bsz=bsz, s_len=s_len, mode="band")

    wr = jnp.zeros((D_MODEL, LANES), f32)
    wr = wr.at[:, :N_GROUPS].set(w_router_group[l]).at[:, N_GROUPS:N_GROUPS + N_EXPERTS].set(w_router_expert[l])
    wrh = wr.astype(bf16)
    wrl = (wr - wrh.astype(f32)).astype(bf16)
    br = jnp.zeros((1, LANES), f32)
    br = br.at[0, :N_GROUPS].set(b_router_group[l]).at[0, N_GROUPS:N_GROUPS + N_EXPERTS].set(b_router_expert[l])
    x1, hn, route = _post(oa, ob, ga, gb, x2, w_branch_a[l].astype(bf16), w_branch_b[l].astype(bf16),
                          w_out[l].astype(bf16), norm2_g[l][None].astype(f32), wrh, wrl, br)

    ltri = jnp.asarray(np.tril(np.ones((TM_RANK, TM_RANK)), -1), dtype=bf16)
    utri = jnp.asarray(np.triu(np.ones((LANES, LANES)), 1), dtype=bf16)
    dest, cnt = _rank(route, ltri, utri)

    counts = cnt[0, :N_EXPERTS].astype(jnp.int32)
    n_rows = n_tok * TOP_K
    seg = _segments(counts, n_rows)

    dest3 = dest[:, :TOP_K].astype(jnp.int32).reshape(n_tok // TM_ROWS, 1, TOP_K * TM_ROWS)
    xs = _dispatch(dest3, hn, n_rows)
    ys = _ffn(seg, xs, w_gate[l], w_up[l], w_down[l])
    out = _combine(dest3, route, x1, ys)
    return out.reshape(bsz, s_len, D_MODEL)
```

```python
import functools
import math

import jax
import jax.numpy as jnp
import numpy as np
from jax import lax
from jax.experimental import pallas as pl
from jax.experimental.pallas import tpu as pltpu

D_MODEL = 1024
CHUNK = 64
A_HEADS = 4
A_HEAD_DIM = 64
A_VDIM = 2 * A_HEAD_DIM
B_HEADS = 8
B_HEAD_DIM = 64
B_LEFT_CHUNKS = 8
B_MAX_REL = 128
T5_BUCKETS = 32
T5_MAX_DIST = 128
N_GROUPS = 4
EXPERTS_PER_GROUP = 8
N_EXPERTS = N_GROUPS * EXPERTS_PER_GROUP
TOP_K = 2
D_EXPERT = 512
EPS = 1e-6
NEG = -1e30

CHUNK_SHIFT = CHUNK.bit_length() - 1
assert 1 << CHUNK_SHIFT == CHUNK
LANES = 128
A_W = A_HEADS * 2 * A_HEAD_DIM
B_W = B_HEADS * B_HEAD_DIM
PROJ_W = 4 * 256 + 4 * 512 + 2 * D_MODEL

TM_PROJ = 512
TQ = 512
TK = 512
TM_POST = 512
TM_RANK = 512
TM_ROWS = 512
ROW_UNROLL = 8
FFN_BLK = 256
VMEM_LIMIT = 56 * 1024 * 1024


def _cparams(sem):
    return pltpu.CompilerParams(dimension_semantics=sem, vmem_limit_bytes=VMEM_LIMIT)


def _proj_kernel(x_ref, g1_ref, w_ref, gn_ref, gmat_ref,
                 qa_ref, ka_ref, va_ref, qb_ref, kb_ref, vb_ref, ga_ref, gb_ref):
    x = x_ref[...]
    xn = x * lax.rsqrt(jnp.mean(x * x, axis=-1, keepdims=True) + EPS) * g1_ref[...]
    xn = xn.astype(jnp.bfloat16)

    def slab(c0, width):
        return jnp.dot(xn, w_ref[:, c0:c0 + width], preferred_element_type=jnp.float32)

    def headnorm(y, gi):
        ss = jnp.dot((y * y).astype(jnp.bfloat16), gmat_ref[...],
                     preferred_element_type=jnp.float32)
        return y * lax.rsqrt(ss * (1.0 / A_HEAD_DIM) + EPS) * gn_ref[gi:gi + 1, :]

    qa_ref[...] = headnorm(slab(0, 512), 0).astype(jnp.bfloat16)
    ka_ref[...] = headnorm(slab(512, 512), 1).astype(jnp.bfloat16)
    va_ref[...] = slab(1024, 512).astype(jnp.bfloat16)
    qb_ref[...] = headnorm(slab(1536, 512), 2).astype(jnp.bfloat16)
    kb_ref[...] = headnorm(slab(2048, 512), 3).astype(jnp.bfloat16)
    vb_ref[...] = slab(2560, 512).astype(jnp.bfloat16)
    for j in range(2):
        ga_ref[:, j * 512:(j + 1) * 512] = jax.nn.sigmoid(slab(3072 + j * 512, 512)).astype(jnp.bfloat16)
        gb_ref[:, j * 512:(j + 1) * 512] = jax.nn.sigmoid(slab(4096 + j * 512, 512)).astype(jnp.bfloat16)


def _proj(x2, g1, w_perm, gn, gmat):
    t = x2.shape[0]
    n = t // TM_PROJ
    row = lambda w: pl.BlockSpec((TM_PROJ, w), lambda i: (i, 0))
    full = lambda a: pl.BlockSpec(a.shape, lambda i: (0,) * a.ndim)
    outs = [jax.ShapeDtypeStruct((t, 512), jnp.bfloat16)] * 6 + [jax.ShapeDtypeStruct((t, D_MODEL), jnp.bfloat16)] * 2
    return pl.pallas_call(
        _proj_kernel,
        grid=(n,),
        in_specs=[row(D_MODEL), full(g1), full(w_perm), full(gn), full(gmat)],
        out_specs=[row(512)] * 6 + [row(D_MODEL)] * 2,
        out_shape=outs,
        compiler_params=_cparams(("arbitrary",)),
        name="proj",
    )(x2, g1, w_perm, gn, gmat)


def _attn_kernel(scal_ref, q_ref, k_ref, v_ref, vec_ref, gsub_ref, o_ref,
                 bias_sc, qz_sc, m_sc, l_sc, acc_sc, mf_sc, lf_sc, accf_sc, *, mode, n_near):
    h = pl.program_id(1)
    qi = pl.program_id(2)
    nc = TK // LANES

    @pl.when(qi == 0)
    def _():
        qchunk = lax.shift_right_arithmetic(lax.broadcasted_iota(jnp.int32, (TQ, TK), 0), CHUNK_SHIFT)
        kcol = lax.broadcasted_iota(jnp.int32, (TQ, TK), 1)
        for j in range(n_near):
            dchunk = qchunk - lax.shift_right_arithmetic(kcol - j * TK, CHUNK_SHIFT)
            allowed = (dchunk >= 0) & (dchunk <= B_LEFT_CHUNKS) if mode == "band" else dchunk >= 0
            for half in range(2):
                vec = jnp.broadcast_to(vec_ref[0, j, half:half + 1, :], (TQ, TQ + TK))
                tile = pltpu.roll(vec, 0, 1, stride=1, stride_axis=0)[:, :TK]
                bias_sc[j, half * TQ:(half + 1) * TQ, :] = jnp.where(allowed, tile, NEG)

    q = q_ref[...]
    lane = lax.broadcasted_iota(jnp.int32, q.shape, 1)
    zero = jnp.zeros_like(q)
    qz_sc[0:TQ, :] = jnp.where(lane < 64, q, zero)
    qz_sc[TQ:2 * TQ, :] = jnp.where(lane >= 64, q, zero)

    def scores(kblk):
        k = k_ref[pl.ds(pl.multiple_of(kblk * TK, TK), TK), :]
        return lax.dot_general(qz_sc[...], k, (((1,), (1,)), ((), ())), preferred_element_type=jnp.float32)

    def update(s, kblk, m_ref, l_ref, a_ref):
        cols = [s[:, c * LANES:(c + 1) * LANES] for c in range(nc)]
        m_old = m_ref[...]
        m_new = jnp.maximum(m_old, jnp.max(functools.reduce(jnp.maximum, cols), axis=-1, keepdims=True))
        alpha = jnp.exp(m_old - m_new)
        ps = [jnp.exp(c - m_new) for c in cols]
        l_ref[...] = alpha * l_ref[...] + functools.reduce(jnp.add, ps)
        p = jnp.concatenate([x.astype(jnp.bfloat16) for x in ps], axis=1)
        v = v_ref[pl.ds(pl.multiple_of(kblk * TK, TK), TK), :]
        a_ref[...] = alpha * a_ref[...] + jnp.dot(p, v, preferred_element_type=jnp.float32)
        m_ref[...] = m_new

    m_sc[...] = jnp.full(m_sc.shape, NEG, jnp.float32)
    l_sc[...] = jnp.zeros(l_sc.shape, jnp.float32)
    acc_sc[...] = jnp.zeros(acc_sc.shape, jnp.float32)

    for j in range(n_near):
        def near(j=j):
            update(scores(qi - j) + bias_sc[j], qi - j, m_sc, l_sc, acc_sc)
        if j == 0:
            near()
        else:
            pl.when(qi >= j)(near)

    if mode == "diff":
        mf_sc[...] = jnp.full(mf_sc.shape, NEG, jnp.float32)
        lf_sc[...] = jnp.zeros(lf_sc.shape, jnp.float32)
        accf_sc[...] = jnp.zeros(accf_sc.shape, jnp.float32)

        def far(kblk, carry):
            update(scores(kblk), kblk, mf_sc, lf_sc, accf_sc)
            return carry

        n_far = jnp.maximum(qi - (n_near - 1), 0)

        def far_pair(i, carry):
            far(2 * i, carry)
            return far(2 * i + 1, carry)

        lax.fori_loop(0, n_far // 2, far_pair, 0)
        lax.fori_loop((n_far // 2) * 2, n_far, far, 0)

        cfar = scal_ref[1 + h]
        mf = mf_sc[...] + cfar
        mn = m_sc[...]
        m = jnp.maximum(mf, mn)
        wf = jnp.exp(mf - m)
        wn = jnp.exp(mn - m)
        l = jnp.sum(wf * lf_sc[...] + wn * l_sc[...], axis=-1, keepdims=True)
        o = (wf * accf_sc[...] + wn * acc_sc[...]) / l
        lam = scal_ref[0]
        od = o[:TQ] - lam * o[TQ:]
        od = od * lax.rsqrt(jnp.mean(od * od, axis=-1, keepdims=True) + EPS) * gsub_ref[...]
        o_ref[...] = od.astype(o_ref.dtype)
    else:
        o = acc_sc[...] / jnp.sum(l_sc[...], axis=-1, keepdims=True)
        o_ref[...] = jnp.where(lane < 64, o[:TQ], o[TQ:]).astype(o_ref.dtype)


def _attention(scal, q, k, v, vecs, gsub, *, bsz, s_len, mode):
    n_blk = q.shape[1] // LANES
    nq = s_len // TQ
    n_near = vecs.shape[1]
    assert vecs.shape == (n_blk, n_near, 2, TQ + TK)
    kern = functools.partial(_attn_kernel, mode=mode, n_near=n_near)
    acc = lambda: pltpu.VMEM((2 * TQ, LANES), jnp.float32)
    return pl.pallas_call(
        kern,
        grid=(bsz, n_blk, nq),
        in_specs=[
            pl.BlockSpec(memory_space=pltpu.SMEM),
            pl.BlockSpec((TQ, LANES), lambda b, h, i: (b * nq + i, h)),
            pl.BlockSpec((s_len, LANES), lambda b, h, i: (b, h)),
            pl.BlockSpec((s_len, LANES), lambda b, h, i: (b, h)),
            pl.BlockSpec((1, n_near, 2, TQ + TK), lambda b, h, i: (h, 0, 0, 0)),
            pl.BlockSpec((1, LANES), lambda b, h, i: (0, 0)),
        ],
        out_specs=pl.BlockSpec((TQ, LANES), lambda b, h, i: (b * nq + i, h)),
        out_shape=jax.ShapeDtypeStruct(q.shape, jnp.bfloat16),
        scratch_shapes=[pltpu.VMEM((n_near, 2 * TQ, TK), jnp.float32),
                        pltpu.VMEM((2 * TQ, LANES), jnp.bfloat16), acc(), acc(), acc(), acc(), acc(), acc()],
        compiler_params=_cparams(("arbitrary", "arbitrary", "arbitrary")),
        name="attn_" + mode,
    )(scal, q, k, v, vecs, gsub)


def _post_kernel(oa_ref, ob_ref, ga_ref, gb_ref, x_ref, wa_ref, wb_ref, wo_ref, g2_ref,
                 wrh_ref, wrl_ref, br_ref, x1_ref, hn_ref, route_ref):
    f32 = jnp.float32
    ya = jnp.dot(oa_ref[...], wa_ref[...], preferred_element_type=f32)
    yb = jnp.dot(ob_ref[...], wb_ref[...], preferred_element_type=f32)
    mixed = ga_ref[...].astype(f32) * ya + gb_ref[...].astype(f32) * yb
    x1 = x_ref[...] + jnp.dot(mixed.astype(jnp.bfloat16), wo_ref[...], preferred_element_type=f32)
    x1_ref[...] = x1
    hn = x1 * lax.rsqrt(jnp.mean(x1 * x1, axis=-1, keepdims=True) + EPS) * g2_ref[...]
    hh = hn.astype(jnp.bfloat16)
    bits = lax.bitcast_convert_type(hh.astype(f32), jnp.uint32)
    half = D_MODEL // 2
    hn_ref[...] = (bits[:, :half] & jnp.uint32(0xFFFF0000)) | lax.shift_right_logical(bits[:, half:], jnp.uint32(16))

    hl = (hn - hh.astype(f32)).astype(jnp.bfloat16)
    lg = (jnp.dot(hh, wrh_ref[...], preferred_element_type=f32)
          + jnp.dot(hh, wrl_ref[...], preferred_element_type=f32)
          + jnp.dot(hl, wrh_ref[...], preferred_element_type=f32)) + br_ref[...]

    lanei = lax.broadcasted_iota(jnp.int32, lg.shape, 1)
    lanef = lanei.astype(f32)
    big = 999.0
    gmask = lanei < N_GROUPS
    gl = jnp.where(gmask, lg, NEG)
    gm = jnp.max(gl, axis=-1, keepdims=True)
    ge = jnp.where(gmask, jnp.exp(gl - gm), 0.0)
    gp = ge / jnp.sum(ge, axis=-1, keepdims=True)
    p_g = jnp.max(gp, axis=-1, keepdims=True)
    gidx = jnp.min(jnp.where(gmask & (gp == p_g), lanef, big), axis=-1, keepdims=True)
    egrp = lax.shift_right_arithmetic(lanei - N_GROUPS, 3).astype(f32)
    emask = (lanei >= N_GROUPS) & (lanei < N_GROUPS + N_EXPERTS) & (egrp == gidx)
    el = jnp.where(emask, lg, NEG)
    v1 = jnp.max(el, axis=-1, keepdims=True)
    i1 = jnp.min(jnp.where(emask & (el == v1), lanef, big), axis=-1, keepdims=True)
    emask2 = emask & (lanef != i1)
    el2 = jnp.where(emask2, lg, NEG)
    v2 = jnp.max(el2, axis=-1, keepdims=True)
    i2 = jnp.min(jnp.where(emask2 & (el2 == v2), lanef, big), axis=-1, keepdims=True)
    t = jnp.exp(v2 - v1)
    den = 1.0 + t
    w1 = p_g * (1.0 / den)
    w2 = p_g * (t / den)
    route = jnp.where(lanei == 0, i1 - N_GROUPS,
                      jnp.where(lanei == 1, i2 - N_GROUPS,
                                jnp.where(lanei == 2, w1, jnp.where(lanei == 3, w2, 0.0))))
    route_ref[...] = route


def _post(oa, ob, ga, gb, x2, wa, wb, wo, g2, wrh, wrl, br):
    t = x2.shape[0]
    n = t // TM_POST
    row = lambda w: pl.BlockSpec((TM_POST, w), lambda i: (i, 0))
    full = lambda a: pl.BlockSpec(a.shape, lambda i: (0,) * a.ndim)
    return pl.pallas_call(
        _post_kernel,
        grid=(n,),
        in_specs=[row(512), row(512), row(D_MODEL), row(D_MODEL), row(D_MODEL),
                  full(wa), full(wb), full(wo), full(g2), full(wrh), full(wrl), full(br)],
        out_specs=[row(D_MODEL), row(D_MODEL // 2), row(LANES)],
        out_shape=[jax.ShapeDtypeStruct((t, D_MODEL), jnp.float32),
                   jax.ShapeDtypeStruct((t, D_MODEL // 2), jnp.uint32),
                   jax.ShapeDtypeStruct((t, LANES), jnp.float32)],
        compiler_params=_cparams(("arbitrary",)),
        name="post",
    )(oa, ob, ga, gb, x2, wa, wb, wo, g2, wrh, wrl, br)


def _rank_kernel(route_ref, ltri_ref, utri_ref, dest_ref, cnt_ref, cnt_sc, pstart_sc, base_sc):
    f32 = jnp.float32
    p = pl.program_id(0)
    i = pl.program_id(1)
    route = route_ref[...]
    lanef = lax.broadcasted_iota(jnp.int32, route.shape, 1).astype(f32)
    oh1 = (lanef == route[:, 0:1]).astype(f32)
    oh2 = (lanef == route[:, 1:2]).astype(f32)
    both = oh1 + oh2
    colsum = jnp.sum(both, axis=0, keepdims=True)

    @pl.when((p == 0) & (i == 0))
    def _():
        cnt_sc[...] = jnp.zeros(cnt_sc.shape, f32)

    @pl.when(p == 0)
    def _():
        cnt_sc[...] += colsum
        dest_ref[...] = jnp.zeros(dest_ref.shape, f32)
        cnt_ref[...] = jnp.zeros(cnt_ref.shape, f32)

    @pl.when((p == 1) & (i == 0))
    def _():
        cnt = cnt_sc[...]
        chi = jnp.floor(cnt * (1.0 / 256.0))
        clo = cnt - chi * 256.0
        split = jnp.concatenate([jnp.broadcast_to(chi, (8, LANES)), jnp.broadcast_to(clo, (8, LANES))], axis=0)
        excl = jnp.dot(split.astype(jnp.bfloat16), utri_ref[...], preferred_element_type=f32)
        pstart_sc[...] = excl[0:1] * 256.0 + excl[8:9]
        base_sc[...] = jnp.zeros(base_sc.shape, f32)

    @pl.when(p == 1)
    def _():
        prior = jnp.dot(ltri_ref[...], both.astype(jnp.bfloat16), preferred_element_type=f32)
        slot = prior + base_sc[...] + pstart_sc[...]
        d1 = jnp.sum(oh1 * slot, axis=-1, keepdims=True)
        d2 = jnp.sum(oh2 * slot, axis=-1, keepdims=True)
        dest_ref[...] = jnp.where(lanef == 0.0, d1, jnp.where(lanef == 1.0, d2, 0.0))
        base_sc[...] += colsum
        cnt_ref[...] = jnp.broadcast_to(cnt_sc[...], cnt_ref.shape)


def _rank(route, ltri, utri):
    t = route.shape[0]
    n = t // TM_RANK
    full = lambda a: pl.BlockSpec(a.shape, lambda p, i: (0,) * a.ndim)
    row1 = lambda: pltpu.VMEM((1, LANES), jnp.float32)
    return pl.pallas_call(
        _rank_kernel,
        grid=(2, n),
        in_specs=[pl.BlockSpec((TM_RANK, LANES), lambda p, i: (i, 0)), full(ltri), full(utri)],
        out_specs=[pl.BlockSpec((TM_RANK, LANES), lambda p, i: (i * p, 0)),
                   pl.BlockSpec((8, LANES), lambda p, i: (0, 0))],
        out_shape=[jax.ShapeDtypeStruct((t, LANES), jnp.float32),
                   jax.ShapeDtypeStruct((8, LANES), jnp.float32)],
        scratch_shapes=[row1(), row1(), row1()],
        compiler_params=_cparams(("arbitrary", "arbitrary")),
        name="rank",
    )(route, ltri, utri)


def _dispatch_kernel(dest_ref, hn_ref, xs_ref, sems):
    def row_copy(r, k):
        d = dest_ref[0, 0, 2 * r + k]
        return pltpu.make_async_copy(hn_ref.at[pl.ds(r, 1), :], xs_ref.at[pl.ds(d, 1), :], sems.at[k])

    def issue(r, c):
        row_copy(r, 0).start(priority=0)
        row_copy(r, 1).start(priority=1)
        return c

    def drain(r, c):
        row_copy(r, 0).wait()
        row_copy(r, 1).wait()
        return c

    lax.fori_loop(0, TM_ROWS, issue, 0, unroll=ROW_UNROLL)
    lax.fori_loop(0, TM_ROWS, drain, 0, unroll=ROW_UNROLL)


def _dispatch(dest3, hn, n_rows):
    t, w = hn.shape
    n = t // TM_ROWS
    return pl.pallas_call(
        _dispatch_kernel,
        grid=(n,),
        in_specs=[pl.BlockSpec((1, 1, 2 * TM_ROWS), lambda i: (i, 0, 0), memory_space=pltpu.SMEM),
                  pl.BlockSpec((TM_ROWS, w), lambda i: (i, 0))],
        out_specs=pl.BlockSpec(memory_space=pl.ANY),
        out_shape=jax.ShapeDtypeStruct((n_rows, w), hn.dtype),
        scratch_shapes=[pltpu.SemaphoreType.DMA((2,))],
        compiler_params=_cparams(("arbitrary",)),
        name="dispatch",
    )(dest3, hn)


def _ffn_kernel(tile_ref, exp_ref, lo_ref, hi_ref, cast_ref, init_ref,
                xs_ref, wg_ref, wu_ref, wd_ref, ys_ref, wg_sc, wu_sc, wd_sc):
    v = pl.program_id(0)
    lo = lo_ref[v]
    hi = hi_ref[v]

    @pl.when(init_ref[v] == 1)
    def _():
        ys_ref[...] = jnp.zeros(ys_ref.shape, ys_ref.dtype)

    @pl.when(hi > lo)
    def _():
        @pl.when(cast_ref[v] == 1)
        def _():
            wg_sc[...] = wg_ref[0].astype(jnp.bfloat16)
            wu_sc[...] = wu_ref[0].astype(jnp.bfloat16)
            wd_sc[...] = wd_ref[0].astype(jnp.bfloat16)

        pk = xs_ref[...]
        x = jnp.concatenate(
            [lax.bitcast_convert_type(pk & jnp.uint32(0xFFFF0000), jnp.float32),
             lax.bitcast_convert_type(lax.shift_left(pk, jnp.uint32(16)), jnp.float32)], axis=1).astype(jnp.bfloat16)
        g = jnp.dot(x, wg_sc[...], preferred_element_type=jnp.float32)
        u = jnp.dot(x, wu_sc[...], preferred_element_type=jnp.float32)
        hb = (g * jax.nn.sigmoid(g) * u).astype(jnp.bfloat16)
        y = jnp.dot(hb, wd_sc[...], preferred_element_type=jnp.float32)
        rows = tile_ref[v] * FFN_BLK + lax.broadcasted_iota(jnp.int32, y.shape, 0)
        ys_ref[...] = jnp.where((rows >= lo) & (rows < hi), y, ys_ref[...])


def _ffn(seg, xs, w_gate, w_up, w_down):
    n_rows = xs.shape[0]
    n_seg = seg[0].shape[0]
    grid_spec = pltpu.PrefetchScalarGridSpec(
        num_scalar_prefetch=6,
        grid=(n_seg,),
        in_specs=[
            pl.BlockSpec((FFN_BLK, D_MODEL // 2), lambda v, t, e, *_: (t[v], 0)),
            pl.BlockSpec((1, D_MODEL, D_EXPERT), lambda v, t, e, *_: (e[v], 0, 0)),
            pl.BlockSpec((1, D_MODEL, D_EXPERT), lambda v, t, e, *_: (e[v], 0, 0)),
            pl.BlockSpec((1, D_EXPERT, D_MODEL), lambda v, t, e, *_: (e[v], 0, 0)),
        ],
        out_specs=pl.BlockSpec((FFN_BLK, D_MODEL), lambda v, t, e, *_: (t[v], 0)),
        scratch_shapes=[pltpu.VMEM((D_MODEL, D_EXPERT), jnp.bfloat16),
                        pltpu.VMEM((D_MODEL, D_EXPERT), jnp.bfloat16),
                        pltpu.VMEM((D_EXPERT, D_MODEL), jnp.bfloat16)],
    )
    return pl.pallas_call(
        _ffn_kernel,
        grid_spec=grid_spec,
        out_shape=jax.ShapeDtypeStruct((n_rows, D_MODEL), jnp.float32),
        compiler_params=_cparams(("arbitrary",)),
        name="ffn",
    )(*seg, xs, w_gate, w_up, w_down)


def _segments(counts, n_rows):
    i32 = jnp.int32
    n_tiles = n_rows // FFN_BLK
    n_seg = n_tiles + N_EXPERTS
    tri = jnp.tril(jnp.ones((N_EXPERTS, N_EXPERTS), i32))
    ends = jnp.sum(tri * counts[None, :], axis=1)
    starts = ends - counts
    edges = jnp.arange(n_tiles, dtype=i32) * FFN_BLK
    rank_e = jnp.arange(n_tiles, dtype=i32) + jnp.sum(starts[None, :] <= edges[:, None], axis=1)
    rank_s = jnp.arange(N_EXPERTS, dtype=i32) + jnp.sum(edges[None, :] < starts[:, None], axis=1)
    seg = jnp.arange(n_seg, dtype=i32)
    lo = (jnp.sum(jnp.where(rank_e[None, :] == seg[:, None], edges[None, :], 0), axis=1)
          + jnp.sum(jnp.where(rank_s[None, :] == seg[:, None], starts[None, :], 0), axis=1))
    hi = jnp.concatenate([lo[1:], jnp.array([n_rows], i32)])
    valid = hi > lo
    tile = jnp.minimum(lo // FFN_BLK, n_tiles - 1)
    expert = jnp.minimum(jnp.sum(ends[None, :] <= lo[:, None], axis=1), N_EXPERTS - 1).astype(i32)
    upto = seg[None, :] <= seg[:, None]
    expert = jnp.max(jnp.where(upto & valid[None, :], expert[None, :], 0), axis=1)
    prev_expert = jnp.concatenate([jnp.array([-1], i32), expert[:-1]])
    first_valid = valid & (jnp.sum(jnp.where(upto & valid[None, :], 1, 0), axis=1) == 1)
    cast = valid & ((expert != prev_expert) | first_valid)
    prev_tile = jnp.concatenate([jnp.array([-1], i32), tile[:-1]])
    init = tile != prev_tile
    return (tile.astype(i32), expert, lo.astype(i32), hi.astype(i32), cast.astype(i32), init.astype(i32))


def _combine_kernel(dest_ref, route_ref, x1_ref, ys_ref, out_ref, y0_sc, y1_sc, sems):
    def row_copy(r, k):
        d = dest_ref[0, 0, 2 * r + k]
        dst = y0_sc if k == 0 else y1_sc
        return pltpu.make_async_copy(ys_ref.at[pl.ds(d, 1), :], dst.at[pl.ds(r, 1), :], sems.at[k])

    def issue(r, c):
        row_copy(r, 0).start(priority=0)
        row_copy(r, 1).start(priority=1)
        return c

    def drain(r, c):
        row_copy(r, 0).wait()
        row_copy(r, 1).wait()
        return c

    lax.fori_loop(0, TM_ROWS, issue, 0, unroll=ROW_UNROLL)
    lax.fori_loop(0, TM_ROWS, drain, 0, unroll=ROW_UNROLL)
    route = route_ref[...]
    out_ref[...] = x1_ref[...] + (route[:, 2:3] * y0_sc[...] + route[:, 3:4] * y1_sc[...])


def _combine(dest3, route, x1, ys):
    t = x1.shape[0]
    n = t // TM_ROWS
    return pl.pallas_call(
        _combine_kernel,
        grid=(n,),
        in_specs=[pl.BlockSpec((1, 1, 2 * TM_ROWS), lambda i: (i, 0, 0), memory_space=pltpu.SMEM),
                  pl.BlockSpec((TM_ROWS, LANES), lambda i: (i, 0)),
                  pl.BlockSpec((TM_ROWS, D_MODEL), lambda i: (i, 0)),
                  pl.BlockSpec(memory_space=pl.ANY)],
        out_specs=pl.BlockSpec((TM_ROWS, D_MODEL), lambda i: (i, 0)),
        out_shape=jax.ShapeDtypeStruct((t, D_MODEL), jnp.float32),
        scratch_shapes=[pltpu.VMEM((TM_ROWS, D_MODEL), jnp.float32),
                        pltpu.VMEM((TM_ROWS, D_MODEL), jnp.float32),
                        pltpu.SemaphoreType.DMA((2,))],
        compiler_params=_cparams(("arbitrary",)),
        name="combine",
    )(dest3, route, x1, ys)


def _t5_bucket(rel):
    nb = T5_BUCKETS // 2
    max_exact = nb // 2
    side = jnp.where(rel > 0, nb, 0)
    n = jnp.abs(rel)
    nf = jnp.maximum(n, 1).astype(jnp.float32)
    large = max_exact + (jnp.log(nf / max_exact) / math.log(T5_MAX_DIST / max_exact)
                         * (nb - max_exact)).astype(jnp.int32)
    large = jnp.minimum(large, nb - 1)
    return side + jnp.where(n < max_exact, n, large)


def _rel_offsets(j):
    i = jnp.arange(TQ + TK)
    return jnp.where(i < TK, i, i - (TQ + TK)) - j * TK


def _diff_bias_vecs(t5_table):
    vecs = jnp.stack([t5_table[_t5_bucket(_rel_offsets(j))].astype(jnp.float32).T for j in range(2)], axis=1)
    vecs = jnp.stack([vecs, vecs], axis=2)
    far = t5_table[_t5_bucket(jnp.array(-(TK + 1)))].astype(jnp.float32)
    return vecs, far


def _band_bias_vecs(rel_table):
    vecs = jnp.stack([rel_table[jnp.clip(_rel_offsets(j), -B_MAX_REL, B_MAX_REL) + B_MAX_REL].astype(jnp.float32).T
                      for j in range(2)], axis=1)
    return vecs.reshape(B_HEADS // 2, 2, 2, TQ + TK).transpose(0, 2, 1, 3)


def kernel(x, norm1_g, w_in, a_qnorm_g, a_knorm_g, a_lambda, a_subln_g, t5_table, b_qnorm_g, b_knorm_g,
           b_rel_table, w_branch_a, w_branch_b, w_out, norm2_g, w_router_group, b_router_group,
           w_router_expert, b_router_expert, w_gate, w_up, w_down):
    bsz, s_len, _ = x.shape
    n_tok = bsz * s_len
    f32, bf16 = jnp.float32, jnp.bfloat16
    assert s_len % TQ == 0 and TQ == TK and TQ % CHUNK == 0 and n_tok % TM_PROJ == 0
    assert TK >= T5_MAX_DIST and TK >= B_LEFT_CHUNKS * CHUNK
    l = 0
    x2 = x.reshape(n_tok, D_MODEL)

    w = w_in[l]
    qk = w[:, :1024].reshape(D_MODEL, 2, 2, A_HEADS, A_HEAD_DIM)
    qk = qk.transpose(0, 1, 3, 2, 4).reshape(D_MODEL, 1024)
    w_perm = jnp.concatenate([qk, w[:, 1024:]], axis=1).astype(bf16)
    scale = A_HEAD_DIM ** -0.5
    gn = jnp.stack([jnp.tile(a_qnorm_g[l] * scale, 8), jnp.tile(a_knorm_g[l], 8),
                    jnp.tile(b_qnorm_g[l] * (B_HEAD_DIM ** -0.5), 8), jnp.tile(b_knorm_g[l], 8)]).astype(f32)
    gmat = jnp.asarray(np.kron(np.eye(8), np.ones((64, 64))), dtype=bf16)

    qa, ka, va, qb, kb, vb, ga, gb = _proj(x2, norm1_g[l][None].astype(f32), w_perm, gn, gmat)

    lam_init = 0.8 - 0.6 * math.exp(-0.3 * l)
    lp = a_lambda[l].astype(f32)
    lam = jnp.exp(jnp.sum(lp[0] * lp[1])) - jnp.exp(jnp.sum(lp[2] * lp[3])) + lam_init
    bias_a, far_a = _diff_bias_vecs(t5_table)
    scal_a = jnp.concatenate([lam[None], far_a]).astype(f32)
    gsub = (a_subln_g[l] * (1.0 - lam_init))[None].astype(f32)
    oa = _attention(scal_a, qa, ka, va, bias_a, gsub, bsz=bsz, s_len=s_len, mode="diff")

    bias_b = _band_bias_vecs(b_rel_table[l])
    ob = _attention(jnp.zeros((1,), f32), qb, kb, vb, bias_b, gsub, bsz=bsz, s_len=s_len, mode="band")

    wr = jnp.zeros((D_MODEL, LANES), f32)
    wr = wr.at[:, :N_GROUPS].set(w_router_group[l]).at[:, N_GROUPS:N_GROUPS + N_EXPERTS].set(w_router_expert[l])
    wrh = wr.astype(bf16)
    wrl = (wr - wrh.astype(f32)).astype(bf16)
    br = jnp.zeros((1, LANES), f32)
    br = br.at[0, :N_GROUPS].set(b_router_group[l]).at[0, N_GROUPS:N_GROUPS + N_EXPERTS].set(b_router_expert[l])
    x1, hn, route = _post(oa, ob, ga, gb, x2, w_branch_a[l].astype(bf16), w_branch_b[l].astype(bf16),
                          w_out[l].astype(bf16), norm2_g[l][None].astype(f32), wrh, wrl, br)

    ltri = jnp.asarray(np.tril(np.ones((TM_RANK, TM_RANK)), -1), dtype=bf16)
    utri = jnp.asarray(np.triu(np.ones((LANES, LANES)), 1), dtype=bf16)
    dest, cnt = _rank(route, ltri, utri)

    counts = cnt[0, :N_EXPERTS].astype(jnp.int32)
    n_rows = n_tok * TOP_K
    seg = _segments(counts, n_rows)

    dest3 = dest[:, :TOP_K].astype(jnp.int32).reshape(n_tok // TM_ROWS, 1, TOP_K * TM_ROWS)
    xs = _dispatch(dest3, hn, n_rows)
    ys = _ffn(seg, xs, w_gate[l], w_up[l], w_down[l])
    out = _combine(dest3, route, x1, ys)
    return out.reshape(bsz, s_len, D_MODEL)
```

```python
import functools
import math

import jax
import jax.numpy as jnp
import numpy as np
from jax import lax
from jax.experimental import pallas as pl
from jax.experimental.pallas import tpu as pltpu

D_MODEL = 1024
CHUNK = 64
A_HEADS = 4
A_HEAD_DIM = 64
A_VDIM = 2 * A_HEAD_DIM
B_HEADS = 8
B_HEAD_DIM = 64
B_LEFT_CHUNKS = 8
B_MAX_REL = 128
T5_BUCKETS = 32
T5_MAX_DIST = 128
N_GROUPS = 4
EXPERTS_PER_GROUP = 8
N_EXPERTS = N_GROUPS * EXPERTS_PER_GROUP
TOP_K = 2
D_EXPERT = 512
EPS = 1e-6
NEG = -1e30
LOG2E = 1.0 / math.log(2.0)

CHUNK_SHIFT = CHUNK.bit_length() - 1
assert 1 << CHUNK_SHIFT == CHUNK
LANES = 128
A_W = A_HEADS * 2 * A_HEAD_DIM
B_W = B_HEADS * B_HEAD_DIM
PROJ_W = 4 * 256 + 4 * 512 + 2 * D_MODEL

TM_PROJ = 512
TQ = 512
TK = 512
TM_POST = 512
TM_RANK = 1024
TM_ROWS = 512
ROW_UNROLL = 8
FFN_BLK = 256
VMEM_LIMIT = 56 * 1024 * 1024


def _cparams(sem):
    return pltpu.CompilerParams(dimension_semantics=sem, vmem_limit_bytes=VMEM_LIMIT)


def _proj_kernel(x_ref, g1_ref, w_ref, gn_ref, gmat_ref,
                 qa_ref, ka_ref, va_ref, qb_ref, kb_ref, vb_ref, ga_ref, gb_ref):
    x = x_ref[...]
    xn = x * lax.rsqrt(jnp.mean(x * x, axis=-1, keepdims=True) + EPS) * g1_ref[...]
    xn = xn.astype(jnp.bfloat16)

    def slab(c0, width):
        return jnp.dot(xn, w_ref[:, c0:c0 + width], preferred_element_type=jnp.float32)

    def headnorm(y, gi):
        ss = jnp.dot((y * y).astype(jnp.bfloat16), gmat_ref[...],
                     preferred_element_type=jnp.float32)
        return y * lax.rsqrt(ss * (1.0 / A_HEAD_DIM) + EPS) * gn_ref[gi:gi + 1, :]

    qa_ref[...] = headnorm(slab(0, 512), 0).astype(jnp.bfloat16)
    ka_ref[...] = headnorm(slab(512, 512), 1).astype(jnp.bfloat16)
    va_ref[...] = slab(1024, 512).astype(jnp.bfloat16)
    qb_ref[...] = headnorm(slab(1536, 512), 2).astype(jnp.bfloat16)
    kb_ref[...] = headnorm(slab(2048, 512), 3).astype(jnp.bfloat16)
    vb_ref[...] = slab(2560, 512).astype(jnp.bfloat16)
    for j in range(2):
        ga_ref[:, j * 512:(j + 1) * 512] = jax.nn.sigmoid(slab(3072 + j * 512, 512)).astype(jnp.bfloat16)
        gb_ref[:, j * 512:(j + 1) * 512] = jax.nn.sigmoid(slab(4096 + j * 512, 512)).astype(jnp.bfloat16)


def _proj(x2, g1, w_perm, gn, gmat):
    t = x2.shape[0]
    n = t // TM_PROJ
    row = lambda w: pl.BlockSpec((TM_PROJ, w), lambda i: (i, 0))
    full = lambda a: pl.BlockSpec(a.shape, lambda i: (0,) * a.ndim)
    outs = [jax.ShapeDtypeStruct((t, 512), jnp.bfloat16)] * 6 + [jax.ShapeDtypeStruct((t, D_MODEL), jnp.bfloat16)] * 2
    return pl.pallas_call(
        _proj_kernel,
        grid=(n,),
        in_specs=[row(D_MODEL), full(g1), full(w_perm), full(gn), full(gmat)],
        out_specs=[row(512)] * 6 + [row(D_MODEL)] * 2,
        out_shape=outs,
        compiler_params=_cparams(("arbitrary",)),
        name="proj",
    )(x2, g1, w_perm, gn, gmat)


def _attn_kernel(scal_ref, q_ref, k_ref, v_ref, vec_ref, gsub_ref, o_ref,
                 bias_sc, qz_sc, m_sc, acc_sc, mf_sc, accf_sc, *, mode, n_near):
    h = pl.program_id(1)
    qi = pl.program_id(2)
    nc = TK // LANES

    @pl.when(qi == 0)
    def _():
        qchunk = lax.shift_right_arithmetic(lax.broadcasted_iota(jnp.int32, (TQ, TK), 0), CHUNK_SHIFT)
        kcol = lax.broadcasted_iota(jnp.int32, (TQ, TK), 1)
        for j in range(n_near):
            dchunk = qchunk - lax.shift_right_arithmetic(kcol - j * TK, CHUNK_SHIFT)
            allowed = (dchunk >= 0) & (dchunk <= B_LEFT_CHUNKS) if mode == "band" else dchunk >= 0
            for half in range(2):
                vec = jnp.broadcast_to(vec_ref[0, j, half:half + 1, :], (TQ, TQ + TK))
                tile = pltpu.roll(vec, 0, 1, stride=1, stride_axis=0)[:, :TK]
                bias_sc[j, half * TQ:(half + 1) * TQ, :] = jnp.where(allowed, tile, NEG)

    q = q_ref[...]
    lane = lax.broadcasted_iota(jnp.int32, q.shape, 1)
    zero = jnp.zeros_like(q)
    qz_sc[0:TQ, :] = jnp.where(lane < 64, q, zero)
    qz_sc[TQ:2 * TQ, :] = jnp.where(lane >= 64, q, zero)

    def scores(kblk):
        k = k_ref[pl.ds(pl.multiple_of(kblk * TK, TK), TK), :]
        return lax.dot_general(qz_sc[...], k, (((1,), (1,)), ((), ())), preferred_element_type=jnp.float32)

    def update(s, kblk, m_ref, a_ref):
        cols = [s[:, c * LANES:(c + 1) * LANES] for c in range(nc)]
        m_old = m_ref[...]
        m_new = jnp.maximum(m_old, jnp.max(functools.reduce(jnp.maximum, cols), axis=-1, keepdims=True))
        alpha = jnp.exp2(m_old - m_new)
        ps = [jnp.exp2(c - m_new) for c in cols]
        p = jnp.concatenate([x.astype(jnp.bfloat16) for x in ps], axis=1)
        v = v_ref[pl.ds(pl.multiple_of(kblk * TK, TK), TK), :]
        a_ref[:, 0:LANES] = alpha * a_ref[:, 0:LANES] + jnp.dot(p, v, preferred_element_type=jnp.float32)
        a_ref[:, LANES:2 * LANES] = alpha * a_ref[:, LANES:2 * LANES] + functools.reduce(jnp.add, ps)
        m_ref[...] = m_new

    m_sc[...] = jnp.full(m_sc.shape, NEG, jnp.float32)
    acc_sc[...] = jnp.zeros(acc_sc.shape, jnp.float32)

    for j in range(n_near):
        def near(j=j):
            update(scores(qi - j) + bias_sc[j], qi - j, m_sc, acc_sc)
        if j == 0:
            near()
        else:
            pl.when(qi >= j)(near)

    if mode == "diff":
        mf_sc[...] = jnp.full(mf_sc.shape, NEG, jnp.float32)
        accf_sc[...] = jnp.zeros(accf_sc.shape, jnp.float32)

        def far(kblk, carry):
            update(scores(kblk), kblk, mf_sc, accf_sc)
            return carry

        n_far = jnp.maximum(qi - (n_near - 1), 0)

        def far_pair(i, carry):
            far(2 * i, carry)
            return far(2 * i + 1, carry)

        lax.fori_loop(0, n_far // 2, far_pair, 0)
        lax.fori_loop((n_far // 2) * 2, n_far, far, 0)

        cfar = scal_ref[1 + h]
        mf = mf_sc[...] + cfar
        mn = m_sc[...]
        m = jnp.maximum(mf, mn)
        wf = jnp.exp2(mf - m)
        wn = jnp.exp2(mn - m)
        tot = (jnp.concatenate([wf, wf], axis=1) * accf_sc[...]
               + jnp.concatenate([wn, wn], axis=1) * acc_sc[...])
        o = tot[:, 0:LANES] / jnp.sum(tot[:, LANES:2 * LANES], axis=-1, keepdims=True)
        lam = scal_ref[0]
        od = o[:TQ] - lam * o[TQ:]
        od = od * lax.rsqrt(jnp.mean(od * od, axis=-1, keepdims=True) + EPS) * gsub_ref[...]
        o_ref[...] = od.astype(o_ref.dtype)
    else:
        acc = acc_sc[...]
        o = acc[:, 0:LANES] / jnp.sum(acc[:, LANES:2 * LANES], axis=-1, keepdims=True)
        o_ref[...] = jnp.where(lane < 64, o[:TQ], o[TQ:]).astype(o_ref.dtype)


def _attention(scal, q, k, v, vecs, gsub, *, bsz, s_len, mode):
    n_blk = q.shape[1] // LANES
    nq = s_len // TQ
    n_near = vecs.shape[1]
    assert vecs.shape == (n_blk, n_near, 2, TQ + TK)
    kern = functools.partial(_attn_kernel, mode=mode, n_near=n_near)
    stat = lambda: pltpu.VMEM((2 * TQ, LANES), jnp.float32)
    acc = lambda: pltpu.VMEM((2 * TQ, 2 * LANES), jnp.float32)
    return pl.pallas_call(
        kern,
        grid=(bsz, n_blk, nq),
        in_specs=[
            pl.BlockSpec(memory_space=pltpu.SMEM),
            pl.BlockSpec((TQ, LANES), lambda b, h, i: (b * nq + i, h)),
            pl.BlockSpec((s_len, LANES), lambda b, h, i: (b, h)),
            pl.BlockSpec((s_len, LANES), lambda b, h, i: (b, h)),
            pl.BlockSpec((1, n_near, 2, TQ + TK), lambda b, h, i: (h, 0, 0, 0)),
            pl.BlockSpec((1, LANES), lambda b, h, i: (0, 0)),
        ],
        out_specs=pl.BlockSpec((TQ, LANES), lambda b, h, i: (b * nq + i, h)),
        out_shape=jax.ShapeDtypeStruct(q.shape, jnp.bfloat16),
        scratch_shapes=[pltpu.VMEM((n_near, 2 * TQ, TK), jnp.float32),
                        pltpu.VMEM((2 * TQ, LANES), jnp.bfloat16), stat(), acc(), stat(), acc()],
        compiler_params=_cparams(("arbitrary", "arbitrary", "arbitrary")),
        name="attn_" + mode,
    )(scal, q, k, v, vecs, gsub)


def _post_kernel(oa_ref, ob_ref, ga_ref, gb_ref, x_ref, wa_ref, wb_ref, wo_ref, g2_ref,
                 wr2_ref, br_ref, x1_ref, hn_ref, route_ref):
    f32 = jnp.float32
    ya = jnp.dot(oa_ref[...], wa_ref[...], preferred_element_type=f32)
    yb = jnp.dot(ob_ref[...], wb_ref[...], preferred_element_type=f32)
    mixed = ga_ref[...].astype(f32) * ya + gb_ref[...].astype(f32) * yb
    x1 = x_ref[...] + jnp.dot(mixed.astype(jnp.bfloat16), wo_ref[...], preferred_element_type=f32)
    x1_ref[...] = x1
    hn = x1 * lax.rsqrt(jnp.mean(x1 * x1, axis=-1, keepdims=True) + EPS) * g2_ref[...]
    hh = hn.astype(jnp.bfloat16)
    bits = lax.bitcast_convert_type(hh.astype(f32), jnp.uint32)
    half = D_MODEL // 2
    hn_ref[...] = (bits[:, :half] & jnp.uint32(0xFFFF0000)) | lax.shift_right_logical(bits[:, half:], jnp.uint32(16))

    hl = (hn - hh.astype(f32)).astype(jnp.bfloat16)
    hw = jnp.dot(hh, wr2_ref[...], preferred_element_type=f32)
    lg = (hw[:, 0:LANES] + hw[:, LANES:2 * LANES]
          + jnp.dot(hl, wr2_ref[:, 0:LANES], preferred_element_type=f32)) + br_ref[...]

    lanei = lax.broadcasted_iota(jnp.int32, lg.shape, 1)
    lanef = lanei.astype(f32)
    big = 999.0
    gmask = lanei < N_GROUPS
    gl = jnp.where(gmask, lg, NEG)
    gm = jnp.max(gl, axis=-1, keepdims=True)
    ge = jnp.where(gmask, jnp.exp(gl - gm), 0.0)
    gp = ge / jnp.sum(ge, axis=-1, keepdims=True)
    p_g = jnp.max(gp, axis=-1, keepdims=True)
    gidx = jnp.min(jnp.where(gmask & (gp == p_g), lanef, big), axis=-1, keepdims=True)
    egrp = lax.shift_right_arithmetic(lanei - N_GROUPS, 3).astype(f32)
    emask = (lanei >= N_GROUPS) & (lanei < N_GROUPS + N_EXPERTS) & (egrp == gidx)
    el = jnp.where(emask, lg, NEG)
    v1 = jnp.max(el, axis=-1, keepdims=True)
    i1 = jnp.min(jnp.where(emask & (el == v1), lanef, big), axis=-1, keepdims=True)
    emask2 = emask & (lanef != i1)
    el2 = jnp.where(emask2, lg, NEG)
    v2 = jnp.max(el2, axis=-1, keepdims=True)
    i2 = jnp.min(jnp.where(emask2 & (el2 == v2), lanef, big), axis=-1, keepdims=True)
    t = jnp.exp(v2 - v1)
    den = 1.0 + t
    w1 = p_g * (1.0 / den)
    w2 = p_g * (t / den)
    route = jnp.where(lanei == 0, i1 - N_GROUPS,
                      jnp.where(lanei == 1, i2 - N_GROUPS,
                                jnp.where(lanei == 2, w1, jnp.where(lanei == 3, w2, 0.0))))
    route_ref[...] = route


def _post(oa, ob, ga, gb, x2, wa, wb, wo, g2, wr2, br):
    t = x2.shape[0]
    n = t // TM_POST
    row = lambda w: pl.BlockSpec((TM_POST, w), lambda i: (i, 0))
    full = lambda a: pl.BlockSpec(a.shape, lambda i: (0,) * a.ndim)
    return pl.pallas_call(
        _post_kernel,
        grid=(n,),
        in_specs=[row(512), row(512), row(D_MODEL), row(D_MODEL), row(D_MODEL),
                  full(wa), full(wb), full(wo), full(g2), full(wr2), full(br)],
        out_specs=[row(D_MODEL), row(D_MODEL // 2), row(LANES)],
        out_shape=[jax.ShapeDtypeStruct((t, D_MODEL), jnp.float32),
                   jax.ShapeDtypeStruct((t, D_MODEL // 2), jnp.uint32),
                   jax.ShapeDtypeStruct((t, LANES), jnp.float32)],
        compiler_params=_cparams(("arbitrary",)),
        name="post",
    )(oa, ob, ga, gb, x2, wa, wb, wo, g2, wr2, br)


def _rank_kernel(route_ref, ltri_ref, utri_ref, dest_ref, cnt_ref, cnt_sc, pstart_sc, base_sc):
    f32 = jnp.float32
    p = pl.program_id(0)
    i = pl.program_id(1)
    route = route_ref[...]
    lanef = lax.broadcasted_iota(jnp.int32, route.shape, 1).astype(f32)
    oh1 = (lanef == route[:, 0:1]).astype(f32)
    oh2 = (lanef == route[:, 1:2]).astype(f32)
    both = oh1 + oh2
    colsum = jnp.sum(both, axis=0, keepdims=True)

    @pl.when((p == 0) & (i == 0))
    def _():
        cnt_sc[...] = jnp.zeros(cnt_sc.shape, f32)

    @pl.when(p == 0)
    def _():
        cnt_sc[...] += colsum
        dest_ref[...] = jnp.zeros(dest_ref.shape, f32)
        cnt_ref[...] = jnp.zeros(cnt_ref.shape, f32)

    @pl.when((p == 1) & (i == 0))
    def _():
        cnt = cnt_sc[...]
        chi = jnp.floor(cnt * (1.0 / 256.0))
        clo = cnt - chi * 256.0
        split = jnp.concatenate([jnp.broadcast_to(chi, (8, LANES)), jnp.broadcast_to(clo, (8, LANES))], axis=0)
        excl = jnp.dot(split.astype(jnp.bfloat16), utri_ref[...], preferred_element_type=f32)
        pstart_sc[...] = excl[0:1] * 256.0 + excl[8:9]
        base_sc[...] = jnp.zeros(base_sc.shape, f32)

    @pl.when(p == 1)
    def _():
        prior = jnp.dot(ltri_ref[...], both.astype(jnp.bfloat16), preferred_element_type=f32)
        slot = prior + base_sc[...] + pstart_sc[...]
        d1 = jnp.sum(oh1 * slot, axis=-1, keepdims=True)
        d2 = jnp.sum(oh2 * slot, axis=-1, keepdims=True)
        dest_ref[...] = jnp.where(lanef == 0.0, d1, jnp.where(lanef == 1.0, d2, 0.0))
        base_sc[...] += colsum
        cnt_ref[...] = jnp.broadcast_to(cnt_sc[...], cnt_ref.shape)


def _rank(route, ltri, utri):
    t = route.shape[0]
    n = t // TM_RANK
    full = lambda a: pl.BlockSpec(a.shape, lambda p, i: (0,) * a.ndim)
    row1 = lambda: pltpu.VMEM((1, LANES), jnp.float32)
    return pl.pallas_call(
        _rank_kernel,
        grid=(2, n),
        in_specs=[pl.BlockSpec((TM_RANK, LANES), lambda p, i: (i, 0)), full(ltri), full(utri)],
        out_specs=[pl.BlockSpec((TM_RANK, LANES), lambda p, i: (i * p, 0)),
                   pl.BlockSpec((8, LANES), lambda p, i: (0, 0))],
        out_shape=[jax.ShapeDtypeStruct((t, LANES), jnp.float32),
                   jax.ShapeDtypeStruct((8, LANES), jnp.float32)],
        scratch_shapes=[row1(), row1(), row1()],
        compiler_params=_cparams(("arbitrary", "arbitrary")),
        name="rank",
    )(route, ltri, utri)


def _dispatch_kernel(dest_ref, hn_ref, xs_ref, sems):
    def row_copy(r, k):
        d = dest_ref[0, 0, 2 * r + k]
        return pltpu.make_async_copy(hn_ref.at[pl.ds(r, 1), :], xs_ref.at[pl.ds(d, 1), :], sems.at[k])

    def issue(r, c):
        row_copy(r, 0).start(priority=0)
        row_copy(r, 1).start(priority=1)
        return c

    lax.fori_loop(0, TM_ROWS, issue, 0, unroll=ROW_UNROLL)
    for k in range(TOP_K):
        pltpu.make_async_copy(hn_ref, xs_ref.at[pl.ds(0, TM_ROWS), :], sems.at[k]).wait()


def _dispatch(dest3, hn, n_rows):
    t, w = hn.shape
    n = t // TM_ROWS
    return pl.pallas_call(
        _dispatch_kernel,
        grid=(n,),
        in_specs=[pl.BlockSpec((1, 1, 2 * TM_ROWS), lambda i: (i, 0, 0), memory_space=pltpu.SMEM),
                  pl.BlockSpec((TM_ROWS, w), lambda i: (i, 0))],
        out_specs=pl.BlockSpec(memory_space=pl.ANY),
        out_shape=jax.ShapeDtypeStruct((n_rows, w), hn.dtype),
        scratch_shapes=[pltpu.SemaphoreType.DMA((2,))],
        compiler_params=_cparams(("arbitrary",)),
        name="dispatch",
    )(dest3, hn)


def _ffn_kernel(tile_ref, exp_ref, lo_ref, hi_ref, cast_ref, init_ref,
                xs_ref, wg_ref, wu_ref, wd_ref, ys_ref, wg_sc, wu_sc, wd_sc):
    v = pl.program_id(0)
    lo = lo_ref[v]
    hi = hi_ref[v]

    @pl.when(init_ref[v] == 1)
    def _():
        ys_ref[...] = jnp.zeros(ys_ref.shape, ys_ref.dtype)

    @pl.when(hi > lo)
    def _():
        @pl.when(cast_ref[v] == 1)
        def _():
            wg_sc[...] = wg_ref[0].astype(jnp.bfloat16)
            wu_sc[...] = wu_ref[0].astype(jnp.bfloat16)
            wd_sc[...] = wd_ref[0].astype(jnp.bfloat16)

        pk = xs_ref[...]
        x = jnp.concatenate(
            [lax.bitcast_convert_type(pk & jnp.uint32(0xFFFF0000), jnp.float32),
             lax.bitcast_convert_type(lax.shift_left(pk, jnp.uint32(16)), jnp.float32)], axis=1).astype(jnp.bfloat16)
        g = jnp.dot(x, wg_sc[...], preferred_element_type=jnp.float32)
        u = jnp.dot(x, wu_sc[...], preferred_element_type=jnp.float32)
        hb = (g * jax.nn.sigmoid(g) * u).astype(jnp.bfloat16)
        y = jnp.dot(hb, wd_sc[...], preferred_element_type=jnp.float32)
        rows = tile_ref[v] * FFN_BLK + lax.broadcasted_iota(jnp.int32, y.shape, 0)
        ys_ref[...] = jnp.where((rows >= lo) & (rows < hi), y, ys_ref[...])


def _ffn(seg, xs, w_gate, w_up, w_down):
    n_rows = xs.shape[0]
    n_seg = seg[0].shape[0]
    grid_spec = pltpu.PrefetchScalarGridSpec(
        num_scalar_prefetch=6,
        grid=(n_seg,),
        in_specs=[
            pl.BlockSpec((FFN_BLK, D_MODEL // 2), lambda v, t, e, *_: (t[v], 0)),
            pl.BlockSpec((1, D_MODEL, D_EXPERT), lambda v, t, e, *_: (e[v], 0, 0)),
            pl.BlockSpec((1, D_MODEL, D_EXPERT), lambda v, t, e, *_: (e[v], 0, 0)),
            pl.BlockSpec((1, D_EXPERT, D_MODEL), lambda v, t, e, *_: (e[v], 0, 0)),
        ],
        out_specs=pl.BlockSpec((FFN_BLK, D_MODEL), lambda v, t, e, *_: (t[v], 0)),
        scratch_shapes=[pltpu.VMEM((D_MODEL, D_EXPERT), jnp.bfloat16),
                        pltpu.VMEM((D_MODEL, D_EXPERT), jnp.bfloat16),
                        pltpu.VMEM((D_EXPERT, D_MODEL), jnp.bfloat16)],
    )
    return pl.pallas_call(
        _ffn_kernel,
        grid_spec=grid_spec,
        out_shape=jax.ShapeDtypeStruct((n_rows, D_MODEL), jnp.float32),
        compiler_params=_cparams(("arbitrary",)),
        name="ffn",
    )(*seg, xs, w_gate, w_up, w_down)


def _segments(counts, n_rows):
    i32 = jnp.int32
    n_tiles = n_rows // FFN_BLK
    n_seg = n_tiles + N_EXPERTS
    tri = jnp.tril(jnp.ones((N_EXPERTS, N_EXPERTS), i32))
    ends = jnp.sum(tri * counts[None, :], axis=1)
    starts = ends - counts
    edges = jnp.arange(n_tiles, dtype=i32) * FFN_BLK
    rank_e = jnp.arange(n_tiles, dtype=i32) + jnp.sum(starts[None, :] <= edges[:, None], axis=1)
    rank_s = jnp.arange(N_EXPERTS, dtype=i32) + jnp.sum(edges[None, :] < starts[:, None], axis=1)
    seg = jnp.arange(n_seg, dtype=i32)
    lo = (jnp.sum(jnp.where(rank_e[None, :] == seg[:, None], edges[None, :], 0), axis=1)
          + jnp.sum(jnp.where(rank_s[None, :] == seg[:, None], starts[None, :], 0), axis=1))
    hi = jnp.concatenate([lo[1:], jnp.array([n_rows], i32)])
    valid = hi > lo
    tile = jnp.minimum(lo // FFN_BLK, n_tiles - 1)
    expert = jnp.minimum(jnp.sum(ends[None, :] <= lo[:, None], axis=1), N_EXPERTS - 1).astype(i32)
    upto = seg[None, :] <= seg[:, None]
    expert = jnp.max(jnp.where(upto & valid[None, :], expert[None, :], 0), axis=1)
    prev_expert = jnp.concatenate([jnp.array([-1], i32), expert[:-1]])
    first_valid = valid & (jnp.sum(jnp.where(upto & valid[None, :], 1, 0), axis=1) == 1)
    cast = valid & ((expert != prev_expert) | first_valid)
    prev_tile = jnp.concatenate([jnp.array([-1], i32), tile[:-1]])
    init = tile != prev_tile
    return (tile.astype(i32), expert, lo.astype(i32), hi.astype(i32), cast.astype(i32), init.astype(i32))


def _combine_kernel(dest_ref, route_ref, x1_ref, ys_ref, out_ref, y0_sc, y1_sc, sems):
    def row_copy(r, k):
        d = dest_ref[0, 0, 2 * r + k]
        dst = y0_sc if k == 0 else y1_sc
        return pltpu.make_async_copy(ys_ref.at[pl.ds(d, 1), :], dst.at[pl.ds(r, 1), :], sems.at[k])

    def issue(r, c):
        row_copy(r, 0).start(priority=0)
        row_copy(r, 1).start(priority=1)
        return c

    lax.fori_loop(0, TM_ROWS, issue, 0, unroll=ROW_UNROLL)
    pltpu.make_async_copy(ys_ref.at[pl.ds(0, TM_ROWS), :], y0_sc, sems.at[0]).wait()
    pltpu.make_async_copy(ys_ref.at[pl.ds(0, TM_ROWS), :], y1_sc, sems.at[1]).wait()
    route = route_ref[...]
    out_ref[...] = x1_ref[...] + (route[:, 2:3] * y0_sc[...] + route[:, 3:4] * y1_sc[...])


def _combine(dest3, route, x1, ys):
    t = x1.shape[0]
    n = t // TM_ROWS
    return pl.pallas_call(
        _combine_kernel,
        grid=(n,),
        in_specs=[pl.BlockSpec((1, 1, 2 * TM_ROWS), lambda i: (i, 0, 0), memory_space=pltpu.SMEM),
                  pl.BlockSpec((TM_ROWS, LANES), lambda i: (i, 0)),
                  pl.BlockSpec((TM_ROWS, D_MODEL), lambda i: (i, 0)),
                  pl.BlockSpec(memory_space=pl.ANY)],
        out_specs=pl.BlockSpec((TM_ROWS, D_MODEL), lambda i: (i, 0)),
        out_shape=jax.ShapeDtypeStruct((t, D_MODEL), jnp.float32),
        scratch_shapes=[pltpu.VMEM((TM_ROWS, D_MODEL), jnp.float32),
                        pltpu.VMEM((TM_ROWS, D_MODEL), jnp.float32),
                        pltpu.SemaphoreType.DMA((2,))],
        compiler_params=_cparams(("arbitrary",)),
        name="combine",
    )(dest3, route, x1, ys)


def _t5_bucket(rel):
    nb = T5_BUCKETS // 2
    max_exact = nb // 2
    side = jnp.where(rel > 0, nb, 0)
    n = jnp.abs(rel)
    nf = jnp.maximum(n, 1).astype(jnp.float32)
    large = max_exact + (jnp.log(nf / max_exact) / math.log(T5_MAX_DIST / max_exact)
                         * (nb - max_exact)).astype(jnp.int32)
    large = jnp.minimum(large, nb - 1)
    return side + jnp.where(n < max_exact, n, large)


def _rel_offsets(j):
    i = jnp.arange(TQ + TK)
    return jnp.where(i < TK, i, i - (TQ + TK)) - j * TK


def _diff_bias_vecs(t5_table):
    vecs = jnp.stack([t5_table[_t5_bucket(_rel_offsets(j))].astype(jnp.float32).T for j in range(2)], axis=1)
    vecs = jnp.stack([vecs, vecs], axis=2)
    far = t5_table[_t5_bucket(jnp.array(-(TK + 1)))].astype(jnp.float32)
    return vecs, far


def _band_bias_vecs(rel_table):
    vecs = jnp.stack([rel_table[jnp.clip(_rel_offsets(j), -B_MAX_REL, B_MAX_REL) + B_MAX_REL].astype(jnp.float32).T
                      for j in range(2)], axis=1)
    return vecs.reshape(B_HEADS // 2, 2, 2, TQ + TK).transpose(0, 2, 1, 3)


def kernel(x, norm1_g, w_in, a_qnorm_g, a_knorm_g, a_lambda, a_subln_g, t5_table, b_qnorm_g, b_knorm_g,
           b_rel_table, w_branch_a, w_branch_b, w_out, norm2_g, w_router_group, b_router_group,
           w_router_expert, b_router_expert, w_gate, w_up, w_down):
    bsz, s_len, _ = x.shape
    n_tok = bsz * s_len
    f32, bf16 = jnp.float32, jnp.bfloat16
    assert s_len % TQ == 0 and TQ == TK and TQ % CHUNK == 0 and n_tok % TM_PROJ == 0
    assert TK >= T5_MAX_DIST and TK >= B_LEFT_CHUNKS * CHUNK
    l = 0
    x2 = x.reshape(n_tok, D_MODEL)

    w = w_in[l]
    qk = w[:, :1024].reshape(D_MODEL, 2, 2, A_HEADS, A_HEAD_DIM)
    qk = qk.transpose(0, 1, 3, 2, 4).reshape(D_MODEL, 1024)
    w_perm = jnp.concatenate([qk, w[:, 1024:]], axis=1).astype(bf16)
    gn = jnp.stack([jnp.tile(a_qnorm_g[l] * (A_HEAD_DIM ** -0.5 * LOG2E), 8), jnp.tile(a_knorm_g[l], 8),
                    jnp.tile(b_qnorm_g[l] * (B_HEAD_DIM ** -0.5 * LOG2E), 8), jnp.tile(b_knorm_g[l], 8)]).astype(f32)
    gmat = jnp.asarray(np.kron(np.eye(8), np.ones((64, 64))), dtype=bf16)

    qa, ka, va, qb, kb, vb, ga, gb = _proj(x2, norm1_g[l][None].astype(f32), w_perm, gn, gmat)

    lam_init = 0.8 - 0.6 * math.exp(-0.3 * l)
    lp = a_lambda[l].astype(f32)
    lam = jnp.exp(jnp.sum(lp[0] * lp[1])) - jnp.exp(jnp.sum(lp[2] * lp[3])) + lam_init
    bias_a, far_a = _diff_bias_vecs(t5_table)
    scal_a = jnp.concatenate([lam[None], far_a * LOG2E]).astype(f32)
    gsub = (a_subln_g[l] * (1.0 - lam_init))[None].astype(f32)
    oa = _attention(scal_a, qa, ka, va, bias_a * LOG2E, gsub, bsz=bsz, s_len=s_len, mode="diff")

    bias_b = _band_bias_vecs(b_rel_table[l])
    ob = _attention(jnp.zeros((1,), f32), qb, kb, vb, bias_b * LOG2E, gsub, bsz=bsz, s_len=s_len, mode="band")

    wr = jnp.zeros((D_MODEL, LANES), f32)
    wr = wr.at[:, :N_GROUPS].set(w_router_group[l]).at[:, N_GROUPS:N_GROUPS + N_EXPERTS].set(w_router_expert[l])
    wrh = wr.astype(bf16)
    wr2 = jnp.concatenate([wrh, (wr - wrh.astype(f32)).astype(bf16)], axis=1)
    br = jnp.zeros((1, LANES), f32)
    br = br.at[0, :N_GROUPS].set(b_router_group[l]).at[0, N_GROUPS:N_GROUPS + N_EXPERTS].set(b_router_expert[l])
    x1, hn, route = _post(oa, ob, ga, gb, x2, w_branch_a[l].astype(bf16), w_branch_b[l].astype(bf16),
                          w_out[l].astype(bf16), norm2_g[l][None].astype(f32), wr2, br)

    ltri = jnp.asarray(np.tril(np.ones((TM_RANK, TM_RANK)), -1), dtype=bf16)
    utri = jnp.asarray(np.triu(np.ones((LANES, LANES)), 1), dtype=bf16)
    dest, cnt = _rank(route, ltri, utri)

    counts = cnt[0, :N_EXPERTS].astype(jnp.int32)
    n_rows = n_tok * TOP_K
    seg = _segments(counts, n_rows)

    dest3 = dest[:, :TOP_K].astype(jnp.int32).reshape(n_tok // TM_ROWS, 1, TOP_K * TM_ROWS)
    xs = _dispatch(dest3, hn, n_rows)
    ys = _ffn(seg, xs, w_gate[l], w_up[l], w_down[l])
    out = _combine(dest3, route, x1, ys)
    return out.reshape(bsz, s_len, D_MODEL)
```

```python
import functools
import math

import jax
import jax.numpy as jnp
import numpy as np
from jax import lax
from jax.experimental import pallas as pl
from jax.experimental.pallas import tpu as pltpu

D_MODEL = 1024
CHUNK = 64
A_HEADS = 4
A_HEAD_DIM = 64
A_VDIM = 2 * A_HEAD_DIM
B_HEADS = 8
B_HEAD_DIM = 64
B_LEFT_CHUNKS = 8
B_MAX_REL = 128
T5_BUCKETS = 32
T5_MAX_DIST = 128
N_GROUPS = 4
EXPERTS_PER_GROUP = 8
N_EXPERTS = N_GROUPS * EXPERTS_PER_GROUP
TOP_K = 2
D_EXPERT = 512
EPS = 1e-6
NEG = -1e30
LOG2E = 1.0 / math.log(2.0)
DENOM_FLOOR = 2.0 ** -100

CHUNK_SHIFT = CHUNK.bit_length() - 1
assert 1 << CHUNK_SHIFT == CHUNK
LANES = 128
A_W = A_HEADS * 2 * A_HEAD_DIM
B_W = B_HEADS * B_HEAD_DIM
PROJ_W = 4 * 256 + 4 * 512 + 2 * D_MODEL

TM_PROJ = 512
TQ = 512
TK = 512
FAR_UNROLL = 4
TM_POST = 512
TM_RANK = 1024
TM_ROWS = 512
ROW_UNROLL = 8
FFN_BLK = 256
VMEM_LIMIT = 56 * 1024 * 1024


def _cparams(sem):
    return pltpu.CompilerParams(dimension_semantics=sem, vmem_limit_bytes=VMEM_LIMIT)


def _proj_kernel(x_ref, g1_ref, w_ref, gn_ref, gmat_ref,
                 qa_ref, ka_ref, va_ref, qb_ref, kb_ref, vb_ref, ga_ref, gb_ref):
    x = x_ref[...]
    xn = x * lax.rsqrt(jnp.mean(x * x, axis=-1, keepdims=True) + EPS) * g1_ref[...]
    xn = xn.astype(jnp.bfloat16)

    def slab(c0, width):
        return jnp.dot(xn, w_ref[:, c0:c0 + width], preferred_element_type=jnp.float32)

    def headnorm(y, gi):
        ss = jnp.dot((y * y).astype(jnp.bfloat16), gmat_ref[...],
                     preferred_element_type=jnp.float32)
        return y * lax.rsqrt(ss * (1.0 / A_HEAD_DIM) + EPS) * gn_ref[gi:gi + 1, :]

    qa_ref[...] = headnorm(slab(0, 512), 0).astype(jnp.bfloat16)
    ka_ref[...] = headnorm(slab(512, 512), 1).astype(jnp.bfloat16)
    va_ref[...] = slab(1024, 512).astype(jnp.bfloat16)
    qb_ref[...] = headnorm(slab(1536, 512), 2).astype(jnp.bfloat16)
    kb_ref[...] = headnorm(slab(2048, 512), 3).astype(jnp.bfloat16)
    vb_ref[...] = slab(2560, 512).astype(jnp.bfloat16)
    for j in range(2):
        ga_ref[:, j * 512:(j + 1) * 512] = jax.nn.sigmoid(slab(3072 + j * 512, 512)).astype(jnp.bfloat16)
        gb_ref[:, j * 512:(j + 1) * 512] = jax.nn.sigmoid(slab(4096 + j * 512, 512)).astype(jnp.bfloat16)


def _proj(x2, g1, w_perm, gn, gmat):
    t = x2.shape[0]
    n = t // TM_PROJ
    row = lambda w: pl.BlockSpec((TM_PROJ, w), lambda i: (i, 0))
    full = lambda a: pl.BlockSpec(a.shape, lambda i: (0,) * a.ndim)
    outs = [jax.ShapeDtypeStruct((t, 512), jnp.bfloat16)] * 6 + [jax.ShapeDtypeStruct((t, D_MODEL), jnp.bfloat16)] * 2
    return pl.pallas_call(
        _proj_kernel,
        grid=(n,),
        in_specs=[row(D_MODEL), full(g1), full(w_perm), full(gn), full(gmat)],
        out_specs=[row(512)] * 6 + [row(D_MODEL)] * 2,
        out_shape=outs,
        compiler_params=_cparams(("arbitrary",)),
        name="proj",
    )(x2, g1, w_perm, gn, gmat)


def _attn_kernel(scal_ref, q_ref, k_ref, v_ref, vec_ref, gsub_ref, o_ref,
                 bias_sc, kmax_sc, qz_sc, refn_sc, reff_sc, l_sc, acc_sc,
                 m_sc, acc2_sc, mf_sc, accf_sc, *, mode, n_near):
    f32 = jnp.float32
    h = pl.program_id(1)
    qi = pl.program_id(2)
    nc = TK // LANES
    n_heads = pl.num_programs(1)

    @pl.when(qi == 0)
    def _():
        qchunk = lax.shift_right_arithmetic(lax.broadcasted_iota(jnp.int32, (TQ, TK), 0), CHUNK_SHIFT)
        kcol = lax.broadcasted_iota(jnp.int32, (TQ, TK), 1)
        for j in range(n_near):
            dchunk = qchunk - lax.shift_right_arithmetic(kcol - j * TK, CHUNK_SHIFT)
            allowed = (dchunk >= 0) & (dchunk <= B_LEFT_CHUNKS) if mode == "band" else dchunk >= 0
            for half in range(2):
                vec = jnp.broadcast_to(vec_ref[0, j, half:half + 1, :], (TQ, TQ + TK))
                tile = pltpu.roll(vec, 0, 1, stride=1, stride_axis=0)[:, :TK]
                bias_sc[j, half * TQ:(half + 1) * TQ, :] = jnp.where(allowed, tile, NEG)
        kf = k_ref[...].astype(f32)
        ksq = kf * kf
        klane = lax.broadcasted_iota(jnp.int32, ksq.shape, 1)
        for half in range(2):
            sel = (klane < 64) if half == 0 else (klane >= 64)
            norm2 = jnp.max(jnp.sum(jnp.where(sel, ksq, 0.0), axis=-1, keepdims=True), axis=0, keepdims=True)
            kmax_sc[half * TQ:(half + 1) * TQ, :] = jnp.broadcast_to(jnp.sqrt(norm2), (TQ, LANES))

    q = q_ref[...]
    lane = lax.broadcasted_iota(jnp.int32, q.shape, 1)
    zero = jnp.zeros_like(q)
    qz_sc[0:TQ, :] = jnp.where(lane < 64, q, zero)
    qz_sc[TQ:2 * TQ, :] = jnp.where(lane >= 64, q, zero)
    if mode == "diff":
        cfar = scal_ref[1 + h]
        bias_max = scal_ref[1 + n_heads + h]
    else:
        bias_max = scal_ref[h]

    def scores(kblk):
        k = k_ref[pl.ds(pl.multiple_of(kblk * TK, TK), TK), :]
        return lax.dot_general(qz_sc[...], k, (((1,), (1,)), ((), ())), preferred_element_type=f32)

    def value_block(kblk):
        return v_ref[pl.ds(pl.multiple_of(kblk * TK, TK), TK), :]

    def finish(o):
        if mode == "diff":
            od = o[:TQ] - scal_ref[0] * o[TQ:]
            od = od * lax.rsqrt(jnp.mean(od * od, axis=-1, keepdims=True) + EPS) * gsub_ref[...]
            o_ref[...] = od.astype(o_ref.dtype)
        else:
            o_ref[...] = jnp.where(lane < 64, o[:TQ], o[TQ:]).astype(o_ref.dtype)

    qf = qz_sc[...].astype(f32)
    qnorm = jnp.sqrt(jnp.sum(qf * qf, axis=-1, keepdims=True))
    refn_sc[...] = qnorm * kmax_sc[...] + bias_max
    l_sc[...] = jnp.zeros(l_sc.shape, f32)
    acc_sc[...] = jnp.zeros(acc_sc.shape, f32)

    def accumulate(s, kblk, ref_ref):
        ref = ref_ref[...]
        ps = [jnp.exp2(s[:, c * LANES:(c + 1) * LANES] - ref) for c in range(nc)]
        l_sc[...] += functools.reduce(jnp.add, ps)
        p = jnp.concatenate([x.astype(jnp.bfloat16) for x in ps], axis=1)
        acc_sc[...] += jnp.dot(p, value_block(kblk), preferred_element_type=f32)

    for j in range(n_near):
        def near(j=j):
            accumulate(scores(qi - j) + bias_sc[j], qi - j, refn_sc)
        if j == 0:
            near()
        else:
            pl.when(qi >= j)(near)

    n_far = jnp.maximum(qi - (n_near - 1), 0)
    if mode == "diff":
        reff_sc[...] = refn_sc[...] - cfar

        def far(kblk, carry):
            accumulate(scores(kblk), kblk, reff_sc)
            return carry

        def far_quad(i, carry):
            for u in range(FAR_UNROLL):
                far(FAR_UNROLL * i + u, carry)
            return carry

        lax.fori_loop(0, n_far // FAR_UNROLL, far_quad, 0)
        lax.fori_loop((n_far // FAR_UNROLL) * FAR_UNROLL, n_far, far, 0)

    denom = jnp.sum(l_sc[...], axis=-1, keepdims=True)
    finish(acc_sc[...] / denom)

    @pl.when(jnp.logical_not(jnp.min(denom) >= DENOM_FLOOR))
    def _():
        def update(s, kblk, m_ref, a_ref):
            cols = [s[:, c * LANES:(c + 1) * LANES] for c in range(nc)]
            m_old = m_ref[...]
            m_new = jnp.maximum(m_old, jnp.max(functools.reduce(jnp.maximum, cols), axis=-1, keepdims=True))
            alpha = jnp.exp2(m_old - m_new)
            ps = [jnp.exp2(c - m_new) for c in cols]
            p = jnp.concatenate([x.astype(jnp.bfloat16) for x in ps], axis=1)
            a_ref[:, 0:LANES] = alpha * a_ref[:, 0:LANES] + jnp.dot(p, value_block(kblk), preferred_element_type=f32)
            a_ref[:, LANES:2 * LANES] = alpha * a_ref[:, LANES:2 * LANES] + functools.reduce(jnp.add, ps)
            m_ref[...] = m_new

        m_sc[...] = jnp.full(m_sc.shape, NEG, f32)
        acc2_sc[...] = jnp.zeros(acc2_sc.shape, f32)
        for j in range(n_near):
            def near(j=j):
                update(scores(qi - j) + bias_sc[j], qi - j, m_sc, acc2_sc)
            if j == 0:
                near()
            else:
                pl.when(qi >= j)(near)

        if mode == "diff":
            mf_sc[...] = jnp.full(mf_sc.shape, NEG, f32)
            accf_sc[...] = jnp.zeros(accf_sc.shape, f32)

            def far(kblk, carry):
                update(scores(kblk), kblk, mf_sc, accf_sc)
                return carry

            lax.fori_loop(0, n_far, far, 0)
            mf = mf_sc[...] + cfar
            mn = m_sc[...]
            m = jnp.maximum(mf, mn)
            wf = jnp.exp2(mf - m)
            wn = jnp.exp2(mn - m)
            tot = (jnp.concatenate([wf, wf], axis=1) * accf_sc[...]
                   + jnp.concatenate([wn, wn], axis=1) * acc2_sc[...])
        else:
            tot = acc2_sc[...]
        finish(tot[:, 0:LANES] / jnp.sum(tot[:, LANES:2 * LANES], axis=-1, keepdims=True))


def _attention(scal, q, k, v, vecs, gsub, *, bsz, s_len, mode):
    n_blk = q.shape[1] // LANES
    nq = s_len // TQ
    n_near = vecs.shape[1]
    assert vecs.shape == (n_blk, n_near, 2, TQ + TK)
    kern = functools.partial(_attn_kernel, mode=mode, n_near=n_near)
    stat = lambda: pltpu.VMEM((2 * TQ, LANES), jnp.float32)
    acc = lambda: pltpu.VMEM((2 * TQ, 2 * LANES), jnp.float32)
    return pl.pallas_call(
        kern,
        grid=(bsz, n_blk, nq),
        in_specs=[
            pl.BlockSpec(memory_space=pltpu.SMEM),
            pl.BlockSpec((TQ, LANES), lambda b, h, i: (b * nq + i, h)),
            pl.BlockSpec((s_len, LANES), lambda b, h, i: (b, h)),
            pl.BlockSpec((s_len, LANES), lambda b, h, i: (b, h)),
            pl.BlockSpec((1, n_near, 2, TQ + TK), lambda b, h, i: (h, 0, 0, 0)),
            pl.BlockSpec((1, LANES), lambda b, h, i: (0, 0)),
        ],
        out_specs=pl.BlockSpec((TQ, LANES), lambda b, h, i: (b * nq + i, h)),
        out_shape=jax.ShapeDtypeStruct(q.shape, jnp.bfloat16),
        scratch_shapes=[pltpu.VMEM((n_near, 2 * TQ, TK), jnp.float32), stat(),
                        pltpu.VMEM((2 * TQ, LANES), jnp.bfloat16), stat(), stat(), stat(), stat(),
                        stat(), acc(), stat(), acc()],
        compiler_params=_cparams(("arbitrary", "arbitrary", "arbitrary")),
        name="attn_" + mode,
    )(scal, q, k, v, vecs, gsub)


def _post_kernel(oa_ref, ob_ref, ga_ref, gb_ref, x_ref, wa_ref, wb_ref, wo_ref, g2_ref,
                 wr2_ref, br_ref, x1_ref, hn_ref, route_ref):
    f32 = jnp.float32
    ya = jnp.dot(oa_ref[...], wa_ref[...], preferred_element_type=f32)
    yb = jnp.dot(ob_ref[...], wb_ref[...], preferred_element_type=f32)
    mixed = ga_ref[...].astype(f32) * ya + gb_ref[...].astype(f32) * yb
    x1 = x_ref[...] + jnp.dot(mixed.astype(jnp.bfloat16), wo_ref[...], preferred_element_type=f32)
    x1_ref[...] = x1
    hn = x1 * lax.rsqrt(jnp.mean(x1 * x1, axis=-1, keepdims=True) + EPS) * g2_ref[...]
    hh = hn.astype(jnp.bfloat16)
    bits = lax.bitcast_convert_type(hh.astype(f32), jnp.uint32)
    half = D_MODEL // 2
    hn_ref[...] = (bits[:, :half] & jnp.uint32(0xFFFF0000)) | lax.shift_right_logical(bits[:, half:], jnp.uint32(16))

    hl = (hn - hh.astype(f32)).astype(jnp.bfloat16)
    hw = jnp.dot(hh, wr2_ref[...], preferred_element_type=f32)
    lg = (hw[:, 0:LANES] + hw[:, LANES:2 * LANES]
          + jnp.dot(hl, wr2_ref[:, 0:LANES], preferred_element_type=f32)) + br_ref[...]

    lanei = lax.broadcasted_iota(jnp.int32, lg.shape, 1)
    lanef = lanei.astype(f32)
    big = 999.0
    gmask = lanei < N_GROUPS
    gl = jnp.where(gmask, lg, NEG)
    gm = jnp.max(gl, axis=-1, keepdims=True)
    ge = jnp.where(gmask, jnp.exp(gl - gm), 0.0)
    gp = ge / jnp.sum(ge, axis=-1, keepdims=True)
    p_g = jnp.max(gp, axis=-1, keepdims=True)
    gidx = jnp.min(jnp.where(gmask & (gp == p_g), lanef, big), axis=-1, keepdims=True)
    egrp = lax.shift_right_arithmetic(lanei - N_GROUPS, 3).astype(f32)
    emask = (lanei >= N_GROUPS) & (lanei < N_GROUPS + N_EXPERTS) & (egrp == gidx)
    el = jnp.where(emask, lg, NEG)
    v1 = jnp.max(el, axis=-1, keepdims=True)
    i1 = jnp.min(jnp.where(emask & (el == v1), lanef, big), axis=-1, keepdims=True)
    emask2 = emask & (lanef != i1)
    el2 = jnp.where(emask2, lg, NEG)
    v2 = jnp.max(el2, axis=-1, keepdims=True)
    i2 = jnp.min(jnp.where(emask2 & (el2 == v2), lanef, big), axis=-1, keepdims=True)
    t = jnp.exp(v2 - v1)
    den = 1.0 + t
    w1 = p_g * (1.0 / den)
    w2 = p_g * (t / den)
    route = jnp.where(lanei == 0, i1 - N_GROUPS,
                      jnp.where(lanei == 1, i2 - N_GROUPS,
                                jnp.where(lanei == 2, w1, jnp.where(lanei == 3, w2, 0.0))))
    route_ref[...] = route


def _post(oa, ob, ga, gb, x2, wa, wb, wo, g2, wr2, br):
    t = x2.shape[0]
    n = t // TM_POST
    row = lambda w: pl.BlockSpec((TM_POST, w), lambda i: (i, 0))
    full = lambda a: pl.BlockSpec(a.shape, lambda i: (0,) * a.ndim)
    return pl.pallas_call(
        _post_kernel,
        grid=(n,),
        in_specs=[row(512), row(512), row(D_MODEL), row(D_MODEL), row(D_MODEL),
                  full(wa), full(wb), full(wo), full(g2), full(wr2), full(br)],
        out_specs=[row(D_MODEL), row(D_MODEL // 2), row(LANES)],
        out_shape=[jax.ShapeDtypeStruct((t, D_MODEL), jnp.float32),
                   jax.ShapeDtypeStruct((t, D_MODEL // 2), jnp.uint32),
                   jax.ShapeDtypeStruct((t, LANES), jnp.float32)],
        compiler_params=_cparams(("arbitrary",)),
        name="post",
    )(oa, ob, ga, gb, x2, wa, wb, wo, g2, wr2, br)


def _rank_kernel(route_ref, ltri_ref, utri_ref, dest_ref, cnt_ref, cnt_sc, pstart_sc, base_sc):
    f32 = jnp.float32
    p = pl.program_id(0)
    i = pl.program_id(1)
    route = route_ref[...]
    lanef = lax.broadcasted_iota(jnp.int32, route.shape, 1).astype(f32)
    oh1 = (lanef == route[:, 0:1]).astype(f32)
    oh2 = (lanef == route[:, 1:2]).astype(f32)
    both = oh1 + oh2
    colsum = jnp.sum(both, axis=0, keepdims=True)

    @pl.when((p == 0) & (i == 0))
    def _():
        cnt_sc[...] = jnp.zeros(cnt_sc.shape, f32)

    @pl.when(p == 0)
    def _():
        cnt_sc[...] += colsum
        dest_ref[...] = jnp.zeros(dest_ref.shape, f32)
        cnt_ref[...] = jnp.zeros(cnt_ref.shape, f32)

    @pl.when((p == 1) & (i == 0))
    def _():
        cnt = cnt_sc[...]
        chi = jnp.floor(cnt * (1.0 / 256.0))
        clo = cnt - chi * 256.0
        split = jnp.concatenate([jnp.broadcast_to(chi, (8, LANES)), jnp.broadcast_to(clo, (8, LANES))], axis=0)
        excl = jnp.dot(split.astype(jnp.bfloat16), utri_ref[...], preferred_element_type=f32)
        pstart_sc[...] = excl[0:1] * 256.0 + excl[8:9]
        base_sc[...] = jnp.zeros(base_sc.shape, f32)

    @pl.when(p == 1)
    def _():
        prior = jnp.dot(ltri_ref[...], both.astype(jnp.bfloat16), preferred_element_type=f32)
        slot = prior + base_sc[...] + pstart_sc[...]
        d1 = jnp.sum(oh1 * slot, axis=-1, keepdims=True)
        d2 = jnp.sum(oh2 * slot, axis=-1, keepdims=True)
        dest_ref[...] = jnp.where(lanef == 0.0, d1, jnp.where(lanef == 1.0, d2, 0.0))
        base_sc[...] += colsum
        cnt_ref[...] = jnp.broadcast_to(cnt_sc[...], cnt_ref.shape)


def _rank(route, ltri, utri):
    t = route.shape[0]
    n = t // TM_RANK
    full = lambda a: pl.BlockSpec(a.shape, lambda p, i: (0,) * a.ndim)
    row1 = lambda: pltpu.VMEM((1, LANES), jnp.float32)
    return pl.pallas_call(
        _rank_kernel,
        grid=(2, n),
        in_specs=[pl.BlockSpec((TM_RANK, LANES), lambda p, i: (i, 0)), full(ltri), full(utri)],
        out_specs=[pl.BlockSpec((TM_RANK, LANES), lambda p, i: (i * p, 0)),
                   pl.BlockSpec((8, LANES), lambda p, i: (0, 0))],
        out_shape=[jax.ShapeDtypeStruct((t, LANES), jnp.float32),
                   jax.ShapeDtypeStruct((8, LANES), jnp.float32)],
        scratch_shapes=[row1(), row1(), row1()],
        compiler_params=_cparams(("arbitrary", "arbitrary")),
        name="rank",
    )(route, ltri, utri)


def _dispatch_kernel(dest_ref, hn_ref, xs_ref, sems):
    def row_copy(r, k):
        d = dest_ref[0, 0, 2 * r + k]
        return pltpu.make_async_copy(hn_ref.at[pl.ds(r, 1), :], xs_ref.at[pl.ds(d, 1), :], sems.at[k])

    def issue(r, c):
        row_copy(r, 0).start(priority=0)
        row_copy(r, 1).start(priority=1)
        return c

    lax.fori_loop(0, TM_ROWS, issue, 0, unroll=ROW_UNROLL)
    for k in range(TOP_K):
        pltpu.make_async_copy(hn_ref, xs_ref.at[pl.ds(0, TM_ROWS), :], sems.at[k]).wait()


def _dispatch(dest3, hn, n_rows):
    t, w = hn.shape
    n = t // TM_ROWS
    return pl.pallas_call(
        _dispatch_kernel,
        grid=(n,),
        in_specs=[pl.BlockSpec((1, 1, 2 * TM_ROWS), lambda i: (i, 0, 0), memory_space=pltpu.SMEM),
                  pl.BlockSpec((TM_ROWS, w), lambda i: (i, 0))],
        out_specs=pl.BlockSpec(memory_space=pl.ANY),
        out_shape=jax.ShapeDtypeStruct((n_rows, w), hn.dtype),
        scratch_shapes=[pltpu.SemaphoreType.DMA((2,))],
        compiler_params=_cparams(("arbitrary",)),
        name="dispatch",
    )(dest3, hn)


def _ffn_kernel(tile_ref, exp_ref, lo_ref, hi_ref, cast_ref, init_ref,
                xs_ref, wg_ref, wu_ref, wd_ref, ys_ref, wg_sc, wu_sc, wd_sc):
    v = pl.program_id(0)
    lo = lo_ref[v]
    hi = hi_ref[v]

    @pl.when(init_ref[v] == 1)
    def _():
        ys_ref[...] = jnp.zeros(ys_ref.shape, ys_ref.dtype)

    @pl.when(hi > lo)
    def _():
        @pl.when(cast_ref[v] == 1)
        def _():
            wg_sc[...] = wg_ref[0].astype(jnp.bfloat16)
            wu_sc[...] = wu_ref[0].astype(jnp.bfloat16)
            wd_sc[...] = wd_ref[0].astype(jnp.bfloat16)

        pk = xs_ref[...]
        x = jnp.concatenate(
            [lax.bitcast_convert_type(pk & jnp.uint32(0xFFFF0000), jnp.float32),
             lax.bitcast_convert_type(lax.shift_left(pk, jnp.uint32(16)), jnp.float32)], axis=1).astype(jnp.bfloat16)
        g = jnp.dot(x, wg_sc[...], preferred_element_type=jnp.float32)
        u = jnp.dot(x, wu_sc[...], preferred_element_type=jnp.float32)
        hb = (g * jax.nn.sigmoid(g) * u).astype(jnp.bfloat16)
        y = jnp.dot(hb, wd_sc[...], preferred_element_type=jnp.float32)
        rows = tile_ref[v] * FFN_BLK + lax.broadcasted_iota(jnp.int32, y.shape, 0)
        ys_ref[...] = jnp.where((rows >= lo) & (rows < hi), y, ys_ref[...])


def _ffn(seg, xs, w_gate, w_up, w_down):
    n_rows = xs.shape[0]
    n_seg = seg[0].shape[0]
    grid_spec = pltpu.PrefetchScalarGridSpec(
        num_scalar_prefetch=6,
        grid=(n_seg,),
        in_specs=[
            pl.BlockSpec((FFN_BLK, D_MODEL // 2), lambda v, t, e, *_: (t[v], 0)),
            pl.BlockSpec((1, D_MODEL, D_EXPERT), lambda v, t, e, *_: (e[v], 0, 0)),
            pl.BlockSpec((1, D_MODEL, D_EXPERT), lambda v, t, e, *_: (e[v], 0, 0)),
            pl.BlockSpec((1, D_EXPERT, D_MODEL), lambda v, t, e, *_: (e[v], 0, 0)),
        ],
        out_specs=pl.BlockSpec((FFN_BLK, D_MODEL), lambda v, t, e, *_: (t[v], 0)),
        scratch_shapes=[pltpu.VMEM((D_MODEL, D_EXPERT), jnp.bfloat16),
                        pltpu.VMEM((D_MODEL, D_EXPERT), jnp.bfloat16),
                        pltpu.VMEM((D_EXPERT, D_MODEL), jnp.bfloat16)],
    )
    return pl.pallas_call(
        _ffn_kernel,
        grid_spec=grid_spec,
        out_shape=jax.ShapeDtypeStruct((n_rows, D_MODEL), jnp.float32),
        compiler_params=_cparams(("arbitrary",)),
        name="ffn",
    )(*seg, xs, w_gate, w_up, w_down)


def _segments(counts, n_rows):
    i32 = jnp.int32
    n_tiles = n_rows // FFN_BLK
    n_seg = n_tiles + N_EXPERTS
    tri = jnp.tril(jnp.ones((N_EXPERTS, N_EXPERTS), i32))
    ends = jnp.sum(tri * counts[None, :], axis=1)
    starts = ends - counts
    edges = jnp.arange(n_tiles, dtype=i32) * FFN_BLK
    rank_e = jnp.arange(n_tiles, dtype=i32) + jnp.sum(starts[None, :] <= edges[:, None], axis=1)
    rank_s = jnp.arange(N_EXPERTS, dtype=i32) + jnp.sum(edges[None, :] < starts[:, None], axis=1)
    seg = jnp.arange(n_seg, dtype=i32)
    lo = (jnp.sum(jnp.where(rank_e[None, :] == seg[:, None], edges[None, :], 0), axis=1)
          + jnp.sum(jnp.where(rank_s[None, :] == seg[:, None], starts[None, :], 0), axis=1))
    hi = jnp.concatenate([lo[1:], jnp.array([n_rows], i32)])
    valid = hi > lo
    tile = jnp.minimum(lo // FFN_BLK, n_tiles - 1)
    expert = jnp.minimum(jnp.sum(ends[None, :] <= lo[:, None], axis=1), N_EXPERTS - 1).astype(i32)
    upto = seg[None, :] <= seg[:, None]
    expert = jnp.max(jnp.where(upto & valid[None, :], expert[None, :], 0), axis=1)
    prev_expert = jnp.concatenate([jnp.array([-1], i32), expert[:-1]])
    first_valid = valid & (jnp.sum(jnp.where(upto & valid[None, :], 1, 0), axis=1) == 1)
    cast = valid & ((expert != prev_expert) | first_valid)
    prev_tile = jnp.concatenate([jnp.array([-1], i32), tile[:-1]])
    init = tile != prev_tile
    return (tile.astype(i32), expert, lo.astype(i32), hi.astype(i32), cast.astype(i32), init.astype(i32))


def _combine_kernel(dest_ref, route_ref, x1_ref, ys_ref, out_ref, y0_sc, y1_sc, sems):
    def row_copy(r, k):
        d = dest_ref[0, 0, 2 * r + k]
        dst = y0_sc if k == 0 else y1_sc
        return pltpu.make_async_copy(ys_ref.at[pl.ds(d, 1), :], dst.at[pl.ds(r, 1), :], sems.at[k])

    def issue(r, c):
        row_copy(r, 0).start(priority=0)
        row_copy(r, 1).start(priority=1)
        return c

    lax.fori_loop(0, TM_ROWS, issue, 0, unroll=ROW_UNROLL)
    pltpu.make_async_copy(ys_ref.at[pl.ds(0, TM_ROWS), :], y0_sc, sems.at[0]).wait()
    pltpu.make_async_copy(ys_ref.at[pl.ds(0, TM_ROWS), :], y1_sc, sems.at[1]).wait()
    route = route_ref[...]
    out_ref[...] = x1_ref[...] + (route[:, 2:3] * y0_sc[...] + route[:, 3:4] * y1_sc[...])


def _combine(dest3, route, x1, ys):
    t = x1.shape[0]
    n = t // TM_ROWS
    return pl.pallas_call(
        _combine_kernel,
        grid=(n,),
        in_specs=[pl.BlockSpec((1, 1, 2 * TM_ROWS), lambda i: (i, 0, 0), memory_space=pltpu.SMEM),
                  pl.BlockSpec((TM_ROWS, LANES), lambda i: (i, 0)),
                  pl.BlockSpec((TM_ROWS, D_MODEL), lambda i: (i, 0)),
                  pl.BlockSpec(memory_space=pl.ANY)],
        out_specs=pl.BlockSpec((TM_ROWS, D_MODEL), lambda i: (i, 0)),
        out_shape=jax.ShapeDtypeStruct((t, D_MODEL), jnp.float32),
        scratch_shapes=[pltpu.VMEM((TM_ROWS, D_MODEL), jnp.float32),
                        pltpu.VMEM((TM_ROWS, D_MODEL), jnp.float32),
                        pltpu.SemaphoreType.DMA((2,))],
        compiler_params=_cparams(("arbitrary",)),
        name="combine",
    )(dest3, route, x1, ys)


def _t5_bucket(rel):
    nb = T5_BUCKETS // 2
    max_exact = nb // 2
    side = jnp.where(rel > 0, nb, 0)
    n = jnp.abs(rel)
    nf = jnp.maximum(n, 1).astype(jnp.float32)
    large = max_exact + (jnp.log(nf / max_exact) / math.log(T5_MAX_DIST / max_exact)
                         * (nb - max_exact)).astype(jnp.int32)
    large = jnp.minimum(large, nb - 1)
    return side + jnp.where(n < max_exact, n, large)


def _rel_offsets(j):
    i = jnp.arange(TQ + TK)
    return jnp.where(i < TK, i, i - (TQ + TK)) - j * TK


def _diff_bias_vecs(t5_table):
    vecs = jnp.stack([t5_table[_t5_bucket(_rel_offsets(j))].astype(jnp.float32).T for j in range(2)], axis=1)
    vecs = jnp.stack([vecs, vecs], axis=2)
    far = t5_table[_t5_bucket(jnp.array(-(TK + 1)))].astype(jnp.float32)
    return vecs, far


def _band_bias_vecs(rel_table):
    vecs = jnp.stack([rel_table[jnp.clip(_rel_offsets(j), -B_MAX_REL, B_MAX_REL) + B_MAX_REL].astype(jnp.float32).T
                      for j in range(2)], axis=1)
    return vecs.reshape(B_HEADS // 2, 2, 2, TQ + TK).transpose(0, 2, 1, 3)


def kernel(x, norm1_g, w_in, a_qnorm_g, a_knorm_g, a_lambda, a_subln_g, t5_table, b_qnorm_g, b_knorm_g,
           b_rel_table, w_branch_a, w_branch_b, w_out, norm2_g, w_router_group, b_router_group,
           w_router_expert, b_router_expert, w_gate, w_up, w_down):
    bsz, s_len, _ = x.shape
    n_tok = bsz * s_len
    f32, bf16 = jnp.float32, jnp.bfloat16
    assert s_len % TQ == 0 and TQ == TK and TQ % CHUNK == 0 and n_tok % TM_PROJ == 0
    assert TK >= T5_MAX_DIST and TK >= B_LEFT_CHUNKS * CHUNK
    l = 0
    x2 = x.reshape(n_tok, D_MODEL)

    w = w_in[l]
    qk = w[:, :1024].reshape(D_MODEL, 2, 2, A_HEADS, A_HEAD_DIM)
    qk = qk.transpose(0, 1, 3, 2, 4).reshape(D_MODEL, 1024)
    w_perm = jnp.concatenate([qk, w[:, 1024:]], axis=1).astype(bf16)
    gn = jnp.stack([jnp.tile(a_qnorm_g[l] * (A_HEAD_DIM ** -0.5 * LOG2E), 8), jnp.tile(a_knorm_g[l], 8),
                    jnp.tile(b_qnorm_g[l] * (B_HEAD_DIM ** -0.5 * LOG2E), 8), jnp.tile(b_knorm_g[l], 8)]).astype(f32)
    gmat = jnp.asarray(np.kron(np.eye(8), np.ones((64, 64))), dtype=bf16)

    qa, ka, va, qb, kb, vb, ga, gb = _proj(x2, norm1_g[l][None].astype(f32), w_perm, gn, gmat)

    lam_init = 0.8 - 0.6 * math.exp(-0.3 * l)
    lp = a_lambda[l].astype(f32)
    lam = jnp.exp(jnp.sum(lp[0] * lp[1])) - jnp.exp(jnp.sum(lp[2] * lp[3])) + lam_init
    bias_a, far_a = _diff_bias_vecs(t5_table)
    bmax_a = jnp.maximum(jnp.max(bias_a, axis=(1, 2, 3)), far_a)
    scal_a = (jnp.concatenate([lam[None], far_a, bmax_a]) * jnp.array([1.0] + [LOG2E] * (2 * A_HEADS))).astype(f32)
    gsub = (a_subln_g[l] * (1.0 - lam_init))[None].astype(f32)
    oa = _attention(scal_a, qa, ka, va, bias_a * LOG2E, gsub, bsz=bsz, s_len=s_len, mode="diff")

    bias_b = _band_bias_vecs(b_rel_table[l])
    scal_b = (jnp.max(bias_b, axis=(1, 2, 3)) * LOG2E).astype(f32)
    ob = _attention(scal_b, qb, kb, vb, bias_b * LOG2E, gsub, bsz=bsz, s_len=s_len, mode="band")

    wr = jnp.zeros((D_MODEL, LANES), f32)
    wr = wr.at[:, :N_GROUPS].set(w_router_group[l]).at[:, N_GROUPS:N_GROUPS + N_EXPERTS].set(w_router_expert[l])
    wrh = wr.astype(bf16)
    wr2 = jnp.concatenate([wrh, (wr - wrh.astype(f32)).astype(bf16)], axis=1)
    br = jnp.zeros((1, LANES), f32)
    br = br.at[0, :N_GROUPS].set(b_router_group[l]).at[0, N_GROUPS:N_GROUPS + N_EXPERTS].set(b_router_expert[l])
    x1, hn, route = _post(oa, ob, ga, gb, x2, w_branch_a[l].astype(bf16), w_branch_b[l].astype(bf16),
                          w_out[l].astype(bf16), norm2_g[l][None].astype(f32), wr2, br)

    ltri = jnp.asarray(np.tril(np.ones((TM_RANK, TM_RANK)), -1), dtype=bf16)
    utri = jnp.asarray(np.triu(np.ones((LANES, LANES)), 1), dtype=bf16)
    dest, cnt = _rank(route, ltri, utri)

    counts = cnt[0, :N_EXPERTS].astype(jnp.int32)
    n_rows = n_tok * TOP_K
    seg = _segments(counts, n_rows)

    dest3 = dest[:, :TOP_K].astype(jnp.int32).reshape(n_tok // TM_ROWS, 1, TOP_K * TM_ROWS)
    xs = _dispatch(dest3, hn, n_rows)
    ys = _ffn(seg, xs, w_gate[l], w_up[l], w_down[l])
    out = _combine(dest3, route, x1, ys)
    return out.reshape(bsz, s_len, D_MODEL)
```

```python
import functools
import math

import jax
import jax.numpy as jnp
import numpy as np
from jax import lax
from jax.experimental import pallas as pl
from jax.experimental.pallas import tpu as pltpu

D_MODEL = 1024
CHUNK = 64
A_HEADS = 4
A_HEAD_DIM = 64
A_VDIM = 2 * A_HEAD_DIM
B_HEADS = 8
B_HEAD_DIM = 64
B_LEFT_CHUNKS = 8
B_MAX_REL = 128
T5_BUCKETS = 32
T5_MAX_DIST = 128
N_GROUPS = 4
EXPERTS_PER_GROUP = 8
N_EXPERTS = N_GROUPS * EXPERTS_PER_GROUP
TOP_K = 2
D_EXPERT = 512
EPS = 1e-6
NEG = -1e30
LOG2E = 1.0 / math.log(2.0)
DENOM_FLOOR = 2.0 ** -100

CHUNK_SHIFT = CHUNK.bit_length() - 1
assert 1 << CHUNK_SHIFT == CHUNK
LANES = 128
A_W = A_HEADS * 2 * A_HEAD_DIM
B_W = B_HEADS * B_HEAD_DIM
PROJ_W = 4 * 256 + 4 * 512 + 2 * D_MODEL

TM_PROJ = 512
TQ = 512
TK = 512
FAR_UNROLL = 4
TM_POST = 512
TM_RANK = 1024
TM_ROWS = 512
ROW_UNROLL = 8
FFN_BLK = 256
VMEM_LIMIT = 56 * 1024 * 1024


def _cparams(sem):
    return pltpu.CompilerParams(dimension_semantics=sem, vmem_limit_bytes=VMEM_LIMIT)


def _proj_kernel(x_ref, g1_ref, w_ref, gn_ref, gmat_ref,
                 qa_ref, ka_ref, va_ref, qb_ref, kb_ref, vb_ref, ga_ref, gb_ref):
    x = x_ref[...]
    xn = x * lax.rsqrt(jnp.mean(x * x, axis=-1, keepdims=True) + EPS) * g1_ref[...]
    xn = xn.astype(jnp.bfloat16)

    def slab(c0, width):
        return jnp.dot(xn, w_ref[:, c0:c0 + width], preferred_element_type=jnp.float32)

    def headnorm(y, gi):
        sq = (y * y).astype(jnp.bfloat16)
        half = gmat_ref.shape[0]
        ss = jnp.concatenate([jnp.dot(sq[:, c:c + half], gmat_ref[...], preferred_element_type=jnp.float32)
                              for c in range(0, y.shape[1], half)], axis=1)
        return y * lax.rsqrt(ss * (1.0 / A_HEAD_DIM) + EPS) * gn_ref[gi:gi + 1, :]

    qa_ref[...] = headnorm(slab(0, 512), 0).astype(jnp.bfloat16)
    ka_ref[...] = headnorm(slab(512, 512), 1).astype(jnp.bfloat16)
    va_ref[...] = slab(1024, 512).astype(jnp.bfloat16)
    qb_ref[...] = headnorm(slab(1536, 512), 2).astype(jnp.bfloat16)
    kb_ref[...] = headnorm(slab(2048, 512), 3).astype(jnp.bfloat16)
    vb_ref[...] = slab(2560, 512).astype(jnp.bfloat16)
    for j in range(2):
        ga_ref[:, j * 512:(j + 1) * 512] = jax.nn.sigmoid(slab(3072 + j * 512, 512)).astype(jnp.bfloat16)
        gb_ref[:, j * 512:(j + 1) * 512] = jax.nn.sigmoid(slab(4096 + j * 512, 512)).astype(jnp.bfloat16)


def _proj(x2, g1, w_perm, gn, gmat):
    t = x2.shape[0]
    n = t // TM_PROJ
    row = lambda w: pl.BlockSpec((TM_PROJ, w), lambda i: (i, 0))
    full = lambda a: pl.BlockSpec(a.shape, lambda i: (0,) * a.ndim)
    outs = [jax.ShapeDtypeStruct((t, 512), jnp.bfloat16)] * 6 + [jax.ShapeDtypeStruct((t, D_MODEL), jnp.bfloat16)] * 2
    return pl.pallas_call(
        _proj_kernel,
        grid=(n,),
        in_specs=[row(D_MODEL), full(g1), full(w_perm), full(gn), full(gmat)],
        out_specs=[row(512)] * 6 + [row(D_MODEL)] * 2,
        out_shape=outs,
        compiler_params=_cparams(("arbitrary",)),
        name="proj",
    )(x2, g1, w_perm, gn, gmat)


def _attn_kernel(scal_ref, q_ref, k_ref, v_ref, vec_ref, gsub_ref, o_ref,
                 bias_sc, kmax_sc, qz_sc, refn_sc, reff_sc, l_sc, acc_sc,
                 m_sc, acc2_sc, mf_sc, accf_sc, *, mode, n_near):
    f32 = jnp.float32
    h = pl.program_id(1)
    qi = pl.program_id(2)
    nc = TK // LANES
    n_heads = pl.num_programs(1)

    def bias_cols(j):
        return slice((n_near - 1 - j) * TK, (n_near - j) * TK)

    @pl.when(qi == 0)
    def _():
        qchunk = lax.shift_right_arithmetic(lax.broadcasted_iota(jnp.int32, (TQ, TK), 0), CHUNK_SHIFT)
        kcol = lax.broadcasted_iota(jnp.int32, (TQ, TK), 1)
        for j in range(n_near):
            dchunk = qchunk - lax.shift_right_arithmetic(kcol - j * TK, CHUNK_SHIFT)
            allowed = (dchunk >= 0) & (dchunk <= B_LEFT_CHUNKS) if mode == "band" else dchunk >= 0
            for half in range(2):
                vec = jnp.broadcast_to(vec_ref[0, j, half:half + 1, :], (TQ, TQ + TK))
                tile = pltpu.roll(vec, 0, 1, stride=1, stride_axis=0)[:, :TK]
                bias_sc[half * TQ:(half + 1) * TQ, bias_cols(j)] = jnp.where(allowed, tile, NEG)
        if mode == "diff":
            kf = k_ref[...].astype(f32)
            ksq = kf * kf
            klane = lax.broadcasted_iota(jnp.int32, ksq.shape, 1)
            for half in range(2):
                sel = (klane < 64) if half == 0 else (klane >= 64)
                norm2 = jnp.max(jnp.sum(jnp.where(sel, ksq, 0.0), axis=-1, keepdims=True), axis=0, keepdims=True)
                kmax_sc[half * TQ:(half + 1) * TQ, :] = jnp.broadcast_to(jnp.sqrt(norm2), (TQ, LANES))

    q = q_ref[...]
    lane = lax.broadcasted_iota(jnp.int32, q.shape, 1)
    zero = jnp.zeros_like(q)
    qz_sc[0:TQ, :] = jnp.where(lane < 64, q, zero)
    qz_sc[TQ:2 * TQ, :] = jnp.where(lane >= 64, q, zero)
    if mode == "diff":
        cfar = scal_ref[1 + h]
        bias_max = scal_ref[1 + n_heads + h]
    else:
        bias_max = scal_ref[h]

    def scores(kblk):
        k = k_ref[pl.ds(pl.multiple_of(kblk * TK, TK), TK), :]
        return lax.dot_general(qz_sc[...], k, (((1,), (1,)), ((), ())), preferred_element_type=f32)

    def value_block(kblk):
        return v_ref[pl.ds(pl.multiple_of(kblk * TK, TK), TK), :]

    def finish(o):
        if mode == "diff":
            od = o[:TQ] - scal_ref[0] * o[TQ:]
            od = od * lax.rsqrt(jnp.mean(od * od, axis=-1, keepdims=True) + EPS) * gsub_ref[...]
            o_ref[...] = od.astype(o_ref.dtype)
        else:
            o_ref[...] = jnp.where(lane < 64, o[:TQ], o[TQ:]).astype(o_ref.dtype)

    n_far = jnp.maximum(qi - (n_near - 1), 0)

    def fixed_reference_pass():
        qf = qz_sc[...].astype(f32)
        qnorm = jnp.sqrt(jnp.sum(qf * qf, axis=-1, keepdims=True))
        refn_sc[...] = qnorm * kmax_sc[...] + bias_max
        l_sc[...] = jnp.zeros(l_sc.shape, f32)
        acc_sc[...] = jnp.zeros(acc_sc.shape, f32)

        def accumulate(s, kblk, ref_ref):
            ref = ref_ref[...]
            ps = [jnp.exp2(s[:, c * LANES:(c + 1) * LANES] - ref) for c in range(nc)]
            l_sc[...] += functools.reduce(jnp.add, ps)
            p = jnp.concatenate([x.astype(jnp.bfloat16) for x in ps], axis=1)
            acc_sc[...] += jnp.dot(p, value_block(kblk), preferred_element_type=f32)

        for j in range(n_near):
            def near(j=j):
                accumulate(scores(qi - j) + bias_sc[:, bias_cols(j)], qi - j, refn_sc)
            if j == 0:
                near()
            else:
                pl.when(qi >= j)(near)

        if mode == "diff":
            reff_sc[...] = refn_sc[...] - cfar

            def far(kblk, carry):
                accumulate(scores(kblk), kblk, reff_sc)
                return carry

            def far_quad(i, carry):
                for u in range(FAR_UNROLL):
                    far(FAR_UNROLL * i + u, carry)
                return carry

            lax.fori_loop(0, n_far // FAR_UNROLL, far_quad, 0)
            lax.fori_loop((n_far // FAR_UNROLL) * FAR_UNROLL, n_far, far, 0)

        denom = jnp.sum(l_sc[...], axis=-1, keepdims=True)
        finish(acc_sc[...] / denom)
        return denom

    def exact_pass():
        def update(s, kblk, m_ref, a_ref):
            cols = [s[:, c * LANES:(c + 1) * LANES] for c in range(nc)]
            m_old = m_ref[...]
            m_new = jnp.maximum(m_old, jnp.max(functools.reduce(jnp.maximum, cols), axis=-1, keepdims=True))
            alpha = jnp.exp2(m_old - m_new)
            ps = [jnp.exp2(c - m_new) for c in cols]
            p = jnp.concatenate([x.astype(jnp.bfloat16) for x in ps], axis=1)
            a_ref[:, 0:LANES] = alpha * a_ref[:, 0:LANES] + jnp.dot(p, value_block(kblk), preferred_element_type=f32)
            a_ref[:, LANES:2 * LANES] = alpha * a_ref[:, LANES:2 * LANES] + functools.reduce(jnp.add, ps)
            m_ref[...] = m_new

        m_sc[...] = jnp.full(m_sc.shape, NEG, f32)
        acc2_sc[...] = jnp.zeros(acc2_sc.shape, f32)
        for j in range(n_near):
            def near(j=j):
                update(scores(qi - j) + bias_sc[:, bias_cols(j)], qi - j, m_sc, acc2_sc)
            if j == 0:
                near()
            else:
                pl.when(qi >= j)(near)

        if mode == "diff":
            mf_sc[...] = jnp.full(mf_sc.shape, NEG, f32)
            accf_sc[...] = jnp.zeros(accf_sc.shape, f32)

            def far(kblk, carry):
                update(scores(kblk), kblk, mf_sc, accf_sc)
                return carry

            lax.fori_loop(0, n_far, far, 0)
            mf = mf_sc[...] + cfar
            mn = m_sc[...]
            m = jnp.maximum(mf, mn)
            wf = jnp.exp2(mf - m)
            wn = jnp.exp2(mn - m)
            tot = (jnp.concatenate([wf, wf], axis=1) * accf_sc[...]
                   + jnp.concatenate([wn, wn], axis=1) * acc2_sc[...])
        else:
            tot = acc2_sc[...]
        finish(tot[:, 0:LANES] / jnp.sum(tot[:, LANES:2 * LANES], axis=-1, keepdims=True))

    if mode == "diff":
        denom = fixed_reference_pass()
        pl.when(jnp.logical_not(jnp.min(denom) >= DENOM_FLOOR))(exact_pass)
    else:
        exact_pass()


def _attention(scal, q, k, v, vecs, gsub, *, bsz, s_len, mode):
    n_blk = q.shape[1] // LANES
    nq = s_len // TQ
    n_near = vecs.shape[1]
    assert vecs.shape == (n_blk, n_near, 2, TQ + TK)
    kern = functools.partial(_attn_kernel, mode=mode, n_near=n_near)
    stat = lambda: pltpu.VMEM((2 * TQ, LANES), jnp.float32)
    acc = lambda: pltpu.VMEM((2 * TQ, 2 * LANES), jnp.float32)
    return pl.pallas_call(
        kern,
        grid=(bsz, n_blk, nq),
        in_specs=[
            pl.BlockSpec(memory_space=pltpu.SMEM),
            pl.BlockSpec((TQ, LANES), lambda b, h, i: (b * nq + i, h)),
            pl.BlockSpec((s_len, LANES), lambda b, h, i: (b, h)),
            pl.BlockSpec((s_len, LANES), lambda b, h, i: (b, h)),
            pl.BlockSpec((1, n_near, 2, TQ + TK), lambda b, h, i: (h, 0, 0, 0)),
            pl.BlockSpec((1, LANES), lambda b, h, i: (0, 0)),
        ],
        out_specs=pl.BlockSpec((TQ, LANES), lambda b, h, i: (b * nq + i, h)),
        out_shape=jax.ShapeDtypeStruct(q.shape, jnp.bfloat16),
        scratch_shapes=[pltpu.VMEM((2 * TQ, n_near * TK), jnp.float32), stat(),
                        pltpu.VMEM((2 * TQ, LANES), jnp.bfloat16), stat(), stat(), stat(), stat(),
                        stat(), acc(), stat(), acc()],
        compiler_params=_cparams(("arbitrary", "arbitrary", "arbitrary")),
        name="attn_" + mode,
    )(scal, q, k, v, vecs, gsub)


def _post_kernel(oa_ref, ob_ref, ga_ref, gb_ref, x_ref, wa_ref, wb_ref, wo_ref, g2_ref,
                 wr2_ref, br_ref, x1_ref, hn_ref, route_ref):
    f32 = jnp.float32
    ya = jnp.dot(oa_ref[...], wa_ref[...], preferred_element_type=f32)
    yb = jnp.dot(ob_ref[...], wb_ref[...], preferred_element_type=f32)
    mixed = ga_ref[...].astype(f32) * ya + gb_ref[...].astype(f32) * yb
    x1 = x_ref[...] + jnp.dot(mixed.astype(jnp.bfloat16), wo_ref[...], preferred_element_type=f32)
    x1_ref[...] = x1
    hn = x1 * lax.rsqrt(jnp.mean(x1 * x1, axis=-1, keepdims=True) + EPS) * g2_ref[...]
    hh = hn.astype(jnp.bfloat16)
    bits = lax.bitcast_convert_type(hh.astype(f32), jnp.uint32)
    half = D_MODEL // 2
    hn_ref[...] = (bits[:, :half] & jnp.uint32(0xFFFF0000)) | lax.shift_right_logical(bits[:, half:], jnp.uint32(16))

    hl = (hn - hh.astype(f32)).astype(jnp.bfloat16)
    hw = jnp.dot(hh, wr2_ref[...], preferred_element_type=f32)
    lg = (hw[:, 0:LANES] + hw[:, LANES:2 * LANES]
          + jnp.dot(hl, wr2_ref[:, 0:LANES], preferred_element_type=f32)) + br_ref[...]

    lanei = lax.broadcasted_iota(jnp.int32, lg.shape, 1)
    lanef = lanei.astype(f32)
    big = 999.0
    gmask = lanei < N_GROUPS
    gl = jnp.where(gmask, lg, NEG)
    gm = jnp.max(gl, axis=-1, keepdims=True)
    ge = jnp.where(gmask, jnp.exp(gl - gm), 0.0)
    gp = ge / jnp.sum(ge, axis=-1, keepdims=True)
    p_g = jnp.max(gp, axis=-1, keepdims=True)
    gidx = jnp.min(jnp.where(gmask & (gp == p_g), lanef, big), axis=-1, keepdims=True)
    egrp = lax.shift_right_arithmetic(lanei - N_GROUPS, 3).astype(f32)
    emask = (lanei >= N_GROUPS) & (lanei < N_GROUPS + N_EXPERTS) & (egrp == gidx)
    el = jnp.where(emask, lg, NEG)
    v1 = jnp.max(el, axis=-1, keepdims=True)
    i1 = jnp.min(jnp.where(emask & (el == v1), lanef, big), axis=-1, keepdims=True)
    emask2 = emask & (lanef != i1)
    el2 = jnp.where(emask2, lg, NEG)
    v2 = jnp.max(el2, axis=-1, keepdims=True)
    i2 = jnp.min(jnp.where(emask2 & (el2 == v2), lanef, big), axis=-1, keepdims=True)
    t = jnp.exp(v2 - v1)
    den = 1.0 + t
    w1 = p_g * (1.0 / den)
    w2 = p_g * (t / den)
    route = jnp.where(lanei == 0, i1 - N_GROUPS,
                      jnp.where(lanei == 1, i2 - N_GROUPS,
                                jnp.where(lanei == 2, w1, jnp.where(lanei == 3, w2, 0.0))))
    route_ref[...] = route


def _post(oa, ob, ga, gb, x2, wa, wb, wo, g2, wr2, br):
    t = x2.shape[0]
    n = t // TM_POST
    row = lambda w: pl.BlockSpec((TM_POST, w), lambda i: (i, 0))
    full = lambda a: pl.BlockSpec(a.shape, lambda i: (0,) * a.ndim)
    return pl.pallas_call(
        _post_kernel,
        grid=(n,),
        in_specs=[row(512), row(512), row(D_MODEL), row(D_MODEL), row(D_MODEL),
                  full(wa), full(wb), full(wo), full(g2), full(wr2), full(br)],
        out_specs=[row(D_MODEL), row(D_MODEL // 2), row(LANES)],
        out_shape=[jax.ShapeDtypeStruct((t, D_MODEL), jnp.float32),
                   jax.ShapeDtypeStruct((t, D_MODEL // 2), jnp.uint32),
                   jax.ShapeDtypeStruct((t, LANES), jnp.float32)],
        compiler_params=_cparams(("arbitrary",)),
        name="post",
    )(oa, ob, ga, gb, x2, wa, wb, wo, g2, wr2, br)


def _rank_kernel(route_ref, ltri_ref, utri_ref, dest_ref, cnt_ref, cnt_sc, pstart_sc, base_sc):
    f32 = jnp.float32
    p = pl.program_id(0)
    i = pl.program_id(1)
    route = route_ref[...]
    lanef = lax.broadcasted_iota(jnp.int32, route.shape, 1).astype(f32)
    oh1 = (lanef == route[:, 0:1]).astype(f32)
    oh2 = (lanef == route[:, 1:2]).astype(f32)
    both = oh1 + oh2
    colsum = jnp.sum(both, axis=0, keepdims=True)

    @pl.when((p == 0) & (i == 0))
    def _():
        cnt_sc[...] = jnp.zeros(cnt_sc.shape, f32)

    @pl.when(p == 0)
    def _():
        cnt_sc[...] += colsum
        dest_ref[...] = jnp.zeros(dest_ref.shape, f32)
        cnt_ref[...] = jnp.zeros(cnt_ref.shape, f32)

    @pl.when((p == 1) & (i == 0))
    def _():
        cnt = cnt_sc[...]
        chi = jnp.floor(cnt * (1.0 / 256.0))
        clo = cnt - chi * 256.0
        split = jnp.concatenate([jnp.broadcast_to(chi, (8, LANES)), jnp.broadcast_to(clo, (8, LANES))], axis=0)
        excl = jnp.dot(split.astype(jnp.bfloat16), utri_ref[...], preferred_element_type=f32)
        pstart_sc[...] = excl[0:1] * 256.0 + excl[8:9]
        base_sc[...] = jnp.zeros(base_sc.shape, f32)

    @pl.when(p == 1)
    def _():
        prior = jnp.dot(ltri_ref[...], both.astype(jnp.bfloat16), preferred_element_type=f32)
        slot = prior + base_sc[...] + pstart_sc[...]
        d1 = jnp.sum(oh1 * slot, axis=-1, keepdims=True)
        d2 = jnp.sum(oh2 * slot, axis=-1, keepdims=True)
        dest_ref[...] = jnp.where(lanef == 0.0, d1, jnp.where(lanef == 1.0, d2, 0.0))
        base_sc[...] += colsum
        cnt_ref[...] = jnp.broadcast_to(cnt_sc[...], cnt_ref.shape)


def _rank(route, ltri, utri):
    t = route.shape[0]
    n = t // TM_RANK
    full = lambda a: pl.BlockSpec(a.shape, lambda p, i: (0,) * a.ndim)
    row1 = lambda: pltpu.VMEM((1, LANES), jnp.float32)
    return pl.pallas_call(
        _rank_kernel,
        grid=(2, n),
        in_specs=[pl.BlockSpec((TM_RANK, LANES), lambda p, i: (i, 0)), full(ltri), full(utri)],
        out_specs=[pl.BlockSpec((TM_RANK, LANES), lambda p, i: (i * p, 0)),
                   pl.BlockSpec((8, LANES), lambda p, i: (0, 0))],
        out_shape=[jax.ShapeDtypeStruct((t, LANES), jnp.float32),
                   jax.ShapeDtypeStruct((8, LANES), jnp.float32)],
        scratch_shapes=[row1(), row1(), row1()],
        compiler_params=_cparams(("arbitrary", "arbitrary")),
        name="rank",
    )(route, ltri, utri)


def _dispatch_kernel(dest_ref, hn_ref, xs_ref, sems):
    def row_copy(r, k):
        d = dest_ref[0, 0, 2 * r + k]
        return pltpu.make_async_copy(hn_ref.at[pl.ds(r, 1), :], xs_ref.at[pl.ds(d, 1), :], sems.at[k])

    def issue(r, c):
        row_copy(r, 0).start(priority=0)
        row_copy(r, 1).start(priority=1)
        return c

    lax.fori_loop(0, TM_ROWS, issue, 0, unroll=ROW_UNROLL)
    for k in range(TOP_K):
        pltpu.make_async_copy(hn_ref, xs_ref.at[pl.ds(0, TM_ROWS), :], sems.at[k]).wait()


def _dispatch(dest3, hn, n_rows):
    t, w = hn.shape
    n = t // TM_ROWS
    return pl.pallas_call(
        _dispatch_kernel,
        grid=(n,),
        in_specs=[pl.BlockSpec((1, 1, 2 * TM_ROWS), lambda i: (i, 0, 0), memory_space=pltpu.SMEM),
                  pl.BlockSpec((TM_ROWS, w), lambda i: (i, 0))],
        out_specs=pl.BlockSpec(memory_space=pl.ANY),
        out_shape=jax.ShapeDtypeStruct((n_rows, w), hn.dtype),
        scratch_shapes=[pltpu.SemaphoreType.DMA((2,))],
        compiler_params=_cparams(("arbitrary",)),
        name="dispatch",
    )(dest3, hn)


def _ffn_kernel(tile_ref, exp_ref, lo_ref, hi_ref, cast_ref, init_ref,
                xs_ref, wg_ref, wu_ref, wd_ref, ys_ref, wg_sc, wu_sc, wd_sc):
    v = pl.program_id(0)
    lo = lo_ref[v]
    hi = hi_ref[v]

    @pl.when(init_ref[v] == 1)
    def _():
        ys_ref[...] = jnp.zeros(ys_ref.shape, ys_ref.dtype)

    @pl.when(hi > lo)
    def _():
        @pl.when(cast_ref[v] == 1)
        def _():
            wg_sc[...] = wg_ref[0].astype(jnp.bfloat16)
            wu_sc[...] = wu_ref[0].astype(jnp.bfloat16)
            wd_sc[...] = wd_ref[0].astype(jnp.bfloat16)

        pk = xs_ref[...]
        x = jnp.concatenate(
            [lax.bitcast_convert_type(pk & jnp.uint32(0xFFFF0000), jnp.float32),
             lax.bitcast_convert_type(lax.shift_left(pk, jnp.uint32(16)), jnp.float32)], axis=1).astype(jnp.bfloat16)
        g = jnp.dot(x, wg_sc[...], preferred_element_type=jnp.float32)
        u = jnp.dot(x, wu_sc[...], preferred_element_type=jnp.float32)
        hb = (g * jax.nn.sigmoid(g) * u).astype(jnp.bfloat16)
        y = jnp.dot(hb, wd_sc[...], preferred_element_type=jnp.float32)
        rows = tile_ref[v] * FFN_BLK + lax.broadcasted_iota(jnp.int32, y.shape, 0)
        ys_ref[...] = jnp.where((rows >= lo) & (rows < hi), y, ys_ref[...])


def _ffn(seg, xs, w_gate, w_up, w_down):
    n_rows = xs.shape[0]
    n_seg = seg[0].shape[0]
    grid_spec = pltpu.PrefetchScalarGridSpec(
        num_scalar_prefetch=6,
        grid=(n_seg,),
        in_specs=[
            pl.BlockSpec((FFN_BLK, D_MODEL // 2), lambda v, t, e, *_: (t[v], 0)),
            pl.BlockSpec((1, D_MODEL, D_EXPERT), lambda v, t, e, *_: (e[v], 0, 0)),
            pl.BlockSpec((1, D_MODEL, D_EXPERT), lambda v, t, e, *_: (e[v], 0, 0)),
            pl.BlockSpec((1, D_EXPERT, D_MODEL), lambda v, t, e, *_: (e[v], 0, 0)),
        ],
        out_specs=pl.BlockSpec((FFN_BLK, D_MODEL), lambda v, t, e, *_: (t[v], 0)),
        scratch_shapes=[pltpu.VMEM((D_MODEL, D_EXPERT), jnp.bfloat16),
                        pltpu.VMEM((D_MODEL, D_EXPERT), jnp.bfloat16),
                        pltpu.VMEM((D_EXPERT, D_MODEL), jnp.bfloat16)],
    )
    return pl.pallas_call(
        _ffn_kernel,
        grid_spec=grid_spec,
        out_shape=jax.ShapeDtypeStruct((n_rows, D_MODEL), jnp.float32),
        compiler_params=_cparams(("arbitrary",)),
        name="ffn",
    )(*seg, xs, w_gate, w_up, w_down)


def _segments(counts, n_rows):
    i32 = jnp.int32
    n_tiles = n_rows // FFN_BLK
    n_seg = n_tiles + N_EXPERTS
    tri = jnp.tril(jnp.ones((N_EXPERTS, N_EXPERTS), i32))
    ends = jnp.sum(tri * counts[None, :], axis=1)
    starts = ends - counts
    edges = jnp.arange(n_tiles, dtype=i32) * FFN_BLK
    rank_e = jnp.arange(n_tiles, dtype=i32) + jnp.sum(starts[None, :] <= edges[:, None], axis=1)
    rank_s = jnp.arange(N_EXPERTS, dtype=i32) + jnp.sum(edges[None, :] < starts[:, None], axis=1)
    seg = jnp.arange(n_seg, dtype=i32)
    lo = (jnp.sum(jnp.where(rank_e[None, :] == seg[:, None], edges[None, :], 0), axis=1)
          + jnp.sum(jnp.where(rank_s[None, :] == seg[:, None], starts[None, :], 0), axis=1))
    hi = jnp.concatenate([lo[1:], jnp.array([n_rows], i32)])
    valid = hi > lo
    tile = jnp.minimum(lo // FFN_BLK, n_tiles - 1)
    expert = jnp.minimum(jnp.sum(ends[None, :] <= lo[:, None], axis=1), N_EXPERTS - 1).astype(i32)
    upto = seg[None, :] <= seg[:, None]
    expert = jnp.max(jnp.where(upto & valid[None, :], expert[None, :], 0), axis=1)
    prev_expert = jnp.concatenate([jnp.array([-1], i32), expert[:-1]])
    first_valid = valid & (jnp.sum(jnp.where(upto & valid[None, :], 1, 0), axis=1) == 1)
    cast = valid & ((expert != prev_expert) | first_valid)
    prev_tile = jnp.concatenate([jnp.array([-1], i32), tile[:-1]])
    init = tile != prev_tile
    return (tile.astype(i32), expert, lo.astype(i32), hi.astype(i32), cast.astype(i32), init.astype(i32))


def _combine_kernel(dest_ref, route_ref, x1_ref, ys_ref, out_ref, y0_sc, y1_sc, sems):
    def row_copy(r, k):
        d = dest_ref[0, 0, 2 * r + k]
        dst = y0_sc if k == 0 else y1_sc
        return pltpu.make_async_copy(ys_ref.at[pl.ds(d, 1), :], dst.at[pl.ds(r, 1), :], sems.at[k])

    def issue(r, c):
        row_copy(r, 0).start(priority=0)
        row_copy(r, 1).start(priority=1)
        return c

    lax.fori_loop(0, TM_ROWS, issue, 0, unroll=ROW_UNROLL)
    pltpu.make_async_copy(ys_ref.at[pl.ds(0, TM_ROWS), :], y0_sc, sems.at[0]).wait()
    pltpu.make_async_copy(ys_ref.at[pl.ds(0, TM_ROWS), :], y1_sc, sems.at[1]).wait()
    route = route_ref[...]
    out_ref[...] = x1_ref[...] + (route[:, 2:3] * y0_sc[...] + route[:, 3:4] * y1_sc[...])


def _combine(dest3, route, x1, ys):
    t = x1.shape[0]
    n = t // TM_ROWS
    return pl.pallas_call(
        _combine_kernel,
        grid=(n,),
        in_specs=[pl.BlockSpec((1, 1, 2 * TM_ROWS), lambda i: (i, 0, 0), memory_space=pltpu.SMEM),
                  pl.BlockSpec((TM_ROWS, LANES), lambda i: (i, 0)),
                  pl.BlockSpec((TM_ROWS, D_MODEL), lambda i: (i, 0)),
                  pl.BlockSpec(memory_space=pl.ANY)],
        out_specs=pl.BlockSpec((TM_ROWS, D_MODEL), lambda i: (i, 0)),
        out_shape=jax.ShapeDtypeStruct((t, D_MODEL), jnp.float32),
        scratch_shapes=[pltpu.VMEM((TM_ROWS, D_MODEL), jnp.float32),
                        pltpu.VMEM((TM_ROWS, D_MODEL), jnp.float32),
                        pltpu.SemaphoreType.DMA((2,))],
        compiler_params=_cparams(("arbitrary",)),
        name="combine",
    )(dest3, route, x1, ys)


def _t5_bucket(rel):
    nb = T5_BUCKETS // 2
    max_exact = nb // 2
    side = jnp.where(rel > 0, nb, 0)
    n = jnp.abs(rel)
    nf = jnp.maximum(n, 1).astype(jnp.float32)
    large = max_exact + (jnp.log(nf / max_exact) / math.log(T5_MAX_DIST / max_exact)
                         * (nb - max_exact)).astype(jnp.int32)
    large = jnp.minimum(large, nb - 1)
    return side + jnp.where(n < max_exact, n, large)


def _rel_offsets(j):
    i = jnp.arange(TQ + TK)
    return jnp.where(i < TK, i, i - (TQ + TK)) - j * TK


def _diff_bias_vecs(t5_table):
    vecs = jnp.stack([t5_table[_t5_bucket(_rel_offsets(j))].astype(jnp.float32).T for j in range(2)], axis=1)
    vecs = jnp.stack([vecs, vecs], axis=2)
    far = t5_table[_t5_bucket(jnp.array(-(TK + 1)))].astype(jnp.float32)
    return vecs, far


def _band_bias_vecs(rel_table):
    vecs = jnp.stack([rel_table[jnp.clip(_rel_offsets(j), -B_MAX_REL, B_MAX_REL) + B_MAX_REL].astype(jnp.float32).T
                      for j in range(2)], axis=1)
    return vecs.reshape(B_HEADS // 2, 2, 2, TQ + TK).transpose(0, 2, 1, 3)


def kernel(x, norm1_g, w_in, a_qnorm_g, a_knorm_g, a_lambda, a_subln_g, t5_table, b_qnorm_g, b_knorm_g,
           b_rel_table, w_branch_a, w_branch_b, w_out, norm2_g, w_router_group, b_router_group,
           w_router_expert, b_router_expert, w_gate, w_up, w_down):
    bsz, s_len, _ = x.shape
    n_tok = bsz * s_len
    f32, bf16 = jnp.float32, jnp.bfloat16
    assert s_len % TQ == 0 and TQ == TK and TQ % CHUNK == 0 and n_tok % TM_PROJ == 0
    assert TK >= T5_MAX_DIST and TK >= B_LEFT_CHUNKS * CHUNK
    l = 0
    x2 = x.reshape(n_tok, D_MODEL)

    w = w_in[l]
    qk = w[:, :1024].reshape(D_MODEL, 2, 2, A_HEADS, A_HEAD_DIM)
    qk = qk.transpose(0, 1, 3, 2, 4).reshape(D_MODEL, 1024)
    w_perm = jnp.concatenate([qk, w[:, 1024:]], axis=1).astype(bf16)
    gn = jnp.stack([jnp.tile(a_qnorm_g[l] * (A_HEAD_DIM ** -0.5 * LOG2E), 8), jnp.tile(a_knorm_g[l], 8),
                    jnp.tile(b_qnorm_g[l] * (B_HEAD_DIM ** -0.5 * LOG2E), 8), jnp.tile(b_knorm_g[l], 8)]).astype(f32)
    gmat = jnp.asarray(np.kron(np.eye(4), np.ones((A_HEAD_DIM, A_HEAD_DIM))), dtype=bf16)

    qa, ka, va, qb, kb, vb, ga, gb = _proj(x2, norm1_g[l][None].astype(f32), w_perm, gn, gmat)

    lam_init = 0.8 - 0.6 * math.exp(-0.3 * l)
    lp = a_lambda[l].astype(f32)
    lam = jnp.exp(jnp.sum(lp[0] * lp[1])) - jnp.exp(jnp.sum(lp[2] * lp[3])) + lam_init
    bias_a, far_a = _diff_bias_vecs(t5_table)
    bmax_a = jnp.maximum(jnp.max(bias_a, axis=(1, 2, 3)), far_a)
    scal_a = (jnp.concatenate([lam[None], far_a, bmax_a]) * jnp.array([1.0] + [LOG2E] * (2 * A_HEADS))).astype(f32)
    gsub = (a_subln_g[l] * (1.0 - lam_init))[None].astype(f32)
    oa = _attention(scal_a, qa, ka, va, bias_a * LOG2E, gsub, bsz=bsz, s_len=s_len, mode="diff")

    bias_b = _band_bias_vecs(b_rel_table[l])
    scal_b = (jnp.max(bias_b, axis=(1, 2, 3)) * LOG2E).astype(f32)
    ob = _attention(scal_b, qb, kb, vb, bias_b * LOG2E, gsub, bsz=bsz, s_len=s_len, mode="band")

    wr = jnp.zeros((D_MODEL, LANES), f32)
    wr = wr.at[:, :N_GROUPS].set(w_router_group[l]).at[:, N_GROUPS:N_GROUPS + N_EXPERTS].set(w_router_expert[l])
    wrh = wr.astype(bf16)
    wr2 = jnp.concatenate([wrh, (wr - wrh.astype(f32)).astype(bf16)], axis=1)
    br = jnp.zeros((1, LANES), f32)
    br = br.at[0, :N_GROUPS].set(b_router_group[l]).at[0, N_GROUPS:N_GROUPS + N_EXPERTS].set(b_router_expert[l])
    x1, hn, route = _post(oa, ob, ga, gb, x2, w_branch_a[l].astype(bf16), w_branch_b[l].astype(bf16),
                          w_out[l].astype(bf16), norm2_g[l][None].astype(f32), wr2, br)

    ltri = jnp.asarray(np.tril(np.ones((TM_RANK, TM_RANK)), -1), dtype=bf16)
    utri = jnp.asarray(np.triu(np.ones((LANES, LANES)), 1), dtype=bf16)
    dest, cnt = _rank(route, ltri, utri)

    counts = cnt[0, :N_EXPERTS].astype(jnp.int32)
    n_rows = n_tok * TOP_K
    seg = _segments(counts, n_rows)

    dest3 = dest[:, :TOP_K].astype(jnp.int32).reshape(n_tok // TM_ROWS, 1, TOP_K * TM_ROWS)
    xs = _dispatch(dest3, hn, n_rows)
    ys = _ffn(seg, xs, w_gate[l], w_up[l], w_down[l])
    out = _combine(dest3, route, x1, ys)
    return out.reshape(bsz, s_len, D_MODEL)
```

```python
import functools
import math

import jax
import jax.numpy as jnp
import numpy as np
from jax import lax
from jax.experimental import pallas as pl
from jax.experimental.pallas import tpu as pltpu

D_MODEL = 1024
CHUNK = 64
A_HEADS = 4
A_HEAD_DIM = 64
A_VDIM = 2 * A_HEAD_DIM
B_HEADS = 8
B_HEAD_DIM = 64
B_LEFT_CHUNKS = 8
B_MAX_REL = 128
T5_BUCKETS = 32
T5_MAX_DIST = 128
N_GROUPS = 4
EXPERTS_PER_GROUP = 8
N_EXPERTS = N_GROUPS * EXPERTS_PER_GROUP
TOP_K = 2
D_EXPERT = 512
EPS = 1e-6
NEG = -1e30
LOG2E = 1.0 / math.log(2.0)
DENOM_FLOOR = 2.0 ** -100

CHUNK_SHIFT = CHUNK.bit_length() - 1
assert 1 << CHUNK_SHIFT == CHUNK
LANES = 128
A_W = A_HEADS * 2 * A_HEAD_DIM
B_W = B_HEADS * B_HEAD_DIM
PROJ_W = 4 * 256 + 4 * 512 + 2 * D_MODEL

TM_PROJ = 512
TQ = 512
TK = 512
FAR_UNROLL = 4
TM_POST = 512
TM_RANK = 1024
TM_DISPATCH = 2048
TM_COMBINE = 1024
ROW_UNROLL = 8
FFN_BLK = 256
VMEM_LIMIT = 56 * 1024 * 1024


def _cparams(sem):
    return pltpu.CompilerParams(dimension_semantics=sem, vmem_limit_bytes=VMEM_LIMIT)


def _proj_kernel(x_ref, g1_ref, w_ref, gn_ref, gmat_ref,
                 qa_ref, ka_ref, va_ref, qb_ref, kb_ref, vb_ref, ga_ref, gb_ref):
    x = x_ref[...]
    xn = x * lax.rsqrt(jnp.mean(x * x, axis=-1, keepdims=True) + EPS) * g1_ref[...]
    xn = xn.astype(jnp.bfloat16)

    def slab(c0, width):
        return jnp.dot(xn, w_ref[:, c0:c0 + width], preferred_element_type=jnp.float32)

    def headnorm(y, gi):
        sq = (y * y).astype(jnp.bfloat16)
        half = gmat_ref.shape[0]
        ss = jnp.concatenate([jnp.dot(sq[:, c:c + half], gmat_ref[...], preferred_element_type=jnp.float32)
                              for c in range(0, y.shape[1], half)], axis=1)
        return y * lax.rsqrt(ss * (1.0 / A_HEAD_DIM) + EPS) * gn_ref[gi:gi + 1, :]

    qa_ref[...] = headnorm(slab(0, 512), 0).astype(jnp.bfloat16)
    ka_ref[...] = headnorm(slab(512, 512), 1).astype(jnp.bfloat16)
    va_ref[...] = slab(1024, 512).astype(jnp.bfloat16)
    qb_ref[...] = headnorm(slab(1536, 512), 2).astype(jnp.bfloat16)
    kb_ref[...] = headnorm(slab(2048, 512), 3).astype(jnp.bfloat16)
    vb_ref[...] = slab(2560, 512).astype(jnp.bfloat16)
    for j in range(2):
        ga_ref[:, j * 512:(j + 1) * 512] = jax.nn.sigmoid(slab(3072 + j * 512, 512)).astype(jnp.bfloat16)
        gb_ref[:, j * 512:(j + 1) * 512] = jax.nn.sigmoid(slab(4096 + j * 512, 512)).astype(jnp.bfloat16)


def _proj(x2, g1, w_perm, gn, gmat):
    t = x2.shape[0]
    n = t // TM_PROJ
    row = lambda w: pl.BlockSpec((TM_PROJ, w), lambda i: (i, 0))
    full = lambda a: pl.BlockSpec(a.shape, lambda i: (0,) * a.ndim)
    outs = [jax.ShapeDtypeStruct((t, 512), jnp.bfloat16)] * 6 + [jax.ShapeDtypeStruct((t, D_MODEL), jnp.bfloat16)] * 2
    return pl.pallas_call(
        _proj_kernel,
        grid=(n,),
        in_specs=[row(D_MODEL), full(g1), full(w_perm), full(gn), full(gmat)],
        out_specs=[row(512)] * 6 + [row(D_MODEL)] * 2,
        out_shape=outs,
        compiler_params=_cparams(("arbitrary",)),
        name="proj",
    )(x2, g1, w_perm, gn, gmat)


def _diff_kernel(scal_ref, q_ref, k_ref, v_ref, vec_ref, gsub_ref, o_ref,
                 bias_sc, kmax_sc, qz_sc, refn_sc, reff_sc, l_sc, acc_sc,
                 m_sc, acc2_sc, mf_sc, accf_sc, *, n_near):
    f32 = jnp.float32
    h = pl.program_id(1)
    qi = pl.program_id(2)
    nc = TK // LANES
    n_heads = pl.num_programs(1)

    def bias_cols(j):
        return slice((n_near - 1 - j) * TK, (n_near - j) * TK)

    @pl.when(qi == 0)
    def _():
        qchunk = lax.shift_right_arithmetic(lax.broadcasted_iota(jnp.int32, (TQ, TK), 0), CHUNK_SHIFT)
        kcol = lax.broadcasted_iota(jnp.int32, (TQ, TK), 1)
        for j in range(n_near):
            dchunk = qchunk - lax.shift_right_arithmetic(kcol - j * TK, CHUNK_SHIFT)
            allowed = dchunk >= 0
            for half in range(2):
                vec = jnp.broadcast_to(vec_ref[0, j, half:half + 1, :], (TQ, TQ + TK))
                tile = pltpu.roll(vec, 0, 1, stride=1, stride_axis=0)[:, :TK]
                bias_sc[half * TQ:(half + 1) * TQ, bias_cols(j)] = jnp.where(allowed, tile, NEG)
        kf = k_ref[...].astype(f32)
        ksq = kf * kf
        klane = lax.broadcasted_iota(jnp.int32, ksq.shape, 1)
        for half in range(2):
            sel = (klane < 64) if half == 0 else (klane >= 64)
            norm2 = jnp.max(jnp.sum(jnp.where(sel, ksq, 0.0), axis=-1, keepdims=True), axis=0, keepdims=True)
            kmax_sc[half * TQ:(half + 1) * TQ, :] = jnp.broadcast_to(jnp.sqrt(norm2), (TQ, LANES))

    q = q_ref[...]
    lane = lax.broadcasted_iota(jnp.int32, q.shape, 1)
    zero = jnp.zeros_like(q)
    qz_sc[0:TQ, :] = jnp.where(lane < 64, q, zero)
    qz_sc[TQ:2 * TQ, :] = jnp.where(lane >= 64, q, zero)
    cfar = scal_ref[1 + h]
    bias_max = scal_ref[1 + n_heads + h]

    def scores(kblk):
        k = k_ref[pl.ds(pl.multiple_of(kblk * TK, TK), TK), :]
        return lax.dot_general(qz_sc[...], k, (((1,), (1,)), ((), ())), preferred_element_type=f32)

    def value_block(kblk):
        return v_ref[pl.ds(pl.multiple_of(kblk * TK, TK), TK), :]

    def finish(o):
        od = o[:TQ] - scal_ref[0] * o[TQ:]
        od = od * lax.rsqrt(jnp.mean(od * od, axis=-1, keepdims=True) + EPS) * gsub_ref[...]
        o_ref[...] = od.astype(o_ref.dtype)

    n_far = jnp.maximum(qi - (n_near - 1), 0)

    def fixed_reference_pass():
        qf = qz_sc[...].astype(f32)
        qnorm = jnp.sqrt(jnp.sum(qf * qf, axis=-1, keepdims=True))
        refn_sc[...] = qnorm * kmax_sc[...] + bias_max
        l_sc[...] = jnp.zeros(l_sc.shape, f32)
        acc_sc[...] = jnp.zeros(acc_sc.shape, f32)

        def accumulate(s, kblk, ref_ref):
            ref = ref_ref[...]
            ps = [jnp.exp2(s[:, c * LANES:(c + 1) * LANES] - ref) for c in range(nc)]
            l_sc[...] += functools.reduce(jnp.add, ps)
            p = jnp.concatenate([x.astype(jnp.bfloat16) for x in ps], axis=1)
            acc_sc[...] += jnp.dot(p, value_block(kblk), preferred_element_type=f32)

        for j in range(n_near):
            def near(j=j):
                accumulate(scores(qi - j) + bias_sc[:, bias_cols(j)], qi - j, refn_sc)
            if j == 0:
                near()
            else:
                pl.when(qi >= j)(near)

        reff_sc[...] = refn_sc[...] - cfar

        def far(kblk, carry):
            accumulate(scores(kblk), kblk, reff_sc)
            return carry

        def far_quad(i, carry):
            for u in range(FAR_UNROLL):
                far(FAR_UNROLL * i + u, carry)
            return carry

        lax.fori_loop(0, n_far // FAR_UNROLL, far_quad, 0)
        lax.fori_loop((n_far // FAR_UNROLL) * FAR_UNROLL, n_far, far, 0)

        denom = jnp.sum(l_sc[...], axis=-1, keepdims=True)
        finish(acc_sc[...] / denom)
        return denom

    def exact_pass():
        def update(s, kblk, m_ref, a_ref):
            cols = [s[:, c * LANES:(c + 1) * LANES] for c in range(nc)]
            m_old = m_ref[...]
            m_new = jnp.maximum(m_old, jnp.max(functools.reduce(jnp.maximum, cols), axis=-1, keepdims=True))
            alpha = jnp.exp2(m_old - m_new)
            ps = [jnp.exp2(c - m_new) for c in cols]
            p = jnp.concatenate([x.astype(jnp.bfloat16) for x in ps], axis=1)
            a_ref[:, 0:LANES] = alpha * a_ref[:, 0:LANES] + jnp.dot(p, value_block(kblk), preferred_element_type=f32)
            a_ref[:, LANES:2 * LANES] = alpha * a_ref[:, LANES:2 * LANES] + functools.reduce(jnp.add, ps)
            m_ref[...] = m_new

        m_sc[...] = jnp.full(m_sc.shape, NEG, f32)
        acc2_sc[...] = jnp.zeros(acc2_sc.shape, f32)
        for j in range(n_near):
            def near(j=j):
                update(scores(qi - j) + bias_sc[:, bias_cols(j)], qi - j, m_sc, acc2_sc)
            if j == 0:
                near()
            else:
                pl.when(qi >= j)(near)

        mf_sc[...] = jnp.full(mf_sc.shape, NEG, f32)
        accf_sc[...] = jnp.zeros(accf_sc.shape, f32)

        def far(kblk, carry):
            update(scores(kblk), kblk, mf_sc, accf_sc)
            return carry

        lax.fori_loop(0, n_far, far, 0)
        mf = mf_sc[...] + cfar
        mn = m_sc[...]
        m = jnp.maximum(mf, mn)
        wf = jnp.exp2(mf - m)
        wn = jnp.exp2(mn - m)
        tot = (jnp.concatenate([wf, wf], axis=1) * accf_sc[...]
               + jnp.concatenate([wn, wn], axis=1) * acc2_sc[...])
        finish(tot[:, 0:LANES] / jnp.sum(tot[:, LANES:2 * LANES], axis=-1, keepdims=True))

    denom = fixed_reference_pass()
    pl.when(jnp.logical_not(jnp.min(denom) >= DENOM_FLOOR))(exact_pass)


def _diff_attention(scal, q, k, v, vecs, gsub, *, bsz, s_len):
    n_blk = q.shape[1] // LANES
    nq = s_len // TQ
    n_near = vecs.shape[1]
    assert vecs.shape == (n_blk, n_near, 2, TQ + TK)
    kern = functools.partial(_diff_kernel, n_near=n_near)
    stat = lambda: pltpu.VMEM((2 * TQ, LANES), jnp.float32)
    acc = lambda: pltpu.VMEM((2 * TQ, 2 * LANES), jnp.float32)
    return pl.pallas_call(
        kern,
        grid=(bsz, n_blk, nq),
        in_specs=[
            pl.BlockSpec(memory_space=pltpu.SMEM),
            pl.BlockSpec((TQ, LANES), lambda b, h, i: (b * nq + i, h)),
            pl.BlockSpec((s_len, LANES), lambda b, h, i: (b, h)),
            pl.BlockSpec((s_len, LANES), lambda b, h, i: (b, h)),
            pl.BlockSpec((1, n_near, 2, TQ + TK), lambda b, h, i: (h, 0, 0, 0)),
            pl.BlockSpec((1, LANES), lambda b, h, i: (0, 0)),
        ],
        out_specs=pl.BlockSpec((TQ, LANES), lambda b, h, i: (b * nq + i, h)),
        out_shape=jax.ShapeDtypeStruct(q.shape, jnp.bfloat16),
        scratch_shapes=[pltpu.VMEM((2 * TQ, n_near * TK), jnp.float32), stat(),
                        pltpu.VMEM((2 * TQ, LANES), jnp.bfloat16), stat(), stat(), stat(), stat(),
                        stat(), acc(), stat(), acc()],
        compiler_params=_cparams(("arbitrary", "arbitrary", "arbitrary")),
        name="attn_diff",
    )(scal, q, k, v, vecs, gsub)


BAND_HQ = TQ // 2
BAND_W = B_LEFT_CHUNKS * CHUNK + BAND_HQ


def _band_kernel(q_ref, k_ref, v_ref, vec_ref, o_ref, bias_sc, qz_sc):
    f32 = jnp.float32
    qi = pl.program_id(2)
    hq, w = BAND_HQ, BAND_W
    back = w - hq

    @pl.when(qi == 0)
    def _():
        qchunk = lax.shift_right_arithmetic(lax.broadcasted_iota(jnp.int32, (hq, w), 0), CHUNK_SHIFT)
        kchunk = lax.shift_right_arithmetic(lax.broadcasted_iota(jnp.int32, (hq, w), 1) - back, CHUNK_SHIFT)
        dchunk = qchunk - kchunk
        allowed = (dchunk >= 0) & (dchunk <= B_LEFT_CHUNKS)
        for half in range(2):
            vec = jnp.broadcast_to(vec_ref[0, half:half + 1, :], (hq, hq + w))
            tile = pltpu.roll(vec, 0, 1, stride=1, stride_axis=0)[:, :w]
            bias_sc[half * hq:(half + 1) * hq, :] = jnp.where(allowed, tile, NEG)

    q = q_ref[...]
    lane = lax.broadcasted_iota(jnp.int32, (hq, LANES), 1)
    zero = jnp.zeros((hq, LANES), q.dtype)
    for u in range(2):
        qu = q[u * hq:(u + 1) * hq, :]
        qz_sc[(2 * u) * hq:(2 * u + 1) * hq, :] = jnp.where(lane < 64, qu, zero)
        qz_sc[(2 * u + 1) * hq:(2 * u + 2) * hq, :] = jnp.where(lane >= 64, qu, zero)

    def sub(u, k0, n_keys):
        c0 = w - n_keys
        k = k_ref[pl.ds(k0, n_keys), :]
        s = lax.dot_general(qz_sc[2 * u * hq:(2 * u + 2) * hq, :], k, (((1,), (1,)), ((), ())),
                            preferred_element_type=f32) + bias_sc[:, c0:w]
        cols = [s[:, c * LANES:(c + 1) * LANES] for c in range(n_keys // LANES)]
        m = jnp.max(functools.reduce(jnp.maximum, cols), axis=-1, keepdims=True)
        ps = [jnp.exp2(c - m) for c in cols]
        denom = jnp.sum(functools.reduce(jnp.add, ps), axis=-1, keepdims=True)
        p = jnp.concatenate([x.astype(jnp.bfloat16) for x in ps], axis=1)
        o = jnp.dot(p, v_ref[pl.ds(k0, n_keys), :], preferred_element_type=f32) / denom
        o_ref[u * hq:(u + 1) * hq, :] = jnp.where(lane < 64, o[:hq], o[hq:]).astype(o_ref.dtype)

    @pl.when(qi == 0)
    def _():
        sub(0, 0, hq)
        sub(1, 0, 2 * hq)

    @pl.when(qi > 0)
    def _():
        start = pl.multiple_of(qi * TQ - back, hq)
        sub(0, start, w)
        sub(1, pl.multiple_of(start + hq, hq), w)


def _band_attention(q, k, v, vecs, *, bsz, s_len):
    n_blk = q.shape[1] // LANES
    nq = s_len // TQ
    assert vecs.shape == (n_blk, 2, BAND_HQ + BAND_W)
    return pl.pallas_call(
        _band_kernel,
        grid=(bsz, n_blk, nq),
        in_specs=[
            pl.BlockSpec((TQ, LANES), lambda b, h, i: (b * nq + i, h)),
            pl.BlockSpec((s_len, LANES), lambda b, h, i: (b, h)),
            pl.BlockSpec((s_len, LANES), lambda b, h, i: (b, h)),
            pl.BlockSpec((1, 2, BAND_HQ + BAND_W), lambda b, h, i: (h, 0, 0)),
        ],
        out_specs=pl.BlockSpec((TQ, LANES), lambda b, h, i: (b * nq + i, h)),
        out_shape=jax.ShapeDtypeStruct(q.shape, jnp.bfloat16),
        scratch_shapes=[pltpu.VMEM((2 * BAND_HQ, BAND_W), jnp.float32),
                        pltpu.VMEM((2 * TQ, LANES), jnp.bfloat16)],
        compiler_params=_cparams(("arbitrary", "arbitrary", "arbitrary")),
        name="attn_band",
    )(q, k, v, vecs)


def _post_kernel(oa_ref, ob_ref, ga_ref, gb_ref, x_ref, wa_ref, wb_ref, wo_ref, g2_ref,
                 wr2_ref, br_ref, x1_ref, hn_ref, route_ref):
    f32 = jnp.float32
    ya = jnp.dot(oa_ref[...], wa_ref[...], preferred_element_type=f32)
    yb = jnp.dot(ob_ref[...], wb_ref[...], preferred_element_type=f32)
    mixed = ga_ref[...].astype(f32) * ya + gb_ref[...].astype(f32) * yb
    x1 = x_ref[...] + jnp.dot(mixed.astype(jnp.bfloat16), wo_ref[...], preferred_element_type=f32)
    x1_ref[...] = x1
    hn = x1 * lax.rsqrt(jnp.mean(x1 * x1, axis=-1, keepdims=True) + EPS) * g2_ref[...]
    hh = hn.astype(jnp.bfloat16)
    bits = lax.bitcast_convert_type(hh.astype(f32), jnp.uint32)
    half = D_MODEL // 2
    hn_ref[...] = (bits[:, :half] & jnp.uint32(0xFFFF0000)) | lax.shift_right_logical(bits[:, half:], jnp.uint32(16))

    hl = (hn - hh.astype(f32)).astype(jnp.bfloat16)
    hw = jnp.dot(hh, wr2_ref[...], preferred_element_type=f32)
    lg = (hw[:, 0:LANES] + hw[:, LANES:2 * LANES]
          + jnp.dot(hl, wr2_ref[:, 0:LANES], preferred_element_type=f32)) + br_ref[...]

    lanei = lax.broadcasted_iota(jnp.int32, lg.shape, 1)
    lanef = lanei.astype(f32)
    big = 999.0
    gmask = lanei < N_GROUPS
    gl = jnp.where(gmask, lg, NEG)
    gm = jnp.max(gl, axis=-1, keepdims=True)
    ge = jnp.where(gmask, jnp.exp(gl - gm), 0.0)
    gp = ge / jnp.sum(ge, axis=-1, keepdims=True)
    p_g = jnp.max(gp, axis=-1, keepdims=True)
    gidx = jnp.min(jnp.where(gmask & (gp == p_g), lanef, big), axis=-1, keepdims=True)
    egrp = lax.shift_right_arithmetic(lanei - N_GROUPS, 3).astype(f32)
    emask = (lanei >= N_GROUPS) & (lanei < N_GROUPS + N_EXPERTS) & (egrp == gidx)
    el = jnp.where(emask, lg, NEG)
    v1 = jnp.max(el, axis=-1, keepdims=True)
    i1 = jnp.min(jnp.where(emask & (el == v1), lanef, big), axis=-1, keepdims=True)
    emask2 = emask & (lanef != i1)
    el2 = jnp.where(emask2, lg, NEG)
    v2 = jnp.max(el2, axis=-1, keepdims=True)
    i2 = jnp.min(jnp.where(emask2 & (el2 == v2), lanef, big), axis=-1, keepdims=True)
    t = jnp.exp(v2 - v1)
    den = 1.0 + t
    w1 = p_g * (1.0 / den)
    w2 = p_g * (t / den)
    route = jnp.where(lanei == 0, i1 - N_GROUPS,
                      jnp.where(lanei == 1, i2 - N_GROUPS,
                                jnp.where(lanei == 2, w1, jnp.where(lanei == 3, w2, 0.0))))
    route_ref[...] = route


def _post(oa, ob, ga, gb, x2, wa, wb, wo, g2, wr2, br):
    t = x2.shape[0]
    n = t // TM_POST
    row = lambda w: pl.BlockSpec((TM_POST, w), lambda i: (i, 0))
    full = lambda a: pl.BlockSpec(a.shape, lambda i: (0,) * a.ndim)
    return pl.pallas_call(
        _post_kernel,
        grid=(n,),
        in_specs=[row(512), row(512), row(D_MODEL), row(D_MODEL), row(D_MODEL),
                  full(wa), full(wb), full(wo), full(g2), full(wr2), full(br)],
        out_specs=[row(D_MODEL), row(D_MODEL // 2), row(LANES)],
        out_shape=[jax.ShapeDtypeStruct((t, D_MODEL), jnp.float32),
                   jax.ShapeDtypeStruct((t, D_MODEL // 2), jnp.uint32),
                   jax.ShapeDtypeStruct((t, LANES), jnp.float32)],
        compiler_params=_cparams(("arbitrary",)),
        name="post",
    )(oa, ob, ga, gb, x2, wa, wb, wo, g2, wr2, br)


def _rank_kernel(route_ref, ltri_ref, utri_ref, dest_ref, cnt_ref, cnt_sc, pstart_sc, base_sc):
    f32 = jnp.float32
    p = pl.program_id(0)
    i = pl.program_id(1)
    route = route_ref[...]
    lanef = lax.broadcasted_iota(jnp.int32, route.shape, 1).astype(f32)
    oh1 = (lanef == route[:, 0:1]).astype(f32)
    oh2 = (lanef == route[:, 1:2]).astype(f32)
    both = oh1 + oh2
    colsum = jnp.sum(both, axis=0, keepdims=True)

    @pl.when((p == 0) & (i == 0))
    def _():
        cnt_sc[...] = jnp.zeros(cnt_sc.shape, f32)

    @pl.when(p == 0)
    def _():
        cnt_sc[...] += colsum
        dest_ref[...] = jnp.zeros(dest_ref.shape, f32)
        cnt_ref[...] = jnp.zeros(cnt_ref.shape, f32)

    @pl.when((p == 1) & (i == 0))
    def _():
        cnt = cnt_sc[...]
        chi = jnp.floor(cnt * (1.0 / 256.0))
        clo = cnt - chi * 256.0
        split = jnp.concatenate([jnp.broadcast_to(chi, (8, LANES)), jnp.broadcast_to(clo, (8, LANES))], axis=0)
        excl = jnp.dot(split.astype(jnp.bfloat16), utri_ref[...], preferred_element_type=f32)
        pstart_sc[...] = excl[0:1] * 256.0 + excl[8:9]
        base_sc[...] = jnp.zeros(base_sc.shape, f32)

    @pl.when(p == 1)
    def _():
        prior = jnp.dot(ltri_ref[...], both.astype(jnp.bfloat16), preferred_element_type=f32)
        slot = prior + base_sc[...] + pstart_sc[...]
        d1 = jnp.sum(oh1 * slot, axis=-1, keepdims=True)
        d2 = jnp.sum(oh2 * slot, axis=-1, keepdims=True)
        dest_ref[...] = jnp.where(lanef == 0.0, d1, jnp.where(lanef == 1.0, d2, 0.0))
        base_sc[...] += colsum
        cnt_ref[...] = jnp.broadcast_to(cnt_sc[...], cnt_ref.shape)


def _rank(route, ltri, utri):
    t = route.shape[0]
    n = t // TM_RANK
    full = lambda a: pl.BlockSpec(a.shape, lambda p, i: (0,) * a.ndim)
    row1 = lambda: pltpu.VMEM((1, LANES), jnp.float32)
    return pl.pallas_call(
        _rank_kernel,
        grid=(2, n),
        in_specs=[pl.BlockSpec((TM_RANK, LANES), lambda p, i: (i, 0)), full(ltri), full(utri)],
        out_specs=[pl.BlockSpec((TM_RANK, LANES), lambda p, i: (i * p, 0)),
                   pl.BlockSpec((8, LANES), lambda p, i: (0, 0))],
        out_shape=[jax.ShapeDtypeStruct((t, LANES), jnp.float32),
                   jax.ShapeDtypeStruct((8, LANES), jnp.float32)],
        scratch_shapes=[row1(), row1(), row1()],
        compiler_params=_cparams(("arbitrary", "arbitrary")),
        name="rank",
    )(route, ltri, utri)


def _dispatch_kernel(dest_ref, hn_ref, xs_ref, sems):
    def row_copy(r, k):
        d = dest_ref[0, 0, 2 * r + k]
        return pltpu.make_async_copy(hn_ref.at[pl.ds(r, 1), :], xs_ref.at[pl.ds(d, 1), :], sems.at[k])

    def issue(r, c):
        row_copy(r, 0).start(priority=0)
        row_copy(r, 1).start(priority=1)
        return c

    lax.fori_loop(0, TM_DISPATCH, issue, 0, unroll=ROW_UNROLL)
    for k in range(TOP_K):
        pltpu.make_async_copy(hn_ref, xs_ref.at[pl.ds(0, TM_DISPATCH), :], sems.at[k]).wait()


def _dispatch(dest3, hn, n_rows):
    t, w = hn.shape
    n = t // TM_DISPATCH
    return pl.pallas_call(
        _dispatch_kernel,
        grid=(n,),
        in_specs=[pl.BlockSpec((1, 1, 2 * TM_DISPATCH), lambda i: (i, 0, 0), memory_space=pltpu.SMEM),
                  pl.BlockSpec((TM_DISPATCH, w), lambda i: (i, 0))],
        out_specs=pl.BlockSpec(memory_space=pl.ANY),
        out_shape=jax.ShapeDtypeStruct((n_rows, w), hn.dtype),
        scratch_shapes=[pltpu.SemaphoreType.DMA((2,))],
        compiler_params=_cparams(("arbitrary",)),
        name="dispatch",
    )(dest3, hn)


def _ffn_kernel(tile_ref, exp_ref, lo_ref, hi_ref, cast_ref, init_ref,
                xs_ref, wg_ref, wu_ref, wd_ref, ys_ref, wg_sc, wu_sc, wd_sc):
    v = pl.program_id(0)
    lo = lo_ref[v]
    hi = hi_ref[v]

    @pl.when(init_ref[v] == 1)
    def _():
        ys_ref[...] = jnp.zeros(ys_ref.shape, ys_ref.dtype)

    @pl.when(hi > lo)
    def _():
        @pl.when(cast_ref[v] == 1)
        def _():
            wg_sc[...] = wg_ref[0].astype(jnp.bfloat16)
            wu_sc[...] = wu_ref[0].astype(jnp.bfloat16)
            wd_sc[...] = wd_ref[0].astype(jnp.bfloat16)

        pk = xs_ref[...]
        x = jnp.concatenate(
            [lax.bitcast_convert_type(pk & jnp.uint32(0xFFFF0000), jnp.float32),
             lax.bitcast_convert_type(lax.shift_left(pk, jnp.uint32(16)), jnp.float32)], axis=1).astype(jnp.bfloat16)
        g = jnp.dot(x, wg_sc[...], preferred_element_type=jnp.float32)
        u = jnp.dot(x, wu_sc[...], preferred_element_type=jnp.float32)
        hb = (g * jax.nn.sigmoid(g) * u).astype(jnp.bfloat16)
        y = jnp.dot(hb, wd_sc[...], preferred_element_type=jnp.float32)
        rows = tile_ref[v] * FFN_BLK + lax.broadcasted_iota(jnp.int32, y.shape, 0)
        ys_ref[...] = jnp.where((rows >= lo) & (rows < hi), y, ys_ref[...])


def _ffn(seg, xs, w_gate, w_up, w_down):
    n_rows = xs.shape[0]
    n_seg = seg[0].shape[0]
    grid_spec = pltpu.PrefetchScalarGridSpec(
        num_scalar_prefetch=6,
        grid=(n_seg,),
        in_specs=[
            pl.BlockSpec((FFN_BLK, D_MODEL // 2), lambda v, t, e, *_: (t[v], 0)),
            pl.BlockSpec((1, D_MODEL, D_EXPERT), lambda v, t, e, *_: (e[v], 0, 0)),
            pl.BlockSpec((1, D_MODEL, D_EXPERT), lambda v, t, e, *_: (e[v], 0, 0)),
            pl.BlockSpec((1, D_EXPERT, D_MODEL), lambda v, t, e, *_: (e[v], 0, 0)),
        ],
        out_specs=pl.BlockSpec((FFN_BLK, D_MODEL), lambda v, t, e, *_: (t[v], 0)),
        scratch_shapes=[pltpu.VMEM((D_MODEL, D_EXPERT), jnp.bfloat16),
                        pltpu.VMEM((D_MODEL, D_EXPERT), jnp.bfloat16),
                        pltpu.VMEM((D_EXPERT, D_MODEL), jnp.bfloat16)],
    )
    return pl.pallas_call(
        _ffn_kernel,
        grid_spec=grid_spec,
        out_shape=jax.ShapeDtypeStruct((n_rows, D_MODEL), jnp.float32),
        compiler_params=_cparams(("arbitrary",)),
        name="ffn",
    )(*seg, xs, w_gate, w_up, w_down)


def _segments(counts, n_rows):
    i32 = jnp.int32
    n_tiles = n_rows // FFN_BLK
    n_seg = n_tiles + N_EXPERTS
    tri = jnp.tril(jnp.ones((N_EXPERTS, N_EXPERTS), i32))
    ends = jnp.sum(tri * counts[None, :], axis=1)
    starts = ends - counts
    edges = jnp.arange(n_tiles, dtype=i32) * FFN_BLK
    rank_e = jnp.arange(n_tiles, dtype=i32) + jnp.sum(starts[None, :] <= edges[:, None], axis=1)
    rank_s = jnp.arange(N_EXPERTS, dtype=i32) + jnp.sum(edges[None, :] < starts[:, None], axis=1)
    seg = jnp.arange(n_seg, dtype=i32)
    lo = (jnp.sum(jnp.where(rank_e[None, :] == seg[:, None], edges[None, :], 0), axis=1)
          + jnp.sum(jnp.where(rank_s[None, :] == seg[:, None], starts[None, :], 0), axis=1))
    hi = jnp.concatenate([lo[1:], jnp.array([n_rows], i32)])
    valid = hi > lo
    tile = jnp.minimum(lo // FFN_BLK, n_tiles - 1)
    expert = jnp.minimum(jnp.sum(ends[None, :] <= lo[:, None], axis=1), N_EXPERTS - 1).astype(i32)
    upto = seg[None, :] <= seg[:, None]
    expert = jnp.max(jnp.where(upto & valid[None, :], expert[None, :], 0), axis=1)
    prev_expert = jnp.concatenate([jnp.array([-1], i32), expert[:-1]])
    first_valid = valid & (jnp.sum(jnp.where(upto & valid[None, :], 1, 0), axis=1) == 1)
    cast = valid & ((expert != prev_expert) | first_valid)
    prev_tile = jnp.concatenate([jnp.array([-1], i32), tile[:-1]])
    init = tile != prev_tile
    return (tile.astype(i32), expert, lo.astype(i32), hi.astype(i32), cast.astype(i32), init.astype(i32))


def _combine_kernel(dest_ref, route_ref, x1_ref, ys_ref, out_ref, y0_sc, y1_sc, sems):
    def row_copy(r, k):
        d = dest_ref[0, 0, 2 * r + k]
        dst = y0_sc if k == 0 else y1_sc
        return pltpu.make_async_copy(ys_ref.at[pl.ds(d, 1), :], dst.at[pl.ds(r, 1), :], sems.at[k])

    def issue(r, c):
        row_copy(r, 0).start(priority=0)
        row_copy(r, 1).start(priority=1)
        return c

    lax.fori_loop(0, TM_COMBINE, issue, 0, unroll=ROW_UNROLL)
    pltpu.make_async_copy(ys_ref.at[pl.ds(0, TM_COMBINE), :], y0_sc, sems.at[0]).wait()
    pltpu.make_async_copy(ys_ref.at[pl.ds(0, TM_COMBINE), :], y1_sc, sems.at[1]).wait()
    route = route_ref[...]
    out_ref[...] = x1_ref[...] + (route[:, 2:3] * y0_sc[...] + route[:, 3:4] * y1_sc[...])


def _combine(dest3, route, x1, ys):
    t = x1.shape[0]
    n = t // TM_COMBINE
    return pl.pallas_call(
        _combine_kernel,
        grid=(n,),
        in_specs=[pl.BlockSpec((1, 1, 2 * TM_COMBINE), lambda i: (i, 0, 0), memory_space=pltpu.SMEM),
                  pl.BlockSpec((TM_COMBINE, LANES), lambda i: (i, 0)),
                  pl.BlockSpec((TM_COMBINE, D_MODEL), lambda i: (i, 0)),
                  pl.BlockSpec(memory_space=pl.ANY)],
        out_specs=pl.BlockSpec((TM_COMBINE, D_MODEL), lambda i: (i, 0)),
        out_shape=jax.ShapeDtypeStruct((t, D_MODEL), jnp.float32),
        scratch_shapes=[pltpu.VMEM((TM_COMBINE, D_MODEL), jnp.float32),
                        pltpu.VMEM((TM_COMBINE, D_MODEL), jnp.float32),
                        pltpu.SemaphoreType.DMA((2,))],
        compiler_params=_cparams(("arbitrary",)),
        name="combine",
    )(dest3, route, x1, ys)


def _t5_bucket(rel):
    nb = T5_BUCKETS // 2
    max_exact = nb // 2
    side = jnp.where(rel > 0, nb, 0)
    n = jnp.abs(rel)
    nf = jnp.maximum(n, 1).astype(jnp.float32)
    large = max_exact + (jnp.log(nf / max_exact) / math.log(T5_MAX_DIST / max_exact)
                         * (nb - max_exact)).astype(jnp.int32)
    large = jnp.minimum(large, nb - 1)
    return side + jnp.where(n < max_exact, n, large)


def _rel_offsets(j):
    i = jnp.arange(TQ + TK)
    return jnp.where(i < TK, i, i - (TQ + TK)) - j * TK


def _diff_bias_vecs(t5_table):
    vecs = jnp.stack([t5_table[_t5_bucket(_rel_offsets(j))].astype(jnp.float32).T for j in range(2)], axis=1)
    vecs = jnp.stack([vecs, vecs], axis=2)
    far = t5_table[_t5_bucket(jnp.array(-(TK + 1)))].astype(jnp.float32)
    return vecs, far


def _band_bias_vecs(rel_table):
    i = jnp.arange(BAND_HQ + BAND_W)
    rel = jnp.where(i < BAND_W, i, i - (BAND_HQ + BAND_W)) - (BAND_W - BAND_HQ)
    vecs = rel_table[jnp.clip(rel, -B_MAX_REL, B_MAX_REL) + B_MAX_REL].astype(jnp.float32).T
    return vecs.reshape(B_HEADS // 2, 2, BAND_HQ + BAND_W)


def kernel(x, norm1_g, w_in, a_qnorm_g, a_knorm_g, a_lambda, a_subln_g, t5_table, b_qnorm_g, b_knorm_g,
           b_rel_table, w_branch_a, w_branch_b, w_out, norm2_g, w_router_group, b_router_group,
           w_router_expert, b_router_expert, w_gate, w_up, w_down):
    bsz, s_len, _ = x.shape
    n_tok = bsz * s_len
    f32, bf16 = jnp.float32, jnp.bfloat16
    assert s_len % TQ == 0 and TQ == TK and TQ % CHUNK == 0 and n_tok % TM_PROJ == 0
    assert TK >= T5_MAX_DIST and BAND_HQ % CHUNK == 0 and BAND_W % LANES == 0 and BAND_W - BAND_HQ <= TQ
    l = 0
    x2 = x.reshape(n_tok, D_MODEL)

    w = w_in[l]
    qk = w[:, :1024].reshape(D_MODEL, 2, 2, A_HEADS, A_HEAD_DIM)
    qk = qk.transpose(0, 1, 3, 2, 4).reshape(D_MODEL, 1024)
    w_perm = jnp.concatenate([qk, w[:, 1024:]], axis=1).astype(bf16)
    gn = jnp.stack([jnp.tile(a_qnorm_g[l] * (A_HEAD_DIM ** -0.5 * LOG2E), 8), jnp.tile(a_knorm_g[l], 8),
                    jnp.tile(b_qnorm_g[l] * (B_HEAD_DIM ** -0.5 * LOG2E), 8), jnp.tile(b_knorm_g[l], 8)]).astype(f32)
    gmat = jnp.asarray(np.kron(np.eye(4), np.ones((A_HEAD_DIM, A_HEAD_DIM))), dtype=bf16)

    qa, ka, va, qb, kb, vb, ga, gb = _proj(x2, norm1_g[l][None].astype(f32), w_perm, gn, gmat)

    lam_init = 0.8 - 0.6 * math.exp(-0.3 * l)
    lp = a_lambda[l].astype(f32)
    lam = jnp.exp(jnp.sum(lp[0] * lp[1])) - jnp.exp(jnp.sum(lp[2] * lp[3])) + lam_init
    bias_a, far_a = _diff_bias_vecs(t5_table)
    bmax_a = jnp.maximum(jnp.max(bias_a, axis=(1, 2, 3)), far_a)
    scal_a = (jnp.concatenate([lam[None], far_a, bmax_a]) * jnp.array([1.0] + [LOG2E] * (2 * A_HEADS))).astype(f32)
    gsub = (a_subln_g[l] * (1.0 - lam_init))[None].astype(f32)
    oa = _diff_attention(scal_a, qa, ka, va, bias_a * LOG2E, gsub, bsz=bsz, s_len=s_len)

    bias_b = _band_bias_vecs(b_rel_table[l])
    ob = _band_attention(qb, kb, vb, bias_b * LOG2E, bsz=bsz, s_len=s_len)

    wr = jnp.zeros((D_MODEL, LANES), f32)
    wr = wr.at[:, :N_GROUPS].set(w_router_group[l]).at[:, N_GROUPS:N_GROUPS + N_EXPERTS].set(w_router_expert[l])
    wrh = wr.astype(bf16)
    wr2 = jnp.concatenate([wrh, (wr - wrh.astype(f32)).astype(bf16)], axis=1)
    br = jnp.zeros((1, LANES), f32)
    br = br.at[0, :N_GROUPS].set(b_router_group[l]).at[0, N_GROUPS:N_GROUPS + N_EXPERTS].set(b_router_expert[l])
    x1, hn, route = _post(oa, ob, ga, gb, x2, w_branch_a[l].astype(bf16), w_branch_b[l].astype(bf16),
                          w_out[l].astype(bf16), norm2_g[l][None].astype(f32), wr2, br)

    ltri = jnp.asarray(np.tril(np.ones((TM_RANK, TM_RANK)), -1), dtype=bf16)
    utri = jnp.asarray(np.triu(np.ones((LANES, LANES)), 1), dtype=bf16)
    dest, cnt = _rank(route, ltri, utri)

    counts = cnt[0, :N_EXPERTS].astype(jnp.int32)
    n_rows = n_tok * TOP_K
    seg = _segments(counts, n_rows)

    dest2 = dest[:, :TOP_K].astype(jnp.int32)
    xs = _dispatch(dest2.reshape(n_tok // TM_DISPATCH, 1, TOP_K * TM_DISPATCH), hn, n_rows)
    ys = _ffn(seg, xs, w_gate[l], w_up[l], w_down[l])
    out = _combine(dest2.reshape(n_tok // TM_COMBINE, 1, TOP_K * TM_COMBINE), route, x1, ys)
    return out.reshape(bsz, s_len, D_MODEL)
```

```python
import functools
import math

import jax
import jax.numpy as jnp
import numpy as np
from jax import lax
from jax.experimental import pallas as pl
from jax.experimental.pallas import tpu as pltpu

D_MODEL = 1024
CHUNK = 64
A_HEADS = 4
A_HEAD_DIM = 64
A_VDIM = 2 * A_HEAD_DIM
B_HEADS = 8
B_HEAD_DIM = 64
B_LEFT_CHUNKS = 8
B_MAX_REL = 128
T5_BUCKETS = 32
T5_MAX_DIST = 128
N_GROUPS = 4
EXPERTS_PER_GROUP = 8
N_EXPERTS = N_GROUPS * EXPERTS_PER_GROUP
TOP_K = 2
D_EXPERT = 512
EPS = 1e-6
NEG = -1e30
LOG2E = 1.0 / math.log(2.0)
DENOM_FLOOR = 2.0 ** -100

CHUNK_SHIFT = CHUNK.bit_length() - 1
assert 1 << CHUNK_SHIFT == CHUNK
LANES = 128
A_W = A_HEADS * 2 * A_HEAD_DIM
B_W = B_HEADS * B_HEAD_DIM
PROJ_W = 4 * 256 + 4 * 512 + 2 * D_MODEL

TM_PROJ = 512
TQ = 512
TK = 512
FAR_UNROLL = 4
TM_POST = 1024
TM_RANK = 1024
TM_DISPATCH = 2048
TM_COMBINE = 1024
ROW_UNROLL = 8
FFN_BLK = 256
VMEM_LIMIT = 56 * 1024 * 1024


def _cparams(sem):
    return pltpu.CompilerParams(dimension_semantics=sem, vmem_limit_bytes=VMEM_LIMIT)


def _proj_kernel(x_ref, g1_ref, w_ref, gn_ref, gmat_ref,
                 qa_ref, ka_ref, va_ref, qb_ref, kb_ref, vb_ref, ga_ref, gb_ref):
    x = x_ref[...]
    xn = x * lax.rsqrt(jnp.mean(x * x, axis=-1, keepdims=True) + EPS) * g1_ref[...]
    xn = xn.astype(jnp.bfloat16)

    def slab(c0, width):
        return jnp.dot(xn, w_ref[:, c0:c0 + width], preferred_element_type=jnp.float32)

    def headnorm(y, gi):
        sq = (y * y).astype(jnp.bfloat16)
        half = gmat_ref.shape[0]
        ss = jnp.concatenate([jnp.dot(sq[:, c:c + half], gmat_ref[...], preferred_element_type=jnp.float32)
                              for c in range(0, y.shape[1], half)], axis=1)
        return y * lax.rsqrt(ss * (1.0 / A_HEAD_DIM) + EPS) * gn_ref[gi:gi + 1, :]

    qa_ref[...] = headnorm(slab(0, 512), 0).astype(jnp.bfloat16)
    ka_ref[...] = headnorm(slab(512, 512), 1).astype(jnp.bfloat16)
    va_ref[...] = slab(1024, 512).astype(jnp.bfloat16)
    qb_ref[...] = headnorm(slab(1536, 512), 2).astype(jnp.bfloat16)
    kb_ref[...] = headnorm(slab(2048, 512), 3).astype(jnp.bfloat16)
    vb_ref[...] = slab(2560, 512).astype(jnp.bfloat16)
    for j in range(2):
        ga_ref[:, j * 512:(j + 1) * 512] = jax.nn.sigmoid(slab(3072 + j * 512, 512)).astype(jnp.bfloat16)
        gb_ref[:, j * 512:(j + 1) * 512] = jax.nn.sigmoid(slab(4096 + j * 512, 512)).astype(jnp.bfloat16)


def _proj(x2, g1, w_perm, gn, gmat):
    t = x2.shape[0]
    n = t // TM_PROJ
    row = lambda w: pl.BlockSpec((TM_PROJ, w), lambda i: (i, 0))
    full = lambda a: pl.BlockSpec(a.shape, lambda i: (0,) * a.ndim)
    outs = [jax.ShapeDtypeStruct((t, 512), jnp.bfloat16)] * 6 + [jax.ShapeDtypeStruct((t, D_MODEL), jnp.bfloat16)] * 2
    return pl.pallas_call(
        _proj_kernel,
        grid=(n,),
        in_specs=[row(D_MODEL), full(g1), full(w_perm), full(gn), full(gmat)],
        out_specs=[row(512)] * 6 + [row(D_MODEL)] * 2,
        out_shape=outs,
        compiler_params=_cparams(("arbitrary",)),
        name="proj",
    )(x2, g1, w_perm, gn, gmat)


def _diff_kernel(scal_ref, q_ref, k_ref, v_ref, vec_ref, gsub_ref, o_ref,
                 bias_sc, kmax_sc, qz_sc, refn_sc, reff_sc, l_sc, acc_sc,
                 m_sc, acc2_sc, mf_sc, accf_sc, *, n_near):
    f32 = jnp.float32
    h = pl.program_id(1)
    qi = pl.program_id(2)
    nc = TK // LANES
    n_heads = pl.num_programs(1)

    def bias_cols(j):
        return slice((n_near - 1 - j) * TK, (n_near - j) * TK)

    @pl.when(qi == 0)
    def _():
        qchunk = lax.shift_right_arithmetic(lax.broadcasted_iota(jnp.int32, (TQ, TK), 0), CHUNK_SHIFT)
        kcol = lax.broadcasted_iota(jnp.int32, (TQ, TK), 1)
        for j in range(n_near):
            dchunk = qchunk - lax.shift_right_arithmetic(kcol - j * TK, CHUNK_SHIFT)
            allowed = dchunk >= 0
            for half in range(2):
                vec = jnp.broadcast_to(vec_ref[0, j, half:half + 1, :], (TQ, TQ + TK))
                tile = pltpu.roll(vec, 0, 1, stride=1, stride_axis=0)[:, :TK]
                bias_sc[half * TQ:(half + 1) * TQ, bias_cols(j)] = jnp.where(allowed, tile, NEG)
        kf = k_ref[...].astype(f32)
        ksq = kf * kf
        klane = lax.broadcasted_iota(jnp.int32, ksq.shape, 1)
        for half in range(2):
            sel = (klane < 64) if half == 0 else (klane >= 64)
            norm2 = jnp.max(jnp.sum(jnp.where(sel, ksq, 0.0), axis=-1, keepdims=True), axis=0, keepdims=True)
            kmax_sc[half * TQ:(half + 1) * TQ, :] = jnp.broadcast_to(jnp.sqrt(norm2), (TQ, LANES))

    q = q_ref[...]
    lane = lax.broadcasted_iota(jnp.int32, q.shape, 1)
    zero = jnp.zeros_like(q)
    qz_sc[0:TQ, :] = jnp.where(lane < 64, q, zero)
    qz_sc[TQ:2 * TQ, :] = jnp.where(lane >= 64, q, zero)
    cfar = scal_ref[1 + h]
    bias_max = scal_ref[1 + n_heads + h]

    def scores(kblk):
        k = k_ref[pl.ds(pl.multiple_of(kblk * TK, TK), TK), :]
        return lax.dot_general(qz_sc[...], k, (((1,), (1,)), ((), ())), preferred_element_type=f32)

    def value_block(kblk):
        return v_ref[pl.ds(pl.multiple_of(kblk * TK, TK), TK), :]

    def finish(o):
        od = o[:TQ] - scal_ref[0] * o[TQ:]
        od = od * lax.rsqrt(jnp.mean(od * od, axis=-1, keepdims=True) + EPS) * gsub_ref[...]
        o_ref[...] = od.astype(o_ref.dtype)

    n_far = jnp.maximum(qi - (n_near - 1), 0)

    def fixed_reference_pass():
        qf = qz_sc[...].astype(f32)
        qnorm = jnp.sqrt(jnp.sum(qf * qf, axis=-1, keepdims=True))
        refn_sc[...] = qnorm * kmax_sc[...] + bias_max
        l_sc[...] = jnp.zeros(l_sc.shape, f32)
        acc_sc[...] = jnp.zeros(acc_sc.shape, f32)

        def accumulate(s, kblk, ref_ref):
            ref = ref_ref[...]
            ps = [jnp.exp2(s[:, c * LANES:(c + 1) * LANES] - ref) for c in range(nc)]
            l_sc[...] += functools.reduce(jnp.add, ps)
            p = jnp.concatenate([x.astype(jnp.bfloat16) for x in ps], axis=1)
            acc_sc[...] += jnp.dot(p, value_block(kblk), preferred_element_type=f32)

        for j in range(n_near):
            def near(j=j):
                accumulate(scores(qi - j) + bias_sc[:, bias_cols(j)], qi - j, refn_sc)
            if j == 0:
                near()
            else:
                pl.when(qi >= j)(near)

        reff_sc[...] = refn_sc[...] - cfar

        def far(kblk, carry):
            accumulate(scores(kblk), kblk, reff_sc)
            return carry

        def far_quad(i, carry):
            for u in range(FAR_UNROLL):
                far(FAR_UNROLL * i + u, carry)
            return carry

        lax.fori_loop(0, n_far // FAR_UNROLL, far_quad, 0)
        lax.fori_loop((n_far // FAR_UNROLL) * FAR_UNROLL, n_far, far, 0)

        denom = jnp.sum(l_sc[...], axis=-1, keepdims=True)
        finish(acc_sc[...] / denom)
        return denom

    def exact_pass():
        def update(s, kblk, m_ref, a_ref):
            cols = [s[:, c * LANES:(c + 1) * LANES] for c in range(nc)]
            m_old = m_ref[...]
            m_new = jnp.maximum(m_old, jnp.max(functools.reduce(jnp.maximum, cols), axis=-1, keepdims=True))
            alpha = jnp.exp2(m_old - m_new)
            ps = [jnp.exp2(c - m_new) for c in cols]
            p = jnp.concatenate([x.astype(jnp.bfloat16) for x in ps], axis=1)
            a_ref[:, 0:LANES] = alpha * a_ref[:, 0:LANES] + jnp.dot(p, value_block(kblk), preferred_element_type=f32)
            a_ref[:, LANES:2 * LANES] = alpha * a_ref[:, LANES:2 * LANES] + functools.reduce(jnp.add, ps)
            m_ref[...] = m_new

        m_sc[...] = jnp.full(m_sc.shape, NEG, f32)
        acc2_sc[...] = jnp.zeros(acc2_sc.shape, f32)
        for j in range(n_near):
            def near(j=j):
                update(scores(qi - j) + bias_sc[:, bias_cols(j)], qi - j, m_sc, acc2_sc)
            if j == 0:
                near()
            else:
                pl.when(qi >= j)(near)

        mf_sc[...] = jnp.full(mf_sc.shape, NEG, f32)
        accf_sc[...] = jnp.zeros(accf_sc.shape, f32)

        def far(kblk, carry):
            update(scores(kblk), kblk, mf_sc, accf_sc)
            return carry

        lax.fori_loop(0, n_far, far, 0)
        mf = mf_sc[...] + cfar
        mn = m_sc[...]
        m = jnp.maximum(mf, mn)
        wf = jnp.exp2(mf - m)
        wn = jnp.exp2(mn - m)
        tot = (jnp.concatenate([wf, wf], axis=1) * accf_sc[...]
               + jnp.concatenate([wn, wn], axis=1) * acc2_sc[...])
        finish(tot[:, 0:LANES] / jnp.sum(tot[:, LANES:2 * LANES], axis=-1, keepdims=True))

    denom = fixed_reference_pass()
    pl.when(jnp.logical_not(jnp.min(denom) >= DENOM_FLOOR))(exact_pass)


def _diff_attention(scal, q, k, v, vecs, gsub, *, bsz, s_len):
    n_blk = q.shape[1] // LANES
    nq = s_len // TQ
    n_near = vecs.shape[1]
    assert vecs.shape == (n_blk, n_near, 2, TQ + TK)
    kern = functools.partial(_diff_kernel, n_near=n_near)
    stat = lambda: pltpu.VMEM((2 * TQ, LANES), jnp.float32)
    acc = lambda: pltpu.VMEM((2 * TQ, 2 * LANES), jnp.float32)
    return pl.pallas_call(
        kern,
        grid=(bsz, n_blk, nq),
        in_specs=[
            pl.BlockSpec(memory_space=pltpu.SMEM),
            pl.BlockSpec((TQ, LANES), lambda b, h, i: (b * nq + i, h)),
            pl.BlockSpec((s_len, LANES), lambda b, h, i: (b, h)),
            pl.BlockSpec((s_len, LANES), lambda b, h, i: (b, h)),
            pl.BlockSpec((1, n_near, 2, TQ + TK), lambda b, h, i: (h, 0, 0, 0)),
            pl.BlockSpec((1, LANES), lambda b, h, i: (0, 0)),
        ],
        out_specs=pl.BlockSpec((TQ, LANES), lambda b, h, i: (b * nq + i, h)),
        out_shape=jax.ShapeDtypeStruct(q.shape, jnp.bfloat16),
        scratch_shapes=[pltpu.VMEM((2 * TQ, n_near * TK), jnp.float32), stat(),
                        pltpu.VMEM((2 * TQ, LANES), jnp.bfloat16), stat(), stat(), stat(), stat(),
                        stat(), acc(), stat(), acc()],
        compiler_params=_cparams(("arbitrary", "arbitrary", "arbitrary")),
        name="attn_diff",
    )(scal, q, k, v, vecs, gsub)


BAND_HQ = TQ // 2
BAND_W = B_LEFT_CHUNKS * CHUNK + BAND_HQ


def _band_kernel(q_ref, k_ref, v_ref, vec_ref, o_ref, bias_sc, qz_sc):
    f32 = jnp.float32
    qi = pl.program_id(2)
    hq, w = BAND_HQ, BAND_W
    back = w - hq

    @pl.when(qi == 0)
    def _():
        qchunk = lax.shift_right_arithmetic(lax.broadcasted_iota(jnp.int32, (hq, w), 0), CHUNK_SHIFT)
        kchunk = lax.shift_right_arithmetic(lax.broadcasted_iota(jnp.int32, (hq, w), 1) - back, CHUNK_SHIFT)
        dchunk = qchunk - kchunk
        allowed = (dchunk >= 0) & (dchunk <= B_LEFT_CHUNKS)
        for half in range(2):
            vec = jnp.broadcast_to(vec_ref[0, half:half + 1, :], (hq, hq + w))
            tile = pltpu.roll(vec, 0, 1, stride=1, stride_axis=0)[:, :w]
            bias_sc[half * hq:(half + 1) * hq, :] = jnp.where(allowed, tile, NEG)

    q = q_ref[...]
    lane = lax.broadcasted_iota(jnp.int32, (hq, LANES), 1)
    zero = jnp.zeros((hq, LANES), q.dtype)
    for u in range(2):
        qu = q[u * hq:(u + 1) * hq, :]
        qz_sc[(2 * u) * hq:(2 * u + 1) * hq, :] = jnp.where(lane < 64, qu, zero)
        qz_sc[(2 * u + 1) * hq:(2 * u + 2) * hq, :] = jnp.where(lane >= 64, qu, zero)

    def scores(u, k0, n_keys):
        k = k_ref[pl.ds(k0, n_keys), :]
        return lax.dot_general(qz_sc[2 * u * hq:(2 * u + 2) * hq, :], k, (((1,), (1,)), ((), ())),
                               preferred_element_type=f32) + bias_sc[:, w - n_keys:w]

    def softmax(s):
        cols = [s[:, c * LANES:(c + 1) * LANES] for c in range(s.shape[1] // LANES)]
        m = jnp.max(functools.reduce(jnp.maximum, cols), axis=-1, keepdims=True)
        ps = [jnp.exp2(c - m) for c in cols]
        denom = jnp.sum(functools.reduce(jnp.add, ps), axis=-1, keepdims=True)
        return jnp.concatenate([x.astype(jnp.bfloat16) for x in ps], axis=1), denom

    def finish(u, p, denom, k0):
        o = jnp.dot(p, v_ref[pl.ds(k0, p.shape[1]), :], preferred_element_type=f32) / denom
        o_ref[u * hq:(u + 1) * hq, :] = jnp.where(lane < 64, o[:hq], o[hq:]).astype(o_ref.dtype)

    def step(k0s, n_keys):
        ss = [scores(u, k0s[u], n_keys[u]) for u in range(2)]
        pd = [softmax(x) for x in ss]
        for u in range(2):
            finish(u, pd[u][0], pd[u][1], k0s[u])

    @pl.when(qi == 0)
    def _():
        step((0, 0), (hq, 2 * hq))

    @pl.when(qi > 0)
    def _():
        start = pl.multiple_of(qi * TQ - back, hq)
        step((start, pl.multiple_of(start + hq, hq)), (w, w))


def _band_attention(q, k, v, vecs, *, bsz, s_len):
    n_blk = q.shape[1] // LANES
    nq = s_len // TQ
    assert vecs.shape == (n_blk, 2, BAND_HQ + BAND_W)
    return pl.pallas_call(
        _band_kernel,
        grid=(bsz, n_blk, nq),
        in_specs=[
            pl.BlockSpec((TQ, LANES), lambda b, h, i: (b * nq + i, h)),
            pl.BlockSpec((s_len, LANES), lambda b, h, i: (b, h)),
            pl.BlockSpec((s_len, LANES), lambda b, h, i: (b, h)),
            pl.BlockSpec((1, 2, BAND_HQ + BAND_W), lambda b, h, i: (h, 0, 0)),
        ],
        out_specs=pl.BlockSpec((TQ, LANES), lambda b, h, i: (b * nq + i, h)),
        out_shape=jax.ShapeDtypeStruct(q.shape, jnp.bfloat16),
        scratch_shapes=[pltpu.VMEM((2 * BAND_HQ, BAND_W), jnp.float32),
                        pltpu.VMEM((2 * TQ, LANES), jnp.bfloat16)],
        compiler_params=_cparams(("arbitrary", "arbitrary", "arbitrary")),
        name="attn_band",
    )(q, k, v, vecs)


def _post_kernel(oa_ref, ob_ref, ga_ref, gb_ref, x_ref, wa_ref, wb_ref, wo_ref, g2_ref,
                 wr2_ref, br_ref, x1_ref, hn_ref, route_ref):
    f32 = jnp.float32
    ya = jnp.dot(oa_ref[...], wa_ref[...], preferred_element_type=f32)
    yb = jnp.dot(ob_ref[...], wb_ref[...], preferred_element_type=f32)
    mixed = ga_ref[...].astype(f32) * ya + gb_ref[...].astype(f32) * yb
    x1 = x_ref[...] + jnp.dot(mixed.astype(jnp.bfloat16), wo_ref[...], preferred_element_type=f32)
    x1_ref[...] = x1
    hn = x1 * lax.rsqrt(jnp.mean(x1 * x1, axis=-1, keepdims=True) + EPS) * g2_ref[...]
    hh = hn.astype(jnp.bfloat16)
    bits = lax.bitcast_convert_type(hh.astype(f32), jnp.uint32)
    half = D_MODEL // 2
    hn_ref[...] = (bits[:, :half] & jnp.uint32(0xFFFF0000)) | lax.shift_right_logical(bits[:, half:], jnp.uint32(16))

    hl = (hn - hh.astype(f32)).astype(jnp.bfloat16)
    hw = jnp.dot(hh, wr2_ref[...], preferred_element_type=f32)
    lg = (hw[:, 0:LANES] + hw[:, LANES:2 * LANES]
          + jnp.dot(hl, wr2_ref[:, 0:LANES], preferred_element_type=f32)) + br_ref[...]

    lanei = lax.broadcasted_iota(jnp.int32, lg.shape, 1)
    lanef = lanei.astype(f32)
    big = 999.0
    gmask = lanei < N_GROUPS
    gl = jnp.where(gmask, lg, NEG)
    gm = jnp.max(gl, axis=-1, keepdims=True)
    ge = jnp.where(gmask, jnp.exp(gl - gm), 0.0)
    gp = ge / jnp.sum(ge, axis=-1, keepdims=True)
    p_g = jnp.max(gp, axis=-1, keepdims=True)
    gidx = jnp.min(jnp.where(gmask & (gp == p_g), lanef, big), axis=-1, keepdims=True)
    egrp = lax.shift_right_arithmetic(lanei - N_GROUPS, 3).astype(f32)
    emask = (lanei >= N_GROUPS) & (lanei < N_GROUPS + N_EXPERTS) & (egrp == gidx)
    el = jnp.where(emask, lg, NEG)
    v1 = jnp.max(el, axis=-1, keepdims=True)
    i1 = jnp.min(jnp.where(emask & (el == v1), lanef, big), axis=-1, keepdims=True)
    emask2 = emask & (lanef != i1)
    el2 = jnp.where(emask2, lg, NEG)
    v2 = jnp.max(el2, axis=-1, keepdims=True)
    i2 = jnp.min(jnp.where(emask2 & (el2 == v2), lanef, big), axis=-1, keepdims=True)
    t = jnp.exp(v2 - v1)
    den = 1.0 + t
    w1 = p_g * (1.0 / den)
    w2 = p_g * (t / den)
    route = jnp.where(lanei == 0, i1 - N_GROUPS,
                      jnp.where(lanei == 1, i2 - N_GROUPS,
                                jnp.where(lanei == 2, w1, jnp.where(lanei == 3, w2, 0.0))))
    route_ref[...] = route


def _post(oa, ob, ga, gb, x2, wa, wb, wo, g2, wr2, br):
    t = x2.shape[0]
    n = t // TM_POST
    row = lambda w: pl.BlockSpec((TM_POST, w), lambda i: (i, 0))
    full = lambda a: pl.BlockSpec(a.shape, lambda i: (0,) * a.ndim)
    return pl.pallas_call(
        _post_kernel,
        grid=(n,),
        in_specs=[row(512), row(512), row(D_MODEL), row(D_MODEL), row(D_MODEL),
                  full(wa), full(wb), full(wo), full(g2), full(wr2), full(br)],
        out_specs=[row(D_MODEL), row(D_MODEL // 2), row(LANES)],
        out_shape=[jax.ShapeDtypeStruct((t, D_MODEL), jnp.float32),
                   jax.ShapeDtypeStruct((t, D_MODEL // 2), jnp.uint32),
                   jax.ShapeDtypeStruct((t, LANES), jnp.float32)],
        compiler_params=_cparams(("arbitrary",)),
        name="post",
    )(oa, ob, ga, gb, x2, wa, wb, wo, g2, wr2, br)


def _rank_kernel(route_ref, ltri_ref, utri_ref, dest_ref, cnt_ref, cnt_sc, pstart_sc, base_sc):
    f32 = jnp.float32
    p = pl.program_id(0)
    i = pl.program_id(1)
    route = route_ref[...]
    lanef = lax.broadcasted_iota(jnp.int32, route.shape, 1).astype(f32)
    oh1 = (lanef == route[:, 0:1]).astype(f32)
    oh2 = (lanef == route[:, 1:2]).astype(f32)
    both = oh1 + oh2
    colsum = jnp.sum(both, axis=0, keepdims=True)

    @pl.when((p == 0) & (i == 0))
    def _():
        cnt_sc[...] = jnp.zeros(cnt_sc.shape, f32)

    @pl.when(p == 0)
    def _():
        cnt_sc[...] += colsum
        dest_ref[...] = jnp.zeros(dest_ref.shape, f32)
        cnt_ref[...] = jnp.zeros(cnt_ref.shape, f32)

    @pl.when((p == 1) & (i == 0))
    def _():
        cnt = cnt_sc[...]
        chi = jnp.floor(cnt * (1.0 / 256.0))
        clo = cnt - chi * 256.0
        split = jnp.concatenate([jnp.broadcast_to(chi, (8, LANES)), jnp.broadcast_to(clo, (8, LANES))], axis=0)
        excl = jnp.dot(split.astype(jnp.bfloat16), utri_ref[...], preferred_element_type=f32)
        pstart_sc[...] = excl[0:1] * 256.0 + excl[8:9]
        base_sc[...] = jnp.zeros(base_sc.shape, f32)

    @pl.when(p == 1)
    def _():
        prior = jnp.dot(ltri_ref[...], both.astype(jnp.bfloat16), preferred_element_type=f32)
        slot = prior + base_sc[...] + pstart_sc[...]
        d1 = jnp.sum(oh1 * slot, axis=-1, keepdims=True)
        d2 = jnp.sum(oh2 * slot, axis=-1, keepdims=True)
        dest_ref[...] = jnp.where(lanef == 0.0, d1, jnp.where(lanef == 1.0, d2, 0.0))
        base_sc[...] += colsum
        cnt_ref[...] = jnp.broadcast_to(cnt_sc[...], cnt_ref.shape)


def _rank(route, ltri, utri):
    t = route.shape[0]
    n = t // TM_RANK
    full = lambda a: pl.BlockSpec(a.shape, lambda p, i: (0,) * a.ndim)
    row1 = lambda: pltpu.VMEM((1, LANES), jnp.float32)
    return pl.pallas_call(
        _rank_kernel,
        grid=(2, n),
        in_specs=[pl.BlockSpec((TM_RANK, LANES), lambda p, i: (i, 0)), full(ltri), full(utri)],
        out_specs=[pl.BlockSpec((TM_RANK, LANES), lambda p, i: (i * p, 0)),
                   pl.BlockSpec((8, LANES), lambda p, i: (0, 0))],
        out_shape=[jax.ShapeDtypeStruct((t, LANES), jnp.float32),
                   jax.ShapeDtypeStruct((8, LANES), jnp.float32)],
        scratch_shapes=[row1(), row1(), row1()],
        compiler_params=_cparams(("arbitrary", "arbitrary")),
        name="rank",
    )(route, ltri, utri)


def _dispatch_kernel(dest_ref, hn_ref, xs_ref, sems):
    def row_copy(r, k):
        d = dest_ref[0, 0, 2 * r + k]
        return pltpu.make_async_copy(hn_ref.at[pl.ds(r, 1), :], xs_ref.at[pl.ds(d, 1), :], sems.at[k])

    def issue(r, c):
        row_copy(r, 0).start(priority=0)
        row_copy(r, 1).start(priority=1)
        return c

    lax.fori_loop(0, TM_DISPATCH, issue, 0, unroll=ROW_UNROLL)
    for k in range(TOP_K):
        pltpu.make_async_copy(hn_ref, xs_ref.at[pl.ds(0, TM_DISPATCH), :], sems.at[k]).wait()


def _dispatch(dest3, hn, n_rows):
    t, w = hn.shape
    n = t // TM_DISPATCH
    return pl.pallas_call(
        _dispatch_kernel,
        grid=(n,),
        in_specs=[pl.BlockSpec((1, 1, 2 * TM_DISPATCH), lambda i: (i, 0, 0), memory_space=pltpu.SMEM),
                  pl.BlockSpec((TM_DISPATCH, w), lambda i: (i, 0))],
        out_specs=pl.BlockSpec(memory_space=pl.ANY),
        out_shape=jax.ShapeDtypeStruct((n_rows, w), hn.dtype),
        scratch_shapes=[pltpu.SemaphoreType.DMA((2,))],
        compiler_params=_cparams(("arbitrary",)),
        name="dispatch",
    )(dest3, hn)


def _ffn_kernel(tile_ref, exp_ref, lo_ref, hi_ref, cast_ref, init_ref,
                xs_ref, wg_ref, wu_ref, wd_ref, ys_ref, wg_sc, wu_sc, wd_sc):
    v = pl.program_id(0)
    lo = lo_ref[v]
    hi = hi_ref[v]

    @pl.when(init_ref[v] == 1)
    def _():
        ys_ref[...] = jnp.zeros(ys_ref.shape, ys_ref.dtype)

    @pl.when(hi > lo)
    def _():
        @pl.when(cast_ref[v] == 1)
        def _():
            wg_sc[...] = wg_ref[0].astype(jnp.bfloat16)
            wu_sc[...] = wu_ref[0].astype(jnp.bfloat16)
            wd_sc[...] = wd_ref[0].astype(jnp.bfloat16)

        pk = xs_ref[...]
        x = jnp.concatenate(
            [lax.bitcast_convert_type(pk & jnp.uint32(0xFFFF0000), jnp.float32),
             lax.bitcast_convert_type(lax.shift_left(pk, jnp.uint32(16)), jnp.float32)], axis=1).astype(jnp.bfloat16)
        g = jnp.dot(x, wg_sc[...], preferred_element_type=jnp.float32)
        u = jnp.dot(x, wu_sc[...], preferred_element_type=jnp.float32)
        hb = (g * jax.nn.sigmoid(g) * u).astype(jnp.bfloat16)
        y = jnp.dot(hb, wd_sc[...], preferred_element_type=jnp.float32)
        rows = tile_ref[v] * FFN_BLK + lax.broadcasted_iota(jnp.int32, y.shape, 0)
        ys_ref[...] = jnp.where((rows >= lo) & (rows < hi), y, ys_ref[...])


def _ffn(seg, xs, w_gate, w_up, w_down):
    n_rows = xs.shape[0]
    n_seg = seg[0].shape[0]
    grid_spec = pltpu.PrefetchScalarGridSpec(
        num_scalar_prefetch=6,
        grid=(n_seg,),
        in_specs=[
            pl.BlockSpec((FFN_BLK, D_MODEL // 2), lambda v, t, e, *_: (t[v], 0)),
            pl.BlockSpec((1, D_MODEL, D_EXPERT), lambda v, t, e, *_: (e[v], 0, 0)),
            pl.BlockSpec((1, D_MODEL, D_EXPERT), lambda v, t, e, *_: (e[v], 0, 0)),
            pl.BlockSpec((1, D_EXPERT, D_MODEL), lambda v, t, e, *_: (e[v], 0, 0)),
        ],
        out_specs=pl.BlockSpec((FFN_BLK, D_MODEL), lambda v, t, e, *_: (t[v], 0)),
        scratch_shapes=[pltpu.VMEM((D_MODEL, D_EXPERT), jnp.bfloat16),
                        pltpu.VMEM((D_MODEL, D_EXPERT), jnp.bfloat16),
                        pltpu.VMEM((D_EXPERT, D_MODEL), jnp.bfloat16)],
    )
    return pl.pallas_call(
        _ffn_kernel,
        grid_spec=grid_spec,
        out_shape=jax.ShapeDtypeStruct((n_rows, D_MODEL), jnp.float32),
        compiler_params=_cparams(("arbitrary",)),
        name="ffn",
    )(*seg, xs, w_gate, w_up, w_down)


def _segments(counts, n_rows):
    i32 = jnp.int32
    n_tiles = n_rows // FFN_BLK
    n_seg = n_tiles + N_EXPERTS
    tri = jnp.tril(jnp.ones((N_EXPERTS, N_EXPERTS), i32))
    ends = jnp.sum(tri * counts[None, :], axis=1)
    starts = ends - counts
    edges = jnp.arange(n_tiles, dtype=i32) * FFN_BLK
    rank_e = jnp.arange(n_tiles, dtype=i32) + jnp.sum(starts[None, :] <= edges[:, None], axis=1)
    rank_s = jnp.arange(N_EXPERTS, dtype=i32) + jnp.sum(edges[None, :] < starts[:, None], axis=1)
    seg = jnp.arange(n_seg, dtype=i32)
    lo = (jnp.sum(jnp.where(rank_e[None, :] == seg[:, None], edges[None, :], 0), axis=1)
          + jnp.sum(jnp.where(rank_s[None, :] == seg[:, None], starts[None, :], 0), axis=1))
    hi = jnp.concatenate([lo[1:], jnp.array([n_rows], i32)])
    valid = hi > lo
    tile = jnp.minimum(lo // FFN_BLK, n_tiles - 1)
    expert = jnp.minimum(jnp.sum(ends[None, :] <= lo[:, None], axis=1), N_EXPERTS - 1).astype(i32)
    upto = seg[None, :] <= seg[:, None]
    expert = jnp.max(jnp.where(upto & valid[None, :], expert[None, :], 0), axis=1)
    prev_expert = jnp.concatenate([jnp.array([-1], i32), expert[:-1]])
    first_valid = valid & (jnp.sum(jnp.where(upto & valid[None, :], 1, 0), axis=1) == 1)
    cast = valid & ((expert != prev_expert) | first_valid)
    prev_tile = jnp.concatenate([jnp.array([-1], i32), tile[:-1]])
    init = tile != prev_tile
    return (tile.astype(i32), expert, lo.astype(i32), hi.astype(i32), cast.astype(i32), init.astype(i32))


def _combine_kernel(dest_ref, route_ref, x1_ref, ys_ref, out_ref, y0_sc, y1_sc, sems):
    def row_copy(r, k):
        d = dest_ref[0, 0, 2 * r + k]
        dst = y0_sc if k == 0 else y1_sc
        return pltpu.make_async_copy(ys_ref.at[pl.ds(d, 1), :], dst.at[pl.ds(r, 1), :], sems.at[k])

    def issue(r, c):
        row_copy(r, 0).start(priority=0)
        row_copy(r, 1).start(priority=1)
        return c

    lax.fori_loop(0, TM_COMBINE, issue, 0, unroll=ROW_UNROLL)
    pltpu.make_async_copy(ys_ref.at[pl.ds(0, TM_COMBINE), :], y0_sc, sems.at[0]).wait()
    pltpu.make_async_copy(ys_ref.at[pl.ds(0, TM_COMBINE), :], y1_sc, sems.at[1]).wait()
    route = route_ref[...]
    out_ref[...] = x1_ref[...] + (route[:, 2:3] * y0_sc[...] + route[:, 3:4] * y1_sc[...])


def _combine(dest3, route, x1, ys):
    t = x1.shape[0]
    n = t // TM_COMBINE
    return pl.pallas_call(
        _combine_kernel,
        grid=(n,),
        in_specs=[pl.BlockSpec((1, 1, 2 * TM_COMBINE), lambda i: (i, 0, 0), memory_space=pltpu.SMEM),
                  pl.BlockSpec((TM_COMBINE, LANES), lambda i: (i, 0)),
                  pl.BlockSpec((TM_COMBINE, D_MODEL), lambda i: (i, 0)),
                  pl.BlockSpec(memory_space=pl.ANY)],
        out_specs=pl.BlockSpec((TM_COMBINE, D_MODEL), lambda i: (i, 0)),
        out_shape=jax.ShapeDtypeStruct((t, D_MODEL), jnp.float32),
        scratch_shapes=[pltpu.VMEM((TM_COMBINE, D_MODEL), jnp.float32),
                        pltpu.VMEM((TM_COMBINE, D_MODEL), jnp.float32),
                        pltpu.SemaphoreType.DMA((2,))],
        compiler_params=_cparams(("arbitrary",)),
        name="combine",
    )(dest3, route, x1, ys)


def _t5_bucket(rel):
    nb = T5_BUCKETS // 2
    max_exact = nb // 2
    side = jnp.where(rel > 0, nb, 0)
    n = jnp.abs(rel)
    nf = jnp.maximum(n, 1).astype(jnp.float32)
    large = max_exact + (jnp.log(nf / max_exact) / math.log(T5_MAX_DIST / max_exact)
                         * (nb - max_exact)).astype(jnp.int32)
    large = jnp.minimum(large, nb - 1)
    return side + jnp.where(n < max_exact, n, large)


def _rel_offsets(j):
    i = jnp.arange(TQ + TK)
    return jnp.where(i < TK, i, i - (TQ + TK)) - j * TK


def _diff_bias_vecs(t5_table):
    vecs = jnp.stack([t5_table[_t5_bucket(_rel_offsets(j))].astype(jnp.float32).T for j in range(2)], axis=1)
    vecs = jnp.stack([vecs, vecs], axis=2)
    far = t5_table[_t5_bucket(jnp.array(-(TK + 1)))].astype(jnp.float32)
    return vecs, far


def _band_bias_vecs(rel_table):
    i = jnp.arange(BAND_HQ + BAND_W)
    rel = jnp.where(i < BAND_W, i, i - (BAND_HQ + BAND_W)) - (BAND_W - BAND_HQ)
    vecs = rel_table[jnp.clip(rel, -B_MAX_REL, B_MAX_REL) + B_MAX_REL].astype(jnp.float32).T
    return vecs.reshape(B_HEADS // 2, 2, BAND_HQ + BAND_W)


def kernel(x, norm1_g, w_in, a_qnorm_g, a_knorm_g, a_lambda, a_subln_g, t5_table, b_qnorm_g, b_knorm_g,
           b_rel_table, w_branch_a, w_branch_b, w_out, norm2_g, w_router_group, b_router_group,
           w_router_expert, b_router_expert, w_gate, w_up, w_down):
    bsz, s_len, _ = x.shape
    n_tok = bsz * s_len
    f32, bf16 = jnp.float32, jnp.bfloat16
    assert s_len % TQ == 0 and TQ == TK and TQ % CHUNK == 0 and n_tok % TM_PROJ == 0
    assert TK >= T5_MAX_DIST and BAND_HQ % CHUNK == 0 and BAND_W % LANES == 0 and BAND_W - BAND_HQ <= TQ
    l = 0
    x2 = x.reshape(n_tok, D_MODEL)

    w = w_in[l]
    qk = w[:, :1024].reshape(D_MODEL, 2, 2, A_HEADS, A_HEAD_DIM)
    qk = qk.transpose(0, 1, 3, 2, 4).reshape(D_MODEL, 1024)
    w_perm = jnp.concatenate([qk, w[:, 1024:]], axis=1).astype(bf16)
    gn = jnp.stack([jnp.tile(a_qnorm_g[l] * (A_HEAD_DIM ** -0.5 * LOG2E), 8), jnp.tile(a_knorm_g[l], 8),
                    jnp.tile(b_qnorm_g[l] * (B_HEAD_DIM ** -0.5 * LOG2E), 8), jnp.tile(b_knorm_g[l], 8)]).astype(f32)
    gmat = jnp.asarray(np.kron(np.eye(4), np.ones((A_HEAD_DIM, A_HEAD_DIM))), dtype=bf16)

    qa, ka, va, qb, kb, vb, ga, gb = _proj(x2, norm1_g[l][None].astype(f32), w_perm, gn, gmat)

    lam_init = 0.8 - 0.6 * math.exp(-0.3 * l)
    lp = a_lambda[l].astype(f32)
    lam = jnp.exp(jnp.sum(lp[0] * lp[1])) - jnp.exp(jnp.sum(lp[2] * lp[3])) + lam_init
    bias_a, far_a = _diff_bias_vecs(t5_table)
    bmax_a = jnp.maximum(jnp.max(bias_a, axis=(1, 2, 3)), far_a)
    scal_a = (jnp.concatenate([lam[None], far_a, bmax_a]) * jnp.array([1.0] + [LOG2E] * (2 * A_HEADS))).astype(f32)
    gsub = (a_subln_g[l] * (1.0 - lam_init))[None].astype(f32)
    oa = _diff_attention(scal_a, qa, ka, va, bias_a * LOG2E, gsub, bsz=bsz, s_len=s_len)

    bias_b = _band_bias_vecs(b_rel_table[l])
    ob = _band_attention(qb, kb, vb, bias_b * LOG2E, bsz=bsz, s_len=s_len)

    wr = jnp.zeros((D_MODEL, LANES), f32)
    wr = wr.at[:, :N_GROUPS].set(w_router_group[l]).at[:, N_GROUPS:N_GROUPS + N_EXPERTS].set(w_router_expert[l])
    wrh = wr.astype(bf16)
    wr2 = jnp.concatenate([wrh, (wr - wrh.astype(f32)).astype(bf16)], axis=1)
    br = jnp.zeros((1, LANES), f32)
    br = br.at[0, :N_GROUPS].set(b_router_group[l]).at[0, N_GROUPS:N_GROUPS + N_EXPERTS].set(b_router_expert[l])
    x1, hn, route = _post(oa, ob, ga, gb, x2, w_branch_a[l].astype(bf16), w_branch_b[l].astype(bf16),
                          w_out[l].astype(bf16), norm2_g[l][None].astype(f32), wr2, br)

    ltri = jnp.asarray(np.tril(np.ones((TM_RANK, TM_RANK)), -1), dtype=bf16)
    utri = jnp.asarray(np.triu(np.ones((LANES, LANES)), 1), dtype=bf16)
    dest, cnt = _rank(route, ltri, utri)

    counts = cnt[0, :N_EXPERTS].astype(jnp.int32)
    n_rows = n_tok * TOP_K
    seg = _segments(counts, n_rows)

    dest2 = dest[:, :TOP_K].astype(jnp.int32)
    xs = _dispatch(dest2.reshape(n_tok // TM_DISPATCH, 1, TOP_K * TM_DISPATCH), hn, n_rows)
    ys = _ffn(seg, xs, w_gate[l], w_up[l], w_down[l])
    out = _combine(dest2.reshape(n_tok // TM_COMBINE, 1, TOP_K * TM_COMBINE), route, x1, ys)
    return out.reshape(bsz, s_len, D_MODEL)
```

```python
import functools
import math

import jax
import jax.numpy as jnp
import numpy as np
from jax import lax
from jax.experimental import pallas as pl
from jax.experimental.pallas import tpu as pltpu

D_MODEL = 1024
CHUNK = 64
A_HEADS = 4
A_HEAD_DIM = 64
A_VDIM = 2 * A_HEAD_DIM
B_HEADS = 8
B_HEAD_DIM = 64
B_LEFT_CHUNKS = 8
B_MAX_REL = 128
T5_BUCKETS = 32
T5_MAX_DIST = 128
N_GROUPS = 4
EXPERTS_PER_GROUP = 8
N_EXPERTS = N_GROUPS * EXPERTS_PER_GROUP
TOP_K = 2
D_EXPERT = 512
EPS = 1e-6
NEG = -1e30
LOG2E = 1.0 / math.log(2.0)
DENOM_FLOOR = 2.0 ** -100

CHUNK_SHIFT = CHUNK.bit_length() - 1
assert 1 << CHUNK_SHIFT == CHUNK
LANES = 128
A_W = A_HEADS * 2 * A_HEAD_DIM
B_W = B_HEADS * B_HEAD_DIM
PROJ_W = 4 * 256 + 4 * 512 + 2 * D_MODEL

TM_PROJ = 512
TQ = 512
TK = 512
FAR_UNROLL = 4
TM_POST = 1024
TM_RANK = 1024
TM_DISPATCH = 2048
TM_COMBINE = 1024
ROW_UNROLL = 8
FFN_BLK = 256
VMEM_LIMIT = 56 * 1024 * 1024


def _cparams(sem):
    return pltpu.CompilerParams(dimension_semantics=sem, vmem_limit_bytes=VMEM_LIMIT)


def _proj_kernel(x_ref, g1_ref, w_ref, gn_ref, gmat_ref,
                 qa_ref, ka_ref, va_ref, qb_ref, kb_ref, vb_ref, ga_ref, gb_ref):
    x = x_ref[...]
    xn = x * lax.rsqrt(jnp.mean(x * x, axis=-1, keepdims=True) + EPS) * g1_ref[...]
    xn = xn.astype(jnp.bfloat16)

    def slab(c0, width):
        return jnp.dot(xn, w_ref[:, c0:c0 + width], preferred_element_type=jnp.float32)

    def headnorm(y, gi):
        sq = (y * y).astype(jnp.bfloat16)
        half = gmat_ref.shape[0]
        ss = jnp.concatenate([jnp.dot(sq[:, c:c + half], gmat_ref[...], preferred_element_type=jnp.float32)
                              for c in range(0, y.shape[1], half)], axis=1)
        return y * lax.rsqrt(ss * (1.0 / A_HEAD_DIM) + EPS) * gn_ref[gi:gi + 1, :]

    qa_ref[...] = headnorm(slab(0, 512), 0).astype(jnp.bfloat16)
    ka_ref[...] = headnorm(slab(512, 512), 1).astype(jnp.bfloat16)
    va_ref[...] = slab(1024, 512).astype(jnp.bfloat16)
    qb_ref[...] = headnorm(slab(1536, 512), 2).astype(jnp.bfloat16)
    kb_ref[...] = headnorm(slab(2048, 512), 3).astype(jnp.bfloat16)
    vb_ref[...] = slab(2560, 512).astype(jnp.bfloat16)
    for j in range(2):
        ga_ref[:, j * 512:(j + 1) * 512] = jax.nn.sigmoid(slab(3072 + j * 512, 512)).astype(jnp.bfloat16)
        gb_ref[:, j * 512:(j + 1) * 512] = jax.nn.sigmoid(slab(4096 + j * 512, 512)).astype(jnp.bfloat16)


def _proj(x2, g1, w_perm, gn, gmat):
    t = x2.shape[0]
    n = t // TM_PROJ
    row = lambda w: pl.BlockSpec((TM_PROJ, w), lambda i: (i, 0))
    full = lambda a: pl.BlockSpec(a.shape, lambda i: (0,) * a.ndim)
    outs = [jax.ShapeDtypeStruct((t, 512), jnp.bfloat16)] * 6 + [jax.ShapeDtypeStruct((t, D_MODEL), jnp.bfloat16)] * 2
    return pl.pallas_call(
        _proj_kernel,
        grid=(n,),
        in_specs=[row(D_MODEL), full(g1), full(w_perm), full(gn), full(gmat)],
        out_specs=[row(512)] * 6 + [row(D_MODEL)] * 2,
        out_shape=outs,
        compiler_params=_cparams(("arbitrary",)),
        name="proj",
    )(x2, g1, w_perm, gn, gmat)


def _diff_kernel(scal_ref, q_ref, k_ref, v_ref, vec_ref, vect_ref, gsub_ref, o_ref,
                 bias_sc, biast_sc, vt_sc, kmaxt_sc, qz_sc, qzt_sc, refn_sc, reff_sc, l_sc, acc_sc,
                 m_sc, acc2_sc, mf_sc, accf_sc, *, n_near):
    f32 = jnp.float32
    h = pl.program_id(1)
    qi = pl.program_id(2)
    nc = TK // LANES
    r2 = 2 * TQ
    n_heads = pl.num_programs(1)

    def near_span(j):
        return slice((n_near - 1 - j) * TK, (n_near - j) * TK)

    @pl.when(qi == 0)
    def _():
        row = lax.broadcasted_iota(jnp.int32, (TQ, TK), 0)
        col = lax.broadcasted_iota(jnp.int32, (TQ, TK), 1)
        for j in range(n_near):
            ok = lax.shift_right_arithmetic(row, CHUNK_SHIFT) >= lax.shift_right_arithmetic(col - j * TK, CHUNK_SHIFT)
            okt = lax.shift_right_arithmetic(col, CHUNK_SHIFT) >= lax.shift_right_arithmetic(row - j * TK, CHUNK_SHIFT)
            for half in range(2):
                vec = jnp.broadcast_to(vec_ref[0, j, half:half + 1, :], (TQ, TQ + TK))
                tile = pltpu.roll(vec, 0, 1, stride=1, stride_axis=0)[:, :TK]
                bias_sc[half * TQ:(half + 1) * TQ, near_span(j)] = jnp.where(ok, tile, NEG)
                vect = jnp.broadcast_to(vect_ref[0, j, half:half + 1, :], (TK, TQ + TK))
                tilet = pltpu.roll(vect, 0, 1, stride=1, stride_axis=0)[:, :TQ]
                biast_sc[near_span(j), half * TQ:(half + 1) * TQ] = jnp.where(okt, tilet, NEG)
        vt_sc[...] = v_ref[...].T
        kf = k_ref[...].astype(f32)
        ksq = kf * kf
        klane = lax.broadcasted_iota(jnp.int32, ksq.shape, 1)
        for half in range(2):
            sel = (klane < 64) if half == 0 else (klane >= 64)
            norm2 = jnp.max(jnp.sum(jnp.where(sel, ksq, 0.0), axis=-1, keepdims=True), axis=0, keepdims=True)
            kmaxt_sc[:, half * TQ:(half + 1) * TQ] = jnp.broadcast_to(jnp.sqrt(norm2), (8, TQ))

    q = q_ref[...]
    lane = lax.broadcasted_iota(jnp.int32, q.shape, 1)
    zero = jnp.zeros_like(q)
    q1 = jnp.where(lane < 64, q, zero)
    q2 = jnp.where(lane >= 64, q, zero)
    qz_sc[0:TQ, :] = q1
    qz_sc[TQ:r2, :] = q2
    qzt_sc[:, 0:TQ] = q1.T
    qzt_sc[:, TQ:r2] = q2.T
    cfar = scal_ref[1 + h]
    bias_max = scal_ref[1 + n_heads + h]
    n_far = jnp.maximum(qi - (n_near - 1), 0)

    def key_block(kblk):
        return k_ref[pl.ds(pl.multiple_of(kblk * TK, TK), TK), :]

    def finish(od):
        od = od * lax.rsqrt(jnp.mean(od * od, axis=-1, keepdims=True) + EPS) * gsub_ref[...]
        o_ref[...] = od.astype(o_ref.dtype)

    def fixed_reference_pass():
        qf = qzt_sc[...].astype(f32)
        qnorm = jnp.sqrt(jnp.sum(qf * qf, axis=0, keepdims=True))
        refn_sc[...] = qnorm * kmaxt_sc[...] + bias_max
        l_sc[...] = jnp.zeros(l_sc.shape, f32)
        acc_sc[...] = jnp.zeros(acc_sc.shape, f32)

        def accumulate(st, kblk, ref_ref):
            p = jnp.exp2(st.reshape(TK // 8, 8, r2) - ref_ref[...][None])
            l_sc[...] += jnp.sum(p, axis=0)
            vt = vt_sc[:, pl.ds(pl.multiple_of(kblk * TK, TK), TK)]
            acc_sc[...] += jnp.dot(vt, p.reshape(TK, r2).astype(jnp.bfloat16), preferred_element_type=f32)

        def scores_t(kblk):
            return jnp.dot(key_block(kblk), qzt_sc[...], preferred_element_type=f32)

        for j in range(n_near):
            def near(j=j):
                accumulate(scores_t(qi - j) + biast_sc[near_span(j), :], qi - j, refn_sc)
            if j == 0:
                near()
            else:
                pl.when(qi >= j)(near)

        reff_sc[...] = refn_sc[...] - cfar

        def far(kblk, carry):
            accumulate(scores_t(kblk), kblk, reff_sc)
            return carry

        def far_group(i, carry):
            for u in range(FAR_UNROLL):
                far(FAR_UNROLL * i + u, carry)
            return carry

        lax.fori_loop(0, n_far // FAR_UNROLL, far_group, 0)
        lax.fori_loop((n_far // FAR_UNROLL) * FAR_UNROLL, n_far, far, 0)

        denom = jnp.sum(l_sc[...], axis=0, keepdims=True)
        ot = acc_sc[...] / denom
        finish((ot[:, 0:TQ] - scal_ref[0] * ot[:, TQ:r2]).T)
        return denom

    def exact_pass():
        def scores(kblk):
            return lax.dot_general(qz_sc[...], key_block(kblk), (((1,), (1,)), ((), ())), preferred_element_type=f32)

        def update(s, kblk, m_ref, a_ref):
            cols = [s[:, c * LANES:(c + 1) * LANES] for c in range(nc)]
            m_old = m_ref[...]
            m_new = jnp.maximum(m_old, jnp.max(functools.reduce(jnp.maximum, cols), axis=-1, keepdims=True))
            alpha = jnp.exp2(m_old - m_new)
            ps = [jnp.exp2(c - m_new) for c in cols]
            p = jnp.concatenate([x.astype(jnp.bfloat16) for x in ps], axis=1)
            v = v_ref[pl.ds(pl.multiple_of(kblk * TK, TK), TK), :]
            a_ref[:, 0:LANES] = alpha * a_ref[:, 0:LANES] + jnp.dot(p, v, preferred_element_type=f32)
            a_ref[:, LANES:2 * LANES] = alpha * a_ref[:, LANES:2 * LANES] + functools.reduce(jnp.add, ps)
            m_ref[...] = m_new

        m_sc[...] = jnp.full(m_sc.shape, NEG, f32)
        acc2_sc[...] = jnp.zeros(acc2_sc.shape, f32)
        for j in range(n_near):
            def near(j=j):
                update(scores(qi - j) + bias_sc[:, near_span(j)], qi - j, m_sc, acc2_sc)
            if j == 0:
                near()
            else:
                pl.when(qi >= j)(near)

        mf_sc[...] = jnp.full(mf_sc.shape, NEG, f32)
        accf_sc[...] = jnp.zeros(accf_sc.shape, f32)

        def far(kblk, carry):
            update(scores(kblk), kblk, mf_sc, accf_sc)
            return carry

        lax.fori_loop(0, n_far, far, 0)
        mf = mf_sc[...] + cfar
        mn = m_sc[...]
        m = jnp.maximum(mf, mn)
        wf = jnp.exp2(mf - m)
        wn = jnp.exp2(mn - m)
        tot = (jnp.concatenate([wf, wf], axis=1) * accf_sc[...]
               + jnp.concatenate([wn, wn], axis=1) * acc2_sc[...])
        o = tot[:, 0:LANES] / jnp.sum(tot[:, LANES:2 * LANES], axis=-1, keepdims=True)
        finish(o[:TQ] - scal_ref[0] * o[TQ:])

    denom = fixed_reference_pass()
    pl.when(jnp.logical_not(jnp.min(denom) >= DENOM_FLOOR))(exact_pass)


def _diff_attention(scal, q, k, v, vecs, vecs_t, gsub, *, bsz, s_len):
    n_blk = q.shape[1] // LANES
    nq = s_len // TQ
    n_near = vecs.shape[1]
    assert vecs.shape == vecs_t.shape == (n_blk, n_near, 2, TQ + TK)
    kern = functools.partial(_diff_kernel, n_near=n_near)
    f32, bf16 = jnp.float32, jnp.bfloat16
    vm = pltpu.VMEM
    vec_spec = pl.BlockSpec((1, n_near, 2, TQ + TK), lambda b, h, i: (h, 0, 0, 0))
    return pl.pallas_call(
        kern,
        grid=(bsz, n_blk, nq),
        in_specs=[
            pl.BlockSpec(memory_space=pltpu.SMEM),
            pl.BlockSpec((TQ, LANES), lambda b, h, i: (b * nq + i, h)),
            pl.BlockSpec((s_len, LANES), lambda b, h, i: (b, h)),
            pl.BlockSpec((s_len, LANES), lambda b, h, i: (b, h)),
            vec_spec, vec_spec,
            pl.BlockSpec((1, LANES), lambda b, h, i: (0, 0)),
        ],
        out_specs=pl.BlockSpec((TQ, LANES), lambda b, h, i: (b * nq + i, h)),
        out_shape=jax.ShapeDtypeStruct(q.shape, bf16),
        scratch_shapes=[vm((2 * TQ, n_near * TK), f32), vm((n_near * TK, 2 * TQ), f32),
                        vm((LANES, s_len), bf16), vm((8, 2 * TQ), f32),
                        vm((2 * TQ, LANES), bf16), vm((LANES, 2 * TQ), bf16),
                        vm((8, 2 * TQ), f32), vm((8, 2 * TQ), f32), vm((8, 2 * TQ), f32),
                        vm((LANES, 2 * TQ), f32),
                        vm((2 * TQ, LANES), f32), vm((2 * TQ, 2 * LANES), f32),
                        vm((2 * TQ, LANES), f32), vm((2 * TQ, 2 * LANES), f32)],
        compiler_params=_cparams(("arbitrary", "arbitrary", "arbitrary")),
        name="attn_diff",
    )(scal, q, k, v, vecs, vecs_t, gsub)


BAND_HQ = TQ // 2
BAND_W = B_LEFT_CHUNKS * CHUNK + BAND_HQ


def _band_kernel(q_ref, k_ref, v_ref, vec_ref, o_ref, bias_sc, qz_sc):
    f32 = jnp.float32
    qi = pl.program_id(2)
    hq, w = BAND_HQ, BAND_W
    back = w - hq

    @pl.when(qi == 0)
    def _():
        qchunk = lax.shift_right_arithmetic(lax.broadcasted_iota(jnp.int32, (hq, w), 0), CHUNK_SHIFT)
        kchunk = lax.shift_right_arithmetic(lax.broadcasted_iota(jnp.int32, (hq, w), 1) - back, CHUNK_SHIFT)
        dchunk = qchunk - kchunk
        allowed = (dchunk >= 0) & (dchunk <= B_LEFT_CHUNKS)
        for half in range(2):
            vec = jnp.broadcast_to(vec_ref[0, half:half + 1, :], (hq, hq + w))
            tile = pltpu.roll(vec, 0, 1, stride=1, stride_axis=0)[:, :w]
            bias_sc[half * hq:(half + 1) * hq, :] = jnp.where(allowed, tile, NEG)

    q = q_ref[...]
    lane = lax.broadcasted_iota(jnp.int32, (hq, LANES), 1)
    zero = jnp.zeros((hq, LANES), q.dtype)
    for u in range(2):
        qu = q[u * hq:(u + 1) * hq, :]
        qz_sc[(2 * u) * hq:(2 * u + 1) * hq, :] = jnp.where(lane < 64, qu, zero)
        qz_sc[(2 * u + 1) * hq:(2 * u + 2) * hq, :] = jnp.where(lane >= 64, qu, zero)

    def scores(u, k0, n_keys):
        k = k_ref[pl.ds(k0, n_keys), :]
        return lax.dot_general(qz_sc[2 * u * hq:(2 * u + 2) * hq, :], k, (((1,), (1,)), ((), ())),
                               preferred_element_type=f32) + bias_sc[:, w - n_keys:w]

    def softmax(s):
        cols = [s[:, c * LANES:(c + 1) * LANES] for c in range(s.shape[1] // LANES)]
        m = jnp.max(functools.reduce(jnp.maximum, cols), axis=-1, keepdims=True)
        ps = [jnp.exp2(c - m) for c in cols]
        denom = jnp.sum(functools.reduce(jnp.add, ps), axis=-1, keepdims=True)
        return jnp.concatenate([x.astype(jnp.bfloat16) for x in ps], axis=1), denom

    def finish(u, p, denom, k0):
        o = jnp.dot(p, v_ref[pl.ds(k0, p.shape[1]), :], preferred_element_type=f32) / denom
        o_ref[u * hq:(u + 1) * hq, :] = jnp.where(lane < 64, o[:hq], o[hq:]).astype(o_ref.dtype)

    def step(k0s, n_keys):
        ss = [scores(u, k0s[u], n_keys[u]) for u in range(2)]
        pd = [softmax(x) for x in ss]
        for u in range(2):
            finish(u, pd[u][0], pd[u][1], k0s[u])

    @pl.when(qi == 0)
    def _():
        step((0, 0), (hq, 2 * hq))

    @pl.when(qi > 0)
    def _():
        start = pl.multiple_of(qi * TQ - back, hq)
        step((start, pl.multiple_of(start + hq, hq)), (w, w))


def _band_attention(q, k, v, vecs, *, bsz, s_len):
    n_blk = q.shape[1] // LANES
    nq = s_len // TQ
    assert vecs.shape == (n_blk, 2, BAND_HQ + BAND_W)
    return pl.pallas_call(
        _band_kernel,
        grid=(bsz, n_blk, nq),
        in_specs=[
            pl.BlockSpec((TQ, LANES), lambda b, h, i: (b * nq + i, h)),
            pl.BlockSpec((s_len, LANES), lambda b, h, i: (b, h)),
            pl.BlockSpec((s_len, LANES), lambda b, h, i: (b, h)),
            pl.BlockSpec((1, 2, BAND_HQ + BAND_W), lambda b, h, i: (h, 0, 0)),
        ],
        out_specs=pl.BlockSpec((TQ, LANES), lambda b, h, i: (b * nq + i, h)),
        out_shape=jax.ShapeDtypeStruct(q.shape, jnp.bfloat16),
        scratch_shapes=[pltpu.VMEM((2 * BAND_HQ, BAND_W), jnp.float32),
                        pltpu.VMEM((2 * TQ, LANES), jnp.bfloat16)],
        compiler_params=_cparams(("arbitrary", "arbitrary", "arbitrary")),
        name="attn_band",
    )(q, k, v, vecs)


def _post_kernel(oa_ref, ob_ref, ga_ref, gb_ref, x_ref, wa_ref, wb_ref, wo_ref, g2_ref,
                 wr2_ref, br_ref, x1_ref, hn_ref, route_ref):
    f32 = jnp.float32
    ya = jnp.dot(oa_ref[...], wa_ref[...], preferred_element_type=f32)
    yb = jnp.dot(ob_ref[...], wb_ref[...], preferred_element_type=f32)
    mixed = ga_ref[...].astype(f32) * ya + gb_ref[...].astype(f32) * yb
    x1 = x_ref[...] + jnp.dot(mixed.astype(jnp.bfloat16), wo_ref[...], preferred_element_type=f32)
    x1_ref[...] = x1
    hn = x1 * lax.rsqrt(jnp.mean(x1 * x1, axis=-1, keepdims=True) + EPS) * g2_ref[...]
    hh = hn.astype(jnp.bfloat16)
    bits = lax.bitcast_convert_type(hh.astype(f32), jnp.uint32)
    half = D_MODEL // 2
    hn_ref[...] = (bits[:, :half] & jnp.uint32(0xFFFF0000)) | lax.shift_right_logical(bits[:, half:], jnp.uint32(16))

    hl = (hn - hh.astype(f32)).astype(jnp.bfloat16)
    hw = jnp.dot(hh, wr2_ref[...], preferred_element_type=f32)
    lg = (hw[:, 0:LANES] + hw[:, LANES:2 * LANES]
          + jnp.dot(hl, wr2_ref[:, 0:LANES], preferred_element_type=f32)) + br_ref[...]

    lanei = lax.broadcasted_iota(jnp.int32, lg.shape, 1)
    lanef = lanei.astype(f32)
    big = 999.0
    gmask = lanei < N_GROUPS
    gl = jnp.where(gmask, lg, NEG)
    gm = jnp.max(gl, axis=-1, keepdims=True)
    ge = jnp.where(gmask, jnp.exp(gl - gm), 0.0)
    gp = ge / jnp.sum(ge, axis=-1, keepdims=True)
    p_g = jnp.max(gp, axis=-1, keepdims=True)
    gidx = jnp.min(jnp.where(gmask & (gp == p_g), lanef, big), axis=-1, keepdims=True)
    egrp = lax.shift_right_arithmetic(lanei - N_GROUPS, 3).astype(f32)
    emask = (lanei >= N_GROUPS) & (lanei < N_GROUPS + N_EXPERTS) & (egrp == gidx)
    el = jnp.where(emask, lg, NEG)
    v1 = jnp.max(el, axis=-1, keepdims=True)
    i1 = jnp.min(jnp.where(emask & (el == v1), lanef, big), axis=-1, keepdims=True)
    emask2 = emask & (lanef != i1)
    el2 = jnp.where(emask2, lg, NEG)
    v2 = jnp.max(el2, axis=-1, keepdims=True)
    i2 = jnp.min(jnp.where(emask2 & (el2 == v2), lanef, big), axis=-1, keepdims=True)
    t = jnp.exp(v2 - v1)
    den = 1.0 + t
    w1 = p_g * (1.0 / den)
    w2 = p_g * (t / den)
    route = jnp.where(lanei == 0, i1 - N_GROUPS,
                      jnp.where(lanei == 1, i2 - N_GROUPS,
                                jnp.where(lanei == 2, w1, jnp.where(lanei == 3, w2, 0.0))))
    route_ref[...] = route


def _post(oa, ob, ga, gb, x2, wa, wb, wo, g2, wr2, br):
    t = x2.shape[0]
    n = t // TM_POST
    row = lambda w: pl.BlockSpec((TM_POST, w), lambda i: (i, 0))
    full = lambda a: pl.BlockSpec(a.shape, lambda i: (0,) * a.ndim)
    return pl.pallas_call(
        _post_kernel,
        grid=(n,),
        in_specs=[row(512), row(512), row(D_MODEL), row(D_MODEL), row(D_MODEL),
                  full(wa), full(wb), full(wo), full(g2), full(wr2), full(br)],
        out_specs=[row(D_MODEL), row(D_MODEL // 2), row(LANES)],
        out_shape=[jax.ShapeDtypeStruct((t, D_MODEL), jnp.float32),
                   jax.ShapeDtypeStruct((t, D_MODEL // 2), jnp.uint32),
                   jax.ShapeDtypeStruct((t, LANES), jnp.float32)],
        compiler_params=_cparams(("arbitrary",)),
        name="post",
    )(oa, ob, ga, gb, x2, wa, wb, wo, g2, wr2, br)


def _rank_kernel(route_ref, ltri_ref, utri_ref, dest_ref, cnt_ref, cnt_sc, pstart_sc, base_sc):
    f32 = jnp.float32
    p = pl.program_id(0)
    i = pl.program_id(1)
    route = route_ref[...]
    lanef = lax.broadcasted_iota(jnp.int32, route.shape, 1).astype(f32)
    oh1 = (lanef == route[:, 0:1]).astype(f32)
    oh2 = (lanef == route[:, 1:2]).astype(f32)
    both = oh1 + oh2
    colsum = jnp.sum(both, axis=0, keepdims=True)

    @pl.when((p == 0) & (i == 0))
    def _():
        cnt_sc[...] = jnp.zeros(cnt_sc.shape, f32)

    @pl.when(p == 0)
    def _():
        cnt_sc[...] += colsum
        dest_ref[...] = jnp.zeros(dest_ref.shape, f32)
        cnt_ref[...] = jnp.zeros(cnt_ref.shape, f32)

    @pl.when((p == 1) & (i == 0))
    def _():
        cnt = cnt_sc[...]
        chi = jnp.floor(cnt * (1.0 / 256.0))
        clo = cnt - chi * 256.0
        split = jnp.concatenate([jnp.broadcast_to(chi, (8, LANES)), jnp.broadcast_to(clo, (8, LANES))], axis=0)
        excl = jnp.dot(split.astype(jnp.bfloat16), utri_ref[...], preferred_element_type=f32)
        pstart_sc[...] = excl[0:1] * 256.0 + excl[8:9]
        base_sc[...] = jnp.zeros(base_sc.shape, f32)

    @pl.when(p == 1)
    def _():
        prior = jnp.dot(ltri_ref[...], both.astype(jnp.bfloat16), preferred_element_type=f32)
        slot = prior + base_sc[...] + pstart_sc[...]
        d1 = jnp.sum(oh1 * slot, axis=-1, keepdims=True)
        d2 = jnp.sum(oh2 * slot, axis=-1, keepdims=True)
        dest_ref[...] = jnp.where(lanef == 0.0, d1, jnp.where(lanef == 1.0, d2, 0.0))
        base_sc[...] += colsum
        cnt_ref[...] = jnp.broadcast_to(cnt_sc[...], cnt_ref.shape)


def _rank(route, ltri, utri):
    t = route.shape[0]
    n = t // TM_RANK
    full = lambda a: pl.BlockSpec(a.shape, lambda p, i: (0,) * a.ndim)
    row1 = lambda: pltpu.VMEM((1, LANES), jnp.float32)
    return pl.pallas_call(
        _rank_kernel,
        grid=(2, n),
        in_specs=[pl.BlockSpec((TM_RANK, LANES), lambda p, i: (i, 0)), full(ltri), full(utri)],
        out_specs=[pl.BlockSpec((TM_RANK, LANES), lambda p, i: (i * p, 0)),
                   pl.BlockSpec((8, LANES), lambda p, i: (0, 0))],
        out_shape=[jax.ShapeDtypeStruct((t, LANES), jnp.float32),
                   jax.ShapeDtypeStruct((8, LANES), jnp.float32)],
        scratch_shapes=[row1(), row1(), row1()],
        compiler_params=_cparams(("arbitrary", "arbitrary")),
        name="rank",
    )(route, ltri, utri)


def _dispatch_kernel(dest_ref, hn_ref, xs_ref, sems):
    def row_copy(r, k):
        d = dest_ref[0, 0, 2 * r + k]
        return pltpu.make_async_copy(hn_ref.at[pl.ds(r, 1), :], xs_ref.at[pl.ds(d, 1), :], sems.at[k])

    def issue(r, c):
        row_copy(r, 0).start(priority=0)
        row_copy(r, 1).start(priority=1)
        return c

    lax.fori_loop(0, TM_DISPATCH, issue, 0, unroll=ROW_UNROLL)
    for k in range(TOP_K):
        pltpu.make_async_copy(hn_ref, xs_ref.at[pl.ds(0, TM_DISPATCH), :], sems.at[k]).wait()


def _dispatch(dest3, hn, n_rows):
    t, w = hn.shape
    n = t // TM_DISPATCH
    return pl.pallas_call(
        _dispatch_kernel,
        grid=(n,),
        in_specs=[pl.BlockSpec((1, 1, 2 * TM_DISPATCH), lambda i: (i, 0, 0), memory_space=pltpu.SMEM),
                  pl.BlockSpec((TM_DISPATCH, w), lambda i: (i, 0))],
        out_specs=pl.BlockSpec(memory_space=pl.ANY),
        out_shape=jax.ShapeDtypeStruct((n_rows, w), hn.dtype),
        scratch_shapes=[pltpu.SemaphoreType.DMA((2,))],
        compiler_params=_cparams(("arbitrary",)),
        name="dispatch",
    )(dest3, hn)


def _ffn_kernel(tile_ref, exp_ref, lo_ref, hi_ref, cast_ref, init_ref,
                xs_ref, wg_ref, wu_ref, wd_ref, ys_ref, wg_sc, wu_sc, wd_sc):
    v = pl.program_id(0)
    lo = lo_ref[v]
    hi = hi_ref[v]

    @pl.when(init_ref[v] == 1)
    def _():
        ys_ref[...] = jnp.zeros(ys_ref.shape, ys_ref.dtype)

    @pl.when(hi > lo)
    def _():
        @pl.when(cast_ref[v] == 1)
        def _():
            wg_sc[...] = wg_ref[0].astype(jnp.bfloat16)
            wu_sc[...] = wu_ref[0].astype(jnp.bfloat16)
            wd_sc[...] = wd_ref[0].astype(jnp.bfloat16)

        pk = xs_ref[...]
        x = jnp.concatenate(
            [lax.bitcast_convert_type(pk & jnp.uint32(0xFFFF0000), jnp.float32),
             lax.bitcast_convert_type(lax.shift_left(pk, jnp.uint32(16)), jnp.float32)], axis=1).astype(jnp.bfloat16)
        g = jnp.dot(x, wg_sc[...], preferred_element_type=jnp.float32)
        u = jnp.dot(x, wu_sc[...], preferred_element_type=jnp.float32)
        hb = (g * jax.nn.sigmoid(g) * u).astype(jnp.bfloat16)
        y = jnp.dot(hb, wd_sc[...], preferred_element_type=jnp.float32)
        rows = tile_ref[v] * FFN_BLK + lax.broadcasted_iota(jnp.int32, y.shape, 0)
        ys_ref[...] = jnp.where((rows >= lo) & (rows < hi), y, ys_ref[...])


def _ffn(seg, xs, w_gate, w_up, w_down):
    n_rows = xs.shape[0]
    n_seg = seg[0].shape[0]
    grid_spec = pltpu.PrefetchScalarGridSpec(
        num_scalar_prefetch=6,
        grid=(n_seg,),
        in_specs=[
            pl.BlockSpec((FFN_BLK, D_MODEL // 2), lambda v, t, e, *_: (t[v], 0)),
            pl.BlockSpec((1, D_MODEL, D_EXPERT), lambda v, t, e, *_: (e[v], 0, 0)),
            pl.BlockSpec((1, D_MODEL, D_EXPERT), lambda v, t, e, *_: (e[v], 0, 0)),
            pl.BlockSpec((1, D_EXPERT, D_MODEL), lambda v, t, e, *_: (e[v], 0, 0)),
        ],
        out_specs=pl.BlockSpec((FFN_BLK, D_MODEL), lambda v, t, e, *_: (t[v], 0)),
        scratch_shapes=[pltpu.VMEM((D_MODEL, D_EXPERT), jnp.bfloat16),
                        pltpu.VMEM((D_MODEL, D_EXPERT), jnp.bfloat16),
                        pltpu.VMEM((D_EXPERT, D_MODEL), jnp.bfloat16)],
    )
    return pl.pallas_call(
        _ffn_kernel,
        grid_spec=grid_spec,
        out_shape=jax.ShapeDtypeStruct((n_rows, D_MODEL), jnp.float32),
        compiler_params=_cparams(("arbitrary",)),
        name="ffn",
    )(*seg, xs, w_gate, w_up, w_down)


def _segments(counts, n_rows):
    i32 = jnp.int32
    n_tiles = n_rows // FFN_BLK
    n_seg = n_tiles + N_EXPERTS
    tri = jnp.tril(jnp.ones((N_EXPERTS, N_EXPERTS), i32))
    ends = jnp.sum(tri * counts[None, :], axis=1)
    starts = ends - counts
    edges = jnp.arange(n_tiles, dtype=i32) * FFN_BLK
    rank_e = jnp.arange(n_tiles, dtype=i32) + jnp.sum(starts[None, :] <= edges[:, None], axis=1)
    rank_s = jnp.arange(N_EXPERTS, dtype=i32) + jnp.sum(edges[None, :] < starts[:, None], axis=1)
    seg = jnp.arange(n_seg, dtype=i32)
    lo = (jnp.sum(jnp.where(rank_e[None, :] == seg[:, None], edges[None, :], 0), axis=1)
          + jnp.sum(jnp.where(rank_s[None, :] == seg[:, None], starts[None, :], 0), axis=1))
    hi = jnp.concatenate([lo[1:], jnp.array([n_rows], i32)])
    valid = hi > lo
    tile = jnp.minimum(lo // FFN_BLK, n_tiles - 1)
    expert = jnp.minimum(jnp.sum(ends[None, :] <= lo[:, None], axis=1), N_EXPERTS - 1).astype(i32)
    upto = seg[None, :] <= seg[:, None]
    expert = jnp.max(jnp.where(upto & valid[None, :], expert[None, :], 0), axis=1)
    prev_expert = jnp.concatenate([jnp.array([-1], i32), expert[:-1]])
    first_valid = valid & (jnp.sum(jnp.where(upto & valid[None, :], 1, 0), axis=1) == 1)
    cast = valid & ((expert != prev_expert) | first_valid)
    prev_tile = jnp.concatenate([jnp.array([-1], i32), tile[:-1]])
    init = tile != prev_tile
    return (tile.astype(i32), expert, lo.astype(i32), hi.astype(i32), cast.astype(i32), init.astype(i32))


def _combine_kernel(dest_ref, route_ref, x1_ref, ys_ref, out_ref, y0_sc, y1_sc, sems):
    def row_copy(r, k):
        d = dest_ref[0, 0, 2 * r + k]
        dst = y0_sc if k == 0 else y1_sc
        return pltpu.make_async_copy(ys_ref.at[pl.ds(d, 1), :], dst.at[pl.ds(r, 1), :], sems.at[k])

    def issue(r, c):
        row_copy(r, 0).start(priority=0)
        row_copy(r, 1).start(priority=1)
        return c

    lax.fori_loop(0, TM_COMBINE, issue, 0, unroll=ROW_UNROLL)
    pltpu.make_async_copy(ys_ref.at[pl.ds(0, TM_COMBINE), :], y0_sc, sems.at[0]).wait()
    pltpu.make_async_copy(ys_ref.at[pl.ds(0, TM_COMBINE), :], y1_sc, sems.at[1]).wait()
    route = route_ref[...]
    out_ref[...] = x1_ref[...] + (route[:, 2:3] * y0_sc[...] + route[:, 3:4] * y1_sc[...])


def _combine(dest3, route, x1, ys):
    t = x1.shape[0]
    n = t // TM_COMBINE
    return pl.pallas_call(
        _combine_kernel,
        grid=(n,),
        in_specs=[pl.BlockSpec((1, 1, 2 * TM_COMBINE), lambda i: (i, 0, 0), memory_space=pltpu.SMEM),
                  pl.BlockSpec((TM_COMBINE, LANES), lambda i: (i, 0)),
                  pl.BlockSpec((TM_COMBINE, D_MODEL), lambda i: (i, 0)),
                  pl.BlockSpec(memory_space=pl.ANY)],
        out_specs=pl.BlockSpec((TM_COMBINE, D_MODEL), lambda i: (i, 0)),
        out_shape=jax.ShapeDtypeStruct((t, D_MODEL), jnp.float32),
        scratch_shapes=[pltpu.VMEM((TM_COMBINE, D_MODEL), jnp.float32),
                        pltpu.VMEM((TM_COMBINE, D_MODEL), jnp.float32),
                        pltpu.SemaphoreType.DMA((2,))],
        compiler_params=_cparams(("arbitrary",)),
        name="combine",
    )(dest3, route, x1, ys)


def _t5_bucket(rel):
    nb = T5_BUCKETS // 2
    max_exact = nb // 2
    side = jnp.where(rel > 0, nb, 0)
    n = jnp.abs(rel)
    nf = jnp.maximum(n, 1).astype(jnp.float32)
    large = max_exact + (jnp.log(nf / max_exact) / math.log(T5_MAX_DIST / max_exact)
                         * (nb - max_exact)).astype(jnp.int32)
    large = jnp.minimum(large, nb - 1)
    return side + jnp.where(n < max_exact, n, large)


def _rel_offsets(j, transposed=False):
    i = jnp.arange(TQ + TK)
    if transposed:
        return jnp.where(i < TQ, -i, (TQ + TK) - i) - j * TK
    return jnp.where(i < TK, i, i - (TQ + TK)) - j * TK


def _diff_bias_vecs(t5_table, transposed=False):
    vecs = jnp.stack([t5_table[_t5_bucket(_rel_offsets(j, transposed))].astype(jnp.float32).T for j in range(2)],
                     axis=1)
    vecs = jnp.stack([vecs, vecs], axis=2)
    far = t5_table[_t5_bucket(jnp.array(-(TK + 1)))].astype(jnp.float32)
    return vecs, far


def _band_bias_vecs(rel_table):
    i = jnp.arange(BAND_HQ + BAND_W)
    rel = jnp.where(i < BAND_W, i, i - (BAND_HQ + BAND_W)) - (BAND_W - BAND_HQ)
    vecs = rel_table[jnp.clip(rel, -B_MAX_REL, B_MAX_REL) + B_MAX_REL].astype(jnp.float32).T
    return vecs.reshape(B_HEADS // 2, 2, BAND_HQ + BAND_W)


def kernel(x, norm1_g, w_in, a_qnorm_g, a_knorm_g, a_lambda, a_subln_g, t5_table, b_qnorm_g, b_knorm_g,
           b_rel_table, w_branch_a, w_branch_b, w_out, norm2_g, w_router_group, b_router_group,
           w_router_expert, b_router_expert, w_gate, w_up, w_down):
    bsz, s_len, _ = x.shape
    n_tok = bsz * s_len
    f32, bf16 = jnp.float32, jnp.bfloat16
    assert s_len % TQ == 0 and TQ == TK and TQ % CHUNK == 0 and n_tok % TM_PROJ == 0
    assert TK >= T5_MAX_DIST and BAND_HQ % CHUNK == 0 and BAND_W % LANES == 0 and BAND_W - BAND_HQ <= TQ
    l = 0
    x2 = x.reshape(n_tok, D_MODEL)

    w = w_in[l]
    qk = w[:, :1024].reshape(D_MODEL, 2, 2, A_HEADS, A_HEAD_DIM)
    qk = qk.transpose(0, 1, 3, 2, 4).reshape(D_MODEL, 1024)
    w_perm = jnp.concatenate([qk, w[:, 1024:]], axis=1).astype(bf16)
    gn = jnp.stack([jnp.tile(a_qnorm_g[l] * (A_HEAD_DIM ** -0.5 * LOG2E), 8), jnp.tile(a_knorm_g[l], 8),
                    jnp.tile(b_qnorm_g[l] * (B_HEAD_DIM ** -0.5 * LOG2E), 8), jnp.tile(b_knorm_g[l], 8)]).astype(f32)
    gmat = jnp.asarray(np.kron(np.eye(4), np.ones((A_HEAD_DIM, A_HEAD_DIM))), dtype=bf16)

    qa, ka, va, qb, kb, vb, ga, gb = _proj(x2, norm1_g[l][None].astype(f32), w_perm, gn, gmat)

    lam_init = 0.8 - 0.6 * math.exp(-0.3 * l)
    lp = a_lambda[l].astype(f32)
    lam = jnp.exp(jnp.sum(lp[0] * lp[1])) - jnp.exp(jnp.sum(lp[2] * lp[3])) + lam_init
    bias_a, far_a = _diff_bias_vecs(t5_table)
    bmax_a = jnp.maximum(jnp.max(bias_a, axis=(1, 2, 3)), far_a)
    scal_a = (jnp.concatenate([lam[None], far_a, bmax_a]) * jnp.array([1.0] + [LOG2E] * (2 * A_HEADS))).astype(f32)
    gsub = (a_subln_g[l] * (1.0 - lam_init))[None].astype(f32)
    bias_at, _ = _diff_bias_vecs(t5_table, transposed=True)
    oa = _diff_attention(scal_a, qa, ka, va, bias_a * LOG2E, bias_at * LOG2E, gsub, bsz=bsz, s_len=s_len)

    bias_b = _band_bias_vecs(b_rel_table[l])
    ob = _band_attention(qb, kb, vb, bias_b * LOG2E, bsz=bsz, s_len=s_len)

    wr = jnp.zeros((D_MODEL, LANES), f32)
    wr = wr.at[:, :N_GROUPS].set(w_router_group[l]).at[:, N_GROUPS:N_GROUPS + N_EXPERTS].set(w_router_expert[l])
    wrh = wr.astype(bf16)
    wr2 = jnp.concatenate([wrh, (wr - wrh.astype(f32)).astype(bf16)], axis=1)
    br = jnp.zeros((1, LANES), f32)
    br = br.at[0, :N_GROUPS].set(b_router_group[l]).at[0, N_GROUPS:N_GROUPS + N_EXPERTS].set(b_router_expert[l])
    x1, hn, route = _post(oa, ob, ga, gb, x2, w_branch_a[l].astype(bf16), w_branch_b[l].astype(bf16),
                          w_out[l].astype(bf16), norm2_g[l][None].astype(f32), wr2, br)

    ltri = jnp.asarray(np.tril(np.ones((TM_RANK, TM_RANK)), -1), dtype=bf16)
    utri = jnp.asarray(np.triu(np.ones((LANES, LANES)), 1), dtype=bf16)
    dest, cnt = _rank(route, ltri, utri)

    counts = cnt[0, :N_EXPERTS].astype(jnp.int32)
    n_rows = n_tok * TOP_K
    seg = _segments(counts, n_rows)

    dest2 = dest[:, :TOP_K].astype(jnp.int32)
    xs = _dispatch(dest2.reshape(n_tok // TM_DISPATCH, 1, TOP_K * TM_DISPATCH), hn, n_rows)
    ys = _ffn(seg, xs, w_gate[l], w_up[l], w_down[l])
    out = _combine(dest2.reshape(n_tok // TM_COMBINE, 1, TOP_K * TM_COMBINE), route, x1, ys)
    return out.reshape(bsz, s_len, D_MODEL)
```

```python
import functools
import math

import jax
import jax.numpy as jnp
import numpy as np
from jax import lax
from jax.experimental import pallas as pl
from jax.experimental.pallas import tpu as pltpu

D_MODEL = 1024
CHUNK = 64
A_HEADS = 4
A_HEAD_DIM = 64
A_VDIM = 2 * A_HEAD_DIM
B_HEADS = 8
B_HEAD_DIM = 64
B_LEFT_CHUNKS = 8
B_MAX_REL = 128
T5_BUCKETS = 32
T5_MAX_DIST = 128
N_GROUPS = 4
EXPERTS_PER_GROUP = 8
N_EXPERTS = N_GROUPS * EXPERTS_PER_GROUP
TOP_K = 2
D_EXPERT = 512
EPS = 1e-6
NEG = -1e30
LOG2E = 1.0 / math.log(2.0)
DENOM_FLOOR = 2.0 ** -100

CHUNK_SHIFT = CHUNK.bit_length() - 1
assert 1 << CHUNK_SHIFT == CHUNK
LANES = 128
A_W = A_HEADS * 2 * A_HEAD_DIM
B_W = B_HEADS * B_HEAD_DIM
PROJ_W = 4 * 256 + 4 * 512 + 2 * D_MODEL

TM_PROJ = 512
TQ = 512
TK = 512
FAR_UNROLL = 4
TM_POST = 1024
TM_RANK = 1024
TM_DISPATCH = 2048
TM_COMBINE = 1024
ROW_UNROLL = 8
FFN_BLK = 512
FFN_SUB = 256
ROW_SLABS = D_MODEL // 2 // LANES
VMEM_LIMIT = 56 * 1024 * 1024


def _cparams(sem):
    return pltpu.CompilerParams(dimension_semantics=sem, vmem_limit_bytes=VMEM_LIMIT)


def _pack_bf16_pairs(x):
    bits = lax.bitcast_convert_type(x.astype(jnp.bfloat16).astype(jnp.float32), jnp.uint32)
    half = x.shape[1] // 2
    return (bits[:, :half] & jnp.uint32(0xFFFF0000)) | lax.shift_right_logical(bits[:, half:], jnp.uint32(16))


def _unpack_bf16_pairs(pk):
    return (lax.bitcast_convert_type(pk & jnp.uint32(0xFFFF0000), jnp.float32),
            lax.bitcast_convert_type(lax.shift_left(pk, jnp.uint32(16)), jnp.float32))


def _proj_kernel(x_ref, g1_ref, w_ref, gn_ref, gmat_ref,
                 qa_ref, ka_ref, va_ref, qb_ref, kb_ref, vb_ref, ga_ref, gb_ref):
    x = x_ref[...]
    xn = x * lax.rsqrt(jnp.mean(x * x, axis=-1, keepdims=True) + EPS) * g1_ref[...]
    xn = xn.astype(jnp.bfloat16)

    def slab(c0, width):
        return jnp.dot(xn, w_ref[:, c0:c0 + width], preferred_element_type=jnp.float32)

    def headnorm(y, gi):
        sq = (y * y).astype(jnp.bfloat16)
        half = gmat_ref.shape[0]
        ss = jnp.concatenate([jnp.dot(sq[:, c:c + half], gmat_ref[...], preferred_element_type=jnp.float32)
                              for c in range(0, y.shape[1], half)], axis=1)
        return y * lax.rsqrt(ss * (1.0 / A_HEAD_DIM) + EPS) * gn_ref[gi:gi + 1, :]

    qa_ref[...] = headnorm(slab(0, 512), 0).astype(jnp.bfloat16)
    ka_ref[...] = headnorm(slab(512, 512), 1).astype(jnp.bfloat16)
    va_ref[...] = slab(1024, 512).astype(jnp.bfloat16)
    qb_ref[...] = headnorm(slab(1536, 512), 2).astype(jnp.bfloat16)
    kb_ref[...] = headnorm(slab(2048, 512), 3).astype(jnp.bfloat16)
    vb_ref[...] = slab(2560, 512).astype(jnp.bfloat16)
    for j in range(2):
        ga_ref[:, j * 512:(j + 1) * 512] = jax.nn.sigmoid(slab(3072 + j * 512, 512)).astype(jnp.bfloat16)
        gb_ref[:, j * 512:(j + 1) * 512] = jax.nn.sigmoid(slab(4096 + j * 512, 512)).astype(jnp.bfloat16)


def _proj(x2, g1, w_perm, gn, gmat):
    t = x2.shape[0]
    n = t // TM_PROJ
    row = lambda w: pl.BlockSpec((TM_PROJ, w), lambda i: (i, 0))
    full = lambda a: pl.BlockSpec(a.shape, lambda i: (0,) * a.ndim)
    outs = [jax.ShapeDtypeStruct((t, 512), jnp.bfloat16)] * 6 + [jax.ShapeDtypeStruct((t, D_MODEL), jnp.bfloat16)] * 2
    return pl.pallas_call(
        _proj_kernel,
        grid=(n,),
        in_specs=[row(D_MODEL), full(g1), full(w_perm), full(gn), full(gmat)],
        out_specs=[row(512)] * 6 + [row(D_MODEL)] * 2,
        out_shape=outs,
        compiler_params=_cparams(("arbitrary",)),
        name="proj",
    )(x2, g1, w_perm, gn, gmat)


def _diff_kernel(scal_ref, q_ref, k_ref, v_ref, vec_ref, vect_ref, gsub_ref, o_ref,
                 bias_sc, biast_sc, vt_sc, kmaxt_sc, qz_sc, qzt_sc, refn_sc, reff_sc, l_sc, acc_sc,
                 m_sc, acc2_sc, mf_sc, accf_sc, *, n_near):
    f32 = jnp.float32
    h = pl.program_id(1)
    qi = pl.program_id(2)
    nc = TK // LANES
    r2 = 2 * TQ
    n_heads = pl.num_programs(1)

    def near_span(j):
        return slice((n_near - 1 - j) * TK, (n_near - j) * TK)

    @pl.when(qi == 0)
    def _():
        row = lax.broadcasted_iota(jnp.int32, (TQ, TK), 0)
        col = lax.broadcasted_iota(jnp.int32, (TQ, TK), 1)
        for j in range(n_near):
            ok = lax.shift_right_arithmetic(row, CHUNK_SHIFT) >= lax.shift_right_arithmetic(col - j * TK, CHUNK_SHIFT)
            okt = lax.shift_right_arithmetic(col, CHUNK_SHIFT) >= lax.shift_right_arithmetic(row - j * TK, CHUNK_SHIFT)
            for half in range(2):
                vec = jnp.broadcast_to(vec_ref[0, j, half:half + 1, :], (TQ, TQ + TK))
                tile = pltpu.roll(vec, 0, 1, stride=1, stride_axis=0)[:, :TK]
                bias_sc[half * TQ:(half + 1) * TQ, near_span(j)] = jnp.where(ok, tile, NEG)
                vect = jnp.broadcast_to(vect_ref[0, j, half:half + 1, :], (TK, TQ + TK))
                tilet = pltpu.roll(vect, 0, 1, stride=1, stride_axis=0)[:, :TQ]
                biast_sc[near_span(j), half * TQ:(half + 1) * TQ] = jnp.where(okt, tilet, NEG)
        vt_sc[...] = v_ref[...].T
        kf = k_ref[...].astype(f32)
        ksq = kf * kf
        klane = lax.broadcasted_iota(jnp.int32, ksq.shape, 1)
        for half in range(2):
            sel = (klane < 64) if half == 0 else (klane >= 64)
            norm2 = jnp.max(jnp.sum(jnp.where(sel, ksq, 0.0), axis=-1, keepdims=True), axis=0, keepdims=True)
            kmaxt_sc[:, half * TQ:(half + 1) * TQ] = jnp.broadcast_to(jnp.sqrt(norm2), (8, TQ))

    q = q_ref[...]
    lane = lax.broadcasted_iota(jnp.int32, q.shape, 1)
    zero = jnp.zeros_like(q)
    q1 = jnp.where(lane < 64, q, zero)
    q2 = jnp.where(lane >= 64, q, zero)
    qz_sc[0:TQ, :] = q1
    qz_sc[TQ:r2, :] = q2
    qzt_sc[:, 0:TQ] = q1.T
    qzt_sc[:, TQ:r2] = q2.T
    cfar = scal_ref[1 + h]
    bias_max = scal_ref[1 + n_heads + h]
    n_far = jnp.maximum(qi - (n_near - 1), 0)

    def key_block(kblk):
        return k_ref[pl.ds(pl.multiple_of(kblk * TK, TK), TK), :]

    def finish(od):
        od = od * lax.rsqrt(jnp.mean(od * od, axis=-1, keepdims=True) + EPS) * gsub_ref[...]
        o_ref[...] = od.astype(o_ref.dtype)

    def fixed_reference_pass():
        qf = qzt_sc[...].astype(f32)
        qnorm = jnp.sqrt(jnp.sum(qf * qf, axis=0, keepdims=True))
        refn_sc[...] = qnorm * kmaxt_sc[...] + bias_max
        l_sc[...] = jnp.zeros(l_sc.shape, f32)
        acc_sc[...] = jnp.zeros(acc_sc.shape, f32)

        def accumulate(st, kblk, ref_ref):
            p = jnp.exp2(st.reshape(TK // 8, 8, r2) - ref_ref[...][None])
            l_sc[...] += jnp.sum(p, axis=0)
            vt = vt_sc[:, pl.ds(pl.multiple_of(kblk * TK, TK), TK)]
            acc_sc[...] += jnp.dot(vt, p.reshape(TK, r2).astype(jnp.bfloat16), preferred_element_type=f32)

        def scores_t(kblk):
            return jnp.dot(key_block(kblk), qzt_sc[...], preferred_element_type=f32)

        for j in range(n_near):
            def near(j=j):
                accumulate(scores_t(qi - j) + biast_sc[near_span(j), :], qi - j, refn_sc)
            if j == 0:
                near()
            else:
                pl.when(qi >= j)(near)

        reff_sc[...] = refn_sc[...] - cfar

        def far(kblk, carry):
            accumulate(scores_t(kblk), kblk, reff_sc)
            return carry

        def far_group(i, carry):
            for u in range(FAR_UNROLL):
                far(FAR_UNROLL * i + u, carry)
            return carry

        lax.fori_loop(0, n_far // FAR_UNROLL, far_group, 0)
        lax.fori_loop((n_far // FAR_UNROLL) * FAR_UNROLL, n_far, far, 0)

        denom = jnp.sum(l_sc[...], axis=0, keepdims=True)
        ot = acc_sc[...] / denom
        finish((ot[:, 0:TQ] - scal_ref[0] * ot[:, TQ:r2]).T)
        return denom

    def exact_pass():
        def scores(kblk):
            return lax.dot_general(qz_sc[...], key_block(kblk), (((1,), (1,)), ((), ())), preferred_element_type=f32)

        def update(s, kblk, m_ref, a_ref):
            cols = [s[:, c * LANES:(c + 1) * LANES] for c in range(nc)]
            m_old = m_ref[...]
            m_new = jnp.maximum(m_old, jnp.max(functools.reduce(jnp.maximum, cols), axis=-1, keepdims=True))
            alpha = jnp.exp2(m_old - m_new)
            ps = [jnp.exp2(c - m_new) for c in cols]
            p = jnp.concatenate([x.astype(jnp.bfloat16) for x in ps], axis=1)
            v = v_ref[pl.ds(pl.multiple_of(kblk * TK, TK), TK), :]
            a_ref[:, 0:LANES] = alpha * a_ref[:, 0:LANES] + jnp.dot(p, v, preferred_element_type=f32)
            a_ref[:, LANES:2 * LANES] = alpha * a_ref[:, LANES:2 * LANES] + functools.reduce(jnp.add, ps)
            m_ref[...] = m_new

        m_sc[...] = jnp.full(m_sc.shape, NEG, f32)
        acc2_sc[...] = jnp.zeros(acc2_sc.shape, f32)
        for j in range(n_near):
            def near(j=j):
                update(scores(qi - j) + bias_sc[:, near_span(j)], qi - j, m_sc, acc2_sc)
            if j == 0:
                near()
            else:
                pl.when(qi >= j)(near)

        mf_sc[...] = jnp.full(mf_sc.shape, NEG, f32)
        accf_sc[...] = jnp.zeros(accf_sc.shape, f32)

        def far(kblk, carry):
            update(scores(kblk), kblk, mf_sc, accf_sc)
            return carry

        lax.fori_loop(0, n_far, far, 0)
        mf = mf_sc[...] + cfar
        mn = m_sc[...]
        m = jnp.maximum(mf, mn)
        wf = jnp.exp2(mf - m)
        wn = jnp.exp2(mn - m)
        tot = (jnp.concatenate([wf, wf], axis=1) * accf_sc[...]
               + jnp.concatenate([wn, wn], axis=1) * acc2_sc[...])
        o = tot[:, 0:LANES] / jnp.sum(tot[:, LANES:2 * LANES], axis=-1, keepdims=True)
        finish(o[:TQ] - scal_ref[0] * o[TQ:])

    denom = fixed_reference_pass()
    pl.when(jnp.logical_not(jnp.min(denom) >= DENOM_FLOOR))(exact_pass)


def _diff_attention(scal, q, k, v, vecs, vecs_t, gsub, *, bsz, s_len):
    n_blk = q.shape[1] // LANES
    nq = s_len // TQ
    n_near = vecs.shape[1]
    assert vecs.shape == vecs_t.shape == (n_blk, n_near, 2, TQ + TK)
    kern = functools.partial(_diff_kernel, n_near=n_near)
    f32, bf16 = jnp.float32, jnp.bfloat16
    vm = pltpu.VMEM
    vec_spec = pl.BlockSpec((1, n_near, 2, TQ + TK), lambda b, h, i: (h, 0, 0, 0))
    return pl.pallas_call(
        kern,
        grid=(bsz, n_blk, nq),
        in_specs=[
            pl.BlockSpec(memory_space=pltpu.SMEM),
            pl.BlockSpec((TQ, LANES), lambda b, h, i: (b * nq + i, h)),
            pl.BlockSpec((s_len, LANES), lambda b, h, i: (b, h)),
            pl.BlockSpec((s_len, LANES), lambda b, h, i: (b, h)),
            vec_spec, vec_spec,
            pl.BlockSpec((1, LANES), lambda b, h, i: (0, 0)),
        ],
        out_specs=pl.BlockSpec((TQ, LANES), lambda b, h, i: (b * nq + i, h)),
        out_shape=jax.ShapeDtypeStruct(q.shape, bf16),
        scratch_shapes=[vm((2 * TQ, n_near * TK), f32), vm((n_near * TK, 2 * TQ), f32),
                        vm((LANES, s_len), bf16), vm((8, 2 * TQ), f32),
                        vm((2 * TQ, LANES), bf16), vm((LANES, 2 * TQ), bf16),
                        vm((8, 2 * TQ), f32), vm((8, 2 * TQ), f32), vm((8, 2 * TQ), f32),
                        vm((LANES, 2 * TQ), f32),
                        vm((2 * TQ, LANES), f32), vm((2 * TQ, 2 * LANES), f32),
                        vm((2 * TQ, LANES), f32), vm((2 * TQ, 2 * LANES), f32)],
        compiler_params=_cparams(("arbitrary", "arbitrary", "arbitrary")),
        name="attn_diff",
    )(scal, q, k, v, vecs, vecs_t, gsub)


BAND_HQ = TQ // 2
BAND_W = B_LEFT_CHUNKS * CHUNK + BAND_HQ


def _band_kernel(q_ref, k_ref, v_ref, vec_ref, o_ref, bias_sc, qz_sc):
    f32 = jnp.float32
    qi = pl.program_id(2)
    hq, w = BAND_HQ, BAND_W
    back = w - hq

    @pl.when(qi == 0)
    def _():
        qchunk = lax.shift_right_arithmetic(lax.broadcasted_iota(jnp.int32, (hq, w), 0), CHUNK_SHIFT)
        kchunk = lax.shift_right_arithmetic(lax.broadcasted_iota(jnp.int32, (hq, w), 1) - back, CHUNK_SHIFT)
        dchunk = qchunk - kchunk
        allowed = (dchunk >= 0) & (dchunk <= B_LEFT_CHUNKS)
        for half in range(2):
            vec = jnp.broadcast_to(vec_ref[0, half:half + 1, :], (hq, hq + w))
            tile = pltpu.roll(vec, 0, 1, stride=1, stride_axis=0)[:, :w]
            bias_sc[half * hq:(half + 1) * hq, :] = jnp.where(allowed, tile, NEG)

    q = q_ref[...]
    lane = lax.broadcasted_iota(jnp.int32, (hq, LANES), 1)
    zero = jnp.zeros((hq, LANES), q.dtype)
    for u in range(2):
        qu = q[u * hq:(u + 1) * hq, :]
        qz_sc[(2 * u) * hq:(2 * u + 1) * hq, :] = jnp.where(lane < 64, qu, zero)
        qz_sc[(2 * u + 1) * hq:(2 * u + 2) * hq, :] = jnp.where(lane >= 64, qu, zero)

    def scores(u, k0, n_keys):
        k = k_ref[pl.ds(k0, n_keys), :]
        return lax.dot_general(qz_sc[2 * u * hq:(2 * u + 2) * hq, :], k, (((1,), (1,)), ((), ())),
                               preferred_element_type=f32) + bias_sc[:, w - n_keys:w]

    def softmax(s):
        cols = [s[:, c * LANES:(c + 1) * LANES] for c in range(s.shape[1] // LANES)]
        m = jnp.max(functools.reduce(jnp.maximum, cols), axis=-1, keepdims=True)
        ps = [jnp.exp2(c - m) for c in cols]
        denom = jnp.sum(functools.reduce(jnp.add, ps), axis=-1, keepdims=True)
        return jnp.concatenate([x.astype(jnp.bfloat16) for x in ps], axis=1), denom

    def finish(u, p, denom, k0):
        o = jnp.dot(p, v_ref[pl.ds(k0, p.shape[1]), :], preferred_element_type=f32) / denom
        o_ref[u * hq:(u + 1) * hq, :] = jnp.where(lane < 64, o[:hq], o[hq:]).astype(o_ref.dtype)

    def step(k0s, n_keys):
        ss = [scores(u, k0s[u], n_keys[u]) for u in range(2)]
        pd = [softmax(x) for x in ss]
        for u in range(2):
            finish(u, pd[u][0], pd[u][1], k0s[u])

    @pl.when(qi == 0)
    def _():
        step((0, 0), (hq, 2 * hq))

    @pl.when(qi > 0)
    def _():
        start = pl.multiple_of(qi * TQ - back, hq)
        step((start, pl.multiple_of(start + hq, hq)), (w, w))


def _band_attention(q, k, v, vecs, *, bsz, s_len):
    n_blk = q.shape[1] // LANES
    nq = s_len // TQ
    assert vecs.shape == (n_blk, 2, BAND_HQ + BAND_W)
    return pl.pallas_call(
        _band_kernel,
        grid=(bsz, n_blk, nq),
        in_specs=[
            pl.BlockSpec((TQ, LANES), lambda b, h, i: (b * nq + i, h)),
            pl.BlockSpec((s_len, LANES), lambda b, h, i: (b, h)),
            pl.BlockSpec((s_len, LANES), lambda b, h, i: (b, h)),
            pl.BlockSpec((1, 2, BAND_HQ + BAND_W), lambda b, h, i: (h, 0, 0)),
        ],
        out_specs=pl.BlockSpec((TQ, LANES), lambda b, h, i: (b * nq + i, h)),
        out_shape=jax.ShapeDtypeStruct(q.shape, jnp.bfloat16),
        scratch_shapes=[pltpu.VMEM((2 * BAND_HQ, BAND_W), jnp.float32),
                        pltpu.VMEM((2 * TQ, LANES), jnp.bfloat16)],
        compiler_params=_cparams(("arbitrary", "arbitrary", "arbitrary")),
        name="attn_band",
    )(q, k, v, vecs)


def _post_kernel(oa_ref, ob_ref, ga_ref, gb_ref, x_ref, wa_ref, wb_ref, wo_ref, g2_ref,
                 wr2_ref, br_ref, x1_ref, hn_ref, route_ref):
    f32 = jnp.float32
    ya = jnp.dot(oa_ref[...], wa_ref[...], preferred_element_type=f32)
    yb = jnp.dot(ob_ref[...], wb_ref[...], preferred_element_type=f32)
    mixed = ga_ref[...].astype(f32) * ya + gb_ref[...].astype(f32) * yb
    x1 = x_ref[...] + jnp.dot(mixed.astype(jnp.bfloat16), wo_ref[...], preferred_element_type=f32)
    x1_ref[...] = x1
    hn = x1 * lax.rsqrt(jnp.mean(x1 * x1, axis=-1, keepdims=True) + EPS) * g2_ref[...]
    hh = hn.astype(jnp.bfloat16)
    packed = _pack_bf16_pairs(hn)
    for c in range(ROW_SLABS):
        hn_ref[pl.ds(c, TM_POST, stride=ROW_SLABS), :] = packed[:, c * LANES:(c + 1) * LANES]

    hl = (hn - hh.astype(f32)).astype(jnp.bfloat16)
    hw = jnp.dot(hh, wr2_ref[...], preferred_element_type=f32)
    lg = (hw[:, 0:LANES] + hw[:, LANES:2 * LANES]
          + jnp.dot(hl, wr2_ref[:, 0:LANES], preferred_element_type=f32)) + br_ref[...]

    lanei = lax.broadcasted_iota(jnp.int32, lg.shape, 1)
    lanef = lanei.astype(f32)
    big = 999.0
    gmask = lanei < N_GROUPS
    gl = jnp.where(gmask, lg, NEG)
    gm = jnp.max(gl, axis=-1, keepdims=True)
    ge = jnp.where(gmask, jnp.exp(gl - gm), 0.0)
    gp = ge / jnp.sum(ge, axis=-1, keepdims=True)
    p_g = jnp.max(gp, axis=-1, keepdims=True)
    gidx = jnp.min(jnp.where(gmask & (gp == p_g), lanef, big), axis=-1, keepdims=True)
    egrp = lax.shift_right_arithmetic(lanei - N_GROUPS, 3).astype(f32)
    emask = (lanei >= N_GROUPS) & (lanei < N_GROUPS + N_EXPERTS) & (egrp == gidx)
    el = jnp.where(emask, lg, NEG)
    v1 = jnp.max(el, axis=-1, keepdims=True)
    i1 = jnp.min(jnp.where(emask & (el == v1), lanef, big), axis=-1, keepdims=True)
    emask2 = emask & (lanef != i1)
    el2 = jnp.where(emask2, lg, NEG)
    v2 = jnp.max(el2, axis=-1, keepdims=True)
    i2 = jnp.min(jnp.where(emask2 & (el2 == v2), lanef, big), axis=-1, keepdims=True)
    t = jnp.exp(v2 - v1)
    den = 1.0 + t
    w1 = p_g * (1.0 / den)
    w2 = p_g * (t / den)
    route = jnp.where(lanei == 0, i1 - N_GROUPS,
                      jnp.where(lanei == 1, i2 - N_GROUPS,
                                jnp.where(lanei == 2, w1, jnp.where(lanei == 3, w2, 0.0))))
    route_ref[...] = route


def _post(oa, ob, ga, gb, x2, wa, wb, wo, g2, wr2, br):
    t = x2.shape[0]
    n = t // TM_POST
    row = lambda w: pl.BlockSpec((TM_POST, w), lambda i: (i, 0))
    full = lambda a: pl.BlockSpec(a.shape, lambda i: (0,) * a.ndim)
    return pl.pallas_call(
        _post_kernel,
        grid=(n,),
        in_specs=[row(512), row(512), row(D_MODEL), row(D_MODEL), row(D_MODEL),
                  full(wa), full(wb), full(wo), full(g2), full(wr2), full(br)],
        out_specs=[row(D_MODEL), pl.BlockSpec((TM_POST * ROW_SLABS, LANES), lambda i: (i, 0)), row(LANES)],
        out_shape=[jax.ShapeDtypeStruct((t, D_MODEL), jnp.float32),
                   jax.ShapeDtypeStruct((t * ROW_SLABS, LANES), jnp.uint32),
                   jax.ShapeDtypeStruct((t, LANES), jnp.float32)],
        compiler_params=_cparams(("arbitrary",)),
        name="post",
    )(oa, ob, ga, gb, x2, wa, wb, wo, g2, wr2, br)


def _rank_kernel(route_ref, ltri_ref, utri_ref, dest_ref, cnt_ref, cnt_sc, pstart_sc, base_sc):
    f32 = jnp.float32
    p = pl.program_id(0)
    i = pl.program_id(1)
    route = route_ref[...]
    lanef = lax.broadcasted_iota(jnp.int32, route.shape, 1).astype(f32)
    oh1 = (lanef == route[:, 0:1]).astype(f32)
    oh2 = (lanef == route[:, 1:2]).astype(f32)
    both = oh1 + oh2
    colsum = jnp.sum(both, axis=0, keepdims=True)

    @pl.when((p == 0) & (i == 0))
    def _():
        cnt_sc[...] = jnp.zeros(cnt_sc.shape, f32)

    @pl.when(p == 0)
    def _():
        cnt_sc[...] += colsum
        dest_ref[...] = jnp.zeros(dest_ref.shape, f32)
        cnt_ref[...] = jnp.zeros(cnt_ref.shape, f32)

    @pl.when((p == 1) & (i == 0))
    def _():
        cnt = cnt_sc[...]
        chi = jnp.floor(cnt * (1.0 / 256.0))
        clo = cnt - chi * 256.0
        split = jnp.concatenate([jnp.broadcast_to(chi, (8, LANES)), jnp.broadcast_to(clo, (8, LANES))], axis=0)
        excl = jnp.dot(split.astype(jnp.bfloat16), utri_ref[...], preferred_element_type=f32)
        pstart_sc[...] = excl[0:1] * 256.0 + excl[8:9]
        base_sc[...] = jnp.zeros(base_sc.shape, f32)

    @pl.when(p == 1)
    def _():
        prior = jnp.dot(ltri_ref[...], both.astype(jnp.bfloat16), preferred_element_type=f32)
        slot = prior + base_sc[...] + pstart_sc[...]
        d1 = jnp.sum(oh1 * slot, axis=-1, keepdims=True)
        d2 = jnp.sum(oh2 * slot, axis=-1, keepdims=True)
        dest_ref[...] = jnp.where(lanef == 0.0, d1, jnp.where(lanef == 1.0, d2, 0.0))
        base_sc[...] += colsum
        cnt_ref[...] = jnp.broadcast_to(cnt_sc[...], cnt_ref.shape)


def _rank(route, ltri, utri):
    t = route.shape[0]
    n = t // TM_RANK
    full = lambda a: pl.BlockSpec(a.shape, lambda p, i: (0,) * a.ndim)
    row1 = lambda: pltpu.VMEM((1, LANES), jnp.float32)
    return pl.pallas_call(
        _rank_kernel,
        grid=(2, n),
        in_specs=[pl.BlockSpec((TM_RANK, LANES), lambda p, i: (i, 0)), full(ltri), full(utri)],
        out_specs=[pl.BlockSpec((TM_RANK, LANES), lambda p, i: (i * p, 0)),
                   pl.BlockSpec((8, LANES), lambda p, i: (0, 0))],
        out_shape=[jax.ShapeDtypeStruct((t, LANES), jnp.float32),
                   jax.ShapeDtypeStruct((8, LANES), jnp.float32)],
        scratch_shapes=[row1(), row1(), row1()],
        compiler_params=_cparams(("arbitrary", "arbitrary")),
        name="rank",
    )(route, ltri, utri)


def _dispatch_kernel(dest_ref, hn_ref, xs_ref, sems):
    def row_copy(r, k):
        d = dest_ref[0, 0, 2 * r + k]
        src = hn_ref.at[pl.ds(pl.multiple_of(r * ROW_SLABS, ROW_SLABS), ROW_SLABS), :]
        return pltpu.make_async_copy(src, xs_ref.at[d], sems.at[k])

    def issue(r, c):
        row_copy(r, 0).start(priority=0)
        row_copy(r, 1).start(priority=1)
        return c

    lax.fori_loop(0, TM_DISPATCH, issue, 0, unroll=ROW_UNROLL)
    for k in range(TOP_K):
        pltpu.make_async_copy(hn_ref, hn_ref, sems.at[k]).wait()


def _dispatch(dest3, hn, n_rows):
    t = hn.shape[0] // ROW_SLABS
    n = t // TM_DISPATCH
    return pl.pallas_call(
        _dispatch_kernel,
        grid=(n,),
        in_specs=[pl.BlockSpec((1, 1, 2 * TM_DISPATCH), lambda i: (i, 0, 0), memory_space=pltpu.SMEM),
                  pl.BlockSpec((TM_DISPATCH * ROW_SLABS, LANES), lambda i: (i, 0))],
        out_specs=pl.BlockSpec(memory_space=pl.ANY),
        out_shape=jax.ShapeDtypeStruct((n_rows, ROW_SLABS, LANES), hn.dtype),
        scratch_shapes=[pltpu.SemaphoreType.DMA((2,))],
        compiler_params=_cparams(("arbitrary",)),
        name="dispatch",
    )(dest3, hn)


def _ffn_kernel(tile_ref, exp_ref, lo_ref, hi_ref, cast_ref, init_ref,
                xs_ref, wg_ref, wu_ref, wd_ref, ys_ref, wg_sc, wu_sc, wd_sc):
    v = pl.program_id(0)
    lo = lo_ref[v]
    hi = hi_ref[v]

    @pl.when(init_ref[v] == 1)
    def _():
        ys_ref[...] = jnp.zeros(ys_ref.shape, ys_ref.dtype)

    @pl.when((hi > lo) & (cast_ref[v] == 1))
    def _():
        wg_sc[...] = wg_ref[0].astype(jnp.bfloat16)
        wu_sc[...] = wu_ref[0].astype(jnp.bfloat16)
        wd_sc[...] = wd_ref[0].astype(jnp.bfloat16)

    for piece in range(FFN_BLK // FFN_SUB):
        row0 = tile_ref[v] * FFN_BLK + piece * FFN_SUB

        @pl.when((hi > row0) & (lo < row0 + FFN_SUB))
        def _(piece=piece, row0=row0):
            def slab(ref, c):
                return ref.at[pl.ds(piece * FFN_SUB * ROW_SLABS + c, FFN_SUB, stride=ROW_SLABS), :]

            pk = jnp.concatenate([slab(xs_ref, c)[...] for c in range(ROW_SLABS)], axis=1)
            x = jnp.concatenate(_unpack_bf16_pairs(pk), axis=1).astype(jnp.bfloat16)
            g = jnp.dot(x, wg_sc[...], preferred_element_type=jnp.float32)
            u = jnp.dot(x, wu_sc[...], preferred_element_type=jnp.float32)
            hb = (g * jax.nn.sigmoid(g) * u).astype(jnp.bfloat16)
            packed = _pack_bf16_pairs(jnp.dot(hb, wd_sc[...], preferred_element_type=jnp.float32))
            rows = row0 + lax.broadcasted_iota(jnp.int32, (FFN_SUB, LANES), 0)
            mine = (rows >= lo) & (rows < hi)
            for c in range(ROW_SLABS):
                out = slab(ys_ref, c)
                out[...] = jnp.where(mine, packed[:, c * LANES:(c + 1) * LANES], out[...])


def _ffn(seg, xs, w_gate, w_up, w_down):
    n_seg = seg[0].shape[0]
    grid_spec = pltpu.PrefetchScalarGridSpec(
        num_scalar_prefetch=6,
        grid=(n_seg,),
        in_specs=[
            pl.BlockSpec((FFN_BLK * ROW_SLABS, LANES), lambda v, t, e, *_: (t[v], 0)),
            pl.BlockSpec((1, D_MODEL, D_EXPERT), lambda v, t, e, *_: (e[v], 0, 0)),
            pl.BlockSpec((1, D_MODEL, D_EXPERT), lambda v, t, e, *_: (e[v], 0, 0)),
            pl.BlockSpec((1, D_EXPERT, D_MODEL), lambda v, t, e, *_: (e[v], 0, 0)),
        ],
        out_specs=pl.BlockSpec((FFN_BLK * ROW_SLABS, LANES), lambda v, t, e, *_: (t[v], 0)),
        scratch_shapes=[pltpu.VMEM((D_MODEL, D_EXPERT), jnp.bfloat16),
                        pltpu.VMEM((D_MODEL, D_EXPERT), jnp.bfloat16),
                        pltpu.VMEM((D_EXPERT, D_MODEL), jnp.bfloat16)],
    )
    return pl.pallas_call(
        _ffn_kernel,
        grid_spec=grid_spec,
        out_shape=jax.ShapeDtypeStruct(xs.shape, jnp.uint32),
        compiler_params=_cparams(("arbitrary",)),
        name="ffn",
    )(*seg, xs, w_gate, w_up, w_down)


def _segments(counts, n_rows):
    i32 = jnp.int32
    n_tiles = n_rows // FFN_BLK
    n_seg = n_tiles + N_EXPERTS
    tri = jnp.tril(jnp.ones((N_EXPERTS, N_EXPERTS), i32))
    ends = jnp.sum(tri * counts[None, :], axis=1)
    starts = ends - counts
    edges = jnp.arange(n_tiles, dtype=i32) * FFN_BLK
    rank_e = jnp.arange(n_tiles, dtype=i32) + jnp.sum(starts[None, :] <= edges[:, None], axis=1)
    rank_s = jnp.arange(N_EXPERTS, dtype=i32) + jnp.sum(edges[None, :] < starts[:, None], axis=1)
    seg = jnp.arange(n_seg, dtype=i32)
    lo = (jnp.sum(jnp.where(rank_e[None, :] == seg[:, None], edges[None, :], 0), axis=1)
          + jnp.sum(jnp.where(rank_s[None, :] == seg[:, None], starts[None, :], 0), axis=1))
    hi = jnp.concatenate([lo[1:], jnp.array([n_rows], i32)])
    valid = hi > lo
    tile = jnp.minimum(lo // FFN_BLK, n_tiles - 1)
    expert = jnp.minimum(jnp.sum(ends[None, :] <= lo[:, None], axis=1), N_EXPERTS - 1).astype(i32)
    upto = seg[None, :] <= seg[:, None]
    expert = jnp.max(jnp.where(upto & valid[None, :], expert[None, :], 0), axis=1)
    prev_expert = jnp.concatenate([jnp.array([-1], i32), expert[:-1]])
    first_valid = valid & (jnp.sum(jnp.where(upto & valid[None, :], 1, 0), axis=1) == 1)
    cast = valid & ((expert != prev_expert) | first_valid)
    prev_tile = jnp.concatenate([jnp.array([-1], i32), tile[:-1]])
    init = tile != prev_tile
    return (tile.astype(i32), expert, lo.astype(i32), hi.astype(i32), cast.astype(i32), init.astype(i32))


def _combine_kernel(dest_ref, route_ref, x1_ref, ys_ref, out_ref, y0_sc, y1_sc, sems):
    def row_copy(r, k):
        d = dest_ref[0, 0, 2 * r + k]
        dst = y0_sc if k == 0 else y1_sc
        return pltpu.make_async_copy(ys_ref.at[d], dst.at[pl.ds(pl.multiple_of(r * ROW_SLABS, ROW_SLABS), ROW_SLABS), :],
                                     sems.at[k])

    def issue(r, c):
        row_copy(r, 0).start(priority=0)
        row_copy(r, 1).start(priority=1)
        return c

    lax.fori_loop(0, TM_COMBINE, issue, 0, unroll=ROW_UNROLL)
    pltpu.make_async_copy(y0_sc, y0_sc, sems.at[0]).wait()
    pltpu.make_async_copy(y1_sc, y1_sc, sems.at[1]).wait()
    route = route_ref[...]
    w0 = route[:, 2:3]
    w1 = route[:, 3:4]
    half = D_MODEL // 2
    for c in range(ROW_SLABS):
        hi0, lo0 = _unpack_bf16_pairs(y0_sc[pl.ds(c, TM_COMBINE, stride=ROW_SLABS), :])
        hi1, lo1 = _unpack_bf16_pairs(y1_sc[pl.ds(c, TM_COMBINE, stride=ROW_SLABS), :])
        ch = slice(c * LANES, (c + 1) * LANES)
        cl = slice(half + c * LANES, half + (c + 1) * LANES)
        out_ref[:, ch] = x1_ref[:, ch] + (w0 * hi0 + w1 * hi1)
        out_ref[:, cl] = x1_ref[:, cl] + (w0 * lo0 + w1 * lo1)


def _combine(dest3, route, x1, ys):
    t = x1.shape[0]
    n = t // TM_COMBINE
    return pl.pallas_call(
        _combine_kernel,
        grid=(n,),
        in_specs=[pl.BlockSpec((1, 1, 2 * TM_COMBINE), lambda i: (i, 0, 0), memory_space=pltpu.SMEM),
                  pl.BlockSpec((TM_COMBINE, LANES), lambda i: (i, 0)),
                  pl.BlockSpec((TM_COMBINE, D_MODEL), lambda i: (i, 0)),
                  pl.BlockSpec(memory_space=pl.ANY)],
        out_specs=pl.BlockSpec((TM_COMBINE, D_MODEL), lambda i: (i, 0)),
        out_shape=jax.ShapeDtypeStruct((t, D_MODEL), jnp.float32),
        scratch_shapes=[pltpu.VMEM((TM_COMBINE * ROW_SLABS, LANES), jnp.uint32),
                        pltpu.VMEM((TM_COMBINE * ROW_SLABS, LANES), jnp.uint32),
                        pltpu.SemaphoreType.DMA((2,))],
        compiler_params=_cparams(("arbitrary",)),
        name="combine",
    )(dest3, route, x1, ys)


def _t5_bucket(rel):
    nb = T5_BUCKETS // 2
    max_exact = nb // 2
    side = jnp.where(rel > 0, nb, 0)
    n = jnp.abs(rel)
    nf = jnp.maximum(n, 1).astype(jnp.float32)
    large = max_exact + (jnp.log(nf / max_exact) / math.log(T5_MAX_DIST / max_exact)
                         * (nb - max_exact)).astype(jnp.int32)
    large = jnp.minimum(large, nb - 1)
    return side + jnp.where(n < max_exact, n, large)


def _rel_offsets(j, transposed=False):
    i = jnp.arange(TQ + TK)
    if transposed:
        return jnp.where(i < TQ, -i, (TQ + TK) - i) - j * TK
    return jnp.where(i < TK, i, i - (TQ + TK)) - j * TK


def _diff_bias_vecs(t5_table, transposed=False):
    vecs = jnp.stack([t5_table[_t5_bucket(_rel_offsets(j, transposed))].astype(jnp.float32).T for j in range(2)],
                     axis=1)
    vecs = jnp.stack([vecs, vecs], axis=2)
    far = t5_table[_t5_bucket(jnp.array(-(TK + 1)))].astype(jnp.float32)
    return vecs, far


def _band_bias_vecs(rel_table):
    i = jnp.arange(BAND_HQ + BAND_W)
    rel = jnp.where(i < BAND_W, i, i - (BAND_HQ + BAND_W)) - (BAND_W - BAND_HQ)
    vecs = rel_table[jnp.clip(rel, -B_MAX_REL, B_MAX_REL) + B_MAX_REL].astype(jnp.float32).T
    return vecs.reshape(B_HEADS // 2, 2, BAND_HQ + BAND_W)


def kernel(x, norm1_g, w_in, a_qnorm_g, a_knorm_g, a_lambda, a_subln_g, t5_table, b_qnorm_g, b_knorm_g,
           b_rel_table, w_branch_a, w_branch_b, w_out, norm2_g, w_router_group, b_router_group,
           w_router_expert, b_router_expert, w_gate, w_up, w_down):
    bsz, s_len, _ = x.shape
    n_tok = bsz * s_len
    f32, bf16 = jnp.float32, jnp.bfloat16
    assert s_len % TQ == 0 and TQ == TK and TQ % CHUNK == 0 and n_tok % TM_PROJ == 0
    assert TK >= T5_MAX_DIST and BAND_HQ % CHUNK == 0 and BAND_W % LANES == 0 and BAND_W - BAND_HQ <= TQ
    l = 0
    x2 = x.reshape(n_tok, D_MODEL)

    w = w_in[l]
    qk = w[:, :1024].reshape(D_MODEL, 2, 2, A_HEADS, A_HEAD_DIM)
    qk = qk.transpose(0, 1, 3, 2, 4).reshape(D_MODEL, 1024)
    w_perm = jnp.concatenate([qk, w[:, 1024:]], axis=1).astype(bf16)
    gn = jnp.stack([jnp.tile(a_qnorm_g[l] * (A_HEAD_DIM ** -0.5 * LOG2E), 8), jnp.tile(a_knorm_g[l], 8),
                    jnp.tile(b_qnorm_g[l] * (B_HEAD_DIM ** -0.5 * LOG2E), 8), jnp.tile(b_knorm_g[l], 8)]).astype(f32)
    gmat = jnp.asarray(np.kron(np.eye(4), np.ones((A_HEAD_DIM, A_HEAD_DIM))), dtype=bf16)

    qa, ka, va, qb, kb, vb, ga, gb = _proj(x2, norm1_g[l][None].astype(f32), w_perm, gn, gmat)

    lam_init = 0.8 - 0.6 * math.exp(-0.3 * l)
    lp = a_lambda[l].astype(f32)
    lam = jnp.exp(jnp.sum(lp[0] * lp[1])) - jnp.exp(jnp.sum(lp[2] * lp[3])) + lam_init
    bias_a, far_a = _diff_bias_vecs(t5_table)
    bmax_a = jnp.maximum(jnp.max(bias_a, axis=(1, 2, 3)), far_a)
    scal_a = (jnp.concatenate([lam[None], far_a, bmax_a]) * jnp.array([1.0] + [LOG2E] * (2 * A_HEADS))).astype(f32)
    gsub = (a_subln_g[l] * (1.0 - lam_init))[None].astype(f32)
    bias_at, _ = _diff_bias_vecs(t5_table, transposed=True)
    oa = _diff_attention(scal_a, qa, ka, va, bias_a * LOG2E, bias_at * LOG2E, gsub, bsz=bsz, s_len=s_len)

    bias_b = _band_bias_vecs(b_rel_table[l])
    ob = _band_attention(qb, kb, vb, bias_b * LOG2E, bsz=bsz, s_len=s_len)

    wr = jnp.zeros((D_MODEL, LANES), f32)
    wr = wr.at[:, :N_GROUPS].set(w_router_group[l]).at[:, N_GROUPS:N_GROUPS + N_EXPERTS].set(w_router_expert[l])
    wrh = wr.astype(bf16)
    wr2 = jnp.concatenate([wrh, (wr - wrh.astype(f32)).astype(bf16)], axis=1)
    br = jnp.zeros((1, LANES), f32)
    br = br.at[0, :N_GROUPS].set(b_router_group[l]).at[0, N_GROUPS:N_GROUPS + N_EXPERTS].set(b_router_expert[l])
    x1, hn, route = _post(oa, ob, ga, gb, x2, w_branch_a[l].astype(bf16), w_branch_b[l].astype(bf16),
                          w_out[l].astype(bf16), norm2_g[l][None].astype(f32), wr2, br)

    ltri = jnp.asarray(np.tril(np.ones((TM_RANK, TM_RANK)), -1), dtype=bf16)
    utri = jnp.asarray(np.triu(np.ones((LANES, LANES)), 1), dtype=bf16)
    dest, cnt = _rank(route, ltri, utri)

    counts = cnt[0, :N_EXPERTS].astype(jnp.int32)
    n_rows = n_tok * TOP_K
    seg = _segments(counts, n_rows)

    dest2 = dest[:, :TOP_K].astype(jnp.int32)
    xs = _dispatch(dest2.reshape(n_tok // TM_DISPATCH, 1, TOP_K * TM_DISPATCH), hn, n_rows)
    ys = _ffn(seg, xs.reshape(n_rows * ROW_SLABS, LANES), w_gate[l], w_up[l], w_down[l])
    out = _combine(dest2.reshape(n_tok // TM_COMBINE, 1, TOP_K * TM_COMBINE), route, x1,
                   ys.reshape(n_rows, ROW_SLABS, LANES))
    return out.reshape(bsz, s_len, D_MODEL)
```

```python
import functools
import math

import jax
import jax.numpy as jnp
import numpy as np
from jax import lax
from jax.experimental import pallas as pl
from jax.experimental.pallas import tpu as pltpu

D_MODEL = 1024
CHUNK = 64
A_HEADS = 4
A_HEAD_DIM = 64
A_VDIM = 2 * A_HEAD_DIM
B_HEADS = 8
B_HEAD_DIM = 64
B_LEFT_CHUNKS = 8
B_MAX_REL = 128
T5_BUCKETS = 32
T5_MAX_DIST = 128
N_GROUPS = 4
EXPERTS_PER_GROUP = 8
N_EXPERTS = N_GROUPS * EXPERTS_PER_GROUP
TOP_K = 2
D_EXPERT = 512
EPS = 1e-6
NEG = -1e30
LOG2E = 1.0 / math.log(2.0)
DENOM_FLOOR = 2.0 ** -100

CHUNK_SHIFT = CHUNK.bit_length() - 1
assert 1 << CHUNK_SHIFT == CHUNK
LANES = 128
A_W = A_HEADS * 2 * A_HEAD_DIM
B_W = B_HEADS * B_HEAD_DIM
PROJ_W = 4 * 256 + 4 * 512 + 2 * D_MODEL

TM_PROJ = 512
TQ = 512
TK = 512
FAR_UNROLL = 4
TM_POST = 1024
TM_RANK = 1024
TM_DISPATCH = 4096
TM_COMBINE = 2048
ROW_UNROLL = 8
FFN_BLK = 1024
FFN_SUB = 256
ROW_SLABS = D_MODEL // 2 // LANES
VMEM_LIMIT = 56 * 1024 * 1024


def _cparams(sem):
    return pltpu.CompilerParams(dimension_semantics=sem, vmem_limit_bytes=VMEM_LIMIT)


def _pack_bf16_pairs(x):
    bits = lax.bitcast_convert_type(x.astype(jnp.bfloat16).astype(jnp.float32), jnp.uint32)
    half = x.shape[1] // 2
    return (bits[:, :half] & jnp.uint32(0xFFFF0000)) | lax.shift_right_logical(bits[:, half:], jnp.uint32(16))


def _unpack_bf16_pairs(pk):
    return (lax.bitcast_convert_type(pk & jnp.uint32(0xFFFF0000), jnp.float32),
            lax.bitcast_convert_type(lax.shift_left(pk, jnp.uint32(16)), jnp.float32))


def _proj_kernel(x_ref, g1_ref, wqk_ref, w_ref, gn_ref, gmat_ref,
                 qa_ref, ka_ref, va_ref, qb_ref, kb_ref, vb_ref, ga_ref, gb_ref):
    x = x_ref[...]
    xn = x * lax.rsqrt(jnp.mean(x * x, axis=-1, keepdims=True) + EPS) * g1_ref[...]
    xn = xn.astype(jnp.bfloat16)
    n_qk = wqk_ref.shape[1]

    def slab(c0, width):
        w = wqk_ref[:, c0:c0 + width] if c0 < n_qk else w_ref[:, c0 - n_qk:c0 - n_qk + width]
        return jnp.dot(xn, w, preferred_element_type=jnp.float32)

    def headnorm(y, gi):
        sq = (y * y).astype(jnp.bfloat16)
        half = gmat_ref.shape[0]
        ss = jnp.concatenate([jnp.dot(sq[:, c:c + half], gmat_ref[...], preferred_element_type=jnp.float32)
                              for c in range(0, y.shape[1], half)], axis=1)
        return y * lax.rsqrt(ss * (1.0 / A_HEAD_DIM) + EPS) * gn_ref[gi:gi + 1, :]

    qa_ref[...] = headnorm(slab(0, 512), 0).astype(jnp.bfloat16)
    ka_ref[...] = headnorm(slab(512, 512), 1).astype(jnp.bfloat16)
    va_ref[...] = slab(1024, 512).astype(jnp.bfloat16)
    qb_ref[...] = headnorm(slab(1536, 512), 2).astype(jnp.bfloat16)
    kb_ref[...] = headnorm(slab(2048, 512), 3).astype(jnp.bfloat16)
    vb_ref[...] = slab(2560, 512).astype(jnp.bfloat16)
    for j in range(2):
        ga_ref[:, j * 512:(j + 1) * 512] = jax.nn.sigmoid(slab(3072 + j * 512, 512)).astype(jnp.bfloat16)
        gb_ref[:, j * 512:(j + 1) * 512] = jax.nn.sigmoid(slab(4096 + j * 512, 512)).astype(jnp.bfloat16)


def _proj(x2, g1, w_qk, w_rest, gn, gmat):
    t = x2.shape[0]
    n = t // TM_PROJ
    row = lambda w: pl.BlockSpec((TM_PROJ, w), lambda i: (i, 0))
    full = lambda a: pl.BlockSpec(a.shape, lambda i: (0,) * a.ndim)
    outs = [jax.ShapeDtypeStruct((t, 512), jnp.bfloat16)] * 6 + [jax.ShapeDtypeStruct((t, D_MODEL), jnp.bfloat16)] * 2
    return pl.pallas_call(
        _proj_kernel,
        grid=(n,),
        in_specs=[row(D_MODEL), full(g1), full(w_qk), full(w_rest), full(gn), full(gmat)],
        out_specs=[row(512)] * 6 + [row(D_MODEL)] * 2,
        out_shape=outs,
        compiler_params=_cparams(("arbitrary",)),
        name="proj",
    )(x2, g1, w_qk, w_rest, gn, gmat)


def _diff_kernel(scal_ref, q_ref, k_ref, v_ref, vec_ref, vect_ref, gsub_ref, o_ref,
                 bias_sc, biast_sc, vt_sc, kmaxt_sc, qz_sc, qzt_sc, refn_sc, reff_sc, l_sc, acc_sc,
                 m_sc, acc2_sc, mf_sc, accf_sc, *, n_near):
    f32 = jnp.float32
    h = pl.program_id(1)
    qi = pl.program_id(2)
    nc = TK // LANES
    r2 = 2 * TQ
    n_heads = pl.num_programs(1)

    def near_span(j):
        return slice((n_near - 1 - j) * TK, (n_near - j) * TK)

    @pl.when(qi == 0)
    def _():
        row = lax.broadcasted_iota(jnp.int32, (TQ, TK), 0)
        col = lax.broadcasted_iota(jnp.int32, (TQ, TK), 1)
        for j in range(n_near):
            ok = lax.shift_right_arithmetic(row, CHUNK_SHIFT) >= lax.shift_right_arithmetic(col - j * TK, CHUNK_SHIFT)
            okt = lax.shift_right_arithmetic(col, CHUNK_SHIFT) >= lax.shift_right_arithmetic(row - j * TK, CHUNK_SHIFT)
            for half in range(2):
                vec = jnp.broadcast_to(vec_ref[0, j, half:half + 1, :], (TQ, TQ + TK))
                tile = pltpu.roll(vec, 0, 1, stride=1, stride_axis=0)[:, :TK]
                bias_sc[half * TQ:(half + 1) * TQ, near_span(j)] = jnp.where(ok, tile, NEG)
                vect = jnp.broadcast_to(vect_ref[0, j, half:half + 1, :], (TK, TQ + TK))
                tilet = pltpu.roll(vect, 0, 1, stride=1, stride_axis=0)[:, :TQ]
                biast_sc[near_span(j), half * TQ:(half + 1) * TQ] = jnp.where(okt, tilet, NEG)
        vt_sc[...] = v_ref[...].T
        kf = k_ref[...].astype(f32)
        ksq = kf * kf
        klane = lax.broadcasted_iota(jnp.int32, ksq.shape, 1)
        for half in range(2):
            sel = (klane < 64) if half == 0 else (klane >= 64)
            norm2 = jnp.max(jnp.sum(jnp.where(sel, ksq, 0.0), axis=-1, keepdims=True), axis=0, keepdims=True)
            kmaxt_sc[:, half * TQ:(half + 1) * TQ] = jnp.broadcast_to(jnp.sqrt(norm2), (8, TQ))

    q = q_ref[...]
    lane = lax.broadcasted_iota(jnp.int32, q.shape, 1)
    zero = jnp.zeros_like(q)
    q1 = jnp.where(lane < 64, q, zero)
    q2 = jnp.where(lane >= 64, q, zero)
    qz_sc[0:TQ, :] = q1
    qz_sc[TQ:r2, :] = q2
    qzt_sc[:, 0:TQ] = q1.T
    qzt_sc[:, TQ:r2] = q2.T
    cfar = scal_ref[1 + h]
    bias_max = scal_ref[1 + n_heads + h]
    n_far = jnp.maximum(qi - (n_near - 1), 0)

    def key_block(kblk):
        return k_ref[pl.ds(pl.multiple_of(kblk * TK, TK), TK), :]

    def finish(od):
        od = od * lax.rsqrt(jnp.mean(od * od, axis=-1, keepdims=True) + EPS) * gsub_ref[...]
        o_ref[...] = od.astype(o_ref.dtype)

    def fixed_reference_pass():
        qf = qzt_sc[...].astype(f32)
        qnorm = jnp.sqrt(jnp.sum(qf * qf, axis=0, keepdims=True))
        refn_sc[...] = qnorm * kmaxt_sc[...] + bias_max
        l_sc[...] = jnp.zeros(l_sc.shape, f32)
        acc_sc[...] = jnp.zeros(acc_sc.shape, f32)

        def accumulate(st, kblk, ref_ref):
            p = jnp.exp2(st.reshape(TK // 8, 8, r2) - ref_ref[...][None])
            l_sc[...] += jnp.sum(p, axis=0)
            vt = vt_sc[:, pl.ds(pl.multiple_of(kblk * TK, TK), TK)]
            acc_sc[...] += jnp.dot(vt, p.reshape(TK, r2).astype(jnp.bfloat16), preferred_element_type=f32)

        def scores_t(kblk):
            return jnp.dot(key_block(kblk), qzt_sc[...], preferred_element_type=f32)

        for j in range(n_near):
            def near(j=j):
                accumulate(scores_t(qi - j) + biast_sc[near_span(j), :], qi - j, refn_sc)
            if j == 0:
                near()
            else:
                pl.when(qi >= j)(near)

        reff_sc[...] = refn_sc[...] - cfar

        def far(kblk, carry):
            accumulate(scores_t(kblk), kblk, reff_sc)
            return carry

        def far_group(i, carry):
            for u in range(FAR_UNROLL):
                far(FAR_UNROLL * i + u, carry)
            return carry

        lax.fori_loop(0, n_far // FAR_UNROLL, far_group, 0)
        lax.fori_loop((n_far // FAR_UNROLL) * FAR_UNROLL, n_far, far, 0)

        denom = jnp.sum(l_sc[...], axis=0, keepdims=True)
        ot = acc_sc[...] / denom
        finish((ot[:, 0:TQ] - scal_ref[0] * ot[:, TQ:r2]).T)
        return denom

    def exact_pass():
        def scores(kblk):
            return lax.dot_general(qz_sc[...], key_block(kblk), (((1,), (1,)), ((), ())), preferred_element_type=f32)

        def update(s, kblk, m_ref, a_ref):
            cols = [s[:, c * LANES:(c + 1) * LANES] for c in range(nc)]
            m_old = m_ref[...]
            m_new = jnp.maximum(m_old, jnp.max(functools.reduce(jnp.maximum, cols), axis=-1, keepdims=True))
            alpha = jnp.exp2(m_old - m_new)
            ps = [jnp.exp2(c - m_new) for c in cols]
            p = jnp.concatenate([x.astype(jnp.bfloat16) for x in ps], axis=1)
            v = v_ref[pl.ds(pl.multiple_of(kblk * TK, TK), TK), :]
            a_ref[:, 0:LANES] = alpha * a_ref[:, 0:LANES] + jnp.dot(p, v, preferred_element_type=f32)
            a_ref[:, LANES:2 * LANES] = alpha * a_ref[:, LANES:2 * LANES] + functools.reduce(jnp.add, ps)
            m_ref[...] = m_new

        m_sc[...] = jnp.full(m_sc.shape, NEG, f32)
        acc2_sc[...] = jnp.zeros(acc2_sc.shape, f32)
        for j in range(n_near):
            def near(j=j):
                update(scores(qi - j) + bias_sc[:, near_span(j)], qi - j, m_sc, acc2_sc)
            if j == 0:
                near()
            else:
                pl.when(qi >= j)(near)

        mf_sc[...] = jnp.full(mf_sc.shape, NEG, f32)
        accf_sc[...] = jnp.zeros(accf_sc.shape, f32)

        def far(kblk, carry):
            update(scores(kblk), kblk, mf_sc, accf_sc)
            return carry

        lax.fori_loop(0, n_far, far, 0)
        mf = mf_sc[...] + cfar
        mn = m_sc[...]
        m = jnp.maximum(mf, mn)
        wf = jnp.exp2(mf - m)
        wn = jnp.exp2(mn - m)
        tot = (jnp.concatenate([wf, wf], axis=1) * accf_sc[...]
               + jnp.concatenate([wn, wn], axis=1) * acc2_sc[...])
        o = tot[:, 0:LANES] / jnp.sum(tot[:, LANES:2 * LANES], axis=-1, keepdims=True)
        finish(o[:TQ] - scal_ref[0] * o[TQ:])

    denom = fixed_reference_pass()
    pl.when(jnp.logical_not(jnp.min(denom) >= DENOM_FLOOR))(exact_pass)


def _diff_attention(scal, q, k, v, vecs, vecs_t, gsub, *, bsz, s_len):
    n_blk = q.shape[1] // LANES
    nq = s_len // TQ
    n_near = vecs.shape[1]
    assert vecs.shape == vecs_t.shape == (n_blk, n_near, 2, TQ + TK)
    kern = functools.partial(_diff_kernel, n_near=n_near)
    f32, bf16 = jnp.float32, jnp.bfloat16
    vm = pltpu.VMEM
    vec_spec = pl.BlockSpec((1, n_near, 2, TQ + TK), lambda b, h, i: (h, 0, 0, 0))
    return pl.pallas_call(
        kern,
        grid=(bsz, n_blk, nq),
        in_specs=[
            pl.BlockSpec(memory_space=pltpu.SMEM),
            pl.BlockSpec((TQ, LANES), lambda b, h, i: (b * nq + i, h)),
            pl.BlockSpec((s_len, LANES), lambda b, h, i: (b, h)),
            pl.BlockSpec((s_len, LANES), lambda b, h, i: (b, h)),
            vec_spec, vec_spec,
            pl.BlockSpec((1, LANES), lambda b, h, i: (0, 0)),
        ],
        out_specs=pl.BlockSpec((TQ, LANES), lambda b, h, i: (b * nq + i, h)),
        out_shape=jax.ShapeDtypeStruct(q.shape, bf16),
        scratch_shapes=[vm((2 * TQ, n_near * TK), f32), vm((n_near * TK, 2 * TQ), f32),
                        vm((LANES, s_len), bf16), vm((8, 2 * TQ), f32),
                        vm((2 * TQ, LANES), bf16), vm((LANES, 2 * TQ), bf16),
                        vm((8, 2 * TQ), f32), vm((8, 2 * TQ), f32), vm((8, 2 * TQ), f32),
                        vm((LANES, 2 * TQ), f32),
                        vm((2 * TQ, LANES), f32), vm((2 * TQ, 2 * LANES), f32),
                        vm((2 * TQ, LANES), f32), vm((2 * TQ, 2 * LANES), f32)],
        compiler_params=_cparams(("arbitrary", "arbitrary", "arbitrary")),
        name="attn_diff",
    )(scal, q, k, v, vecs, vecs_t, gsub)


BAND_HQ = TQ // 2
BAND_W = B_LEFT_CHUNKS * CHUNK + BAND_HQ


def _band_kernel(q_ref, k_ref, v_ref, vec_ref, o_ref, bias_sc, qz_sc):
    f32 = jnp.float32
    qi = pl.program_id(2)
    hq, w = BAND_HQ, BAND_W
    back = w - hq

    @pl.when(qi == 0)
    def _():
        qchunk = lax.shift_right_arithmetic(lax.broadcasted_iota(jnp.int32, (hq, w), 0), CHUNK_SHIFT)
        kchunk = lax.shift_right_arithmetic(lax.broadcasted_iota(jnp.int32, (hq, w), 1) - back, CHUNK_SHIFT)
        dchunk = qchunk - kchunk
        allowed = (dchunk >= 0) & (dchunk <= B_LEFT_CHUNKS)
        for half in range(2):
            vec = jnp.broadcast_to(vec_ref[0, half:half + 1, :], (hq, hq + w))
            tile = pltpu.roll(vec, 0, 1, stride=1, stride_axis=0)[:, :w]
            bias_sc[half * hq:(half + 1) * hq, :] = jnp.where(allowed, tile, NEG)

    q = q_ref[...]
    lane = lax.broadcasted_iota(jnp.int32, (hq, LANES), 1)
    zero = jnp.zeros((hq, LANES), q.dtype)
    for u in range(2):
        qu = q[u * hq:(u + 1) * hq, :]
        qz_sc[(2 * u) * hq:(2 * u + 1) * hq, :] = jnp.where(lane < 64, qu, zero)
        qz_sc[(2 * u + 1) * hq:(2 * u + 2) * hq, :] = jnp.where(lane >= 64, qu, zero)

    def scores(u, k0, n_keys):
        k = k_ref[pl.ds(k0, n_keys), :]
        return lax.dot_general(qz_sc[2 * u * hq:(2 * u + 2) * hq, :], k, (((1,), (1,)), ((), ())),
                               preferred_element_type=f32) + bias_sc[:, w - n_keys:w]

    def softmax(s):
        cols = [s[:, c * LANES:(c + 1) * LANES] for c in range(s.shape[1] // LANES)]
        m = jnp.max(functools.reduce(jnp.maximum, cols), axis=-1, keepdims=True)
        ps = [jnp.exp2(c - m) for c in cols]
        denom = jnp.sum(functools.reduce(jnp.add, ps), axis=-1, keepdims=True)
        return jnp.concatenate([x.astype(jnp.bfloat16) for x in ps], axis=1), denom

    def finish(u, p, denom, k0):
        o = jnp.dot(p, v_ref[pl.ds(k0, p.shape[1]), :], preferred_element_type=f32) / denom
        o_ref[u * hq:(u + 1) * hq, :] = jnp.where(lane < 64, o[:hq], o[hq:]).astype(o_ref.dtype)

    def step(k0s, n_keys):
        ss = [scores(u, k0s[u], n_keys[u]) for u in range(2)]
        pd = [softmax(x) for x in ss]
        for u in range(2):
            finish(u, pd[u][0], pd[u][1], k0s[u])

    @pl.when(qi == 0)
    def _():
        step((0, 0), (hq, 2 * hq))

    @pl.when(qi > 0)
    def _():
        start = pl.multiple_of(qi * TQ - back, hq)
        step((start, pl.multiple_of(start + hq, hq)), (w, w))


def _band_attention(q, k, v, vecs, *, bsz, s_len):
    n_blk = q.shape[1] // LANES
    nq = s_len // TQ
    assert vecs.shape == (n_blk, 2, BAND_HQ + BAND_W)
    return pl.pallas_call(
        _band_kernel,
        grid=(bsz, n_blk, nq),
        in_specs=[
            pl.BlockSpec((TQ, LANES), lambda b, h, i: (b * nq + i, h)),
            pl.BlockSpec((s_len, LANES), lambda b, h, i: (b, h)),
            pl.BlockSpec((s_len, LANES), lambda b, h, i: (b, h)),
            pl.BlockSpec((1, 2, BAND_HQ + BAND_W), lambda b, h, i: (h, 0, 0)),
        ],
        out_specs=pl.BlockSpec((TQ, LANES), lambda b, h, i: (b * nq + i, h)),
        out_shape=jax.ShapeDtypeStruct(q.shape, jnp.bfloat16),
        scratch_shapes=[pltpu.VMEM((2 * BAND_HQ, BAND_W), jnp.float32),
                        pltpu.VMEM((2 * TQ, LANES), jnp.bfloat16)],
        compiler_params=_cparams(("arbitrary", "arbitrary", "arbitrary")),
        name="attn_band",
    )(q, k, v, vecs)


def _post_kernel(oa_ref, ob_ref, ga_ref, gb_ref, x_ref, wa_ref, wb_ref, wo_ref, g2_ref,
                 wr2_ref, br_ref, x1_ref, hn_ref, route_ref, cnt_ref):
    f32 = jnp.float32
    ya = jnp.dot(oa_ref[...], wa_ref[...], preferred_element_type=f32)
    yb = jnp.dot(ob_ref[...], wb_ref[...], preferred_element_type=f32)
    mixed = ga_ref[...].astype(f32) * ya + gb_ref[...].astype(f32) * yb
    x1 = x_ref[...] + jnp.dot(mixed.astype(jnp.bfloat16), wo_ref[...], preferred_element_type=f32)
    x1_ref[...] = x1
    hn = x1 * lax.rsqrt(jnp.mean(x1 * x1, axis=-1, keepdims=True) + EPS) * g2_ref[...]
    hh = hn.astype(jnp.bfloat16)
    packed = _pack_bf16_pairs(hn)
    for c in range(ROW_SLABS):
        hn_ref[pl.ds(c, TM_POST, stride=ROW_SLABS), :] = packed[:, c * LANES:(c + 1) * LANES]

    hl = (hn - hh.astype(f32)).astype(jnp.bfloat16)
    hw = jnp.dot(hh, wr2_ref[...], preferred_element_type=f32)
    lg = (hw[:, 0:LANES] + hw[:, LANES:2 * LANES]
          + jnp.dot(hl, wr2_ref[:, 0:LANES], preferred_element_type=f32)) + br_ref[...]

    lanei = lax.broadcasted_iota(jnp.int32, lg.shape, 1)
    lanef = lanei.astype(f32)
    big = 999.0
    gmask = lanei < N_GROUPS
    gl = jnp.where(gmask, lg, NEG)
    gm = jnp.max(gl, axis=-1, keepdims=True)
    ge = jnp.where(gmask, jnp.exp(gl - gm), 0.0)
    gp = ge / jnp.sum(ge, axis=-1, keepdims=True)
    p_g = jnp.max(gp, axis=-1, keepdims=True)
    gidx = jnp.min(jnp.where(gmask & (gp == p_g), lanef, big), axis=-1, keepdims=True)
    egrp = lax.shift_right_arithmetic(lanei - N_GROUPS, 3).astype(f32)
    emask = (lanei >= N_GROUPS) & (lanei < N_GROUPS + N_EXPERTS) & (egrp == gidx)
    el = jnp.where(emask, lg, NEG)
    v1 = jnp.max(el, axis=-1, keepdims=True)
    i1 = jnp.min(jnp.where(emask & (el == v1), lanef, big), axis=-1, keepdims=True)
    emask2 = emask & (lanef != i1)
    el2 = jnp.where(emask2, lg, NEG)
    v2 = jnp.max(el2, axis=-1, keepdims=True)
    i2 = jnp.min(jnp.where(emask2 & (el2 == v2), lanef, big), axis=-1, keepdims=True)
    t = jnp.exp(v2 - v1)
    den = 1.0 + t
    w1 = p_g * (1.0 / den)
    w2 = p_g * (t / den)
    route = jnp.where(lanei == 0, i1 - N_GROUPS,
                      jnp.where(lanei == 1, i2 - N_GROUPS,
                                jnp.where(lanei == 2, w1, jnp.where(lanei == 3, w2, 0.0))))
    route_ref[...] = route

    @pl.when(pl.program_id(0) == 0)
    def _():
        cnt_ref[...] = jnp.zeros(cnt_ref.shape, f32)

    chosen = ((lanef == i1 - N_GROUPS) | (lanef == i2 - N_GROUPS)).astype(f32)
    cnt_ref[...] += jnp.broadcast_to(jnp.sum(chosen, axis=0, keepdims=True), cnt_ref.shape)


def _post(oa, ob, ga, gb, x2, wa, wb, wo, g2, wr2, br):
    t = x2.shape[0]
    n = t // TM_POST
    row = lambda w: pl.BlockSpec((TM_POST, w), lambda i: (i, 0))
    full = lambda a: pl.BlockSpec(a.shape, lambda i: (0,) * a.ndim)
    return pl.pallas_call(
        _post_kernel,
        grid=(n,),
        in_specs=[row(512), row(512), row(D_MODEL), row(D_MODEL), row(D_MODEL),
                  full(wa), full(wb), full(wo), full(g2), full(wr2), full(br)],
        out_specs=[row(D_MODEL), pl.BlockSpec((TM_POST * ROW_SLABS, LANES), lambda i: (i, 0)), row(LANES),
                   pl.BlockSpec((8, LANES), lambda i: (0, 0))],
        out_shape=[jax.ShapeDtypeStruct((t, D_MODEL), jnp.float32),
                   jax.ShapeDtypeStruct((t * ROW_SLABS, LANES), jnp.uint32),
                   jax.ShapeDtypeStruct((t, LANES), jnp.float32),
                   jax.ShapeDtypeStruct((8, LANES), jnp.float32)],
        compiler_params=_cparams(("arbitrary",)),
        name="post",
    )(oa, ob, ga, gb, x2, wa, wb, wo, g2, wr2, br)


def _rank_kernel(route_ref, cnt_ref, ltri_ref, utri_ref, dest_ref, pstart_sc, base_sc):
    f32 = jnp.float32
    i = pl.program_id(0)
    route = route_ref[...]
    lanef = lax.broadcasted_iota(jnp.int32, route.shape, 1).astype(f32)
    oh1 = (lanef == route[:, 0:1]).astype(f32)
    oh2 = (lanef == route[:, 1:2]).astype(f32)
    both = oh1 + oh2

    @pl.when(i == 0)
    def _():
        cnt = cnt_ref[0:1, :]
        chi = jnp.floor(cnt * (1.0 / 256.0))
        clo = cnt - chi * 256.0
        split = jnp.concatenate([jnp.broadcast_to(chi, (8, LANES)), jnp.broadcast_to(clo, (8, LANES))], axis=0)
        excl = jnp.dot(split.astype(jnp.bfloat16), utri_ref[...], preferred_element_type=f32)
        pstart_sc[...] = excl[0:1] * 256.0 + excl[8:9]
        base_sc[...] = jnp.zeros(base_sc.shape, f32)

    prior = jnp.dot(ltri_ref[...], both.astype(jnp.bfloat16), preferred_element_type=f32)
    slot = prior + base_sc[...] + pstart_sc[...]
    d1 = jnp.sum(oh1 * slot, axis=-1, keepdims=True)
    d2 = jnp.sum(oh2 * slot, axis=-1, keepdims=True)
    dest_ref[...] = jnp.where(lanef == 0.0, d1, jnp.where(lanef == 1.0, d2, 0.0))
    base_sc[...] += jnp.sum(both, axis=0, keepdims=True)


def _rank(route, cnt, ltri, utri):
    t = route.shape[0]
    n = t // TM_RANK
    full = lambda a: pl.BlockSpec(a.shape, lambda i: (0,) * a.ndim)
    row1 = lambda: pltpu.VMEM((1, LANES), jnp.float32)
    return pl.pallas_call(
        _rank_kernel,
        grid=(n,),
        in_specs=[pl.BlockSpec((TM_RANK, LANES), lambda i: (i, 0)), full(cnt), full(ltri), full(utri)],
        out_specs=pl.BlockSpec((TM_RANK, LANES), lambda i: (i, 0)),
        out_shape=jax.ShapeDtypeStruct((t, LANES), jnp.float32),
        scratch_shapes=[row1(), row1()],
        compiler_params=_cparams(("arbitrary",)),
        name="rank",
    )(route, cnt, ltri, utri)


def _dispatch_kernel(dest_ref, hn_ref, xs_ref, sems):
    def row_copy(r, k):
        d = dest_ref[0, 0, 2 * r + k]
        src = hn_ref.at[pl.ds(pl.multiple_of(r * ROW_SLABS, ROW_SLABS), ROW_SLABS), :]
        return pltpu.make_async_copy(src, xs_ref.at[d], sems.at[k])

    def issue(r, c):
        row_copy(r, 0).start(priority=0)
        row_copy(r, 1).start(priority=1)
        return c

    lax.fori_loop(0, TM_DISPATCH, issue, 0, unroll=ROW_UNROLL)
    for k in range(TOP_K):
        pltpu.make_async_copy(hn_ref, hn_ref, sems.at[k]).wait()


def _dispatch(dest3, hn, n_rows):
    t = hn.shape[0] // ROW_SLABS
    n = t // TM_DISPATCH
    return pl.pallas_call(
        _dispatch_kernel,
        grid=(n,),
        in_specs=[pl.BlockSpec((1, 1, 2 * TM_DISPATCH), lambda i: (i, 0, 0), memory_space=pltpu.SMEM),
                  pl.BlockSpec((TM_DISPATCH * ROW_SLABS, LANES), lambda i: (i, 0))],
        out_specs=pl.BlockSpec(memory_space=pl.ANY),
        out_shape=jax.ShapeDtypeStruct((n_rows, ROW_SLABS, LANES), hn.dtype),
        scratch_shapes=[pltpu.SemaphoreType.DMA((2,))],
        compiler_params=_cparams(("arbitrary",)),
        name="dispatch",
    )(dest3, hn)


def _ffn_kernel(tile_ref, exp_ref, lo_ref, hi_ref, cast_ref, init_ref,
                xs_ref, wg_ref, wu_ref, wd_ref, ys_ref, wg_sc, wu_sc, wd_sc):
    v = pl.program_id(0)
    lo = lo_ref[v]
    hi = hi_ref[v]

    @pl.when(init_ref[v] == 1)
    def _():
        ys_ref[...] = jnp.zeros(ys_ref.shape, ys_ref.dtype)

    @pl.when((hi > lo) & (cast_ref[v] == 1))
    def _():
        wg_sc[...] = wg_ref[0].astype(jnp.bfloat16)
        wu_sc[...] = wu_ref[0].astype(jnp.bfloat16)
        wd_sc[...] = wd_ref[0].astype(jnp.bfloat16)

    for piece in range(FFN_BLK // FFN_SUB):
        row0 = tile_ref[v] * FFN_BLK + piece * FFN_SUB

        @pl.when((hi > row0) & (lo < row0 + FFN_SUB))
        def _(piece=piece, row0=row0):
            def slab(ref, c):
                return ref.at[pl.ds(piece * FFN_SUB * ROW_SLABS + c, FFN_SUB, stride=ROW_SLABS), :]

            pk = jnp.concatenate([slab(xs_ref, c)[...] for c in range(ROW_SLABS)], axis=1)
            x = jnp.concatenate(_unpack_bf16_pairs(pk), axis=1).astype(jnp.bfloat16)
            g = jnp.dot(x, wg_sc[...], preferred_element_type=jnp.float32)
            u = jnp.dot(x, wu_sc[...], preferred_element_type=jnp.float32)
            hb = (g * jax.nn.sigmoid(g) * u).astype(jnp.bfloat16)
            packed = _pack_bf16_pairs(jnp.dot(hb, wd_sc[...], preferred_element_type=jnp.float32))
            rows = row0 + lax.broadcasted_iota(jnp.int32, (FFN_SUB, LANES), 0)
            mine = (rows >= lo) & (rows < hi)
            for c in range(ROW_SLABS):
                out = slab(ys_ref, c)
                out[...] = jnp.where(mine, packed[:, c * LANES:(c + 1) * LANES], out[...])


def _ffn(seg, xs, w_gate, w_up, w_down):
    n_seg = seg[0].shape[0]
    grid_spec = pltpu.PrefetchScalarGridSpec(
        num_scalar_prefetch=6,
        grid=(n_seg,),
        in_specs=[
            pl.BlockSpec((FFN_BLK * ROW_SLABS, LANES), lambda v, t, e, *_: (t[v], 0)),
            pl.BlockSpec((1, D_MODEL, D_EXPERT), lambda v, t, e, *_: (e[v], 0, 0)),
            pl.BlockSpec((1, D_MODEL, D_EXPERT), lambda v, t, e, *_: (e[v], 0, 0)),
            pl.BlockSpec((1, D_EXPERT, D_MODEL), lambda v, t, e, *_: (e[v], 0, 0)),
        ],
        out_specs=pl.BlockSpec((FFN_BLK * ROW_SLABS, LANES), lambda v, t, e, *_: (t[v], 0)),
        scratch_shapes=[pltpu.VMEM((D_MODEL, D_EXPERT), jnp.bfloat16),
                        pltpu.VMEM((D_MODEL, D_EXPERT), jnp.bfloat16),
                        pltpu.VMEM((D_EXPERT, D_MODEL), jnp.bfloat16)],
    )
    return pl.pallas_call(
        _ffn_kernel,
        grid_spec=grid_spec,
        out_shape=jax.ShapeDtypeStruct(xs.shape, jnp.uint32),
        compiler_params=_cparams(("arbitrary",)),
        name="ffn",
    )(*seg, xs, w_gate, w_up, w_down)


def _segments(counts, n_rows):
    i32 = jnp.int32
    n_tiles = n_rows // FFN_BLK
    n_seg = n_tiles + N_EXPERTS
    tri = jnp.tril(jnp.ones((N_EXPERTS, N_EXPERTS), i32))
    ends = jnp.sum(tri * counts[None, :], axis=1)
    starts = ends - counts
    edges = jnp.arange(n_tiles, dtype=i32) * FFN_BLK
    rank_e = jnp.arange(n_tiles, dtype=i32) + jnp.sum(starts[None, :] <= edges[:, None], axis=1)
    rank_s = jnp.arange(N_EXPERTS, dtype=i32) + jnp.sum(edges[None, :] < starts[:, None], axis=1)
    seg = jnp.arange(n_seg, dtype=i32)
    lo = (jnp.sum(jnp.where(rank_e[None, :] == seg[:, None], edges[None, :], 0), axis=1)
          + jnp.sum(jnp.where(rank_s[None, :] == seg[:, None], starts[None, :], 0), axis=1))
    hi = jnp.concatenate([lo[1:], jnp.array([n_rows], i32)])
    valid = hi > lo
    tile = jnp.minimum(lo // FFN_BLK, n_tiles - 1)
    expert = jnp.minimum(jnp.sum(ends[None, :] <= lo[:, None], axis=1), N_EXPERTS - 1).astype(i32)
    upto = seg[None, :] <= seg[:, None]
    expert = jnp.max(jnp.where(upto & valid[None, :], expert[None, :], 0), axis=1)
    prev_expert = jnp.concatenate([jnp.array([-1], i32), expert[:-1]])
    first_valid = valid & (jnp.sum(jnp.where(upto & valid[None, :], 1, 0), axis=1) == 1)
    cast = valid & ((expert != prev_expert) | first_valid)
    prev_tile = jnp.concatenate([jnp.array([-1], i32), tile[:-1]])
    init = tile != prev_tile
    return (tile.astype(i32), expert, lo.astype(i32), hi.astype(i32), cast.astype(i32), init.astype(i32))


def _combine_kernel(dest_ref, route_ref, x1_ref, ys_ref, out_ref, y0_sc, y1_sc, sems):
    def row_copy(r, k):
        d = dest_ref[0, 0, 2 * r + k]
        dst = y0_sc if k == 0 else y1_sc
        return pltpu.make_async_copy(ys_ref.at[d], dst.at[pl.ds(pl.multiple_of(r * ROW_SLABS, ROW_SLABS), ROW_SLABS), :],
                                     sems.at[k])

    def issue(r, c):
        row_copy(r, 0).start(priority=0)
        row_copy(r, 1).start(priority=1)
        return c

    lax.fori_loop(0, TM_COMBINE, issue, 0, unroll=ROW_UNROLL)
    pltpu.make_async_copy(y0_sc, y0_sc, sems.at[0]).wait()
    pltpu.make_async_copy(y1_sc, y1_sc, sems.at[1]).wait()
    route = route_ref[...]
    w0 = route[:, 2:3]
    w1 = route[:, 3:4]
    half = D_MODEL // 2
    for c in range(ROW_SLABS):
        hi0, lo0 = _unpack_bf16_pairs(y0_sc[pl.ds(c, TM_COMBINE, stride=ROW_SLABS), :])
        hi1, lo1 = _unpack_bf16_pairs(y1_sc[pl.ds(c, TM_COMBINE, stride=ROW_SLABS), :])
        ch = slice(c * LANES, (c + 1) * LANES)
        cl = slice(half + c * LANES, half + (c + 1) * LANES)
        out_ref[:, ch] = x1_ref[:, ch] + (w0 * hi0 + w1 * hi1)
        out_ref[:, cl] = x1_ref[:, cl] + (w0 * lo0 + w1 * lo1)


def _combine(dest3, route, x1, ys):
    t = x1.shape[0]
    n = t // TM_COMBINE
    return pl.pallas_call(
        _combine_kernel,
        grid=(n,),
        in_specs=[pl.BlockSpec((1, 1, 2 * TM_COMBINE), lambda i: (i, 0, 0), memory_space=pltpu.SMEM),
                  pl.BlockSpec((TM_COMBINE, LANES), lambda i: (i, 0)),
                  pl.BlockSpec((TM_COMBINE, D_MODEL), lambda i: (i, 0)),
                  pl.BlockSpec(memory_space=pl.ANY)],
        out_specs=pl.BlockSpec((TM_COMBINE, D_MODEL), lambda i: (i, 0)),
        out_shape=jax.ShapeDtypeStruct((t, D_MODEL), jnp.float32),
        scratch_shapes=[pltpu.VMEM((TM_COMBINE * ROW_SLABS, LANES), jnp.uint32),
                        pltpu.VMEM((TM_COMBINE * ROW_SLABS, LANES), jnp.uint32),
                        pltpu.SemaphoreType.DMA((2,))],
        compiler_params=_cparams(("arbitrary",)),
        name="combine",
    )(dest3, route, x1, ys)


def _t5_bucket(rel):
    nb = T5_BUCKETS // 2
    max_exact = nb // 2
    side = jnp.where(rel > 0, nb, 0)
    n = jnp.abs(rel)
    nf = jnp.maximum(n, 1).astype(jnp.float32)
    large = max_exact + (jnp.log(nf / max_exact) / math.log(T5_MAX_DIST / max_exact)
                         * (nb - max_exact)).astype(jnp.int32)
    large = jnp.minimum(large, nb - 1)
    return side + jnp.where(n < max_exact, n, large)


def _rel_offsets(j, transposed=False):
    i = jnp.arange(TQ + TK)
    if transposed:
        return jnp.where(i < TQ, -i, (TQ + TK) - i) - j * TK
    return jnp.where(i < TK, i, i - (TQ + TK)) - j * TK


def _diff_bias_vecs(t5_table, transposed=False):
    vecs = jnp.stack([t5_table[_t5_bucket(_rel_offsets(j, transposed))].astype(jnp.float32).T for j in range(2)],
                     axis=1)
    vecs = jnp.stack([vecs, vecs], axis=2)
    far = t5_table[_t5_bucket(jnp.array(-(TK + 1)))].astype(jnp.float32)
    return vecs, far


def _band_bias_vecs(rel_table):
    i = jnp.arange(BAND_HQ + BAND_W)
    rel = jnp.where(i < BAND_W, i, i - (BAND_HQ + BAND_W)) - (BAND_W - BAND_HQ)
    vecs = rel_table[jnp.clip(rel, -B_MAX_REL, B_MAX_REL) + B_MAX_REL].astype(jnp.float32).T
    return vecs.reshape(B_HEADS // 2, 2, BAND_HQ + BAND_W)


def kernel(x, norm1_g, w_in, a_qnorm_g, a_knorm_g, a_lambda, a_subln_g, t5_table, b_qnorm_g, b_knorm_g,
           b_rel_table, w_branch_a, w_branch_b, w_out, norm2_g, w_router_group, b_router_group,
           w_router_expert, b_router_expert, w_gate, w_up, w_down):
    bsz, s_len, _ = x.shape
    n_tok = bsz * s_len
    f32, bf16 = jnp.float32, jnp.bfloat16
    assert s_len % TQ == 0 and TQ == TK and TQ % CHUNK == 0 and n_tok % TM_PROJ == 0
    assert TK >= T5_MAX_DIST and BAND_HQ % CHUNK == 0 and BAND_W % LANES == 0 and BAND_W - BAND_HQ <= TQ
    l = 0
    x2 = x.reshape(n_tok, D_MODEL)

    w = w_in[l]
    qk = w[:, :1024].astype(bf16).reshape(D_MODEL, 2, 2, A_HEADS, A_HEAD_DIM)
    w_qk = qk.transpose(0, 1, 3, 2, 4).reshape(D_MODEL, 1024)
    w_rest = w[:, 1024:].astype(bf16)
    gn = jnp.stack([jnp.tile(a_qnorm_g[l] * (A_HEAD_DIM ** -0.5 * LOG2E), 8), jnp.tile(a_knorm_g[l], 8),
                    jnp.tile(b_qnorm_g[l] * (B_HEAD_DIM ** -0.5 * LOG2E), 8), jnp.tile(b_knorm_g[l], 8)]).astype(f32)
    gmat = jnp.asarray(np.kron(np.eye(4), np.ones((A_HEAD_DIM, A_HEAD_DIM))), dtype=bf16)

    qa, ka, va, qb, kb, vb, ga, gb = _proj(x2, norm1_g[l][None].astype(f32), w_qk, w_rest, gn, gmat)

    lam_init = 0.8 - 0.6 * math.exp(-0.3 * l)
    lp = a_lambda[l].astype(f32)
    lam = jnp.exp(jnp.sum(lp[0] * lp[1])) - jnp.exp(jnp.sum(lp[2] * lp[3])) + lam_init
    bias_a, far_a = _diff_bias_vecs(t5_table)
    bmax_a = jnp.maximum(jnp.max(bias_a, axis=(1, 2, 3)), far_a)
    scal_a = (jnp.concatenate([lam[None], far_a, bmax_a]) * jnp.array([1.0] + [LOG2E] * (2 * A_HEADS))).astype(f32)
    gsub = (a_subln_g[l] * (1.0 - lam_init))[None].astype(f32)
    bias_at, _ = _diff_bias_vecs(t5_table, transposed=True)
    oa = _diff_attention(scal_a, qa, ka, va, bias_a * LOG2E, bias_at * LOG2E, gsub, bsz=bsz, s_len=s_len)

    bias_b = _band_bias_vecs(b_rel_table[l])
    ob = _band_attention(qb, kb, vb, bias_b * LOG2E, bsz=bsz, s_len=s_len)

    wr = jnp.zeros((D_MODEL, LANES), f32)
    wr = wr.at[:, :N_GROUPS].set(w_router_group[l]).at[:, N_GROUPS:N_GROUPS + N_EXPERTS].set(w_router_expert[l])
    wrh = wr.astype(bf16)
    wr2 = jnp.concatenate([wrh, (wr - wrh.astype(f32)).astype(bf16)], axis=1)
    br = jnp.zeros((1, LANES), f32)
    br = br.at[0, :N_GROUPS].set(b_router_group[l]).at[0, N_GROUPS:N_GROUPS + N_EXPERTS].set(b_router_expert[l])
    x1, hn, route, cnt = _post(oa, ob, ga, gb, x2, w_branch_a[l].astype(bf16), w_branch_b[l].astype(bf16),
                               w_out[l].astype(bf16), norm2_g[l][None].astype(f32), wr2, br)

    ltri = jnp.asarray(np.tril(np.ones((TM_RANK, TM_RANK)), -1), dtype=bf16)
    utri = jnp.asarray(np.triu(np.ones((LANES, LANES)), 1), dtype=bf16)
    dest = _rank(route, cnt, ltri, utri)

    counts = cnt[0, :N_EXPERTS].astype(jnp.int32)
    n_rows = n_tok * TOP_K
    seg = _segments(counts, n_rows)

    dest2 = dest[:, :TOP_K].astype(jnp.int32)
    xs = _dispatch(dest2.reshape(n_tok // TM_DISPATCH, 1, TOP_K * TM_DISPATCH), hn, n_rows)
    ys = _ffn(seg, xs.reshape(n_rows * ROW_SLABS, LANES), w_gate[l], w_up[l], w_down[l])
    out = _combine(dest2.reshape(n_tok // TM_COMBINE, 1, TOP_K * TM_COMBINE), route, x1,
                   ys.reshape(n_rows, ROW_SLABS, LANES))
    return out.reshape(bsz, s_len, D_MODEL)
```

```python
import functools
import math

import jax
import jax.numpy as jnp
import numpy as np
from jax import lax
from jax.experimental import pallas as pl
from jax.experimental.pallas import tpu as pltpu

D_MODEL = 1024
CHUNK = 64
A_HEADS = 4
A_HEAD_DIM = 64
A_VDIM = 2 * A_HEAD_DIM
B_HEADS = 8
B_HEAD_DIM = 64
B_LEFT_CHUNKS = 8
B_MAX_REL = 128
T5_BUCKETS = 32
T5_MAX_DIST = 128
N_GROUPS = 4
EXPERTS_PER_GROUP = 8
N_EXPERTS = N_GROUPS * EXPERTS_PER_GROUP
TOP_K = 2
D_EXPERT = 512
EPS = 1e-6
NEG = -1e30
LOG2E = 1.0 / math.log(2.0)
DENOM_FLOOR = 2.0 ** -100

CHUNK_SHIFT = CHUNK.bit_length() - 1
assert 1 << CHUNK_SHIFT == CHUNK
LANES = 128
A_W = A_HEADS * 2 * A_HEAD_DIM
B_W = B_HEADS * B_HEAD_DIM
PROJ_W = 4 * 256 + 4 * 512 + 2 * D_MODEL

TM_PROJ = 512
TQ = 512
TK = 512
FAR_UNROLL = 4
TM_POST = 1024
TM_RANK = 1024
TM_DISPATCH = 4096
TM_COMBINE = 2048
ROW_UNROLL = 8
FFN_BLK = 1024
FFN_SUB = 256
ROW_SLABS = D_MODEL // 2 // LANES
VMEM_LIMIT = 56 * 1024 * 1024


def _cparams(sem):
    return pltpu.CompilerParams(dimension_semantics=sem, vmem_limit_bytes=VMEM_LIMIT)


def _pack_bf16_pairs(x):
    bits = lax.bitcast_convert_type(x.astype(jnp.bfloat16).astype(jnp.float32), jnp.uint32)
    half = x.shape[1] // 2
    return (bits[:, :half] & jnp.uint32(0xFFFF0000)) | lax.shift_right_logical(bits[:, half:], jnp.uint32(16))


def _unpack_bf16_pairs(pk):
    return (lax.bitcast_convert_type(pk & jnp.uint32(0xFFFF0000), jnp.float32),
            lax.bitcast_convert_type(lax.shift_left(pk, jnp.uint32(16)), jnp.float32))


def _proj_kernel(x_ref, g1_ref, wqk_ref, w_ref, gn_ref, gmat_ref,
                 qa_ref, ka_ref, va_ref, qb_ref, kb_ref, vb_ref, ga_ref, gb_ref):
    x = x_ref[...]
    xn = x * lax.rsqrt(jnp.mean(x * x, axis=-1, keepdims=True) + EPS) * g1_ref[...]
    xn = xn.astype(jnp.bfloat16)
    n_qk = wqk_ref.shape[1]

    def slab(c0, width):
        w = wqk_ref[:, c0:c0 + width] if c0 < n_qk else w_ref[:, c0 - n_qk:c0 - n_qk + width]
        return jnp.dot(xn, w, preferred_element_type=jnp.float32)

    def headnorm(y, gi):
        sq = (y * y).astype(jnp.bfloat16)
        half = gmat_ref.shape[0]
        ss = jnp.concatenate([jnp.dot(sq[:, c:c + half], gmat_ref[...], preferred_element_type=jnp.float32)
                              for c in range(0, y.shape[1], half)], axis=1)
        return y * lax.rsqrt(ss * (1.0 / A_HEAD_DIM) + EPS) * gn_ref[gi:gi + 1, :]

    qa_ref[...] = headnorm(slab(0, 512), 0).astype(jnp.bfloat16)
    ka_ref[...] = headnorm(slab(512, 512), 1).astype(jnp.bfloat16)
    va_ref[...] = slab(1024, 512).astype(jnp.bfloat16)
    qb_ref[...] = headnorm(slab(1536, 512), 2).astype(jnp.bfloat16)
    kb_ref[...] = headnorm(slab(2048, 512), 3).astype(jnp.bfloat16)
    vb_ref[...] = slab(2560, 512).astype(jnp.bfloat16)
    for j in range(2):
        ga_ref[:, j * 512:(j + 1) * 512] = jax.nn.sigmoid(slab(3072 + j * 512, 512)).astype(jnp.bfloat16)
        gb_ref[:, j * 512:(j + 1) * 512] = jax.nn.sigmoid(slab(4096 + j * 512, 512)).astype(jnp.bfloat16)


def _proj(x2, g1, w_qk, w_rest, gn, gmat):
    t = x2.shape[0]
    n = t // TM_PROJ
    row = lambda w: pl.BlockSpec((TM_PROJ, w), lambda i: (i, 0))
    full = lambda a: pl.BlockSpec(a.shape, lambda i: (0,) * a.ndim)
    outs = [jax.ShapeDtypeStruct((t, 512), jnp.bfloat16)] * 6 + [jax.ShapeDtypeStruct((t, D_MODEL), jnp.bfloat16)] * 2
    return pl.pallas_call(
        _proj_kernel,
        grid=(n,),
        in_specs=[row(D_MODEL), full(g1), full(w_qk), full(w_rest), full(gn), full(gmat)],
        out_specs=[row(512)] * 6 + [row(D_MODEL)] * 2,
        out_shape=outs,
        compiler_params=_cparams(("arbitrary",)),
        name="proj",
    )(x2, g1, w_qk, w_rest, gn, gmat)


def _diff_kernel(scal_ref, q_ref, k_ref, v_ref, vec_ref, vect_ref, gsub_ref, o_ref,
                 bias_sc, biast_sc, vt_sc, kmaxt_sc, qz_sc, qzt_sc, refn_sc, reff_sc, l_sc, acc_sc,
                 m_sc, acc2_sc, mf_sc, accf_sc, *, n_near):
    f32 = jnp.float32
    h = pl.program_id(1)
    qi = pl.program_id(2)
    nc = TK // LANES
    r2 = 2 * TQ
    n_heads = pl.num_programs(1)

    def near_span(j):
        return slice((n_near - 1 - j) * TK, (n_near - j) * TK)

    @pl.when(qi == 0)
    def _():
        row = lax.broadcasted_iota(jnp.int32, (TQ, TK), 0)
        col = lax.broadcasted_iota(jnp.int32, (TQ, TK), 1)
        for j in range(n_near):
            ok = lax.shift_right_arithmetic(row, CHUNK_SHIFT) >= lax.shift_right_arithmetic(col - j * TK, CHUNK_SHIFT)
            okt = lax.shift_right_arithmetic(col, CHUNK_SHIFT) >= lax.shift_right_arithmetic(row - j * TK, CHUNK_SHIFT)
            for half in range(2):
                vec = jnp.broadcast_to(vec_ref[0, j, half:half + 1, :], (TQ, TQ + TK))
                tile = pltpu.roll(vec, 0, 1, stride=1, stride_axis=0)[:, :TK]
                bias_sc[half * TQ:(half + 1) * TQ, near_span(j)] = jnp.where(ok, tile, NEG)
                vect = jnp.broadcast_to(vect_ref[0, j, half:half + 1, :], (TK, TQ + TK))
                tilet = pltpu.roll(vect, 0, 1, stride=1, stride_axis=0)[:, :TQ]
                biast_sc[near_span(j), half * TQ:(half + 1) * TQ] = jnp.where(okt, tilet, NEG)
        vt_sc[...] = v_ref[...].T
        kf = k_ref[...].astype(f32)
        ksq = kf * kf
        klane = lax.broadcasted_iota(jnp.int32, ksq.shape, 1)
        for half in range(2):
            sel = (klane < 64) if half == 0 else (klane >= 64)
            norm2 = jnp.max(jnp.sum(jnp.where(sel, ksq, 0.0), axis=-1, keepdims=True), axis=0, keepdims=True)
            kmaxt_sc[:, half * TQ:(half + 1) * TQ] = jnp.broadcast_to(jnp.sqrt(norm2), (8, TQ))

    q = q_ref[...]
    lane = lax.broadcasted_iota(jnp.int32, q.shape, 1)
    zero = jnp.zeros_like(q)
    q1 = jnp.where(lane < 64, q, zero)
    q2 = jnp.where(lane >= 64, q, zero)
    qzt_sc[:, 0:TQ] = q1.T
    qzt_sc[:, TQ:r2] = q2.T
    cfar = scal_ref[1 + h]
    bias_max = scal_ref[1 + n_heads + h]
    n_far = jnp.maximum(qi - (n_near - 1), 0)

    def key_block(kblk):
        return k_ref[pl.ds(pl.multiple_of(kblk * TK, TK), TK), :]

    def finish(od):
        od = od * lax.rsqrt(jnp.mean(od * od, axis=-1, keepdims=True) + EPS) * gsub_ref[...]
        o_ref[...] = od.astype(o_ref.dtype)

    def fixed_reference_pass():
        qf = qzt_sc[...].astype(f32)
        qnorm = jnp.sqrt(jnp.sum(qf * qf, axis=0, keepdims=True))
        refn_sc[...] = qnorm * kmaxt_sc[...] + bias_max
        l_sc[...] = jnp.zeros(l_sc.shape, f32)
        acc_sc[...] = jnp.zeros(acc_sc.shape, f32)

        def accumulate(st, kblk, ref_ref):
            p = jnp.exp2(st.reshape(TK // 8, 8, r2) - ref_ref[...][None])
            l_sc[...] += jnp.sum(p, axis=0)
            vt = vt_sc[:, pl.ds(pl.multiple_of(kblk * TK, TK), TK)]
            acc_sc[...] += jnp.dot(vt, p.reshape(TK, r2).astype(jnp.bfloat16), preferred_element_type=f32)

        def scores_t(kblk):
            return jnp.dot(key_block(kblk), qzt_sc[...], preferred_element_type=f32)

        for j in range(n_near):
            def near(j=j):
                accumulate(scores_t(qi - j) + biast_sc[near_span(j), :], qi - j, refn_sc)
            if j == 0:
                near()
            else:
                pl.when(qi >= j)(near)

        reff_sc[...] = refn_sc[...] - cfar

        def far(kblk, carry):
            accumulate(scores_t(kblk), kblk, reff_sc)
            return carry

        def far_group(i, carry):
            for u in range(FAR_UNROLL):
                far(FAR_UNROLL * i + u, carry)
            return carry

        lax.fori_loop(0, n_far // FAR_UNROLL, far_group, 0)
        lax.fori_loop((n_far // FAR_UNROLL) * FAR_UNROLL, n_far, far, 0)

        denom = jnp.sum(l_sc[...], axis=0, keepdims=True)
        ot = acc_sc[...] / denom
        finish((ot[:, 0:TQ] - scal_ref[0] * ot[:, TQ:r2]).T)
        return denom

    def exact_pass():
        qz_sc[0:TQ, :] = q1
        qz_sc[TQ:r2, :] = q2

        def scores(kblk):
            return lax.dot_general(qz_sc[...], key_block(kblk), (((1,), (1,)), ((), ())), preferred_element_type=f32)

        def update(s, kblk, m_ref, a_ref):
            cols = [s[:, c * LANES:(c + 1) * LANES] for c in range(nc)]
            m_old = m_ref[...]
            m_new = jnp.maximum(m_old, jnp.max(functools.reduce(jnp.maximum, cols), axis=-1, keepdims=True))
            alpha = jnp.exp2(m_old - m_new)
            ps = [jnp.exp2(c - m_new) for c in cols]
            p = jnp.concatenate([x.astype(jnp.bfloat16) for x in ps], axis=1)
            v = v_ref[pl.ds(pl.multiple_of(kblk * TK, TK), TK), :]
            a_ref[:, 0:LANES] = alpha * a_ref[:, 0:LANES] + jnp.dot(p, v, preferred_element_type=f32)
            a_ref[:, LANES:2 * LANES] = alpha * a_ref[:, LANES:2 * LANES] + functools.reduce(jnp.add, ps)
            m_ref[...] = m_new

        m_sc[...] = jnp.full(m_sc.shape, NEG, f32)
        acc2_sc[...] = jnp.zeros(acc2_sc.shape, f32)
        for j in range(n_near):
            def near(j=j):
                update(scores(qi - j) + bias_sc[:, near_span(j)], qi - j, m_sc, acc2_sc)
            if j == 0:
                near()
            else:
                pl.when(qi >= j)(near)

        mf_sc[...] = jnp.full(mf_sc.shape, NEG, f32)
        accf_sc[...] = jnp.zeros(accf_sc.shape, f32)

        def far(kblk, carry):
            update(scores(kblk), kblk, mf_sc, accf_sc)
            return carry

        lax.fori_loop(0, n_far, far, 0)
        mf = mf_sc[...] + cfar
        mn = m_sc[...]
        m = jnp.maximum(mf, mn)
        wf = jnp.exp2(mf - m)
        wn = jnp.exp2(mn - m)
        tot = (jnp.concatenate([wf, wf], axis=1) * accf_sc[...]
               + jnp.concatenate([wn, wn], axis=1) * acc2_sc[...])
        o = tot[:, 0:LANES] / jnp.sum(tot[:, LANES:2 * LANES], axis=-1, keepdims=True)
        finish(o[:TQ] - scal_ref[0] * o[TQ:])

    denom = fixed_reference_pass()
    pl.when(jnp.logical_not(jnp.min(denom) >= DENOM_FLOOR))(exact_pass)


def _diff_attention(scal, q, k, v, vecs, vecs_t, gsub, *, bsz, s_len):
    n_blk = q.shape[1] // LANES
    nq = s_len // TQ
    n_near = vecs.shape[1]
    assert vecs.shape == vecs_t.shape == (n_blk, n_near, 2, TQ + TK)
    kern = functools.partial(_diff_kernel, n_near=n_near)
    f32, bf16 = jnp.float32, jnp.bfloat16
    vm = pltpu.VMEM
    vec_spec = pl.BlockSpec((1, n_near, 2, TQ + TK), lambda b, h, i: (h, 0, 0, 0))
    return pl.pallas_call(
        kern,
        grid=(bsz, n_blk, nq),
        in_specs=[
            pl.BlockSpec(memory_space=pltpu.SMEM),
            pl.BlockSpec((TQ, LANES), lambda b, h, i: (b * nq + i, h)),
            pl.BlockSpec((s_len, LANES), lambda b, h, i: (b, h)),
            pl.BlockSpec((s_len, LANES), lambda b, h, i: (b, h)),
            vec_spec, vec_spec,
            pl.BlockSpec((1, LANES), lambda b, h, i: (0, 0)),
        ],
        out_specs=pl.BlockSpec((TQ, LANES), lambda b, h, i: (b * nq + i, h)),
        out_shape=jax.ShapeDtypeStruct(q.shape, bf16),
        scratch_shapes=[vm((2 * TQ, n_near * TK), f32), vm((n_near * TK, 2 * TQ), f32),
                        vm((LANES, s_len), bf16), vm((8, 2 * TQ), f32),
                        vm((2 * TQ, LANES), bf16), vm((LANES, 2 * TQ), bf16),
                        vm((8, 2 * TQ), f32), vm((8, 2 * TQ), f32), vm((8, 2 * TQ), f32),
                        vm((LANES, 2 * TQ), f32),
                        vm((2 * TQ, LANES), f32), vm((2 * TQ, 2 * LANES), f32),
                        vm((2 * TQ, LANES), f32), vm((2 * TQ, 2 * LANES), f32)],
        compiler_params=_cparams(("arbitrary", "arbitrary", "arbitrary")),
        name="attn_diff",
    )(scal, q, k, v, vecs, vecs_t, gsub)


BAND_TQ = 1024
BAND_HQ = 256
BAND_SUBS = BAND_TQ // BAND_HQ
BAND_W = B_LEFT_CHUNKS * CHUNK + BAND_HQ


def _band_kernel(q_ref, k_ref, v_ref, vec_ref, o_ref, bias_sc, qz_sc):
    f32 = jnp.float32
    qi = pl.program_id(2)
    hq, w = BAND_HQ, BAND_W
    back = w - hq

    @pl.when(qi == 0)
    def _():
        qchunk = lax.shift_right_arithmetic(lax.broadcasted_iota(jnp.int32, (hq, w), 0), CHUNK_SHIFT)
        kchunk = lax.shift_right_arithmetic(lax.broadcasted_iota(jnp.int32, (hq, w), 1) - back, CHUNK_SHIFT)
        dchunk = qchunk - kchunk
        allowed = (dchunk >= 0) & (dchunk <= B_LEFT_CHUNKS)
        for half in range(2):
            vec = jnp.broadcast_to(vec_ref[0, half:half + 1, :], (hq, hq + w))
            tile = pltpu.roll(vec, 0, 1, stride=1, stride_axis=0)[:, :w]
            bias_sc[half * hq:(half + 1) * hq, :] = jnp.where(allowed, tile, NEG)

    q = q_ref[...]
    lane = lax.broadcasted_iota(jnp.int32, (hq, LANES), 1)
    zero = jnp.zeros((hq, LANES), q.dtype)
    for u in range(BAND_SUBS):
        qu = q[u * hq:(u + 1) * hq, :]
        qz_sc[(2 * u) * hq:(2 * u + 1) * hq, :] = jnp.where(lane < 64, qu, zero)
        qz_sc[(2 * u + 1) * hq:(2 * u + 2) * hq, :] = jnp.where(lane >= 64, qu, zero)

    def scores(u, k0, n_keys):
        k = k_ref[pl.ds(k0, n_keys), :]
        return lax.dot_general(qz_sc[2 * u * hq:(2 * u + 2) * hq, :], k, (((1,), (1,)), ((), ())),
                               preferred_element_type=f32) + bias_sc[:, w - n_keys:w]

    def softmax(s):
        cols = [s[:, c * LANES:(c + 1) * LANES] for c in range(s.shape[1] // LANES)]
        m = jnp.max(functools.reduce(jnp.maximum, cols), axis=-1, keepdims=True)
        ps = [jnp.exp2(c - m) for c in cols]
        denom = jnp.sum(functools.reduce(jnp.add, ps), axis=-1, keepdims=True)
        return jnp.concatenate([x.astype(jnp.bfloat16) for x in ps], axis=1), denom

    def finish(u, p, denom, k0):
        o = jnp.dot(p, v_ref[pl.ds(k0, p.shape[1]), :], preferred_element_type=f32) / denom
        o_ref[u * hq:(u + 1) * hq, :] = jnp.where(lane < 64, o[:hq], o[hq:]).astype(o_ref.dtype)

    def step(k0s, n_keys):
        ss = [scores(u, k0s[u], n_keys[u]) for u in range(BAND_SUBS)]
        pd = [softmax(x) for x in ss]
        for u in range(BAND_SUBS):
            finish(u, pd[u][0], pd[u][1], k0s[u])

    first = tuple(min((u + 1) * hq, w) for u in range(BAND_SUBS))

    @pl.when(qi == 0)
    def _():
        step(tuple(max((u + 1) * hq - w, 0) for u in range(BAND_SUBS)), first)

    @pl.when(qi > 0)
    def _():
        start = pl.multiple_of(qi * BAND_TQ - back, hq)
        step(tuple(pl.multiple_of(start + u * hq, hq) for u in range(BAND_SUBS)), (w,) * BAND_SUBS)


def _band_attention(q, k, v, vecs, *, bsz, s_len):
    n_blk = q.shape[1] // LANES
    nq = s_len // BAND_TQ
    assert vecs.shape == (n_blk, 2, BAND_HQ + BAND_W)
    return pl.pallas_call(
        _band_kernel,
        grid=(bsz, n_blk, nq),
        in_specs=[
            pl.BlockSpec((BAND_TQ, LANES), lambda b, h, i: (b * nq + i, h)),
            pl.BlockSpec((s_len, LANES), lambda b, h, i: (b, h)),
            pl.BlockSpec((s_len, LANES), lambda b, h, i: (b, h)),
            pl.BlockSpec((1, 2, BAND_HQ + BAND_W), lambda b, h, i: (h, 0, 0)),
        ],
        out_specs=pl.BlockSpec((BAND_TQ, LANES), lambda b, h, i: (b * nq + i, h)),
        out_shape=jax.ShapeDtypeStruct(q.shape, jnp.bfloat16),
        scratch_shapes=[pltpu.VMEM((2 * BAND_HQ, BAND_W), jnp.float32),
                        pltpu.VMEM((2 * BAND_TQ, LANES), jnp.bfloat16)],
        compiler_params=_cparams(("arbitrary", "arbitrary", "arbitrary")),
        name="attn_band",
    )(q, k, v, vecs)


def _post_kernel(oa_ref, ob_ref, ga_ref, gb_ref, x_ref, wa_ref, wb_ref, wo_ref, g2_ref,
                 wr2_ref, br_ref, x1_ref, hn_ref, route_ref, cnt_ref):
    f32 = jnp.float32
    ya = jnp.dot(oa_ref[...], wa_ref[...], preferred_element_type=f32)
    yb = jnp.dot(ob_ref[...], wb_ref[...], preferred_element_type=f32)
    mixed = ga_ref[...].astype(f32) * ya + gb_ref[...].astype(f32) * yb
    x1 = x_ref[...] + jnp.dot(mixed.astype(jnp.bfloat16), wo_ref[...], preferred_element_type=f32)
    x1_ref[...] = x1
    hn = x1 * lax.rsqrt(jnp.mean(x1 * x1, axis=-1, keepdims=True) + EPS) * g2_ref[...]
    hh = hn.astype(jnp.bfloat16)
    packed = _pack_bf16_pairs(hn)
    for c in range(ROW_SLABS):
        hn_ref[pl.ds(c, TM_POST, stride=ROW_SLABS), :] = packed[:, c * LANES:(c + 1) * LANES]

    hl = (hn - hh.astype(f32)).astype(jnp.bfloat16)
    hw = jnp.dot(hh, wr2_ref[...], preferred_element_type=f32)
    lg = (hw[:, 0:LANES] + hw[:, LANES:2 * LANES]
          + jnp.dot(hl, wr2_ref[:, 0:LANES], preferred_element_type=f32)) + br_ref[...]

    lanei = lax.broadcasted_iota(jnp.int32, lg.shape, 1)
    lanef = lanei.astype(f32)
    big = 999.0
    gmask = lanei < N_GROUPS
    gl = jnp.where(gmask, lg, NEG)
    gm = jnp.max(gl, axis=-1, keepdims=True)
    ge = jnp.where(gmask, jnp.exp(gl - gm), 0.0)
    gp = ge / jnp.sum(ge, axis=-1, keepdims=True)
    p_g = jnp.max(gp, axis=-1, keepdims=True)
    gidx = jnp.min(jnp.where(gmask & (gp == p_g), lanef, big), axis=-1, keepdims=True)
    egrp = lax.shift_right_arithmetic(lanei - N_GROUPS, 3).astype(f32)
    emask = (lanei >= N_GROUPS) & (lanei < N_GROUPS + N_EXPERTS) & (egrp == gidx)
    el = jnp.where(emask, lg, NEG)
    v1 = jnp.max(el, axis=-1, keepdims=True)
    i1 = jnp.min(jnp.where(emask & (el == v1), lanef, big), axis=-1, keepdims=True)
    emask2 = emask & (lanef != i1)
    el2 = jnp.where(emask2, lg, NEG)
    v2 = jnp.max(el2, axis=-1, keepdims=True)
    i2 = jnp.min(jnp.where(emask2 & (el2 == v2), lanef, big), axis=-1, keepdims=True)
    t = jnp.exp(v2 - v1)
    den = 1.0 + t
    w1 = p_g * (1.0 / den)
    w2 = p_g * (t / den)
    route = jnp.where(lanei == 0, i1 - N_GROUPS,
                      jnp.where(lanei == 1, i2 - N_GROUPS,
                                jnp.where(lanei == 2, w1, jnp.where(lanei == 3, w2, 0.0))))
    route_ref[...] = route

    @pl.when(pl.program_id(0) == 0)
    def _():
        cnt_ref[...] = jnp.zeros(cnt_ref.shape, f32)

    chosen = ((lanef == i1 - N_GROUPS) | (lanef == i2 - N_GROUPS)).astype(f32)
    cnt_ref[...] += jnp.broadcast_to(jnp.sum(chosen, axis=0, keepdims=True), cnt_ref.shape)


def _post(oa, ob, ga, gb, x2, wa, wb, wo, g2, wr2, br):
    t = x2.shape[0]
    n = t // TM_POST
    row = lambda w: pl.BlockSpec((TM_POST, w), lambda i: (i, 0))
    full = lambda a: pl.BlockSpec(a.shape, lambda i: (0,) * a.ndim)
    return pl.pallas_call(
        _post_kernel,
        grid=(n,),
        in_specs=[row(512), row(512), row(D_MODEL), row(D_MODEL), row(D_MODEL),
                  full(wa), full(wb), full(wo), full(g2), full(wr2), full(br)],
        out_specs=[row(D_MODEL), pl.BlockSpec((TM_POST * ROW_SLABS, LANES), lambda i: (i, 0)), row(LANES),
                   pl.BlockSpec((8, LANES), lambda i: (0, 0))],
        out_shape=[jax.ShapeDtypeStruct((t, D_MODEL), jnp.float32),
                   jax.ShapeDtypeStruct((t * ROW_SLABS, LANES), jnp.uint32),
                   jax.ShapeDtypeStruct((t, LANES), jnp.float32),
                   jax.ShapeDtypeStruct((8, LANES), jnp.float32)],
        compiler_params=_cparams(("arbitrary",)),
        name="post",
    )(oa, ob, ga, gb, x2, wa, wb, wo, g2, wr2, br)


def _rank_kernel(route_ref, cnt_ref, ltri_ref, utri_ref, dest_ref, pstart_sc, base_sc):
    f32 = jnp.float32
    i = pl.program_id(0)
    route = route_ref[...]
    lanef = lax.broadcasted_iota(jnp.int32, route.shape, 1).astype(f32)
    oh1 = (lanef == route[:, 0:1]).astype(f32)
    oh2 = (lanef == route[:, 1:2]).astype(f32)
    both = oh1 + oh2

    @pl.when(i == 0)
    def _():
        cnt = cnt_ref[0:1, :]
        chi = jnp.floor(cnt * (1.0 / 256.0))
        clo = cnt - chi * 256.0
        split = jnp.concatenate([jnp.broadcast_to(chi, (8, LANES)), jnp.broadcast_to(clo, (8, LANES))], axis=0)
        excl = jnp.dot(split.astype(jnp.bfloat16), utri_ref[...], preferred_element_type=f32)
        pstart_sc[...] = excl[0:1] * 256.0 + excl[8:9]
        base_sc[...] = jnp.zeros(base_sc.shape, f32)

    prior = jnp.dot(ltri_ref[...], both.astype(jnp.bfloat16), preferred_element_type=f32)
    slot = prior + base_sc[...] + pstart_sc[...]
    d1 = jnp.sum(oh1 * slot, axis=-1, keepdims=True)
    d2 = jnp.sum(oh2 * slot, axis=-1, keepdims=True)
    dest_ref[...] = jnp.where(lanef == 0.0, d1, jnp.where(lanef == 1.0, d2, 0.0))
    base_sc[...] += jnp.sum(both, axis=0, keepdims=True)


def _rank(route, cnt, ltri, utri):
    t = route.shape[0]
    n = t // TM_RANK
    full = lambda a: pl.BlockSpec(a.shape, lambda i: (0,) * a.ndim)
    row1 = lambda: pltpu.VMEM((1, LANES), jnp.float32)
    return pl.pallas_call(
        _rank_kernel,
        grid=(n,),
        in_specs=[pl.BlockSpec((TM_RANK, LANES), lambda i: (i, 0)), full(cnt), full(ltri), full(utri)],
        out_specs=pl.BlockSpec((TM_RANK, LANES), lambda i: (i, 0)),
        out_shape=jax.ShapeDtypeStruct((t, LANES), jnp.float32),
        scratch_shapes=[row1(), row1()],
        compiler_params=_cparams(("arbitrary",)),
        name="rank",
    )(route, cnt, ltri, utri)


def _dispatch_kernel(dest_ref, hn_ref, xs_ref, sems):
    def row_copy(r, k):
        d = dest_ref[0, 0, 2 * r + k]
        src = hn_ref.at[pl.ds(pl.multiple_of(r * ROW_SLABS, ROW_SLABS), ROW_SLABS), :]
        return pltpu.make_async_copy(src, xs_ref.at[d], sems.at[k])

    def issue(r, c):
        row_copy(r, 0).start(priority=0)
        row_copy(r, 1).start(priority=1)
        return c

    lax.fori_loop(0, TM_DISPATCH, issue, 0, unroll=ROW_UNROLL)
    for k in range(TOP_K):
        pltpu.make_async_copy(hn_ref, hn_ref, sems.at[k]).wait()


def _dispatch(dest3, hn, n_rows):
    t = hn.shape[0] // ROW_SLABS
    n = t // TM_DISPATCH
    return pl.pallas_call(
        _dispatch_kernel,
        grid=(n,),
        in_specs=[pl.BlockSpec((1, 1, 2 * TM_DISPATCH), lambda i: (i, 0, 0), memory_space=pltpu.SMEM),
                  pl.BlockSpec((TM_DISPATCH * ROW_SLABS, LANES), lambda i: (i, 0))],
        out_specs=pl.BlockSpec(memory_space=pl.ANY),
        out_shape=jax.ShapeDtypeStruct((n_rows, ROW_SLABS, LANES), hn.dtype),
        scratch_shapes=[pltpu.SemaphoreType.DMA((2,))],
        compiler_params=_cparams(("arbitrary",)),
        name="dispatch",
    )(dest3, hn)


def _ffn_kernel(tile_ref, exp_ref, lo_ref, hi_ref, cast_ref, init_ref,
                xs_ref, wg_ref, wu_ref, wd_ref, ys_ref, wg_sc, wu_sc, wd_sc):
    v = pl.program_id(0)
    lo = lo_ref[v]
    hi = hi_ref[v]

    @pl.when(init_ref[v] == 1)
    def _():
        ys_ref[...] = jnp.zeros(ys_ref.shape, ys_ref.dtype)

    @pl.when((hi > lo) & (cast_ref[v] == 1))
    def _():
        wg_sc[...] = wg_ref[0].astype(jnp.bfloat16)
        wu_sc[...] = wu_ref[0].astype(jnp.bfloat16)
        wd_sc[...] = wd_ref[0].astype(jnp.bfloat16)

    for piece in range(FFN_BLK // FFN_SUB):
        row0 = tile_ref[v] * FFN_BLK + piece * FFN_SUB

        @pl.when((hi > row0) & (lo < row0 + FFN_SUB))
        def _(piece=piece, row0=row0):
            def slab(ref, c):
                return ref.at[pl.ds(piece * FFN_SUB * ROW_SLABS + c, FFN_SUB, stride=ROW_SLABS), :]

            pk = jnp.concatenate([slab(xs_ref, c)[...] for c in range(ROW_SLABS)], axis=1)
            x = jnp.concatenate(_unpack_bf16_pairs(pk), axis=1).astype(jnp.bfloat16)
            g = jnp.dot(x, wg_sc[...], preferred_element_type=jnp.float32)
            u = jnp.dot(x, wu_sc[...], preferred_element_type=jnp.float32)
            hb = (g * jax.nn.sigmoid(g) * u).astype(jnp.bfloat16)
            packed = _pack_bf16_pairs(jnp.dot(hb, wd_sc[...], preferred_element_type=jnp.float32))
            rows = row0 + lax.broadcasted_iota(jnp.int32, (FFN_SUB, LANES), 0)
            mine = (rows >= lo) & (rows < hi)
            for c in range(ROW_SLABS):
                out = slab(ys_ref, c)
                out[...] = jnp.where(mine, packed[:, c * LANES:(c + 1) * LANES], out[...])


def _ffn(seg, xs, w_gate, w_up, w_down):
    n_seg = seg[0].shape[0]
    grid_spec = pltpu.PrefetchScalarGridSpec(
        num_scalar_prefetch=6,
        grid=(n_seg,),
        in_specs=[
            pl.BlockSpec((FFN_BLK * ROW_SLABS, LANES), lambda v, t, e, *_: (t[v], 0)),
            pl.BlockSpec((1, D_MODEL, D_EXPERT), lambda v, t, e, *_: (e[v], 0, 0)),
            pl.BlockSpec((1, D_MODEL, D_EXPERT), lambda v, t, e, *_: (e[v], 0, 0)),
            pl.BlockSpec((1, D_EXPERT, D_MODEL), lambda v, t, e, *_: (e[v], 0, 0)),
        ],
        out_specs=pl.BlockSpec((FFN_BLK * ROW_SLABS, LANES), lambda v, t, e, *_: (t[v], 0)),
        scratch_shapes=[pltpu.VMEM((D_MODEL, D_EXPERT), jnp.bfloat16),
                        pltpu.VMEM((D_MODEL, D_EXPERT), jnp.bfloat16),
                        pltpu.VMEM((D_EXPERT, D_MODEL), jnp.bfloat16)],
    )
    return pl.pallas_call(
        _ffn_kernel,
        grid_spec=grid_spec,
        out_shape=jax.ShapeDtypeStruct(xs.shape, jnp.uint32),
        compiler_params=_cparams(("arbitrary",)),
        name="ffn",
    )(*seg, xs, w_gate, w_up, w_down)


def _segments(counts, n_rows):
    i32 = jnp.int32
    n_tiles = n_rows // FFN_BLK
    n_seg = n_tiles + N_EXPERTS
    tri = jnp.tril(jnp.ones((N_EXPERTS, N_EXPERTS), i32))
    ends = jnp.sum(tri * counts[None, :], axis=1)
    starts = ends - counts
    edges = jnp.arange(n_tiles, dtype=i32) * FFN_BLK
    rank_e = jnp.arange(n_tiles, dtype=i32) + jnp.sum(starts[None, :] <= edges[:, None], axis=1)
    rank_s = jnp.arange(N_EXPERTS, dtype=i32) + jnp.sum(edges[None, :] < starts[:, None], axis=1)
    seg = jnp.arange(n_seg, dtype=i32)
    lo = (jnp.sum(jnp.where(rank_e[None, :] == seg[:, None], edges[None, :], 0), axis=1)
          + jnp.sum(jnp.where(rank_s[None, :] == seg[:, None], starts[None, :], 0), axis=1))
    hi = jnp.concatenate([lo[1:], jnp.array([n_rows], i32)])
    valid = hi > lo
    tile = jnp.minimum(lo // FFN_BLK, n_tiles - 1)
    expert = jnp.minimum(jnp.sum(ends[None, :] <= lo[:, None], axis=1), N_EXPERTS - 1).astype(i32)
    upto = seg[None, :] <= seg[:, None]
    expert = jnp.max(jnp.where(upto & valid[None, :], expert[None, :], 0), axis=1)
    prev_expert = jnp.concatenate([jnp.array([-1], i32), expert[:-1]])
    first_valid = valid & (jnp.sum(jnp.where(upto & valid[None, :], 1, 0), axis=1) == 1)
    cast = valid & ((expert != prev_expert) | first_valid)
    prev_tile = jnp.concatenate([jnp.array([-1], i32), tile[:-1]])
    init = tile != prev_tile
    return (tile.astype(i32), expert, lo.astype(i32), hi.astype(i32), cast.astype(i32), init.astype(i32))


def _combine_kernel(dest_ref, route_ref, x1_ref, ys_ref, out_ref, y0_sc, y1_sc, sems):
    def row_copy(r, k):
        d = dest_ref[0, 0, 2 * r + k]
        dst = y0_sc if k == 0 else y1_sc
        return pltpu.make_async_copy(ys_ref.at[d], dst.at[pl.ds(pl.multiple_of(r * ROW_SLABS, ROW_SLABS), ROW_SLABS), :],
                                     sems.at[k])

    def issue(r, c):
        row_copy(r, 0).start(priority=0)
        row_copy(r, 1).start(priority=1)
        return c

    lax.fori_loop(0, TM_COMBINE, issue, 0, unroll=ROW_UNROLL)
    pltpu.make_async_copy(y0_sc, y0_sc, sems.at[0]).wait()
    pltpu.make_async_copy(y1_sc, y1_sc, sems.at[1]).wait()
    route = route_ref[...]
    w0 = route[:, 2:3]
    w1 = route[:, 3:4]
    half = D_MODEL // 2
    for c in range(ROW_SLABS):
        hi0, lo0 = _unpack_bf16_pairs(y0_sc[pl.ds(c, TM_COMBINE, stride=ROW_SLABS), :])
        hi1, lo1 = _unpack_bf16_pairs(y1_sc[pl.ds(c, TM_COMBINE, stride=ROW_SLABS), :])
        ch = slice(c * LANES, (c + 1) * LANES)
        cl = slice(half + c * LANES, half + (c + 1) * LANES)
        out_ref[:, ch] = x1_ref[:, ch] + (w0 * hi0 + w1 * hi1)
        out_ref[:, cl] = x1_ref[:, cl] + (w0 * lo0 + w1 * lo1)


def _combine(dest3, route, x1, ys):
    t = x1.shape[0]
    n = t // TM_COMBINE
    return pl.pallas_call(
        _combine_kernel,
        grid=(n,),
        in_specs=[pl.BlockSpec((1, 1, 2 * TM_COMBINE), lambda i: (i, 0, 0), memory_space=pltpu.SMEM),
                  pl.BlockSpec((TM_COMBINE, LANES), lambda i: (i, 0)),
                  pl.BlockSpec((TM_COMBINE, D_MODEL), lambda i: (i, 0)),
                  pl.BlockSpec(memory_space=pl.ANY)],
        out_specs=pl.BlockSpec((TM_COMBINE, D_MODEL), lambda i: (i, 0)),
        out_shape=jax.ShapeDtypeStruct((t, D_MODEL), jnp.float32),
        scratch_shapes=[pltpu.VMEM((TM_COMBINE * ROW_SLABS, LANES), jnp.uint32),
                        pltpu.VMEM((TM_COMBINE * ROW_SLABS, LANES), jnp.uint32),
                        pltpu.SemaphoreType.DMA((2,))],
        compiler_params=_cparams(("arbitrary",)),
        name="combine",
    )(dest3, route, x1, ys)


def _t5_bucket(rel):
    nb = T5_BUCKETS // 2
    max_exact = nb // 2
    side = jnp.where(rel > 0, nb, 0)
    n = jnp.abs(rel)
    nf = jnp.maximum(n, 1).astype(jnp.float32)
    large = max_exact + (jnp.log(nf / max_exact) / math.log(T5_MAX_DIST / max_exact)
                         * (nb - max_exact)).astype(jnp.int32)
    large = jnp.minimum(large, nb - 1)
    return side + jnp.where(n < max_exact, n, large)


def _rel_offsets(j, transposed=False):
    i = jnp.arange(TQ + TK)
    if transposed:
        return jnp.where(i < TQ, -i, (TQ + TK) - i) - j * TK
    return jnp.where(i < TK, i, i - (TQ + TK)) - j * TK


def _diff_bias_vecs(t5_table, transposed=False):
    vecs = jnp.stack([t5_table[_t5_bucket(_rel_offsets(j, transposed))].astype(jnp.float32).T for j in range(2)],
                     axis=1)
    vecs = jnp.stack([vecs, vecs], axis=2)
    far = t5_table[_t5_bucket(jnp.array(-(TK + 1)))].astype(jnp.float32)
    return vecs, far


def _band_bias_vecs(rel_table):
    i = jnp.arange(BAND_HQ + BAND_W)
    rel = jnp.where(i < BAND_W, i, i - (BAND_HQ + BAND_W)) - (BAND_W - BAND_HQ)
    vecs = rel_table[jnp.clip(rel, -B_MAX_REL, B_MAX_REL) + B_MAX_REL].astype(jnp.float32).T
    return vecs.reshape(B_HEADS // 2, 2, BAND_HQ + BAND_W)


def kernel(x, norm1_g, w_in, a_qnorm_g, a_knorm_g, a_lambda, a_subln_g, t5_table, b_qnorm_g, b_knorm_g,
           b_rel_table, w_branch_a, w_branch_b, w_out, norm2_g, w_router_group, b_router_group,
           w_router_expert, b_router_expert, w_gate, w_up, w_down):
    bsz, s_len, _ = x.shape
    n_tok = bsz * s_len
    f32, bf16 = jnp.float32, jnp.bfloat16
    assert s_len % TQ == 0 and TQ == TK and TQ % CHUNK == 0 and n_tok % TM_PROJ == 0
    assert TK >= T5_MAX_DIST and BAND_HQ % CHUNK == 0 and BAND_W % LANES == 0
    assert s_len % BAND_TQ == 0 and BAND_W - BAND_HQ <= BAND_TQ
    l = 0
    x2 = x.reshape(n_tok, D_MODEL)

    w = w_in[l]
    qk = w[:, :1024].astype(bf16).reshape(D_MODEL, 2, 2, A_HEADS, A_HEAD_DIM)
    w_qk = qk.transpose(0, 1, 3, 2, 4).reshape(D_MODEL, 1024)
    w_rest = w[:, 1024:].astype(bf16)
    gn = jnp.stack([jnp.tile(a_qnorm_g[l] * (A_HEAD_DIM ** -0.5 * LOG2E), 8), jnp.tile(a_knorm_g[l], 8),
                    jnp.tile(b_qnorm_g[l] * (B_HEAD_DIM ** -0.5 * LOG2E), 8), jnp.tile(b_knorm_g[l], 8)]).astype(f32)
    gmat = jnp.asarray(np.kron(np.eye(4), np.ones((A_HEAD_DIM, A_HEAD_DIM))), dtype=bf16)

    qa, ka, va, qb, kb, vb, ga, gb = _proj(x2, norm1_g[l][None].astype(f32), w_qk, w_rest, gn, gmat)

    lam_init = 0.8 - 0.6 * math.exp(-0.3 * l)
    lp = a_lambda[l].astype(f32)
    lam = jnp.exp(jnp.sum(lp[0] * lp[1])) - jnp.exp(jnp.sum(lp[2] * lp[3])) + lam_init
    bias_a, far_a = _diff_bias_vecs(t5_table)
    bmax_a = jnp.maximum(jnp.max(bias_a, axis=(1, 2, 3)), far_a)
    scal_a = (jnp.concatenate([lam[None], far_a, bmax_a]) * jnp.array([1.0] + [LOG2E] * (2 * A_HEADS))).astype(f32)
    gsub = (a_subln_g[l] * (1.0 - lam_init))[None].astype(f32)
    bias_at, _ = _diff_bias_vecs(t5_table, transposed=True)
    oa = _diff_attention(scal_a, qa, ka, va, bias_a * LOG2E, bias_at * LOG2E, gsub, bsz=bsz, s_len=s_len)

    bias_b = _band_bias_vecs(b_rel_table[l])
    ob = _band_attention(qb, kb, vb, bias_b * LOG2E, bsz=bsz, s_len=s_len)

    wr = jnp.zeros((D_MODEL, LANES), f32)
    wr = wr.at[:, :N_GROUPS].set(w_router_group[l]).at[:, N_GROUPS:N_GROUPS + N_EXPERTS].set(w_router_expert[l])
    wrh = wr.astype(bf16)
    wr2 = jnp.concatenate([wrh, (wr - wrh.astype(f32)).astype(bf16)], axis=1)
    br = jnp.zeros((1, LANES), f32)
    br = br.at[0, :N_GROUPS].set(b_router_group[l]).at[0, N_GROUPS:N_GROUPS + N_EXPERTS].set(b_router_expert[l])
    x1, hn, route, cnt = _post(oa, ob, ga, gb, x2, w_branch_a[l].astype(bf16), w_branch_b[l].astype(bf16),
                               w_out[l].astype(bf16), norm2_g[l][None].astype(f32), wr2, br)

    ltri = jnp.asarray(np.tril(np.ones((TM_RANK, TM_RANK)), -1), dtype=bf16)
    utri = jnp.asarray(np.triu(np.ones((LANES, LANES)), 1), dtype=bf16)
    dest = _rank(route, cnt, ltri, utri)

    counts = cnt[0, :N_EXPERTS].astype(jnp.int32)
    n_rows = n_tok * TOP_K
    seg = _segments(counts, n_rows)

    dest2 = dest[:, :TOP_K].astype(jnp.int32)
    xs = _dispatch(dest2.reshape(n_tok // TM_DISPATCH, 1, TOP_K * TM_DISPATCH), hn, n_rows)
    ys = _ffn(seg, xs.reshape(n_rows * ROW_SLABS, LANES), w_gate[l], w_up[l], w_down[l])
    out = _combine(dest2.reshape(n_tok // TM_COMBINE, 1, TOP_K * TM_COMBINE), route, x1,
                   ys.reshape(n_rows, ROW_SLABS, LANES))
    return out.reshape(bsz, s_len, D_MODEL)
```

```python
import functools
import math

import jax
import jax.numpy as jnp
import numpy as np
from jax import lax
from jax.experimental import pallas as pl
from jax.experimental.pallas import tpu as pltpu

D_MODEL = 1024
CHUNK = 64
A_HEADS = 4
A_HEAD_DIM = 64
A_VDIM = 2 * A_HEAD_DIM
B_HEADS = 8
B_HEAD_DIM = 64
B_LEFT_CHUNKS = 8
B_MAX_REL = 128
T5_BUCKETS = 32
T5_MAX_DIST = 128
N_GROUPS = 4
EXPERTS_PER_GROUP = 8
N_EXPERTS = N_GROUPS * EXPERTS_PER_GROUP
TOP_K = 2
D_EXPERT = 512
EPS = 1e-6
NEG = -1e30
LOG2E = 1.0 / math.log(2.0)
DENOM_FLOOR = 2.0 ** -100

CHUNK_SHIFT = CHUNK.bit_length() - 1
assert 1 << CHUNK_SHIFT == CHUNK
LANES = 128
A_W = A_HEADS * 2 * A_HEAD_DIM
B_W = B_HEADS * B_HEAD_DIM
PROJ_W = 4 * 256 + 4 * 512 + 2 * D_MODEL

TM_PROJ = 512
TQ = 512
TK = 512
FAR_UNROLL = 4
TM_POST = 1024
TM_RANK = 1024
TM_DISPATCH = 4096
TM_COMBINE = 2048
ROW_UNROLL = 8
FFN_BLK = 1024
FFN_SUB = 256
ROW_SLABS = D_MODEL // 2 // LANES
VMEM_LIMIT = 56 * 1024 * 1024


def _cparams(sem):
    return pltpu.CompilerParams(dimension_semantics=sem, vmem_limit_bytes=VMEM_LIMIT)


def _pack_bf16_pairs(x):
    bits = lax.bitcast_convert_type(x.astype(jnp.bfloat16).astype(jnp.float32), jnp.uint32)
    half = x.shape[1] // 2
    return (bits[:, :half] & jnp.uint32(0xFFFF0000)) | lax.shift_right_logical(bits[:, half:], jnp.uint32(16))


def _unpack_bf16_pairs(pk):
    return (lax.bitcast_convert_type(pk & jnp.uint32(0xFFFF0000), jnp.float32),
            lax.bitcast_convert_type(lax.shift_left(pk, jnp.uint32(16)), jnp.float32))


def _proj_kernel(x_ref, g1_ref, wqk_ref, w_ref, gn_ref, gmat_ref,
                 qa_ref, ka_ref, va_ref, qb_ref, kb_ref, vb_ref, ga_ref, gb_ref):
    x = x_ref[...]
    xn = x * lax.rsqrt(jnp.mean(x * x, axis=-1, keepdims=True) + EPS) * g1_ref[...]
    xn = xn.astype(jnp.bfloat16)
    n_qk = wqk_ref.shape[1]

    def slab(c0, width):
        w = wqk_ref[:, c0:c0 + width] if c0 < n_qk else w_ref[:, c0 - n_qk:c0 - n_qk + width]
        return jnp.dot(xn, w, preferred_element_type=jnp.float32)

    def headnorm(y, gi):
        sq = (y * y).astype(jnp.bfloat16)
        half = gmat_ref.shape[0]
        ss = jnp.concatenate([jnp.dot(sq[:, c:c + half], gmat_ref[...], preferred_element_type=jnp.float32)
                              for c in range(0, y.shape[1], half)], axis=1)
        return y * lax.rsqrt(ss * (1.0 / A_HEAD_DIM) + EPS) * gn_ref[gi:gi + 1, :]

    qa_ref[...] = headnorm(slab(0, 512), 0).astype(jnp.bfloat16)
    ka_ref[...] = headnorm(slab(512, 512), 1).astype(jnp.bfloat16)
    va_ref[...] = slab(1024, 512).astype(jnp.bfloat16)
    qb_ref[...] = headnorm(slab(1536, 512), 2).astype(jnp.bfloat16)
    kb_ref[...] = headnorm(slab(2048, 512), 3).astype(jnp.bfloat16)
    vb_ref[...] = slab(2560, 512).astype(jnp.bfloat16)
    for j in range(2):
        ga_ref[:, j * 512:(j + 1) * 512] = jax.nn.sigmoid(slab(3072 + j * 512, 512)).astype(jnp.bfloat16)
        gb_ref[:, j * 512:(j + 1) * 512] = jax.nn.sigmoid(slab(4096 + j * 512, 512)).astype(jnp.bfloat16)


def _proj(x2, g1, w_qk, w_rest, gn, gmat):
    t = x2.shape[0]
    n = t // TM_PROJ
    row = lambda w: pl.BlockSpec((TM_PROJ, w), lambda i: (i, 0))
    full = lambda a: pl.BlockSpec(a.shape, lambda i: (0,) * a.ndim)
    outs = [jax.ShapeDtypeStruct((t, 512), jnp.bfloat16)] * 6 + [jax.ShapeDtypeStruct((t, D_MODEL), jnp.bfloat16)] * 2
    return pl.pallas_call(
        _proj_kernel,
        grid=(n,),
        in_specs=[row(D_MODEL), full(g1), full(w_qk), full(w_rest), full(gn), full(gmat)],
        out_specs=[row(512)] * 6 + [row(D_MODEL)] * 2,
        out_shape=outs,
        compiler_params=_cparams(("arbitrary",)),
        name="proj",
    )(x2, g1, w_qk, w_rest, gn, gmat)


def _diff_kernel(scal_ref, q_ref, k_ref, v_ref, vec_ref, vect_ref, gsub_ref, o_ref,
                 bias_sc, biast_sc, vt_sc, kmaxt_sc, qz_sc, qzt_sc, refn_sc, reff_sc, l_sc, acc_sc,
                 m_sc, acc2_sc, mf_sc, accf_sc, *, n_near):
    f32 = jnp.float32
    h = pl.program_id(1)
    qi = pl.program_id(2)
    nc = TK // LANES
    r2 = 2 * TQ
    n_heads = pl.num_programs(1)

    def near_span(j):
        return slice((n_near - 1 - j) * TK, (n_near - j) * TK)

    @pl.when(qi == 0)
    def _():
        row = lax.broadcasted_iota(jnp.int32, (TK, TQ), 0)
        col = lax.broadcasted_iota(jnp.int32, (TK, TQ), 1)
        for j in range(n_near):
            ok = lax.shift_right_arithmetic(col, CHUNK_SHIFT) >= lax.shift_right_arithmetic(row - j * TK, CHUNK_SHIFT)
            vect = jnp.broadcast_to(vect_ref[0, j:j + 1, :], (TK, TQ + TK))
            tilet = jnp.where(ok, pltpu.roll(vect, 0, 1, stride=1, stride_axis=0)[:, :TQ], NEG)
            for half in range(2):
                biast_sc[near_span(j), half * TQ:(half + 1) * TQ] = tilet
        vt_sc[...] = v_ref[...].T
        kf = k_ref[...].astype(f32)
        ksq = kf * kf
        klane = lax.broadcasted_iota(jnp.int32, ksq.shape, 1)
        for half in range(2):
            sel = (klane < 64) if half == 0 else (klane >= 64)
            norm2 = jnp.max(jnp.sum(jnp.where(sel, ksq, 0.0), axis=-1, keepdims=True), axis=0, keepdims=True)
            kmaxt_sc[:, half * TQ:(half + 1) * TQ] = jnp.broadcast_to(jnp.sqrt(norm2), (8, TQ))

    q = q_ref[...]
    lane = lax.broadcasted_iota(jnp.int32, q.shape, 1)
    zero = jnp.zeros_like(q)
    q1 = jnp.where(lane < 64, q, zero)
    q2 = jnp.where(lane >= 64, q, zero)
    qzt_sc[:, 0:TQ] = q1.T
    qzt_sc[:, TQ:r2] = q2.T
    cfar = scal_ref[1 + h]
    bias_max = scal_ref[1 + n_heads + h]
    n_far = jnp.maximum(qi - (n_near - 1), 0)

    def key_block(kblk):
        return k_ref[pl.ds(pl.multiple_of(kblk * TK, TK), TK), :]

    def finish(od):
        od = od * lax.rsqrt(jnp.mean(od * od, axis=-1, keepdims=True) + EPS) * gsub_ref[...]
        o_ref[...] = od.astype(o_ref.dtype)

    def fixed_reference_pass():
        qf = qzt_sc[...].astype(f32)
        qnorm = jnp.sqrt(jnp.sum(qf * qf, axis=0, keepdims=True))
        refn_sc[...] = qnorm * kmaxt_sc[...] + bias_max
        l_sc[...] = jnp.zeros(l_sc.shape, f32)
        acc_sc[...] = jnp.zeros(acc_sc.shape, f32)

        def accumulate(st, kblk, ref_ref):
            p = jnp.exp2(st.reshape(TK // 8, 8, r2) - ref_ref[...][None])
            l_sc[...] += jnp.sum(p, axis=0)
            vt = vt_sc[:, pl.ds(pl.multiple_of(kblk * TK, TK), TK)]
            acc_sc[...] += jnp.dot(vt, p.reshape(TK, r2).astype(jnp.bfloat16), preferred_element_type=f32)

        def scores_t(kblk):
            return jnp.dot(key_block(kblk), qzt_sc[...], preferred_element_type=f32)

        for j in range(n_near):
            def near(j=j):
                accumulate(scores_t(qi - j) + biast_sc[near_span(j), :], qi - j, refn_sc)
            if j == 0:
                near()
            else:
                pl.when(qi >= j)(near)

        reff_sc[...] = refn_sc[...] - cfar

        def far(kblk, carry):
            accumulate(scores_t(kblk), kblk, reff_sc)
            return carry

        def far_group(i, carry):
            for u in range(FAR_UNROLL):
                far(FAR_UNROLL * i + u, carry)
            return carry

        lax.fori_loop(0, n_far // FAR_UNROLL, far_group, 0)
        lax.fori_loop((n_far // FAR_UNROLL) * FAR_UNROLL, n_far, far, 0)

        denom = jnp.sum(l_sc[...], axis=0, keepdims=True)
        ot = acc_sc[...] / denom
        finish((ot[:, 0:TQ] - scal_ref[0] * ot[:, TQ:r2]).T)
        return denom

    def exact_pass():
        qz_sc[0:TQ, :] = q1
        qz_sc[TQ:r2, :] = q2
        row = lax.broadcasted_iota(jnp.int32, (TQ, TK), 0)
        col = lax.broadcasted_iota(jnp.int32, (TQ, TK), 1)
        for j in range(n_near):
            ok = lax.shift_right_arithmetic(row, CHUNK_SHIFT) >= lax.shift_right_arithmetic(col - j * TK, CHUNK_SHIFT)
            vec = jnp.broadcast_to(vec_ref[0, j:j + 1, :], (TQ, TQ + TK))
            tile = jnp.where(ok, pltpu.roll(vec, 0, 1, stride=1, stride_axis=0)[:, :TK], NEG)
            for half in range(2):
                bias_sc[half * TQ:(half + 1) * TQ, near_span(j)] = tile

        def scores(kblk):
            return lax.dot_general(qz_sc[...], key_block(kblk), (((1,), (1,)), ((), ())), preferred_element_type=f32)

        def update(s, kblk, m_ref, a_ref):
            cols = [s[:, c * LANES:(c + 1) * LANES] for c in range(nc)]
            m_old = m_ref[...]
            m_new = jnp.maximum(m_old, jnp.max(functools.reduce(jnp.maximum, cols), axis=-1, keepdims=True))
            alpha = jnp.exp2(m_old - m_new)
            ps = [jnp.exp2(c - m_new) for c in cols]
            p = jnp.concatenate([x.astype(jnp.bfloat16) for x in ps], axis=1)
            v = v_ref[pl.ds(pl.multiple_of(kblk * TK, TK), TK), :]
            a_ref[:, 0:LANES] = alpha * a_ref[:, 0:LANES] + jnp.dot(p, v, preferred_element_type=f32)
            a_ref[:, LANES:2 * LANES] = alpha * a_ref[:, LANES:2 * LANES] + functools.reduce(jnp.add, ps)
            m_ref[...] = m_new

        m_sc[...] = jnp.full(m_sc.shape, NEG, f32)
        acc2_sc[...] = jnp.zeros(acc2_sc.shape, f32)
        for j in range(n_near):
            def near(j=j):
                update(scores(qi - j) + bias_sc[:, near_span(j)], qi - j, m_sc, acc2_sc)
            if j == 0:
                near()
            else:
                pl.when(qi >= j)(near)

        mf_sc[...] = jnp.full(mf_sc.shape, NEG, f32)
        accf_sc[...] = jnp.zeros(accf_sc.shape, f32)

        def far(kblk, carry):
            update(scores(kblk), kblk, mf_sc, accf_sc)
            return carry

        lax.fori_loop(0, n_far, far, 0)
        mf = mf_sc[...] + cfar
        mn = m_sc[...]
        m = jnp.maximum(mf, mn)
        wf = jnp.exp2(mf - m)
        wn = jnp.exp2(mn - m)
        tot = (jnp.concatenate([wf, wf], axis=1) * accf_sc[...]
               + jnp.concatenate([wn, wn], axis=1) * acc2_sc[...])
        o = tot[:, 0:LANES] / jnp.sum(tot[:, LANES:2 * LANES], axis=-1, keepdims=True)
        finish(o[:TQ] - scal_ref[0] * o[TQ:])

    denom = fixed_reference_pass()
    pl.when(jnp.logical_not(jnp.min(denom) >= DENOM_FLOOR))(exact_pass)


def _diff_attention(scal, q, k, v, vecs, vecs_t, gsub, *, bsz, s_len):
    n_blk = q.shape[1] // LANES
    nq = s_len // TQ
    n_near = vecs.shape[1]
    assert vecs.shape == vecs_t.shape == (n_blk, n_near, TQ + TK)
    kern = functools.partial(_diff_kernel, n_near=n_near)
    f32, bf16 = jnp.float32, jnp.bfloat16
    vm = pltpu.VMEM
    vec_spec = pl.BlockSpec((1, n_near, TQ + TK), lambda b, h, i: (h, 0, 0))
    return pl.pallas_call(
        kern,
        grid=(bsz, n_blk, nq),
        in_specs=[
            pl.BlockSpec(memory_space=pltpu.SMEM),
            pl.BlockSpec((TQ, LANES), lambda b, h, i: (b * nq + i, h)),
            pl.BlockSpec((s_len, LANES), lambda b, h, i: (b, h)),
            pl.BlockSpec((s_len, LANES), lambda b, h, i: (b, h)),
            vec_spec, vec_spec,
            pl.BlockSpec((1, LANES), lambda b, h, i: (0, 0)),
        ],
        out_specs=pl.BlockSpec((TQ, LANES), lambda b, h, i: (b * nq + i, h)),
        out_shape=jax.ShapeDtypeStruct(q.shape, bf16),
        scratch_shapes=[vm((2 * TQ, n_near * TK), f32), vm((n_near * TK, 2 * TQ), f32),
                        vm((LANES, s_len), bf16), vm((8, 2 * TQ), f32),
                        vm((2 * TQ, LANES), bf16), vm((LANES, 2 * TQ), bf16),
                        vm((8, 2 * TQ), f32), vm((8, 2 * TQ), f32), vm((8, 2 * TQ), f32),
                        vm((LANES, 2 * TQ), f32),
                        vm((2 * TQ, LANES), f32), vm((2 * TQ, 2 * LANES), f32),
                        vm((2 * TQ, LANES), f32), vm((2 * TQ, 2 * LANES), f32)],
        compiler_params=_cparams(("arbitrary", "arbitrary", "arbitrary")),
        name="attn_diff",
    )(scal, q, k, v, vecs, vecs_t, gsub)


BAND_TQ = 1024
BAND_HQ = 256
BAND_SUBS = BAND_TQ // BAND_HQ
BAND_W = B_LEFT_CHUNKS * CHUNK + BAND_HQ


def _band_kernel(q_ref, k_ref, v_ref, vec_ref, o_ref, bias_sc, qz_sc):
    f32 = jnp.float32
    qi = pl.program_id(2)
    hq, w = BAND_HQ, BAND_W
    back = w - hq

    @pl.when(qi == 0)
    def _():
        qchunk = lax.shift_right_arithmetic(lax.broadcasted_iota(jnp.int32, (hq, w), 0), CHUNK_SHIFT)
        kchunk = lax.shift_right_arithmetic(lax.broadcasted_iota(jnp.int32, (hq, w), 1) - back, CHUNK_SHIFT)
        dchunk = qchunk - kchunk
        allowed = (dchunk >= 0) & (dchunk <= B_LEFT_CHUNKS)
        for half in range(2):
            vec = jnp.broadcast_to(vec_ref[0, half:half + 1, :], (hq, hq + w))
            tile = pltpu.roll(vec, 0, 1, stride=1, stride_axis=0)[:, :w]
            bias_sc[half * hq:(half + 1) * hq, :] = jnp.where(allowed, tile, NEG)

    q = q_ref[...]
    lane = lax.broadcasted_iota(jnp.int32, (hq, LANES), 1)
    zero = jnp.zeros((hq, LANES), q.dtype)
    for u in range(BAND_SUBS):
        qu = q[u * hq:(u + 1) * hq, :]
        qz_sc[(2 * u) * hq:(2 * u + 1) * hq, :] = jnp.where(lane < 64, qu, zero)
        qz_sc[(2 * u + 1) * hq:(2 * u + 2) * hq, :] = jnp.where(lane >= 64, qu, zero)

    def scores(u, k0, n_keys):
        k = k_ref[pl.ds(k0, n_keys), :]
        return lax.dot_general(qz_sc[2 * u * hq:(2 * u + 2) * hq, :], k, (((1,), (1,)), ((), ())),
                               preferred_element_type=f32) + bias_sc[:, w - n_keys:w]

    def softmax(s):
        cols = [s[:, c * LANES:(c + 1) * LANES] for c in range(s.shape[1] // LANES)]
        m = jnp.max(functools.reduce(jnp.maximum, cols), axis=-1, keepdims=True)
        ps = [jnp.exp2(c - m) for c in cols]
        denom = jnp.sum(functools.reduce(jnp.add, ps), axis=-1, keepdims=True)
        return jnp.concatenate([x.astype(jnp.bfloat16) for x in ps], axis=1), denom

    def finish(u, p, denom, k0):
        o = jnp.dot(p, v_ref[pl.ds(k0, p.shape[1]), :], preferred_element_type=f32) / denom
        o_ref[u * hq:(u + 1) * hq, :] = jnp.where(lane < 64, o[:hq], o[hq:]).astype(o_ref.dtype)

    def step(k0s, n_keys):
        ss = [scores(u, k0s[u], n_keys[u]) for u in range(BAND_SUBS)]
        pd = [softmax(x) for x in ss]
        for u in range(BAND_SUBS):
            finish(u, pd[u][0], pd[u][1], k0s[u])

    first = tuple(min((u + 1) * hq, w) for u in range(BAND_SUBS))

    @pl.when(qi == 0)
    def _():
        step(tuple(max((u + 1) * hq - w, 0) for u in range(BAND_SUBS)), first)

    @pl.when(qi > 0)
    def _():
        start = pl.multiple_of(qi * BAND_TQ - back, hq)
        step(tuple(pl.multiple_of(start + u * hq, hq) for u in range(BAND_SUBS)), (w,) * BAND_SUBS)


def _band_attention(q, k, v, vecs, *, bsz, s_len):
    n_blk = q.shape[1] // LANES
    nq = s_len // BAND_TQ
    assert vecs.shape == (n_blk, 2, BAND_HQ + BAND_W)
    return pl.pallas_call(
        _band_kernel,
        grid=(bsz, n_blk, nq),
        in_specs=[
            pl.BlockSpec((BAND_TQ, LANES), lambda b, h, i: (b * nq + i, h)),
            pl.BlockSpec((s_len, LANES), lambda b, h, i: (b, h)),
            pl.BlockSpec((s_len, LANES), lambda b, h, i: (b, h)),
            pl.BlockSpec((1, 2, BAND_HQ + BAND_W), lambda b, h, i: (h, 0, 0)),
        ],
        out_specs=pl.BlockSpec((BAND_TQ, LANES), lambda b, h, i: (b * nq + i, h)),
        out_shape=jax.ShapeDtypeStruct(q.shape, jnp.bfloat16),
        scratch_shapes=[pltpu.VMEM((2 * BAND_HQ, BAND_W), jnp.float32),
                        pltpu.VMEM((2 * BAND_TQ, LANES), jnp.bfloat16)],
        compiler_params=_cparams(("arbitrary", "arbitrary", "arbitrary")),
        name="attn_band",
    )(q, k, v, vecs)


def _post_kernel(oa_ref, ob_ref, ga_ref, gb_ref, x_ref, wa_ref, wb_ref, wo_ref, g2_ref,
                 wr2_ref, br_ref, x1_ref, hn_ref, route_ref, cnt_ref):
    f32 = jnp.float32
    ya = jnp.dot(oa_ref[...], wa_ref[...], preferred_element_type=f32)
    yb = jnp.dot(ob_ref[...], wb_ref[...], preferred_element_type=f32)
    mixed = ga_ref[...].astype(f32) * ya + gb_ref[...].astype(f32) * yb
    x1 = x_ref[...] + jnp.dot(mixed.astype(jnp.bfloat16), wo_ref[...], preferred_element_type=f32)
    x1_ref[...] = x1
    hn = x1 * lax.rsqrt(jnp.mean(x1 * x1, axis=-1, keepdims=True) + EPS) * g2_ref[...]
    hh = hn.astype(jnp.bfloat16)
    packed = _pack_bf16_pairs(hn)
    for c in range(ROW_SLABS):
        hn_ref[pl.ds(c, TM_POST, stride=ROW_SLABS), :] = packed[:, c * LANES:(c + 1) * LANES]

    hl = (hn - hh.astype(f32)).astype(jnp.bfloat16)
    hw = jnp.dot(hh, wr2_ref[...], preferred_element_type=f32)
    lg = (hw[:, 0:LANES] + hw[:, LANES:2 * LANES]
          + jnp.dot(hl, wr2_ref[:, 0:LANES], preferred_element_type=f32)) + br_ref[...]

    lanei = lax.broadcasted_iota(jnp.int32, lg.shape, 1)
    lanef = lanei.astype(f32)
    big = 999.0
    gmask = lanei < N_GROUPS
    gl = jnp.where(gmask, lg, NEG)
    gm = jnp.max(gl, axis=-1, keepdims=True)
    ge = jnp.where(gmask, jnp.exp(gl - gm), 0.0)
    gp = ge / jnp.sum(ge, axis=-1, keepdims=True)
    p_g = jnp.max(gp, axis=-1, keepdims=True)
    gidx = jnp.min(jnp.where(gmask & (gp == p_g), lanef, big), axis=-1, keepdims=True)
    egrp = lax.shift_right_arithmetic(lanei - N_GROUPS, 3).astype(f32)
    emask = (lanei >= N_GROUPS) & (lanei < N_GROUPS + N_EXPERTS) & (egrp == gidx)
    el = jnp.where(emask, lg, NEG)
    v1 = jnp.max(el, axis=-1, keepdims=True)
    i1 = jnp.min(jnp.where(emask & (el == v1), lanef, big), axis=-1, keepdims=True)
    emask2 = emask & (lanef != i1)
    el2 = jnp.where(emask2, lg, NEG)
    v2 = jnp.max(el2, axis=-1, keepdims=True)
    i2 = jnp.min(jnp.where(emask2 & (el2 == v2), lanef, big), axis=-1, keepdims=True)
    t = jnp.exp(v2 - v1)
    den = 1.0 + t
    w1 = p_g * (1.0 / den)
    w2 = p_g * (t / den)
    route = jnp.where(lanei == 0, i1 - N_GROUPS,
                      jnp.where(lanei == 1, i2 - N_GROUPS,
                                jnp.where(lanei == 2, w1, jnp.where(lanei == 3, w2, 0.0))))
    route_ref[...] = route

    @pl.when(pl.program_id(0) == 0)
    def _():
        cnt_ref[...] = jnp.zeros(cnt_ref.shape, f32)

    chosen = ((lanef == i1 - N_GROUPS) | (lanef == i2 - N_GROUPS)).astype(f32)
    cnt_ref[...] += jnp.broadcast_to(jnp.sum(chosen, axis=0, keepdims=True), cnt_ref.shape)


def _post(oa, ob, ga, gb, x2, wa, wb, wo, g2, wr2, br):
    t = x2.shape[0]
    n = t // TM_POST
    row = lambda w: pl.BlockSpec((TM_POST, w), lambda i: (i, 0))
    full = lambda a: pl.BlockSpec(a.shape, lambda i: (0,) * a.ndim)
    return pl.pallas_call(
        _post_kernel,
        grid=(n,),
        in_specs=[row(512), row(512), row(D_MODEL), row(D_MODEL), row(D_MODEL),
                  full(wa), full(wb), full(wo), full(g2), full(wr2), full(br)],
        out_specs=[row(D_MODEL), pl.BlockSpec((TM_POST * ROW_SLABS, LANES), lambda i: (i, 0)), row(LANES),
                   pl.BlockSpec((8, LANES), lambda i: (0, 0))],
        out_shape=[jax.ShapeDtypeStruct((t, D_MODEL), jnp.float32),
                   jax.ShapeDtypeStruct((t * ROW_SLABS, LANES), jnp.uint32),
                   jax.ShapeDtypeStruct((t, LANES), jnp.float32),
                   jax.ShapeDtypeStruct((8, LANES), jnp.float32)],
        compiler_params=_cparams(("arbitrary",)),
        name="post",
    )(oa, ob, ga, gb, x2, wa, wb, wo, g2, wr2, br)


def _rank_kernel(route_ref, cnt_ref, ltri_ref, utri_ref, dest_ref, pstart_sc, base_sc):
    f32 = jnp.float32
    i = pl.program_id(0)
    route = route_ref[...]
    lanef = lax.broadcasted_iota(jnp.int32, route.shape, 1).astype(f32)
    oh1 = (lanef == route[:, 0:1]).astype(f32)
    oh2 = (lanef == route[:, 1:2]).astype(f32)
    both = oh1 + oh2

    @pl.when(i == 0)
    def _():
        cnt = cnt_ref[0:1, :]
        chi = jnp.floor(cnt * (1.0 / 256.0))
        clo = cnt - chi * 256.0
        split = jnp.concatenate([jnp.broadcast_to(chi, (8, LANES)), jnp.broadcast_to(clo, (8, LANES))], axis=0)
        excl = jnp.dot(split.astype(jnp.bfloat16), utri_ref[...], preferred_element_type=f32)
        pstart_sc[...] = excl[0:1] * 256.0 + excl[8:9]
        base_sc[...] = jnp.zeros(base_sc.shape, f32)

    prior = jnp.dot(ltri_ref[...], both.astype(jnp.bfloat16), preferred_element_type=f32)
    slot = prior + base_sc[...] + pstart_sc[...]
    d1 = jnp.sum(oh1 * slot, axis=-1, keepdims=True)
    d2 = jnp.sum(oh2 * slot, axis=-1, keepdims=True)
    dest_ref[...] = jnp.where(lanef == 0.0, d1, jnp.where(lanef == 1.0, d2, 0.0))
    base_sc[...] += jnp.sum(both, axis=0, keepdims=True)


def _rank(route, cnt, ltri, utri):
    t = route.shape[0]
    n = t // TM_RANK
    full = lambda a: pl.BlockSpec(a.shape, lambda i: (0,) * a.ndim)
    row1 = lambda: pltpu.VMEM((1, LANES), jnp.float32)
    return pl.pallas_call(
        _rank_kernel,
        grid=(n,),
        in_specs=[pl.BlockSpec((TM_RANK, LANES), lambda i: (i, 0)), full(cnt), full(ltri), full(utri)],
        out_specs=pl.BlockSpec((TM_RANK, LANES), lambda i: (i, 0)),
        out_shape=jax.ShapeDtypeStruct((t, LANES), jnp.float32),
        scratch_shapes=[row1(), row1()],
        compiler_params=_cparams(("arbitrary",)),
        name="rank",
    )(route, cnt, ltri, utri)


def _dispatch_kernel(dest_ref, hn_ref, xs_ref, sems):
    def row_copy(r, k):
        d = dest_ref[0, 0, 2 * r + k]
        src = hn_ref.at[pl.ds(pl.multiple_of(r * ROW_SLABS, ROW_SLABS), ROW_SLABS), :]
        return pltpu.make_async_copy(src, xs_ref.at[d], sems.at[k])

    def issue(r, c):
        row_copy(r, 0).start(priority=0)
        row_copy(r, 1).start(priority=1)
        return c

    lax.fori_loop(0, TM_DISPATCH, issue, 0, unroll=ROW_UNROLL)
    for k in range(TOP_K):
        pltpu.make_async_copy(hn_ref, hn_ref, sems.at[k]).wait()


def _dispatch(dest3, hn, n_rows):
    t = hn.shape[0] // ROW_SLABS
    n = t // TM_DISPATCH
    return pl.pallas_call(
        _dispatch_kernel,
        grid=(n,),
        in_specs=[pl.BlockSpec((1, 1, 2 * TM_DISPATCH), lambda i: (i, 0, 0), memory_space=pltpu.SMEM),
                  pl.BlockSpec((TM_DISPATCH * ROW_SLABS, LANES), lambda i: (i, 0))],
        out_specs=pl.BlockSpec(memory_space=pl.ANY),
        out_shape=jax.ShapeDtypeStruct((n_rows, ROW_SLABS, LANES), hn.dtype),
        scratch_shapes=[pltpu.SemaphoreType.DMA((2,))],
        compiler_params=_cparams(("arbitrary",)),
        name="dispatch",
    )(dest3, hn)


def _ffn_kernel(tile_ref, exp_ref, lo_ref, hi_ref, cast_ref, init_ref,
                xs_ref, wg_ref, wu_ref, wd_ref, ys_ref, wg_sc, wu_sc, wd_sc):
    v = pl.program_id(0)
    lo = lo_ref[v]
    hi = hi_ref[v]

    @pl.when(init_ref[v] == 1)
    def _():
        ys_ref[...] = jnp.zeros(ys_ref.shape, ys_ref.dtype)

    @pl.when((hi > lo) & (cast_ref[v] == 1))
    def _():
        wg_sc[...] = wg_ref[0].astype(jnp.bfloat16)
        wu_sc[...] = wu_ref[0].astype(jnp.bfloat16)
        wd_sc[...] = wd_ref[0].astype(jnp.bfloat16)

    for piece in range(FFN_BLK // FFN_SUB):
        row0 = tile_ref[v] * FFN_BLK + piece * FFN_SUB

        @pl.when((hi > row0) & (lo < row0 + FFN_SUB))
        def _(piece=piece, row0=row0):
            def slab(ref, c):
                return ref.at[pl.ds(piece * FFN_SUB * ROW_SLABS + c, FFN_SUB, stride=ROW_SLABS), :]

            pk = jnp.concatenate([slab(xs_ref, c)[...] for c in range(ROW_SLABS)], axis=1)
            x = jnp.concatenate(_unpack_bf16_pairs(pk), axis=1).astype(jnp.bfloat16)
            g = jnp.dot(x, wg_sc[...], preferred_element_type=jnp.float32)
            u = jnp.dot(x, wu_sc[...], preferred_element_type=jnp.float32)
            hb = (g * jax.nn.sigmoid(g) * u).astype(jnp.bfloat16)
            packed = _pack_bf16_pairs(jnp.dot(hb, wd_sc[...], preferred_element_type=jnp.float32))
            rows = row0 + lax.broadcasted_iota(jnp.int32, (FFN_SUB, LANES), 0)
            mine = (rows >= lo) & (rows < hi)
            for c in range(ROW_SLABS):
                out = slab(ys_ref, c)
                out[...] = jnp.where(mine, packed[:, c * LANES:(c + 1) * LANES], out[...])


def _ffn(seg, xs, w_gate, w_up, w_down):
    n_seg = seg[0].shape[0]
    grid_spec = pltpu.PrefetchScalarGridSpec(
        num_scalar_prefetch=6,
        grid=(n_seg,),
        in_specs=[
            pl.BlockSpec((FFN_BLK * ROW_SLABS, LANES), lambda v, t, e, *_: (t[v], 0)),
            pl.BlockSpec((1, D_MODEL, D_EXPERT), lambda v, t, e, *_: (e[v], 0, 0)),
            pl.BlockSpec((1, D_MODEL, D_EXPERT), lambda v, t, e, *_: (e[v], 0, 0)),
            pl.BlockSpec((1, D_EXPERT, D_MODEL), lambda v, t, e, *_: (e[v], 0, 0)),
        ],
        out_specs=pl.BlockSpec((FFN_BLK * ROW_SLABS, LANES), lambda v, t, e, *_: (t[v], 0)),
        scratch_shapes=[pltpu.VMEM((D_MODEL, D_EXPERT), jnp.bfloat16),
                        pltpu.VMEM((D_MODEL, D_EXPERT), jnp.bfloat16),
                        pltpu.VMEM((D_EXPERT, D_MODEL), jnp.bfloat16)],
    )
    return pl.pallas_call(
        _ffn_kernel,
        grid_spec=grid_spec,
        out_shape=jax.ShapeDtypeStruct(xs.shape, jnp.uint32),
        compiler_params=_cparams(("arbitrary",)),
        name="ffn",
    )(*seg, xs, w_gate, w_up, w_down)


def _segments(counts, n_rows):
    i32 = jnp.int32
    n_tiles = n_rows // FFN_BLK
    n_seg = n_tiles + N_EXPERTS
    tri = jnp.tril(jnp.ones((N_EXPERTS, N_EXPERTS), i32))
    ends = jnp.sum(tri * counts[None, :], axis=1)
    starts = ends - counts
    edges = jnp.arange(n_tiles, dtype=i32) * FFN_BLK
    rank_e = jnp.arange(n_tiles, dtype=i32) + jnp.sum(starts[None, :] <= edges[:, None], axis=1)
    rank_s = jnp.arange(N_EXPERTS, dtype=i32) + jnp.sum(edges[None, :] < starts[:, None], axis=1)
    seg = jnp.arange(n_seg, dtype=i32)
    lo = (jnp.sum(jnp.where(rank_e[None, :] == seg[:, None], edges[None, :], 0), axis=1)
          + jnp.sum(jnp.where(rank_s[None, :] == seg[:, None], starts[None, :], 0), axis=1))
    hi = jnp.concatenate([lo[1:], jnp.array([n_rows], i32)])
    valid = hi > lo
    tile = jnp.minimum(lo // FFN_BLK, n_tiles - 1)
    expert = jnp.minimum(jnp.sum(ends[None, :] <= lo[:, None], axis=1), N_EXPERTS - 1).astype(i32)
    upto = seg[None, :] <= seg[:, None]
    expert = jnp.max(jnp.where(upto & valid[None, :], expert[None, :], 0), axis=1)
    prev_expert = jnp.concatenate([jnp.array([-1], i32), expert[:-1]])
    first_valid = valid & (jnp.sum(jnp.where(upto & valid[None, :], 1, 0), axis=1) == 1)
    cast = valid & ((expert != prev_expert) | first_valid)
    prev_tile = jnp.concatenate([jnp.array([-1], i32), tile[:-1]])
    init = tile != prev_tile
    return (tile.astype(i32), expert, lo.astype(i32), hi.astype(i32), cast.astype(i32), init.astype(i32))


def _combine_kernel(dest_ref, route_ref, x1_ref, ys_ref, out_ref, y0_sc, y1_sc, sems):
    def row_copy(r, k):
        d = dest_ref[0, 0, 2 * r + k]
        dst = y0_sc if k == 0 else y1_sc
        return pltpu.make_async_copy(ys_ref.at[d], dst.at[pl.ds(pl.multiple_of(r * ROW_SLABS, ROW_SLABS), ROW_SLABS), :],
                                     sems.at[k])

    def issue(r, c):
        row_copy(r, 0).start(priority=0)
        row_copy(r, 1).start(priority=1)
        return c

    lax.fori_loop(0, TM_COMBINE, issue, 0, unroll=ROW_UNROLL)
    pltpu.make_async_copy(y0_sc, y0_sc, sems.at[0]).wait()
    pltpu.make_async_copy(y1_sc, y1_sc, sems.at[1]).wait()
    route = route_ref[...]
    w0 = route[:, 2:3]
    w1 = route[:, 3:4]
    half = D_MODEL // 2
    for c in range(ROW_SLABS):
        hi0, lo0 = _unpack_bf16_pairs(y0_sc[pl.ds(c, TM_COMBINE, stride=ROW_SLABS), :])
        hi1, lo1 = _unpack_bf16_pairs(y1_sc[pl.ds(c, TM_COMBINE, stride=ROW_SLABS), :])
        ch = slice(c * LANES, (c + 1) * LANES)
        cl = slice(half + c * LANES, half + (c + 1) * LANES)
        out_ref[:, ch] = x1_ref[:, ch] + (w0 * hi0 + w1 * hi1)
        out_ref[:, cl] = x1_ref[:, cl] + (w0 * lo0 + w1 * lo1)


def _combine(dest3, route, x1, ys):
    t = x1.shape[0]
    n = t // TM_COMBINE
    return pl.pallas_call(
        _combine_kernel,
        grid=(n,),
        in_specs=[pl.BlockSpec((1, 1, 2 * TM_COMBINE), lambda i: (i, 0, 0), memory_space=pltpu.SMEM),
                  pl.BlockSpec((TM_COMBINE, LANES), lambda i: (i, 0)),
                  pl.BlockSpec((TM_COMBINE, D_MODEL), lambda i: (i, 0)),
                  pl.BlockSpec(memory_space=pl.ANY)],
        out_specs=pl.BlockSpec((TM_COMBINE, D_MODEL), lambda i: (i, 0)),
        out_shape=jax.ShapeDtypeStruct((t, D_MODEL), jnp.float32),
        scratch_shapes=[pltpu.VMEM((TM_COMBINE * ROW_SLABS, LANES), jnp.uint32),
                        pltpu.VMEM((TM_COMBINE * ROW_SLABS, LANES), jnp.uint32),
                        pltpu.SemaphoreType.DMA((2,))],
        compiler_params=_cparams(("arbitrary",)),
        name="combine",
    )(dest3, route, x1, ys)


def _t5_bucket(rel):
    nb = T5_BUCKETS // 2
    max_exact = nb // 2
    side = jnp.where(rel > 0, nb, 0)
    n = jnp.abs(rel)
    nf = jnp.maximum(n, 1).astype(jnp.float32)
    large = max_exact + (jnp.log(nf / max_exact) / math.log(T5_MAX_DIST / max_exact)
                         * (nb - max_exact)).astype(jnp.int32)
    large = jnp.minimum(large, nb - 1)
    return side + jnp.where(n < max_exact, n, large)


def _rel_offsets(j, transposed=False):
    i = jnp.arange(TQ + TK)
    if transposed:
        return jnp.where(i < TQ, -i, (TQ + TK) - i) - j * TK
    return jnp.where(i < TK, i, i - (TQ + TK)) - j * TK


def _diff_bias_vecs(t5_table, transposed=False):
    vecs = jnp.stack([t5_table[_t5_bucket(_rel_offsets(j, transposed))].astype(jnp.float32).T for j in range(2)],
                     axis=1)
    far = t5_table[_t5_bucket(jnp.array(-(TK + 1)))].astype(jnp.float32)
    return vecs, far


def _band_bias_vecs(rel_table):
    i = jnp.arange(BAND_HQ + BAND_W)
    rel = jnp.where(i < BAND_W, i, i - (BAND_HQ + BAND_W)) - (BAND_W - BAND_HQ)
    vecs = rel_table[jnp.clip(rel, -B_MAX_REL, B_MAX_REL) + B_MAX_REL].astype(jnp.float32).T
    return vecs.reshape(B_HEADS // 2, 2, BAND_HQ + BAND_W)


def kernel(x, norm1_g, w_in, a_qnorm_g, a_knorm_g, a_lambda, a_subln_g, t5_table, b_qnorm_g, b_knorm_g,
           b_rel_table, w_branch_a, w_branch_b, w_out, norm2_g, w_router_group, b_router_group,
           w_router_expert, b_router_expert, w_gate, w_up, w_down):
    bsz, s_len, _ = x.shape
    n_tok = bsz * s_len
    f32, bf16 = jnp.float32, jnp.bfloat16
    assert s_len % TQ == 0 and TQ == TK and TQ % CHUNK == 0 and n_tok % TM_PROJ == 0
    assert TK >= T5_MAX_DIST and BAND_HQ % CHUNK == 0 and BAND_W % LANES == 0
    assert s_len % BAND_TQ == 0 and BAND_W - BAND_HQ <= BAND_TQ
    l = 0
    x2 = x.reshape(n_tok, D_MODEL)

    w = w_in[l]
    qk = w[:, :1024].astype(bf16).reshape(D_MODEL, 2, 2, A_HEADS, A_HEAD_DIM)
    w_qk = qk.transpose(0, 1, 3, 2, 4).reshape(D_MODEL, 1024)
    w_rest = w[:, 1024:].astype(bf16)
    gn = jnp.stack([jnp.tile(a_qnorm_g[l] * (A_HEAD_DIM ** -0.5 * LOG2E), 8), jnp.tile(a_knorm_g[l], 8),
                    jnp.tile(b_qnorm_g[l] * (B_HEAD_DIM ** -0.5 * LOG2E), 8), jnp.tile(b_knorm_g[l], 8)]).astype(f32)
    gmat = jnp.asarray(np.kron(np.eye(4), np.ones((A_HEAD_DIM, A_HEAD_DIM))), dtype=bf16)

    qa, ka, va, qb, kb, vb, ga, gb = _proj(x2, norm1_g[l][None].astype(f32), w_qk, w_rest, gn, gmat)

    lam_init = 0.8 - 0.6 * math.exp(-0.3 * l)
    lp = a_lambda[l].astype(f32)
    lam = jnp.exp(jnp.sum(lp[0] * lp[1])) - jnp.exp(jnp.sum(lp[2] * lp[3])) + lam_init
    bias_a, far_a = _diff_bias_vecs(t5_table)
    bmax_a = jnp.maximum(jnp.max(bias_a, axis=(1, 2)), far_a)
    scal_a = (jnp.concatenate([lam[None], far_a, bmax_a]) * jnp.array([1.0] + [LOG2E] * (2 * A_HEADS))).astype(f32)
    gsub = (a_subln_g[l] * (1.0 - lam_init))[None].astype(f32)
    bias_at, _ = _diff_bias_vecs(t5_table, transposed=True)
    oa = _diff_attention(scal_a, qa, ka, va, bias_a * LOG2E, bias_at * LOG2E, gsub, bsz=bsz, s_len=s_len)

    bias_b = _band_bias_vecs(b_rel_table[l])
    ob = _band_attention(qb, kb, vb, bias_b * LOG2E, bsz=bsz, s_len=s_len)

    wr = jnp.zeros((D_MODEL, LANES), f32)
    wr = wr.at[:, :N_GROUPS].set(w_router_group[l]).at[:, N_GROUPS:N_GROUPS + N_EXPERTS].set(w_router_expert[l])
    wrh = wr.astype(bf16)
    wr2 = jnp.concatenate([wrh, (wr - wrh.astype(f32)).astype(bf16)], axis=1)
    br = jnp.zeros((1, LANES), f32)
    br = br.at[0, :N_GROUPS].set(b_router_group[l]).at[0, N_GROUPS:N_GROUPS + N_EXPERTS].set(b_router_expert[l])
    x1, hn, route, cnt = _post(oa, ob, ga, gb, x2, w_branch_a[l].astype(bf16), w_branch_b[l].astype(bf16),
                               w_out[l].astype(bf16), norm2_g[l][None].astype(f32), wr2, br)

    ltri = jnp.asarray(np.tril(np.ones((TM_RANK, TM_RANK)), -1), dtype=bf16)
    utri = jnp.asarray(np.triu(np.ones((LANES, LANES)), 1), dtype=bf16)
    dest = _rank(route, cnt, ltri, utri)

    counts = cnt[0, :N_EXPERTS].astype(jnp.int32)
    n_rows = n_tok * TOP_K
    seg = _segments(counts, n_rows)

    dest2 = dest[:, :TOP_K].astype(jnp.int32)
    xs = _dispatch(dest2.reshape(n_tok // TM_DISPATCH, 1, TOP_K * TM_DISPATCH), hn, n_rows)
    ys = _ffn(seg, xs.reshape(n_rows * ROW_SLABS, LANES), w_gate[l], w_up[l], w_down[l])
    out = _combine(dest2.reshape(n_tok // TM_COMBINE, 1, TOP_K * TM_COMBINE), route, x1,
                   ys.reshape(n_rows, ROW_SLABS, LANES))
    return out.reshape(bsz, s_len, D_MODEL)
```

```python
import functools
import math

import jax
import jax.numpy as jnp
import numpy as np
from jax import lax
from jax.experimental import pallas as pl
from jax.experimental.pallas import tpu as pltpu

D_MODEL = 1024
CHUNK = 64
A_HEADS = 4
A_HEAD_DIM = 64
A_VDIM = 2 * A_HEAD_DIM
B_HEADS = 8
B_HEAD_DIM = 64
B_LEFT_CHUNKS = 8
B_MAX_REL = 128
T5_BUCKETS = 32
T5_MAX_DIST = 128
N_GROUPS = 4
EXPERTS_PER_GROUP = 8
N_EXPERTS = N_GROUPS * EXPERTS_PER_GROUP
TOP_K = 2
D_EXPERT = 512
EPS = 1e-6
NEG = -1e30
LOG2E = 1.0 / math.log(2.0)
DENOM_FLOOR = 2.0 ** -100

CHUNK_SHIFT = CHUNK.bit_length() - 1
assert 1 << CHUNK_SHIFT == CHUNK
LANES = 128
A_W = A_HEADS * 2 * A_HEAD_DIM
B_W = B_HEADS * B_HEAD_DIM
PROJ_W = 4 * 256 + 4 * 512 + 2 * D_MODEL

TM_PROJ = 512
TQ = 512
TK = 512
FAR_UNROLL = 4
TM_POST = 1024
TM_RANK = 1024
TM_DISPATCH = 4096
TM_COMBINE = 2048
ROW_UNROLL = 8
FFN_BLK = 1024
FFN_SUB = 256
ROW_SLABS = D_MODEL // 2 // LANES
VMEM_LIMIT = 56 * 1024 * 1024


def _cparams(sem):
    return pltpu.CompilerParams(dimension_semantics=sem, vmem_limit_bytes=VMEM_LIMIT)


def _pack_bf16_pairs(x):
    bits = lax.bitcast_convert_type(x.astype(jnp.bfloat16).astype(jnp.float32), jnp.uint32)
    half = x.shape[1] // 2
    return (bits[:, :half] & jnp.uint32(0xFFFF0000)) | lax.shift_right_logical(bits[:, half:], jnp.uint32(16))


def _unpack_bf16_pairs(pk):
    return (lax.bitcast_convert_type(pk & jnp.uint32(0xFFFF0000), jnp.float32),
            lax.bitcast_convert_type(lax.shift_left(pk, jnp.uint32(16)), jnp.float32))


def _proj_kernel(x_ref, g1_ref, wqk_ref, w_ref, gn_ref, gmat_ref,
                 qa_ref, ka_ref, va_ref, qb_ref, kb_ref, vb_ref, ga_ref, gb_ref):
    x = x_ref[...]
    xn = x * lax.rsqrt(jnp.mean(x * x, axis=-1, keepdims=True) + EPS) * g1_ref[...]
    xn = xn.astype(jnp.bfloat16)
    n_qk = wqk_ref.shape[1]

    def slab(c0, width):
        w = wqk_ref[:, c0:c0 + width] if c0 < n_qk else w_ref[:, c0:c0 + width]
        return jnp.dot(xn, w, preferred_element_type=jnp.float32)

    def headnorm(y, gi):
        sq = (y * y).astype(jnp.bfloat16)
        half = gmat_ref.shape[0]
        ss = jnp.concatenate([jnp.dot(sq[:, c:c + half], gmat_ref[...], preferred_element_type=jnp.float32)
                              for c in range(0, y.shape[1], half)], axis=1)
        return y * lax.rsqrt(ss * (1.0 / A_HEAD_DIM) + EPS) * gn_ref[gi:gi + 1, :]

    qa_ref[...] = headnorm(slab(0, 512), 0).astype(jnp.bfloat16)
    ka_ref[...] = headnorm(slab(512, 512), 1).astype(jnp.bfloat16)
    va_ref[...] = slab(1024, 512).astype(jnp.bfloat16)
    qb_ref[...] = headnorm(slab(1536, 512), 2).astype(jnp.bfloat16)
    kb_ref[...] = headnorm(slab(2048, 512), 3).astype(jnp.bfloat16)
    vb_ref[...] = slab(2560, 512).astype(jnp.bfloat16)
    for j in range(2):
        ga_ref[:, j * 512:(j + 1) * 512] = jax.nn.sigmoid(slab(3072 + j * 512, 512)).astype(jnp.bfloat16)
        gb_ref[:, j * 512:(j + 1) * 512] = jax.nn.sigmoid(slab(4096 + j * 512, 512)).astype(jnp.bfloat16)


def _proj(x2, g1, w_qk, w_all, gn, gmat):
    t = x2.shape[0]
    n = t // TM_PROJ
    row = lambda w: pl.BlockSpec((TM_PROJ, w), lambda i: (i, 0))
    full = lambda a: pl.BlockSpec(a.shape, lambda i: (0,) * a.ndim)
    outs = [jax.ShapeDtypeStruct((t, 512), jnp.bfloat16)] * 6 + [jax.ShapeDtypeStruct((t, D_MODEL), jnp.bfloat16)] * 2
    return pl.pallas_call(
        _proj_kernel,
        grid=(n,),
        in_specs=[row(D_MODEL), full(g1), full(w_qk), full(w_all), full(gn), full(gmat)],
        out_specs=[row(512)] * 6 + [row(D_MODEL)] * 2,
        out_shape=outs,
        compiler_params=_cparams(("arbitrary",)),
        name="proj",
    )(x2, g1, w_qk, w_all, gn, gmat)


def _diff_kernel(scal_ref, q_ref, k_ref, v_ref, vec_ref, vect_ref, gsub_ref, o_ref,
                 bias_sc, biast_sc, vt_sc, kmaxt_sc, qz_sc, qzt_sc, refn_sc, reff_sc, l_sc, acc_sc,
                 m_sc, acc2_sc, mf_sc, accf_sc, *, n_near):
    f32 = jnp.float32
    h = pl.program_id(1)
    qi = pl.program_id(2)
    nc = TK // LANES
    r2 = 2 * TQ
    n_heads = pl.num_programs(1)

    def near_span(j):
        return slice((n_near - 1 - j) * TK, (n_near - j) * TK)

    @pl.when(qi == 0)
    def _():
        row = lax.broadcasted_iota(jnp.int32, (TK, TQ), 0)
        col = lax.broadcasted_iota(jnp.int32, (TK, TQ), 1)
        for j in range(n_near):
            ok = lax.shift_right_arithmetic(col, CHUNK_SHIFT) >= lax.shift_right_arithmetic(row - j * TK, CHUNK_SHIFT)
            vect = jnp.broadcast_to(vect_ref[0, j:j + 1, :], (TK, TQ + TK))
            tilet = jnp.where(ok, pltpu.roll(vect, 0, 1, stride=1, stride_axis=0)[:, :TQ], NEG)
            for half in range(2):
                biast_sc[near_span(j), half * TQ:(half + 1) * TQ] = tilet
        vt_sc[...] = v_ref[...].T
        kf = k_ref[...].astype(f32)
        ksq = kf * kf
        klane = lax.broadcasted_iota(jnp.int32, ksq.shape, 1)
        for half in range(2):
            sel = (klane < 64) if half == 0 else (klane >= 64)
            norm2 = jnp.max(jnp.sum(jnp.where(sel, ksq, 0.0), axis=-1, keepdims=True), axis=0, keepdims=True)
            kmaxt_sc[:, half * TQ:(half + 1) * TQ] = jnp.broadcast_to(jnp.sqrt(norm2), (8, TQ))

    q = q_ref[...]
    lane = lax.broadcasted_iota(jnp.int32, q.shape, 1)
    zero = jnp.zeros_like(q)
    q1 = jnp.where(lane < 64, q, zero)
    q2 = jnp.where(lane >= 64, q, zero)
    qzt_sc[:, 0:TQ] = q1.T
    qzt_sc[:, TQ:r2] = q2.T
    cfar = scal_ref[1 + h]
    bias_max = scal_ref[1 + n_heads + h]
    n_far = jnp.maximum(qi - (n_near - 1), 0)

    def key_block(kblk):
        return k_ref[pl.ds(pl.multiple_of(kblk * TK, TK), TK), :]

    def finish(od):
        od = od * lax.rsqrt(jnp.mean(od * od, axis=-1, keepdims=True) + EPS) * gsub_ref[...]
        o_ref[...] = od.astype(o_ref.dtype)

    def fixed_reference_pass():
        qf = qzt_sc[...].astype(f32)
        qnorm = jnp.sqrt(jnp.sum(qf * qf, axis=0, keepdims=True))
        refn_sc[...] = qnorm * kmaxt_sc[...] + bias_max
        l_sc[...] = jnp.zeros(l_sc.shape, f32)
        acc_sc[...] = jnp.zeros(acc_sc.shape, f32)

        def accumulate(st, kblk, ref_ref):
            p = jnp.exp2(st.reshape(TK // 8, 8, r2) - ref_ref[...][None])
            l_sc[...] += jnp.sum(p, axis=0)
            vt = vt_sc[:, pl.ds(pl.multiple_of(kblk * TK, TK), TK)]
            acc_sc[...] += jnp.dot(vt, p.reshape(TK, r2).astype(jnp.bfloat16), preferred_element_type=f32)

        def scores_t(kblk):
            return jnp.dot(key_block(kblk), qzt_sc[...], preferred_element_type=f32)

        for j in range(n_near):
            def near(j=j):
                accumulate(scores_t(qi - j) + biast_sc[near_span(j), :], qi - j, refn_sc)
            if j == 0:
                near()
            else:
                pl.when(qi >= j)(near)

        reff_sc[...] = refn_sc[...] - cfar

        def far(kblk, carry):
            accumulate(scores_t(kblk), kblk, reff_sc)
            return carry

        def far_run(first, count, width):
            def body(i, carry):
                for u in range(width):
                    far(first + width * i + u, carry)
                return carry
            lax.fori_loop(0, count, body, 0)
            return first + count * width

        nxt = far_run(0, n_far // FAR_UNROLL, FAR_UNROLL)
        nxt = far_run(nxt, (n_far - nxt) // 2, 2)
        far_run(nxt, n_far - nxt, 1)

        denom = jnp.sum(l_sc[...], axis=0, keepdims=True)
        ot = acc_sc[...] / denom
        finish((ot[:, 0:TQ] - scal_ref[0] * ot[:, TQ:r2]).T)
        return denom

    def exact_pass():
        qz_sc[0:TQ, :] = q1
        qz_sc[TQ:r2, :] = q2
        row = lax.broadcasted_iota(jnp.int32, (TQ, TK), 0)
        col = lax.broadcasted_iota(jnp.int32, (TQ, TK), 1)
        for j in range(n_near):
            ok = lax.shift_right_arithmetic(row, CHUNK_SHIFT) >= lax.shift_right_arithmetic(col - j * TK, CHUNK_SHIFT)
            vec = jnp.broadcast_to(vec_ref[0, j:j + 1, :], (TQ, TQ + TK))
            tile = jnp.where(ok, pltpu.roll(vec, 0, 1, stride=1, stride_axis=0)[:, :TK], NEG)
            for half in range(2):
                bias_sc[half * TQ:(half + 1) * TQ, near_span(j)] = tile

        def scores(kblk):
            return lax.dot_general(qz_sc[...], key_block(kblk), (((1,), (1,)), ((), ())), preferred_element_type=f32)

        def update(s, kblk, m_ref, a_ref):
            cols = [s[:, c * LANES:(c + 1) * LANES] for c in range(nc)]
            m_old = m_ref[...]
            m_new = jnp.maximum(m_old, jnp.max(functools.reduce(jnp.maximum, cols), axis=-1, keepdims=True))
            alpha = jnp.exp2(m_old - m_new)
            ps = [jnp.exp2(c - m_new) for c in cols]
            p = jnp.concatenate([x.astype(jnp.bfloat16) for x in ps], axis=1)
            v = v_ref[pl.ds(pl.multiple_of(kblk * TK, TK), TK), :]
            a_ref[:, 0:LANES] = alpha * a_ref[:, 0:LANES] + jnp.dot(p, v, preferred_element_type=f32)
            a_ref[:, LANES:2 * LANES] = alpha * a_ref[:, LANES:2 * LANES] + functools.reduce(jnp.add, ps)
            m_ref[...] = m_new

        m_sc[...] = jnp.full(m_sc.shape, NEG, f32)
        acc2_sc[...] = jnp.zeros(acc2_sc.shape, f32)
        for j in range(n_near):
            def near(j=j):
                update(scores(qi - j) + bias_sc[:, near_span(j)], qi - j, m_sc, acc2_sc)
            if j == 0:
                near()
            else:
                pl.when(qi >= j)(near)

        mf_sc[...] = jnp.full(mf_sc.shape, NEG, f32)
        accf_sc[...] = jnp.zeros(accf_sc.shape, f32)

        def far(kblk, carry):
            update(scores(kblk), kblk, mf_sc, accf_sc)
            return carry

        lax.fori_loop(0, n_far, far, 0)
        mf = mf_sc[...] + cfar
        mn = m_sc[...]
        m = jnp.maximum(mf, mn)
        wf = jnp.exp2(mf - m)
        wn = jnp.exp2(mn - m)
        tot = (jnp.concatenate([wf, wf], axis=1) * accf_sc[...]
               + jnp.concatenate([wn, wn], axis=1) * acc2_sc[...])
        o = tot[:, 0:LANES] / jnp.sum(tot[:, LANES:2 * LANES], axis=-1, keepdims=True)
        finish(o[:TQ] - scal_ref[0] * o[TQ:])

    denom = fixed_reference_pass()
    pl.when(jnp.logical_not(jnp.min(denom) >= DENOM_FLOOR))(exact_pass)


def _diff_attention(scal, q, k, v, vecs, vecs_t, gsub, *, bsz, s_len):
    n_blk = q.shape[1] // LANES
    nq = s_len // TQ
    n_near = vecs.shape[1]
    assert vecs.shape == vecs_t.shape == (n_blk, n_near, TQ + TK)
    kern = functools.partial(_diff_kernel, n_near=n_near)
    f32, bf16 = jnp.float32, jnp.bfloat16
    vm = pltpu.VMEM
    vec_spec = pl.BlockSpec((1, n_near, TQ + TK), lambda b, h, i: (h, 0, 0))
    return pl.pallas_call(
        kern,
        grid=(bsz, n_blk, nq),
        in_specs=[
            pl.BlockSpec(memory_space=pltpu.SMEM),
            pl.BlockSpec((TQ, LANES), lambda b, h, i: (b * nq + i, h)),
            pl.BlockSpec((s_len, LANES), lambda b, h, i: (b, h)),
            pl.BlockSpec((s_len, LANES), lambda b, h, i: (b, h)),
            vec_spec, vec_spec,
            pl.BlockSpec((1, LANES), lambda b, h, i: (0, 0)),
        ],
        out_specs=pl.BlockSpec((TQ, LANES), lambda b, h, i: (b * nq + i, h)),
        out_shape=jax.ShapeDtypeStruct(q.shape, bf16),
        scratch_shapes=[vm((2 * TQ, n_near * TK), f32), vm((n_near * TK, 2 * TQ), f32),
                        vm((LANES, s_len), bf16), vm((8, 2 * TQ), f32),
                        vm((2 * TQ, LANES), bf16), vm((LANES, 2 * TQ), bf16),
                        vm((8, 2 * TQ), f32), vm((8, 2 * TQ), f32), vm((8, 2 * TQ), f32),
                        vm((LANES, 2 * TQ), f32),
                        vm((2 * TQ, LANES), f32), vm((2 * TQ, 2 * LANES), f32),
                        vm((2 * TQ, LANES), f32), vm((2 * TQ, 2 * LANES), f32)],
        compiler_params=_cparams(("arbitrary", "arbitrary", "arbitrary")),
        name="attn_diff",
    )(scal, q, k, v, vecs, vecs_t, gsub)


BAND_TQ = 1024
BAND_HQ = 256
BAND_SUBS = BAND_TQ // BAND_HQ
BAND_W = B_LEFT_CHUNKS * CHUNK + BAND_HQ


def _band_kernel(q_ref, k_ref, v_ref, vec_ref, o_ref, bias_sc, qz_sc):
    f32 = jnp.float32
    qi = pl.program_id(2)
    hq, w = BAND_HQ, BAND_W
    back = w - hq

    @pl.when(qi == 0)
    def _():
        qchunk = lax.shift_right_arithmetic(lax.broadcasted_iota(jnp.int32, (hq, w), 0), CHUNK_SHIFT)
        kchunk = lax.shift_right_arithmetic(lax.broadcasted_iota(jnp.int32, (hq, w), 1) - back, CHUNK_SHIFT)
        dchunk = qchunk - kchunk
        allowed = (dchunk >= 0) & (dchunk <= B_LEFT_CHUNKS)
        for half in range(2):
            vec = jnp.broadcast_to(vec_ref[0, half:half + 1, :], (hq, hq + w))
            tile = pltpu.roll(vec, 0, 1, stride=1, stride_axis=0)[:, :w]
            bias_sc[half * hq:(half + 1) * hq, :] = jnp.where(allowed, tile, NEG)

    q = q_ref[...]
    lane = lax.broadcasted_iota(jnp.int32, (hq, LANES), 1)
    zero = jnp.zeros((hq, LANES), q.dtype)
    for u in range(BAND_SUBS):
        qu = q[u * hq:(u + 1) * hq, :]
        qz_sc[(2 * u) * hq:(2 * u + 1) * hq, :] = jnp.where(lane < 64, qu, zero)
        qz_sc[(2 * u + 1) * hq:(2 * u + 2) * hq, :] = jnp.where(lane >= 64, qu, zero)

    def scores(u, k0, n_keys):
        k = k_ref[pl.ds(k0, n_keys), :]
        return lax.dot_general(qz_sc[2 * u * hq:(2 * u + 2) * hq, :], k, (((1,), (1,)), ((), ())),
                               preferred_element_type=f32) + bias_sc[:, w - n_keys:w]

    def softmax(s):
        cols = [s[:, c * LANES:(c + 1) * LANES] for c in range(s.shape[1] // LANES)]
        m = jnp.max(functools.reduce(jnp.maximum, cols), axis=-1, keepdims=True)
        ps = [jnp.exp2(c - m) for c in cols]
        denom = jnp.sum(functools.reduce(jnp.add, ps), axis=-1, keepdims=True)
        return jnp.concatenate([x.astype(jnp.bfloat16) for x in ps], axis=1), denom

    def finish(u, p, denom, k0):
        o = jnp.dot(p, v_ref[pl.ds(k0, p.shape[1]), :], preferred_element_type=f32) / denom
        o_ref[u * hq:(u + 1) * hq, :] = jnp.where(lane < 64, o[:hq], o[hq:]).astype(o_ref.dtype)

    def step(k0s, n_keys):
        ss = [scores(u, k0s[u], n_keys[u]) for u in range(BAND_SUBS)]
        pd = [softmax(x) for x in ss]
        for u in range(BAND_SUBS):
            finish(u, pd[u][0], pd[u][1], k0s[u])

    first = tuple(min((u + 1) * hq, w) for u in range(BAND_SUBS))

    @pl.when(qi == 0)
    def _():
        step(tuple(max((u + 1) * hq - w, 0) for u in range(BAND_SUBS)), first)

    @pl.when(qi > 0)
    def _():
        start = pl.multiple_of(qi * BAND_TQ - back, hq)
        step(tuple(pl.multiple_of(start + u * hq, hq) for u in range(BAND_SUBS)), (w,) * BAND_SUBS)


def _band_attention(q, k, v, vecs, *, bsz, s_len):
    n_blk = q.shape[1] // LANES
    nq = s_len // BAND_TQ
    assert vecs.shape == (n_blk, 2, BAND_HQ + BAND_W)
    return pl.pallas_call(
        _band_kernel,
        grid=(bsz, n_blk, nq),
        in_specs=[
            pl.BlockSpec((BAND_TQ, LANES), lambda b, h, i: (b * nq + i, h)),
            pl.BlockSpec((s_len, LANES), lambda b, h, i: (b, h)),
            pl.BlockSpec((s_len, LANES), lambda b, h, i: (b, h)),
            pl.BlockSpec((1, 2, BAND_HQ + BAND_W), lambda b, h, i: (h, 0, 0)),
        ],
        out_specs=pl.BlockSpec((BAND_TQ, LANES), lambda b, h, i: (b * nq + i, h)),
        out_shape=jax.ShapeDtypeStruct(q.shape, jnp.bfloat16),
        scratch_shapes=[pltpu.VMEM((2 * BAND_HQ, BAND_W), jnp.float32),
                        pltpu.VMEM((2 * BAND_TQ, LANES), jnp.bfloat16)],
        compiler_params=_cparams(("arbitrary", "arbitrary", "arbitrary")),
        name="attn_band",
    )(q, k, v, vecs)


def _post_kernel(oa_ref, ob_ref, ga_ref, gb_ref, x_ref, wa_ref, wb_ref, wo_ref, g2_ref,
                 wr2_ref, br_ref, x1_ref, hn_ref, route_ref, cnt_ref):
    f32 = jnp.float32
    ya = jnp.dot(oa_ref[...], wa_ref[...], preferred_element_type=f32)
    yb = jnp.dot(ob_ref[...], wb_ref[...], preferred_element_type=f32)
    mixed = ga_ref[...].astype(f32) * ya + gb_ref[...].astype(f32) * yb
    x1 = x_ref[...] + jnp.dot(mixed.astype(jnp.bfloat16), wo_ref[...], preferred_element_type=f32)
    x1_ref[...] = x1
    hn = x1 * lax.rsqrt(jnp.mean(x1 * x1, axis=-1, keepdims=True) + EPS) * g2_ref[...]
    hh = hn.astype(jnp.bfloat16)
    packed = _pack_bf16_pairs(hn)
    for c in range(ROW_SLABS):
        hn_ref[pl.ds(c, TM_POST, stride=ROW_SLABS), :] = packed[:, c * LANES:(c + 1) * LANES]

    hl = (hn - hh.astype(f32)).astype(jnp.bfloat16)
    hw = jnp.dot(hh, wr2_ref[...], preferred_element_type=f32)
    lg = (hw[:, 0:LANES] + hw[:, LANES:2 * LANES]
          + jnp.dot(hl, wr2_ref[:, 0:LANES], preferred_element_type=f32)) + br_ref[...]

    lanei = lax.broadcasted_iota(jnp.int32, lg.shape, 1)
    lanef = lanei.astype(f32)
    big = 999.0
    gmask = lanei < N_GROUPS
    gl = jnp.where(gmask, lg, NEG)
    gm = jnp.max(gl, axis=-1, keepdims=True)
    ge = jnp.where(gmask, jnp.exp(gl - gm), 0.0)
    gp = ge / jnp.sum(ge, axis=-1, keepdims=True)
    p_g = jnp.max(gp, axis=-1, keepdims=True)
    gidx = jnp.min(jnp.where(gmask & (gp == p_g), lanef, big), axis=-1, keepdims=True)
    egrp = lax.shift_right_arithmetic(lanei - N_GROUPS, 3).astype(f32)
    emask = (lanei >= N_GROUPS) & (lanei < N_GROUPS + N_EXPERTS) & (egrp == gidx)
    el = jnp.where(emask, lg, NEG)
    v1 = jnp.max(el, axis=-1, keepdims=True)
    i1 = jnp.min(jnp.where(emask & (el == v1), lanef, big), axis=-1, keepdims=True)
    emask2 = emask & (lanef != i1)
    el2 = jnp.where(emask2, lg, NEG)
    v2 = jnp.max(el2, axis=-1, keepdims=True)
    i2 = jnp.min(jnp.where(emask2 & (el2 == v2), lanef, big), axis=-1, keepdims=True)
    t = jnp.exp(v2 - v1)
    den = 1.0 + t
    w1 = p_g * (1.0 / den)
    w2 = p_g * (t / den)
    route = jnp.where(lanei == 0, i1 - N_GROUPS,
                      jnp.where(lanei == 1, i2 - N_GROUPS,
                                jnp.where(lanei == 2, w1, jnp.where(lanei == 3, w2, 0.0))))
    route_ref[...] = route

    @pl.when(pl.program_id(0) == 0)
    def _():
        cnt_ref[...] = jnp.zeros(cnt_ref.shape, f32)

    chosen = ((lanef == i1 - N_GROUPS) | (lanef == i2 - N_GROUPS)).astype(f32)
    cnt_ref[...] += jnp.broadcast_to(jnp.sum(chosen, axis=0, keepdims=True), cnt_ref.shape)


def _post(oa, ob, ga, gb, x2, wa, wb, wo, g2, wr2, br):
    t = x2.shape[0]
    n = t // TM_POST
    row = lambda w: pl.BlockSpec((TM_POST, w), lambda i: (i, 0))
    full = lambda a: pl.BlockSpec(a.shape, lambda i: (0,) * a.ndim)
    return pl.pallas_call(
        _post_kernel,
        grid=(n,),
        in_specs=[row(512), row(512), row(D_MODEL), row(D_MODEL), row(D_MODEL),
                  full(wa), full(wb), full(wo), full(g2), full(wr2), full(br)],
        out_specs=[row(D_MODEL), pl.BlockSpec((TM_POST * ROW_SLABS, LANES), lambda i: (i, 0)), row(LANES),
                   pl.BlockSpec((8, LANES), lambda i: (0, 0))],
        out_shape=[jax.ShapeDtypeStruct((t, D_MODEL), jnp.float32),
                   jax.ShapeDtypeStruct((t * ROW_SLABS, LANES), jnp.uint32),
                   jax.ShapeDtypeStruct((t, LANES), jnp.float32),
                   jax.ShapeDtypeStruct((8, LANES), jnp.float32)],
        compiler_params=_cparams(("arbitrary",)),
        name="post",
    )(oa, ob, ga, gb, x2, wa, wb, wo, g2, wr2, br)


def _rank_kernel(route_ref, cnt_ref, ltri_ref, utri_ref, dest_ref, pstart_sc, base_sc):
    f32 = jnp.float32
    i = pl.program_id(0)
    route = route_ref[...]
    lanef = lax.broadcasted_iota(jnp.int32, route.shape, 1).astype(f32)
    oh1 = (lanef == route[:, 0:1]).astype(f32)
    oh2 = (lanef == route[:, 1:2]).astype(f32)
    both = oh1 + oh2

    @pl.when(i == 0)
    def _():
        cnt = cnt_ref[0:1, :]
        chi = jnp.floor(cnt * (1.0 / 256.0))
        clo = cnt - chi * 256.0
        split = jnp.concatenate([jnp.broadcast_to(chi, (8, LANES)), jnp.broadcast_to(clo, (8, LANES))], axis=0)
        excl = jnp.dot(split.astype(jnp.bfloat16), utri_ref[...], preferred_element_type=f32)
        pstart_sc[...] = excl[0:1] * 256.0 + excl[8:9]
        base_sc[...] = jnp.zeros(base_sc.shape, f32)

    prior = jnp.dot(ltri_ref[...], both.astype(jnp.bfloat16), preferred_element_type=f32)
    slot = prior + base_sc[...] + pstart_sc[...]
    d1 = jnp.sum(oh1 * slot, axis=-1, keepdims=True)
    d2 = jnp.sum(oh2 * slot, axis=-1, keepdims=True)
    dest_ref[...] = jnp.where(lanef == 0.0, d1, jnp.where(lanef == 1.0, d2, 0.0))
    base_sc[...] += jnp.sum(both, axis=0, keepdims=True)


def _rank(route, cnt, ltri, utri):
    t = route.shape[0]
    n = t // TM_RANK
    full = lambda a: pl.BlockSpec(a.shape, lambda i: (0,) * a.ndim)
    row1 = lambda: pltpu.VMEM((1, LANES), jnp.float32)
    return pl.pallas_call(
        _rank_kernel,
        grid=(n,),
        in_specs=[pl.BlockSpec((TM_RANK, LANES), lambda i: (i, 0)), full(cnt), full(ltri), full(utri)],
        out_specs=pl.BlockSpec((TM_RANK, LANES), lambda i: (i, 0)),
        out_shape=jax.ShapeDtypeStruct((t, LANES), jnp.float32),
        scratch_shapes=[row1(), row1()],
        compiler_params=_cparams(("arbitrary",)),
        name="rank",
    )(route, cnt, ltri, utri)


def _dispatch_kernel(dest_ref, hn_ref, xs_ref, sems):
    def row_copy(r, k):
        d = dest_ref[0, 0, 2 * r + k]
        src = hn_ref.at[pl.ds(pl.multiple_of(r * ROW_SLABS, ROW_SLABS), ROW_SLABS), :]
        return pltpu.make_async_copy(src, xs_ref.at[d], sems.at[k])

    def issue(r, c):
        row_copy(r, 0).start(priority=0)
        row_copy(r, 1).start(priority=1)
        return c

    lax.fori_loop(0, TM_DISPATCH, issue, 0, unroll=ROW_UNROLL)
    for k in range(TOP_K):
        pltpu.make_async_copy(hn_ref, hn_ref, sems.at[k]).wait()


def _dispatch(dest3, hn, n_rows):
    t = hn.shape[0] // ROW_SLABS
    n = t // TM_DISPATCH
    return pl.pallas_call(
        _dispatch_kernel,
        grid=(n,),
        in_specs=[pl.BlockSpec((1, 1, 2 * TM_DISPATCH), lambda i: (i, 0, 0), memory_space=pltpu.SMEM),
                  pl.BlockSpec((TM_DISPATCH * ROW_SLABS, LANES), lambda i: (i, 0))],
        out_specs=pl.BlockSpec(memory_space=pl.ANY),
        out_shape=jax.ShapeDtypeStruct((n_rows, ROW_SLABS, LANES), hn.dtype),
        scratch_shapes=[pltpu.SemaphoreType.DMA((2,))],
        compiler_params=_cparams(("arbitrary",)),
        name="dispatch",
    )(dest3, hn)


def _ffn_kernel(tile_ref, exp_ref, lo_ref, hi_ref, cast_ref, init_ref,
                xs_ref, wg_ref, wu_ref, wd_ref, ys_ref, wg_sc, wu_sc, wd_sc):
    v = pl.program_id(0)
    lo = lo_ref[v]
    hi = hi_ref[v]

    @pl.when(init_ref[v] == 1)
    def _():
        ys_ref[...] = jnp.zeros(ys_ref.shape, ys_ref.dtype)

    @pl.when((hi > lo) & (cast_ref[v] == 1))
    def _():
        wg_sc[...] = wg_ref[0].astype(jnp.bfloat16)
        wu_sc[...] = wu_ref[0].astype(jnp.bfloat16)
        wd_sc[...] = wd_ref[0].astype(jnp.bfloat16)

    for piece in range(FFN_BLK // FFN_SUB):
        row0 = tile_ref[v] * FFN_BLK + piece * FFN_SUB

        @pl.when((hi > row0) & (lo < row0 + FFN_SUB))
        def _(piece=piece, row0=row0):
            def slab(ref, c):
                return ref.at[pl.ds(piece * FFN_SUB * ROW_SLABS + c, FFN_SUB, stride=ROW_SLABS), :]

            pk = jnp.concatenate([slab(xs_ref, c)[...] for c in range(ROW_SLABS)], axis=1)
            x = jnp.concatenate(_unpack_bf16_pairs(pk), axis=1).astype(jnp.bfloat16)
            g = jnp.dot(x, wg_sc[...], preferred_element_type=jnp.float32)
            u = jnp.dot(x, wu_sc[...], preferred_element_type=jnp.float32)
            hb = (g * jax.nn.sigmoid(g) * u).astype(jnp.bfloat16)
            packed = _pack_bf16_pairs(jnp.dot(hb, wd_sc[...], preferred_element_type=jnp.float32))
            rows = row0 + lax.broadcasted_iota(jnp.int32, (FFN_SUB, LANES), 0)
            mine = (rows >= lo) & (rows < hi)
            for c in range(ROW_SLABS):
                out = slab(ys_ref, c)
                out[...] = jnp.where(mine, packed[:, c * LANES:(c + 1) * LANES], out[...])


def _ffn(seg, xs, w_gate, w_up, w_down):
    n_seg = seg[0].shape[0]
    grid_spec = pltpu.PrefetchScalarGridSpec(
        num_scalar_prefetch=6,
        grid=(n_seg,),
        in_specs=[
            pl.BlockSpec((FFN_BLK * ROW_SLABS, LANES), lambda v, t, e, *_: (t[v], 0)),
            pl.BlockSpec((1, D_MODEL, D_EXPERT), lambda v, t, e, *_: (e[v], 0, 0)),
            pl.BlockSpec((1, D_MODEL, D_EXPERT), lambda v, t, e, *_: (e[v], 0, 0)),
            pl.BlockSpec((1, D_EXPERT, D_MODEL), lambda v, t, e, *_: (e[v], 0, 0)),
        ],
        out_specs=pl.BlockSpec((FFN_BLK * ROW_SLABS, LANES), lambda v, t, e, *_: (t[v], 0)),
        scratch_shapes=[pltpu.VMEM((D_MODEL, D_EXPERT), jnp.bfloat16),
                        pltpu.VMEM((D_MODEL, D_EXPERT), jnp.bfloat16),
                        pltpu.VMEM((D_EXPERT, D_MODEL), jnp.bfloat16)],
    )
    return pl.pallas_call(
        _ffn_kernel,
        grid_spec=grid_spec,
        out_shape=jax.ShapeDtypeStruct(xs.shape, jnp.uint32),
        compiler_params=_cparams(("arbitrary",)),
        name="ffn",
    )(*seg, xs, w_gate, w_up, w_down)


def _segments(counts, n_rows):
    i32 = jnp.int32
    n_tiles = n_rows // FFN_BLK
    n_seg = n_tiles + N_EXPERTS
    tri = jnp.tril(jnp.ones((N_EXPERTS, N_EXPERTS), i32))
    ends = jnp.sum(tri * counts[None, :], axis=1)
    starts = ends - counts
    edges = jnp.arange(n_tiles, dtype=i32) * FFN_BLK
    rank_e = jnp.arange(n_tiles, dtype=i32) + jnp.sum(starts[None, :] <= edges[:, None], axis=1)
    rank_s = jnp.arange(N_EXPERTS, dtype=i32) + jnp.sum(edges[None, :] < starts[:, None], axis=1)
    seg = jnp.arange(n_seg, dtype=i32)
    lo = (jnp.sum(jnp.where(rank_e[None, :] == seg[:, None], edges[None, :], 0), axis=1)
          + jnp.sum(jnp.where(rank_s[None, :] == seg[:, None], starts[None, :], 0), axis=1))
    hi = jnp.concatenate([lo[1:], jnp.array([n_rows], i32)])
    valid = hi > lo
    tile = jnp.minimum(lo // FFN_BLK, n_tiles - 1)
    expert = jnp.minimum(jnp.sum(ends[None, :] <= lo[:, None], axis=1), N_EXPERTS - 1).astype(i32)
    upto = seg[None, :] <= seg[:, None]
    expert = jnp.max(jnp.where(upto & valid[None, :], expert[None, :], 0), axis=1)
    prev_expert = jnp.concatenate([jnp.array([-1], i32), expert[:-1]])
    first_valid = valid & (jnp.sum(jnp.where(upto & valid[None, :], 1, 0), axis=1) == 1)
    cast = valid & ((expert != prev_expert) | first_valid)
    prev_tile = jnp.concatenate([jnp.array([-1], i32), tile[:-1]])
    init = tile != prev_tile
    return (tile.astype(i32), expert, lo.astype(i32), hi.astype(i32), cast.astype(i32), init.astype(i32))


def _combine_kernel(dest_ref, route_ref, x1_ref, ys_ref, out_ref, y0_sc, y1_sc, sems):
    def row_copy(r, k):
        d = dest_ref[0, 0, 2 * r + k]
        dst = y0_sc if k == 0 else y1_sc
        return pltpu.make_async_copy(ys_ref.at[d], dst.at[pl.ds(pl.multiple_of(r * ROW_SLABS, ROW_SLABS), ROW_SLABS), :],
                                     sems.at[k])

    def issue(r, c):
        row_copy(r, 0).start(priority=0)
        row_copy(r, 1).start(priority=1)
        return c

    lax.fori_loop(0, TM_COMBINE, issue, 0, unroll=ROW_UNROLL)
    pltpu.make_async_copy(y0_sc, y0_sc, sems.at[0]).wait()
    pltpu.make_async_copy(y1_sc, y1_sc, sems.at[1]).wait()
    route = route_ref[...]
    w0 = route[:, 2:3]
    w1 = route[:, 3:4]
    half = D_MODEL // 2
    for c in range(ROW_SLABS):
        hi0, lo0 = _unpack_bf16_pairs(y0_sc[pl.ds(c, TM_COMBINE, stride=ROW_SLABS), :])
        hi1, lo1 = _unpack_bf16_pairs(y1_sc[pl.ds(c, TM_COMBINE, stride=ROW_SLABS), :])
        ch = slice(c * LANES, (c + 1) * LANES)
        cl = slice(half + c * LANES, half + (c + 1) * LANES)
        out_ref[:, ch] = x1_ref[:, ch] + (w0 * hi0 + w1 * hi1)
        out_ref[:, cl] = x1_ref[:, cl] + (w0 * lo0 + w1 * lo1)


def _combine(dest3, route, x1, ys):
    t = x1.shape[0]
    n = t // TM_COMBINE
    return pl.pallas_call(
        _combine_kernel,
        grid=(n,),
        in_specs=[pl.BlockSpec((1, 1, 2 * TM_COMBINE), lambda i: (i, 0, 0), memory_space=pltpu.SMEM),
                  pl.BlockSpec((TM_COMBINE, LANES), lambda i: (i, 0)),
                  pl.BlockSpec((TM_COMBINE, D_MODEL), lambda i: (i, 0)),
                  pl.BlockSpec(memory_space=pl.ANY)],
        out_specs=pl.BlockSpec((TM_COMBINE, D_MODEL), lambda i: (i, 0)),
        out_shape=jax.ShapeDtypeStruct((t, D_MODEL), jnp.float32),
        scratch_shapes=[pltpu.VMEM((TM_COMBINE * ROW_SLABS, LANES), jnp.uint32),
                        pltpu.VMEM((TM_COMBINE * ROW_SLABS, LANES), jnp.uint32),
                        pltpu.SemaphoreType.DMA((2,))],
        compiler_params=_cparams(("arbitrary",)),
        name="combine",
    )(dest3, route, x1, ys)


def _t5_bucket(rel):
    nb = T5_BUCKETS // 2
    max_exact = nb // 2
    side = jnp.where(rel > 0, nb, 0)
    n = jnp.abs(rel)
    nf = jnp.maximum(n, 1).astype(jnp.float32)
    large = max_exact + (jnp.log(nf / max_exact) / math.log(T5_MAX_DIST / max_exact)
                         * (nb - max_exact)).astype(jnp.int32)
    large = jnp.minimum(large, nb - 1)
    return side + jnp.where(n < max_exact, n, large)


def _rel_offsets(j, transposed=False):
    i = jnp.arange(TQ + TK)
    if transposed:
        return jnp.where(i < TQ, -i, (TQ + TK) - i) - j * TK
    return jnp.where(i < TK, i, i - (TQ + TK)) - j * TK


def _diff_bias_vecs(t5_table):
    offs = jnp.stack([_rel_offsets(j, transposed) for transposed in (False, True) for j in range(2)])
    vals = jnp.transpose(t5_table[_t5_bucket(offs)].astype(jnp.float32), (2, 0, 1))
    far = t5_table[_t5_bucket(jnp.array(-(TK + 1)))].astype(jnp.float32)
    return vals[:, 0:2], vals[:, 2:4], far


def _band_bias_vecs(rel_table):
    i = jnp.arange(BAND_HQ + BAND_W)
    rel = jnp.where(i < BAND_W, i, i - (BAND_HQ + BAND_W)) - (BAND_W - BAND_HQ)
    vecs = rel_table[jnp.clip(rel, -B_MAX_REL, B_MAX_REL) + B_MAX_REL].astype(jnp.float32).T
    return vecs.reshape(B_HEADS // 2, 2, BAND_HQ + BAND_W)


def kernel(x, norm1_g, w_in, a_qnorm_g, a_knorm_g, a_lambda, a_subln_g, t5_table, b_qnorm_g, b_knorm_g,
           b_rel_table, w_branch_a, w_branch_b, w_out, norm2_g, w_router_group, b_router_group,
           w_router_expert, b_router_expert, w_gate, w_up, w_down):
    bsz, s_len, _ = x.shape
    n_tok = bsz * s_len
    f32, bf16 = jnp.float32, jnp.bfloat16
    assert s_len % TQ == 0 and TQ == TK and TQ % CHUNK == 0 and n_tok % TM_PROJ == 0
    assert TK >= T5_MAX_DIST and BAND_HQ % CHUNK == 0 and BAND_W % LANES == 0
    assert s_len % BAND_TQ == 0 and BAND_W - BAND_HQ <= BAND_TQ
    l = 0
    x2 = x.reshape(n_tok, D_MODEL)

    w_all = w_in[l].astype(bf16)
    qk = w_all[:, :1024].reshape(D_MODEL, 2, 2, A_HEADS, A_HEAD_DIM)
    w_qk = qk.transpose(0, 1, 3, 2, 4).reshape(D_MODEL, 1024)
    gn = jnp.stack([jnp.tile(a_qnorm_g[l] * (A_HEAD_DIM ** -0.5 * LOG2E), 8), jnp.tile(a_knorm_g[l], 8),
                    jnp.tile(b_qnorm_g[l] * (B_HEAD_DIM ** -0.5 * LOG2E), 8), jnp.tile(b_knorm_g[l], 8)]).astype(f32)
    gmat = jnp.asarray(np.kron(np.eye(4), np.ones((A_HEAD_DIM, A_HEAD_DIM))), dtype=bf16)

    qa, ka, va, qb, kb, vb, ga, gb = _proj(x2, norm1_g[l][None].astype(f32), w_qk, w_all, gn, gmat)

    lam_init = 0.8 - 0.6 * math.exp(-0.3 * l)
    lp = a_lambda[l].astype(f32)
    lam = jnp.exp(jnp.sum(lp[0] * lp[1])) - jnp.exp(jnp.sum(lp[2] * lp[3])) + lam_init
    bias_a, bias_at, far_a = _diff_bias_vecs(t5_table)
    bmax_a = jnp.maximum(jnp.max(bias_a, axis=(1, 2)), far_a)
    scal_a = (jnp.concatenate([lam[None], far_a, bmax_a]) * jnp.array([1.0] + [LOG2E] * (2 * A_HEADS))).astype(f32)
    gsub = (a_subln_g[l] * (1.0 - lam_init))[None].astype(f32)
    oa = _diff_attention(scal_a, qa, ka, va, bias_a * LOG2E, bias_at * LOG2E, gsub, bsz=bsz, s_len=s_len)

    bias_b = _band_bias_vecs(b_rel_table[l])
    ob = _band_attention(qb, kb, vb, bias_b * LOG2E, bsz=bsz, s_len=s_len)

    wr = jnp.zeros((D_MODEL, LANES), f32)
    wr = wr.at[:, :N_GROUPS].set(w_router_group[l]).at[:, N_GROUPS:N_GROUPS + N_EXPERTS].set(w_router_expert[l])
    wrh = wr.astype(bf16)
    wr2 = jnp.concatenate([wrh, (wr - wrh.astype(f32)).astype(bf16)], axis=1)
    br = jnp.zeros((1, LANES), f32)
    br = br.at[0, :N_GROUPS].set(b_router_group[l]).at[0, N_GROUPS:N_GROUPS + N_EXPERTS].set(b_router_expert[l])
    x1, hn, route, cnt = _post(oa, ob, ga, gb, x2, w_branch_a[l].astype(bf16), w_branch_b[l].astype(bf16),
                               w_out[l].astype(bf16), norm2_g[l][None].astype(f32), wr2, br)

    ltri = jnp.asarray(np.tril(np.ones((TM_RANK, TM_RANK)), -1), dtype=bf16)
    utri = jnp.asarray(np.triu(np.ones((LANES, LANES)), 1), dtype=bf16)
    dest = _rank(route, cnt, ltri, utri)

    counts = cnt[0, :N_EXPERTS].astype(jnp.int32)
    n_rows = n_tok * TOP_K
    seg = _segments(counts, n_rows)

    dest2 = dest[:, :TOP_K].astype(jnp.int32)
    xs = _dispatch(dest2.reshape(n_tok // TM_DISPATCH, 1, TOP_K * TM_DISPATCH), hn, n_rows)
    ys = _ffn(seg, xs.reshape(n_rows * ROW_SLABS, LANES), w_gate[l], w_up[l], w_down[l])
    out = _combine(dest2.reshape(n_tok // TM_COMBINE, 1, TOP_K * TM_COMBINE), route, x1,
                   ys.reshape(n_rows, ROW_SLABS, LANES))
    return out.reshape(bsz, s_len, D_MODEL)
```

```python
import functools
import math

import jax
import jax.numpy as jnp
import numpy as np
from jax import lax
from jax.experimental import pallas as pl
from jax.experimental.pallas import tpu as pltpu

D_MODEL = 1024
CHUNK = 64
A_HEADS = 4
A_HEAD_DIM = 64
A_VDIM = 2 * A_HEAD_DIM
B_HEADS = 8
B_HEAD_DIM = 64
B_LEFT_CHUNKS = 8
B_MAX_REL = 128
T5_BUCKETS = 32
T5_MAX_DIST = 128
N_GROUPS = 4
EXPERTS_PER_GROUP = 8
N_EXPERTS = N_GROUPS * EXPERTS_PER_GROUP
TOP_K = 2
D_EXPERT = 512
EPS = 1e-6
NEG = -1e30
LOG2E = 1.0 / math.log(2.0)
DENOM_FLOOR = 2.0 ** -100

CHUNK_SHIFT = CHUNK.bit_length() - 1
assert 1 << CHUNK_SHIFT == CHUNK
LANES = 128
A_W = A_HEADS * 2 * A_HEAD_DIM
B_W = B_HEADS * B_HEAD_DIM
PROJ_W = 4 * 256 + 4 * 512 + 2 * D_MODEL

TM_PROJ = 512
TQ = 512
TK = 512
FAR_UNROLL = 4
TM_POST = 1024
TM_RANK = 1024
TM_DISPATCH = 4096
TM_COMBINE = 2048
ROW_UNROLL = 8
FFN_BLK = 1024
FFN_SUB = 256
ROW_SLABS = D_MODEL // 2 // LANES
VMEM_LIMIT = 56 * 1024 * 1024


def _cparams(sem):
    return pltpu.CompilerParams(dimension_semantics=sem, vmem_limit_bytes=VMEM_LIMIT)


def _pack_bf16_pairs(x):
    bits = lax.bitcast_convert_type(x.astype(jnp.bfloat16).astype(jnp.float32), jnp.uint32)
    half = x.shape[1] // 2
    return (bits[:, :half] & jnp.uint32(0xFFFF0000)) | lax.shift_right_logical(bits[:, half:], jnp.uint32(16))


def _unpack_bf16_pairs(pk):
    return (lax.bitcast_convert_type(pk & jnp.uint32(0xFFFF0000), jnp.float32),
            lax.bitcast_convert_type(lax.shift_left(pk, jnp.uint32(16)), jnp.float32))


def _proj_kernel(x_ref, g1_ref, wqk_ref, w_ref, gn_ref, gmat_ref,
                 qa_ref, ka_ref, va_ref, qb_ref, kb_ref, vb_ref, ga_ref, gb_ref):
    x = x_ref[...]
    xn = x * lax.rsqrt(jnp.mean(x * x, axis=-1, keepdims=True) + EPS) * g1_ref[...]
    xn = xn.astype(jnp.bfloat16)
    n_qk = wqk_ref.shape[1]

    def slab(c0, width):
        w = wqk_ref[:, c0:c0 + width] if c0 < n_qk else w_ref[:, c0:c0 + width]
        return jnp.dot(xn, w, preferred_element_type=jnp.float32)

    def headnorm(y, gi):
        sq = (y * y).astype(jnp.bfloat16)
        half = gmat_ref.shape[0]
        ss = jnp.concatenate([jnp.dot(sq[:, c:c + half], gmat_ref[...], preferred_element_type=jnp.float32)
                              for c in range(0, y.shape[1], half)], axis=1)
        return y * lax.rsqrt(ss * (1.0 / A_HEAD_DIM) + EPS) * gn_ref[gi:gi + 1, :]

    qa_ref[...] = headnorm(slab(0, 512), 0).astype(jnp.bfloat16)
    ka_ref[...] = headnorm(slab(512, 512), 1).astype(jnp.bfloat16)
    va_ref[...] = slab(1024, 512).astype(jnp.bfloat16)
    qb_ref[...] = headnorm(slab(1536, 512), 2).astype(jnp.bfloat16)
    kb_ref[...] = headnorm(slab(2048, 512), 3).astype(jnp.bfloat16)
    vb_ref[...] = slab(2560, 512).astype(jnp.bfloat16)
    for j in range(2):
        ga_ref[:, j * 512:(j + 1) * 512] = jax.nn.sigmoid(slab(3072 + j * 512, 512)).astype(jnp.bfloat16)
        gb_ref[:, j * 512:(j + 1) * 512] = jax.nn.sigmoid(slab(4096 + j * 512, 512)).astype(jnp.bfloat16)


def _proj(x2, g1, w_qk, w_all, gn, gmat):
    t = x2.shape[0]
    n = t // TM_PROJ
    row = lambda w: pl.BlockSpec((TM_PROJ, w), lambda i: (i, 0))
    full = lambda a: pl.BlockSpec(a.shape, lambda i: (0,) * a.ndim)
    outs = [jax.ShapeDtypeStruct((t, 512), jnp.bfloat16)] * 6 + [jax.ShapeDtypeStruct((t, D_MODEL), jnp.bfloat16)] * 2
    return pl.pallas_call(
        _proj_kernel,
        grid=(n,),
        in_specs=[row(D_MODEL), full(g1), full(w_qk), full(w_all), full(gn), full(gmat)],
        out_specs=[row(512)] * 6 + [row(D_MODEL)] * 2,
        out_shape=outs,
        compiler_params=_cparams(("arbitrary",)),
        name="proj",
    )(x2, g1, w_qk, w_all, gn, gmat)


def _diff_kernel(scal_ref, q_ref, k_ref, v_ref, vec_ref, vect_ref, gsub_ref, o_ref,
                 bias_sc, biast_sc, vt_sc, kmaxt_sc, qz_sc, qzt_sc, refn_sc, reff_sc, l_sc, acc_sc,
                 m_sc, acc2_sc, mf_sc, accf_sc, *, n_near):
    f32 = jnp.float32
    h = pl.program_id(1)
    qi = pl.program_id(2)
    nc = TK // LANES
    r2 = 2 * TQ
    n_heads = pl.num_programs(1)

    def near_span(j):
        return slice((n_near - 1 - j) * TK, (n_near - j) * TK)

    @pl.when(qi == 0)
    def _():
        row = lax.broadcasted_iota(jnp.int32, (TK, TQ), 0)
        col = lax.broadcasted_iota(jnp.int32, (TK, TQ), 1)
        for j in range(n_near):
            ok = lax.shift_right_arithmetic(col, CHUNK_SHIFT) >= lax.shift_right_arithmetic(row - j * TK, CHUNK_SHIFT)
            vect = jnp.broadcast_to(vect_ref[0, j:j + 1, :], (TK, TQ + TK))
            tilet = jnp.where(ok, pltpu.roll(vect, 0, 1, stride=1, stride_axis=0)[:, :TQ], NEG)
            for half in range(2):
                biast_sc[near_span(j), half * TQ:(half + 1) * TQ] = tilet
        vt_sc[...] = v_ref[...].T
        kf = k_ref[...].astype(f32)
        ksq = kf * kf
        klane = lax.broadcasted_iota(jnp.int32, ksq.shape, 1)
        for half in range(2):
            sel = (klane < 64) if half == 0 else (klane >= 64)
            norm2 = jnp.max(jnp.sum(jnp.where(sel, ksq, 0.0), axis=-1, keepdims=True), axis=0, keepdims=True)
            kmaxt_sc[:, half * TQ:(half + 1) * TQ] = jnp.broadcast_to(jnp.sqrt(norm2), (8, TQ))

    q = q_ref[...]
    lane = lax.broadcasted_iota(jnp.int32, q.shape, 1)
    zero = jnp.zeros_like(q)
    q1 = jnp.where(lane < 64, q, zero)
    q2 = jnp.where(lane >= 64, q, zero)
    qzt_sc[:, 0:TQ] = q1.T
    qzt_sc[:, TQ:r2] = q2.T
    cfar = scal_ref[1 + h]
    bias_max = scal_ref[1 + n_heads + h]
    n_far = jnp.maximum(qi - (n_near - 1), 0)

    def key_block(kblk):
        return k_ref[pl.ds(pl.multiple_of(kblk * TK, TK), TK), :]

    def finish(od):
        od = od * lax.rsqrt(jnp.mean(od * od, axis=-1, keepdims=True) + EPS) * gsub_ref[...]
        o_ref[...] = od.astype(o_ref.dtype)

    def fixed_reference_pass():
        qf = qzt_sc[...].astype(f32)
        qnorm = jnp.sqrt(jnp.sum(qf * qf, axis=0, keepdims=True))
        refn_sc[...] = qnorm * kmaxt_sc[...] + bias_max
        l_sc[...] = jnp.zeros(l_sc.shape, f32)
        acc_sc[...] = jnp.zeros(acc_sc.shape, f32)

        def accumulate(st, kblk, ref_ref):
            p = jnp.exp2(st.reshape(TK // 8, 8, r2) - ref_ref[...][None])
            l_sc[...] += jnp.sum(p, axis=0)
            vt = vt_sc[:, pl.ds(pl.multiple_of(kblk * TK, TK), TK)]
            acc_sc[...] += jnp.dot(vt, p.reshape(TK, r2).astype(jnp.bfloat16), preferred_element_type=f32)

        def scores_t(kblk):
            return jnp.dot(key_block(kblk), qzt_sc[...], preferred_element_type=f32)

        for j in range(n_near):
            def near(j=j):
                accumulate(scores_t(qi - j) + biast_sc[near_span(j), :], qi - j, refn_sc)
            if j == 0:
                near()
            else:
                pl.when(qi >= j)(near)

        reff_sc[...] = refn_sc[...] - cfar

        def far(kblk, carry):
            accumulate(scores_t(kblk), kblk, reff_sc)
            return carry

        def far_run(first, count, width):
            def body(i, carry):
                for u in range(width):
                    far(first + width * i + u, carry)
                return carry
            lax.fori_loop(0, count, body, 0)
            return first + count * width

        nxt = far_run(0, n_far // FAR_UNROLL, FAR_UNROLL)
        nxt = far_run(nxt, (n_far - nxt) // 2, 2)
        far_run(nxt, n_far - nxt, 1)

        denom = jnp.sum(l_sc[...], axis=0, keepdims=True)
        smallest = jnp.min(denom)
        ot = acc_sc[...] / denom
        finish((ot[:, 0:TQ] - scal_ref[0] * ot[:, TQ:r2]).T)
        return smallest

    def exact_pass():
        qz_sc[0:TQ, :] = q1
        qz_sc[TQ:r2, :] = q2
        row = lax.broadcasted_iota(jnp.int32, (TQ, TK), 0)
        col = lax.broadcasted_iota(jnp.int32, (TQ, TK), 1)
        for j in range(n_near):
            ok = lax.shift_right_arithmetic(row, CHUNK_SHIFT) >= lax.shift_right_arithmetic(col - j * TK, CHUNK_SHIFT)
            vec = jnp.broadcast_to(vec_ref[0, j:j + 1, :], (TQ, TQ + TK))
            tile = jnp.where(ok, pltpu.roll(vec, 0, 1, stride=1, stride_axis=0)[:, :TK], NEG)
            for half in range(2):
                bias_sc[half * TQ:(half + 1) * TQ, near_span(j)] = tile

        def scores(kblk):
            return lax.dot_general(qz_sc[...], key_block(kblk), (((1,), (1,)), ((), ())), preferred_element_type=f32)

        def update(s, kblk, m_ref, a_ref):
            cols = [s[:, c * LANES:(c + 1) * LANES] for c in range(nc)]
            m_old = m_ref[...]
            m_new = jnp.maximum(m_old, jnp.max(functools.reduce(jnp.maximum, cols), axis=-1, keepdims=True))
            alpha = jnp.exp2(m_old - m_new)
            ps = [jnp.exp2(c - m_new) for c in cols]
            p = jnp.concatenate([x.astype(jnp.bfloat16) for x in ps], axis=1)
            v = v_ref[pl.ds(pl.multiple_of(kblk * TK, TK), TK), :]
            a_ref[:, 0:LANES] = alpha * a_ref[:, 0:LANES] + jnp.dot(p, v, preferred_element_type=f32)
            a_ref[:, LANES:2 * LANES] = alpha * a_ref[:, LANES:2 * LANES] + functools.reduce(jnp.add, ps)
            m_ref[...] = m_new

        m_sc[...] = jnp.full(m_sc.shape, NEG, f32)
        acc2_sc[...] = jnp.zeros(acc2_sc.shape, f32)
        for j in range(n_near):
            def near(j=j):
                update(scores(qi - j) + bias_sc[:, near_span(j)], qi - j, m_sc, acc2_sc)
            if j == 0:
                near()
            else:
                pl.when(qi >= j)(near)

        mf_sc[...] = jnp.full(mf_sc.shape, NEG, f32)
        accf_sc[...] = jnp.zeros(accf_sc.shape, f32)

        def far(kblk, carry):
            update(scores(kblk), kblk, mf_sc, accf_sc)
            return carry

        lax.fori_loop(0, n_far, far, 0)
        mf = mf_sc[...] + cfar
        mn = m_sc[...]
        m = jnp.maximum(mf, mn)
        wf = jnp.exp2(mf - m)
        wn = jnp.exp2(mn - m)
        tot = (jnp.concatenate([wf, wf], axis=1) * accf_sc[...]
               + jnp.concatenate([wn, wn], axis=1) * acc2_sc[...])
        o = tot[:, 0:LANES] / jnp.sum(tot[:, LANES:2 * LANES], axis=-1, keepdims=True)
        finish(o[:TQ] - scal_ref[0] * o[TQ:])

    smallest = fixed_reference_pass()
    pl.when(jnp.logical_not(smallest >= DENOM_FLOOR))(exact_pass)


def _diff_attention(scal, q, k, v, vecs, vecs_t, gsub, *, bsz, s_len):
    n_blk = q.shape[1] // LANES
    nq = s_len // TQ
    n_near = vecs.shape[1]
    assert vecs.shape == vecs_t.shape == (n_blk, n_near, TQ + TK)
    kern = functools.partial(_diff_kernel, n_near=n_near)
    f32, bf16 = jnp.float32, jnp.bfloat16
    vm = pltpu.VMEM
    vec_spec = pl.BlockSpec((1, n_near, TQ + TK), lambda b, h, i: (h, 0, 0))
    return pl.pallas_call(
        kern,
        grid=(bsz, n_blk, nq),
        in_specs=[
            pl.BlockSpec(memory_space=pltpu.SMEM),
            pl.BlockSpec((TQ, LANES), lambda b, h, i: (b * nq + i, h)),
            pl.BlockSpec((s_len, LANES), lambda b, h, i: (b, h)),
            pl.BlockSpec((s_len, LANES), lambda b, h, i: (b, h)),
            vec_spec, vec_spec,
            pl.BlockSpec((1, LANES), lambda b, h, i: (0, 0)),
        ],
        out_specs=pl.BlockSpec((TQ, LANES), lambda b, h, i: (b * nq + i, h)),
        out_shape=jax.ShapeDtypeStruct(q.shape, bf16),
        scratch_shapes=[vm((2 * TQ, n_near * TK), f32), vm((n_near * TK, 2 * TQ), f32),
                        vm((LANES, s_len), bf16), vm((8, 2 * TQ), f32),
                        vm((2 * TQ, LANES), bf16), vm((LANES, 2 * TQ), bf16),
                        vm((8, 2 * TQ), f32), vm((8, 2 * TQ), f32), vm((8, 2 * TQ), f32),
                        vm((LANES, 2 * TQ), f32),
                        vm((2 * TQ, LANES), f32), vm((2 * TQ, 2 * LANES), f32),
                        vm((2 * TQ, LANES), f32), vm((2 * TQ, 2 * LANES), f32)],
        compiler_params=_cparams(("arbitrary", "arbitrary", "arbitrary")),
        name="attn_diff",
    )(scal, q, k, v, vecs, vecs_t, gsub)


BAND_TQ = 1024
BAND_HQ = 256
BAND_SUBS = BAND_TQ // BAND_HQ
BAND_W = B_LEFT_CHUNKS * CHUNK + BAND_HQ


def _band_kernel(q_ref, k_ref, v_ref, vec_ref, o_ref, bias_sc, qz_sc):
    f32 = jnp.float32
    qi = pl.program_id(2)
    hq, w = BAND_HQ, BAND_W
    back = w - hq

    @pl.when(qi == 0)
    def _():
        qchunk = lax.shift_right_arithmetic(lax.broadcasted_iota(jnp.int32, (hq, w), 0), CHUNK_SHIFT)
        kchunk = lax.shift_right_arithmetic(lax.broadcasted_iota(jnp.int32, (hq, w), 1) - back, CHUNK_SHIFT)
        dchunk = qchunk - kchunk
        allowed = (dchunk >= 0) & (dchunk <= B_LEFT_CHUNKS)
        for half in range(2):
            vec = jnp.broadcast_to(vec_ref[0, half:half + 1, :], (hq, hq + w))
            tile = pltpu.roll(vec, 0, 1, stride=1, stride_axis=0)[:, :w]
            bias_sc[half * hq:(half + 1) * hq, :] = jnp.where(allowed, tile, NEG)

    q = q_ref[...]
    lane = lax.broadcasted_iota(jnp.int32, (hq, LANES), 1)
    zero = jnp.zeros((hq, LANES), q.dtype)
    for u in range(BAND_SUBS):
        qu = q[u * hq:(u + 1) * hq, :]
        qz_sc[(2 * u) * hq:(2 * u + 1) * hq, :] = jnp.where(lane < 64, qu, zero)
        qz_sc[(2 * u + 1) * hq:(2 * u + 2) * hq, :] = jnp.where(lane >= 64, qu, zero)

    def scores(u, k0, n_keys):
        k = k_ref[pl.ds(k0, n_keys), :]
        return lax.dot_general(qz_sc[2 * u * hq:(2 * u + 2) * hq, :], k, (((1,), (1,)), ((), ())),
                               preferred_element_type=f32) + bias_sc[:, w - n_keys:w]

    def softmax(s):
        cols = [s[:, c * LANES:(c + 1) * LANES] for c in range(s.shape[1] // LANES)]
        m = jnp.max(functools.reduce(jnp.maximum, cols), axis=-1, keepdims=True)
        ps = [jnp.exp2(c - m) for c in cols]
        denom = jnp.sum(functools.reduce(jnp.add, ps), axis=-1, keepdims=True)
        return jnp.concatenate([x.astype(jnp.bfloat16) for x in ps], axis=1), denom

    def finish(u, p, denom, k0):
        o = jnp.dot(p, v_ref[pl.ds(k0, p.shape[1]), :], preferred_element_type=f32) / denom
        o_ref[u * hq:(u + 1) * hq, :] = jnp.where(lane < 64, o[:hq], o[hq:]).astype(o_ref.dtype)

    def step(k0s, n_keys):
        ss = [scores(u, k0s[u], n_keys[u]) for u in range(BAND_SUBS)]
        pd = [softmax(x) for x in ss]
        for u in range(BAND_SUBS):
            finish(u, pd[u][0], pd[u][1], k0s[u])

    first = tuple(min((u + 1) * hq, w) for u in range(BAND_SUBS))

    @pl.when(qi == 0)
    def _():
        step(tuple(max((u + 1) * hq - w, 0) for u in range(BAND_SUBS)), first)

    @pl.when(qi > 0)
    def _():
        start = pl.multiple_of(qi * BAND_TQ - back, hq)
        step(tuple(pl.multiple_of(start + u * hq, hq) for u in range(BAND_SUBS)), (w,) * BAND_SUBS)


def _band_attention(q, k, v, vecs, *, bsz, s_len):
    n_blk = q.shape[1] // LANES
    nq = s_len // BAND_TQ
    assert vecs.shape == (n_blk, 2, BAND_HQ + BAND_W)
    return pl.pallas_call(
        _band_kernel,
        grid=(bsz, n_blk, nq),
        in_specs=[
            pl.BlockSpec((BAND_TQ, LANES), lambda b, h, i: (b * nq + i, h)),
            pl.BlockSpec((s_len, LANES), lambda b, h, i: (b, h)),
            pl.BlockSpec((s_len, LANES), lambda b, h, i: (b, h)),
            pl.BlockSpec((1, 2, BAND_HQ + BAND_W), lambda b, h, i: (h, 0, 0)),
        ],
        out_specs=pl.BlockSpec((BAND_TQ, LANES), lambda b, h, i: (b * nq + i, h)),
        out_shape=jax.ShapeDtypeStruct(q.shape, jnp.bfloat16),
        scratch_shapes=[pltpu.VMEM((2 * BAND_HQ, BAND_W), jnp.float32),
                        pltpu.VMEM((2 * BAND_TQ, LANES), jnp.bfloat16)],
        compiler_params=_cparams(("arbitrary", "arbitrary", "arbitrary")),
        name="attn_band",
    )(q, k, v, vecs)


def _post_kernel(oa_ref, ob_ref, ga_ref, gb_ref, x_ref, wa_ref, wb_ref, wo_ref, g2_ref,
                 wr2_ref, br_ref, x1_ref, hn_ref, route_ref, cnt_ref):
    f32 = jnp.float32
    ya = jnp.dot(oa_ref[...], wa_ref[...], preferred_element_type=f32)
    yb = jnp.dot(ob_ref[...], wb_ref[...], preferred_element_type=f32)
    mixed = ga_ref[...].astype(f32) * ya + gb_ref[...].astype(f32) * yb
    x1 = x_ref[...] + jnp.dot(mixed.astype(jnp.bfloat16), wo_ref[...], preferred_element_type=f32)
    x1_ref[...] = x1
    hn = x1 * lax.rsqrt(jnp.mean(x1 * x1, axis=-1, keepdims=True) + EPS) * g2_ref[...]
    hh = hn.astype(jnp.bfloat16)
    packed = _pack_bf16_pairs(hn)
    for c in range(ROW_SLABS):
        hn_ref[pl.ds(c, TM_POST, stride=ROW_SLABS), :] = packed[:, c * LANES:(c + 1) * LANES]

    hl = (hn - hh.astype(f32)).astype(jnp.bfloat16)
    hw = jnp.dot(hh, wr2_ref[...], preferred_element_type=f32)
    lg = (hw[:, 0:LANES] + hw[:, LANES:2 * LANES]
          + jnp.dot(hl, wr2_ref[:, 0:LANES], preferred_element_type=f32)) + br_ref[...]

    lanei = lax.broadcasted_iota(jnp.int32, lg.shape, 1)
    lanef = lanei.astype(f32)
    big = 999.0
    gmask = lanei < N_GROUPS
    gl = jnp.where(gmask, lg, NEG)
    gm = jnp.max(gl, axis=-1, keepdims=True)
    ge = jnp.where(gmask, jnp.exp(gl - gm), 0.0)
    gp = ge / jnp.sum(ge, axis=-1, keepdims=True)
    p_g = jnp.max(gp, axis=-1, keepdims=True)
    gidx = jnp.min(jnp.where(gmask & (gp == p_g), lanef, big), axis=-1, keepdims=True)
    egrp = lax.shift_right_arithmetic(lanei - N_GROUPS, 3).astype(f32)
    emask = (lanei >= N_GROUPS) & (lanei < N_GROUPS + N_EXPERTS) & (egrp == gidx)
    el = jnp.where(emask, lg, NEG)
    v1 = jnp.max(el, axis=-1, keepdims=True)
    i1 = jnp.min(jnp.where(emask & (el == v1), lanef, big), axis=-1, keepdims=True)
    emask2 = emask & (lanef != i1)
    el2 = jnp.where(emask2, lg, NEG)
    v2 = jnp.max(el2, axis=-1, keepdims=True)
    i2 = jnp.min(jnp.where(emask2 & (el2 == v2), lanef, big), axis=-1, keepdims=True)
    t = jnp.exp(v2 - v1)
    den = 1.0 + t
    w1 = p_g * (1.0 / den)
    w2 = p_g * (t / den)
    route = jnp.where(lanei == 0, i1 - N_GROUPS,
                      jnp.where(lanei == 1, i2 - N_GROUPS,
                                jnp.where(lanei == 2, w1, jnp.where(lanei == 3, w2, 0.0))))
    route_ref[...] = route

    @pl.when(pl.program_id(0) == 0)
    def _():
        cnt_ref[...] = jnp.zeros(cnt_ref.shape, f32)

    chosen = ((lanef == i1 - N_GROUPS) | (lanef == i2 - N_GROUPS)).astype(f32)
    cnt_ref[...] += jnp.broadcast_to(jnp.sum(chosen, axis=0, keepdims=True), cnt_ref.shape)


def _post(oa, ob, ga, gb, x2, wa, wb, wo, g2, wr2, br):
    t = x2.shape[0]
    n = t // TM_POST
    row = lambda w: pl.BlockSpec((TM_POST, w), lambda i: (i, 0))
    full = lambda a: pl.BlockSpec(a.shape, lambda i: (0,) * a.ndim)
    return pl.pallas_call(
        _post_kernel,
        grid=(n,),
        in_specs=[row(512), row(512), row(D_MODEL), row(D_MODEL), row(D_MODEL),
                  full(wa), full(wb), full(wo), full(g2), full(wr2), full(br)],
        out_specs=[row(D_MODEL), pl.BlockSpec((TM_POST * ROW_SLABS, LANES), lambda i: (i, 0)), row(LANES),
                   pl.BlockSpec((8, LANES), lambda i: (0, 0))],
        out_shape=[jax.ShapeDtypeStruct((t, D_MODEL), jnp.float32),
                   jax.ShapeDtypeStruct((t * ROW_SLABS, LANES), jnp.uint32),
                   jax.ShapeDtypeStruct((t, LANES), jnp.float32),
                   jax.ShapeDtypeStruct((8, LANES), jnp.float32)],
        compiler_params=_cparams(("arbitrary",)),
        name="post",
    )(oa, ob, ga, gb, x2, wa, wb, wo, g2, wr2, br)


def _rank_kernel(route_ref, cnt_ref, ltri_ref, utri_ref, dest_ref, pstart_sc, base_sc):
    f32 = jnp.float32
    i = pl.program_id(0)
    route = route_ref[...]
    lanef = lax.broadcasted_iota(jnp.int32, route.shape, 1).astype(f32)
    oh1 = (lanef == route[:, 0:1]).astype(f32)
    oh2 = (lanef == route[:, 1:2]).astype(f32)
    both = oh1 + oh2

    @pl.when(i == 0)
    def _():
        cnt = cnt_ref[0:1, :]
        chi = jnp.floor(cnt * (1.0 / 256.0))
        clo = cnt - chi * 256.0
        split = jnp.concatenate([jnp.broadcast_to(chi, (8, LANES)), jnp.broadcast_to(clo, (8, LANES))], axis=0)
        excl = jnp.dot(split.astype(jnp.bfloat16), utri_ref[...], preferred_element_type=f32)
        pstart_sc[...] = excl[0:1] * 256.0 + excl[8:9]
        base_sc[...] = jnp.zeros(base_sc.shape, f32)

    prior = jnp.dot(ltri_ref[...], both.astype(jnp.bfloat16), preferred_element_type=f32)
    slot = prior + base_sc[...] + pstart_sc[...]
    d1 = jnp.sum(oh1 * slot, axis=-1, keepdims=True)
    d2 = jnp.sum(oh2 * slot, axis=-1, keepdims=True)
    dest_ref[...] = jnp.where(lanef == 0.0, d1, jnp.where(lanef == 1.0, d2, 0.0))
    base_sc[...] += jnp.sum(both, axis=0, keepdims=True)


def _rank(route, cnt, ltri, utri):
    t = route.shape[0]
    n = t // TM_RANK
    full = lambda a: pl.BlockSpec(a.shape, lambda i: (0,) * a.ndim)
    row1 = lambda: pltpu.VMEM((1, LANES), jnp.float32)
    return pl.pallas_call(
        _rank_kernel,
        grid=(n,),
        in_specs=[pl.BlockSpec((TM_RANK, LANES), lambda i: (i, 0)), full(cnt), full(ltri), full(utri)],
        out_specs=pl.BlockSpec((TM_RANK, LANES), lambda i: (i, 0)),
        out_shape=jax.ShapeDtypeStruct((t, LANES), jnp.float32),
        scratch_shapes=[row1(), row1()],
        compiler_params=_cparams(("arbitrary",)),
        name="rank",
    )(route, cnt, ltri, utri)


def _dispatch_kernel(dest_ref, hn_ref, xs_ref, sems):
    def row_copy(r, k):
        d = dest_ref[0, 0, 2 * r + k]
        src = hn_ref.at[pl.ds(pl.multiple_of(r * ROW_SLABS, ROW_SLABS), ROW_SLABS), :]
        return pltpu.make_async_copy(src, xs_ref.at[d], sems.at[k])

    def issue(r, c):
        row_copy(r, 0).start(priority=0)
        row_copy(r, 1).start(priority=1)
        return c

    lax.fori_loop(0, TM_DISPATCH, issue, 0, unroll=ROW_UNROLL)
    for k in range(TOP_K):
        pltpu.make_async_copy(hn_ref, hn_ref, sems.at[k]).wait()


def _dispatch(dest3, hn, n_rows):
    t = hn.shape[0] // ROW_SLABS
    n = t // TM_DISPATCH
    return pl.pallas_call(
        _dispatch_kernel,
        grid=(n,),
        in_specs=[pl.BlockSpec((1, 1, 2 * TM_DISPATCH), lambda i: (i, 0, 0), memory_space=pltpu.SMEM),
                  pl.BlockSpec((TM_DISPATCH * ROW_SLABS, LANES), lambda i: (i, 0))],
        out_specs=pl.BlockSpec(memory_space=pl.ANY),
        out_shape=jax.ShapeDtypeStruct((n_rows, ROW_SLABS, LANES), hn.dtype),
        scratch_shapes=[pltpu.SemaphoreType.DMA((2,))],
        compiler_params=_cparams(("arbitrary",)),
        name="dispatch",
    )(dest3, hn)


def _ffn_kernel(tile_ref, exp_ref, lo_ref, hi_ref, cast_ref, init_ref,
                xs_ref, wg_ref, wu_ref, wd_ref, ys_ref, wg_sc, wu_sc, wd_sc):
    v = pl.program_id(0)
    lo = lo_ref[v]
    hi = hi_ref[v]

    @pl.when(init_ref[v] == 1)
    def _():
        ys_ref[...] = jnp.zeros(ys_ref.shape, ys_ref.dtype)

    @pl.when((hi > lo) & (cast_ref[v] == 1))
    def _():
        wg_sc[...] = wg_ref[0].astype(jnp.bfloat16)
        wu_sc[...] = wu_ref[0].astype(jnp.bfloat16)
        wd_sc[...] = wd_ref[0].astype(jnp.bfloat16)

    for piece in range(FFN_BLK // FFN_SUB):
        row0 = tile_ref[v] * FFN_BLK + piece * FFN_SUB

        @pl.when((hi > row0) & (lo < row0 + FFN_SUB))
        def _(piece=piece, row0=row0):
            def slab(ref, c):
                return ref.at[pl.ds(piece * FFN_SUB * ROW_SLABS + c, FFN_SUB, stride=ROW_SLABS), :]

            pk = jnp.concatenate([slab(xs_ref, c)[...] for c in range(ROW_SLABS)], axis=1)
            x = jnp.concatenate(_unpack_bf16_pairs(pk), axis=1).astype(jnp.bfloat16)
            g = jnp.dot(x, wg_sc[...], preferred_element_type=jnp.float32)
            u = jnp.dot(x, wu_sc[...], preferred_element_type=jnp.float32)
            hb = (g * jax.nn.sigmoid(g) * u).astype(jnp.bfloat16)
            packed = _pack_bf16_pairs(jnp.dot(hb, wd_sc[...], preferred_element_type=jnp.float32))
            rows = row0 + lax.broadcasted_iota(jnp.int32, (FFN_SUB, LANES), 0)
            mine = (rows >= lo) & (rows < hi)
            for c in range(ROW_SLABS):
                out = slab(ys_ref, c)
                out[...] = jnp.where(mine, packed[:, c * LANES:(c + 1) * LANES], out[...])


def _ffn(seg, xs, w_gate, w_up, w_down):
    n_seg = seg[0].shape[0]
    grid_spec = pltpu.PrefetchScalarGridSpec(
        num_scalar_prefetch=6,
        grid=(n_seg,),
        in_specs=[
            pl.BlockSpec((FFN_BLK * ROW_SLABS, LANES), lambda v, t, e, *_: (t[v], 0)),
            pl.BlockSpec((1, D_MODEL, D_EXPERT), lambda v, t, e, *_: (e[v], 0, 0)),
            pl.BlockSpec((1, D_MODEL, D_EXPERT), lambda v, t, e, *_: (e[v], 0, 0)),
            pl.BlockSpec((1, D_EXPERT, D_MODEL), lambda v, t, e, *_: (e[v], 0, 0)),
        ],
        out_specs=pl.BlockSpec((FFN_BLK * ROW_SLABS, LANES), lambda v, t, e, *_: (t[v], 0)),
        scratch_shapes=[pltpu.VMEM((D_MODEL, D_EXPERT), jnp.bfloat16),
                        pltpu.VMEM((D_MODEL, D_EXPERT), jnp.bfloat16),
                        pltpu.VMEM((D_EXPERT, D_MODEL), jnp.bfloat16)],
    )
    return pl.pallas_call(
        _ffn_kernel,
        grid_spec=grid_spec,
        out_shape=jax.ShapeDtypeStruct(xs.shape, jnp.uint32),
        compiler_params=_cparams(("arbitrary",)),
        name="ffn",
    )(*seg, xs, w_gate, w_up, w_down)


def _segments(counts, n_rows):
    i32 = jnp.int32
    n_tiles = n_rows // FFN_BLK
    n_seg = n_tiles + N_EXPERTS
    tri = jnp.tril(jnp.ones((N_EXPERTS, N_EXPERTS), i32))
    ends = jnp.sum(tri * counts[None, :], axis=1)
    starts = ends - counts
    edges = jnp.arange(n_tiles, dtype=i32) * FFN_BLK
    rank_e = jnp.arange(n_tiles, dtype=i32) + jnp.sum(starts[None, :] <= edges[:, None], axis=1)
    rank_s = jnp.arange(N_EXPERTS, dtype=i32) + jnp.sum(edges[None, :] < starts[:, None], axis=1)
    seg = jnp.arange(n_seg, dtype=i32)
    lo = (jnp.sum(jnp.where(rank_e[None, :] == seg[:, None], edges[None, :], 0), axis=1)
          + jnp.sum(jnp.where(rank_s[None, :] == seg[:, None], starts[None, :], 0), axis=1))
    hi = jnp.concatenate([lo[1:], jnp.array([n_rows], i32)])
    valid = hi > lo
    tile = jnp.minimum(lo // FFN_BLK, n_tiles - 1)
    expert = jnp.minimum(jnp.sum(ends[None, :] <= lo[:, None], axis=1), N_EXPERTS - 1).astype(i32)
    upto = seg[None, :] <= seg[:, None]
    expert = jnp.max(jnp.where(upto & valid[None, :], expert[None, :], 0), axis=1)
    prev_expert = jnp.concatenate([jnp.array([-1], i32), expert[:-1]])
    first_valid = valid & (jnp.sum(jnp.where(upto & valid[None, :], 1, 0), axis=1) == 1)
    cast = valid & ((expert != prev_expert) | first_valid)
    prev_tile = jnp.concatenate([jnp.array([-1], i32), tile[:-1]])
    init = tile != prev_tile
    return (tile.astype(i32), expert, lo.astype(i32), hi.astype(i32), cast.astype(i32), init.astype(i32))


def _combine_kernel(dest_ref, route_ref, x1_ref, ys_ref, out_ref, y0_sc, y1_sc, sems):
    def row_copy(r, k):
        d = dest_ref[0, 0, 2 * r + k]
        dst = y0_sc if k == 0 else y1_sc
        return pltpu.make_async_copy(ys_ref.at[d], dst.at[pl.ds(pl.multiple_of(r * ROW_SLABS, ROW_SLABS), ROW_SLABS), :],
                                     sems.at[k])

    def issue(r, c):
        row_copy(r, 0).start(priority=0)
        row_copy(r, 1).start(priority=1)
        return c

    lax.fori_loop(0, TM_COMBINE, issue, 0, unroll=ROW_UNROLL)
    pltpu.make_async_copy(y0_sc, y0_sc, sems.at[0]).wait()
    pltpu.make_async_copy(y1_sc, y1_sc, sems.at[1]).wait()
    route = route_ref[...]
    w0 = route[:, 2:3]
    w1 = route[:, 3:4]
    half = D_MODEL // 2
    for c in range(ROW_SLABS):
        hi0, lo0 = _unpack_bf16_pairs(y0_sc[pl.ds(c, TM_COMBINE, stride=ROW_SLABS), :])
        hi1, lo1 = _unpack_bf16_pairs(y1_sc[pl.ds(c, TM_COMBINE, stride=ROW_SLABS), :])
        ch = slice(c * LANES, (c + 1) * LANES)
        cl = slice(half + c * LANES, half + (c + 1) * LANES)
        out_ref[:, ch] = x1_ref[:, ch] + (w0 * hi0 + w1 * hi1)
        out_ref[:, cl] = x1_ref[:, cl] + (w0 * lo0 + w1 * lo1)


def _combine(dest3, route, x1, ys):
    t = x1.shape[0]
    n = t // TM_COMBINE
    return pl.pallas_call(
        _combine_kernel,
        grid=(n,),
        in_specs=[pl.BlockSpec((1, 1, 2 * TM_COMBINE), lambda i: (i, 0, 0), memory_space=pltpu.SMEM),
                  pl.BlockSpec((TM_COMBINE, LANES), lambda i: (i, 0)),
                  pl.BlockSpec((TM_COMBINE, D_MODEL), lambda i: (i, 0)),
                  pl.BlockSpec(memory_space=pl.ANY)],
        out_specs=pl.BlockSpec((TM_COMBINE, D_MODEL), lambda i: (i, 0)),
        out_shape=jax.ShapeDtypeStruct((t, D_MODEL), jnp.float32),
        scratch_shapes=[pltpu.VMEM((TM_COMBINE * ROW_SLABS, LANES), jnp.uint32),
                        pltpu.VMEM((TM_COMBINE * ROW_SLABS, LANES), jnp.uint32),
                        pltpu.SemaphoreType.DMA((2,))],
        compiler_params=_cparams(("arbitrary",)),
        name="combine",
    )(dest3, route, x1, ys)


def _t5_bucket(rel):
    nb = T5_BUCKETS // 2
    max_exact = nb // 2
    side = jnp.where(rel > 0, nb, 0)
    n = jnp.abs(rel)
    nf = jnp.maximum(n, 1).astype(jnp.float32)
    large = max_exact + (jnp.log(nf / max_exact) / math.log(T5_MAX_DIST / max_exact)
                         * (nb - max_exact)).astype(jnp.int32)
    large = jnp.minimum(large, nb - 1)
    return side + jnp.where(n < max_exact, n, large)


def _rel_offsets(j, transposed=False):
    i = jnp.arange(TQ + TK)
    if transposed:
        return jnp.where(i < TQ, -i, (TQ + TK) - i) - j * TK
    return jnp.where(i < TK, i, i - (TQ + TK)) - j * TK


def _diff_bias_vecs(t5_table):
    offs = jnp.stack([_rel_offsets(j, transposed) for transposed in (False, True) for j in range(2)])
    vals = jnp.transpose(t5_table[_t5_bucket(offs)].astype(jnp.float32), (2, 0, 1))
    far = t5_table[_t5_bucket(jnp.array(-(TK + 1)))].astype(jnp.float32)
    return vals[:, 0:2], vals[:, 2:4], far


def _band_bias_vecs(rel_table):
    i = jnp.arange(BAND_HQ + BAND_W)
    rel = jnp.where(i < BAND_W, i, i - (BAND_HQ + BAND_W)) - (BAND_W - BAND_HQ)
    vecs = rel_table[jnp.clip(rel, -B_MAX_REL, B_MAX_REL) + B_MAX_REL].astype(jnp.float32).T
    return vecs.reshape(B_HEADS // 2, 2, BAND_HQ + BAND_W)


def kernel(x, norm1_g, w_in, a_qnorm_g, a_knorm_g, a_lambda, a_subln_g, t5_table, b_qnorm_g, b_knorm_g,
           b_rel_table, w_branch_a, w_branch_b, w_out, norm2_g, w_router_group, b_router_group,
           w_router_expert, b_router_expert, w_gate, w_up, w_down):
    bsz, s_len, _ = x.shape
    n_tok = bsz * s_len
    f32, bf16 = jnp.float32, jnp.bfloat16
    assert s_len % TQ == 0 and TQ == TK and TQ % CHUNK == 0 and n_tok % TM_PROJ == 0
    assert TK >= T5_MAX_DIST and BAND_HQ % CHUNK == 0 and BAND_W % LANES == 0
    assert s_len % BAND_TQ == 0 and BAND_W - BAND_HQ <= BAND_TQ
    l = 0
    x2 = x.reshape(n_tok, D_MODEL)

    w_all = w_in[l].astype(bf16)
    qk = w_all[:, :1024].reshape(D_MODEL, 2, 2, A_HEADS, A_HEAD_DIM)
    w_qk = qk.transpose(0, 1, 3, 2, 4).reshape(D_MODEL, 1024)
    gn = jnp.stack([jnp.tile(a_qnorm_g[l] * (A_HEAD_DIM ** -0.5 * LOG2E), 8), jnp.tile(a_knorm_g[l], 8),
                    jnp.tile(b_qnorm_g[l] * (B_HEAD_DIM ** -0.5 * LOG2E), 8), jnp.tile(b_knorm_g[l], 8)]).astype(f32)
    gmat = jnp.asarray(np.kron(np.eye(4), np.ones((A_HEAD_DIM, A_HEAD_DIM))), dtype=bf16)

    qa, ka, va, qb, kb, vb, ga, gb = _proj(x2, norm1_g[l][None].astype(f32), w_qk, w_all, gn, gmat)

    lam_init = 0.8 - 0.6 * math.exp(-0.3 * l)
    lp = a_lambda[l].astype(f32)
    lam = jnp.exp(jnp.sum(lp[0] * lp[1])) - jnp.exp(jnp.sum(lp[2] * lp[3])) + lam_init
    bias_a, bias_at, far_a = _diff_bias_vecs(t5_table)
    bmax_a = jnp.maximum(jnp.max(bias_a, axis=(1, 2)), far_a)
    scal_a = (jnp.concatenate([lam[None], far_a, bmax_a]) * jnp.array([1.0] + [LOG2E] * (2 * A_HEADS))).astype(f32)
    gsub = (a_subln_g[l] * (1.0 - lam_init))[None].astype(f32)
    oa = _diff_attention(scal_a, qa, ka, va, bias_a * LOG2E, bias_at * LOG2E, gsub, bsz=bsz, s_len=s_len)

    bias_b = _band_bias_vecs(b_rel_table[l])
    ob = _band_attention(qb, kb, vb, bias_b * LOG2E, bsz=bsz, s_len=s_len)

    wr = jnp.zeros((D_MODEL, LANES), f32)
    wr = wr.at[:, :N_GROUPS].set(w_router_group[l]).at[:, N_GROUPS:N_GROUPS + N_EXPERTS].set(w_router_expert[l])
    wrh = wr.astype(bf16)
    wr2 = jnp.concatenate([wrh, (wr - wrh.astype(f32)).astype(bf16)], axis=1)
    br = jnp.zeros((1, LANES), f32)
    br = br.at[0, :N_GROUPS].set(b_router_group[l]).at[0, N_GROUPS:N_GROUPS + N_EXPERTS].set(b_router_expert[l])
    x1, hn, route, cnt = _post(oa, ob, ga, gb, x2, w_branch_a[l].astype(bf16), w_branch_b[l].astype(bf16),
                               w_out[l].astype(bf16), norm2_g[l][None].astype(f32), wr2, br)

    ltri = jnp.asarray(np.tril(np.ones((TM_RANK, TM_RANK)), -1), dtype=bf16)
    utri = jnp.asarray(np.triu(np.ones((LANES, LANES)), 1), dtype=bf16)
    dest = _rank(route, cnt, ltri, utri)

    counts = cnt[0, :N_EXPERTS].astype(jnp.int32)
    n_rows = n_tok * TOP_K
    seg = _segments(counts, n_rows)

    dest2 = dest[:, :TOP_K].astype(jnp.int32)
    xs = _dispatch(dest2.reshape(n_tok // TM_DISPATCH, 1, TOP_K * TM_DISPATCH), hn, n_rows)
    ys = _ffn(seg, xs.reshape(n_rows * ROW_SLABS, LANES), w_gate[l], w_up[l], w_down[l])
    out = _combine(dest2.reshape(n_tok // TM_COMBINE, 1, TOP_K * TM_COMBINE), route, x1,
                   ys.reshape(n_rows, ROW_SLABS, LANES))
    return out.reshape(bsz, s_len, D_MODEL)
```

```python
import functools
import math

import jax
import jax.numpy as jnp
import numpy as np
from jax import lax
from jax.experimental import pallas as pl
from jax.experimental.pallas import tpu as pltpu

D_MODEL = 1024
CHUNK = 64
A_HEADS = 4
A_HEAD_DIM = 64
A_VDIM = 2 * A_HEAD_DIM
B_HEADS = 8
B_HEAD_DIM = 64
B_LEFT_CHUNKS = 8
B_MAX_REL = 128
T5_BUCKETS = 32
T5_MAX_DIST = 128
N_GROUPS = 4
EXPERTS_PER_GROUP = 8
N_EXPERTS = N_GROUPS * EXPERTS_PER_GROUP
TOP_K = 2
D_EXPERT = 512
EPS = 1e-6
NEG = -1e30
LOG2E = 1.0 / math.log(2.0)
DENOM_FLOOR = 2.0 ** -100

CHUNK_SHIFT = CHUNK.bit_length() - 1
assert 1 << CHUNK_SHIFT == CHUNK
LANES = 128
A_W = A_HEADS * 2 * A_HEAD_DIM
B_W = B_HEADS * B_HEAD_DIM
PROJ_W = 4 * 256 + 4 * 512 + 2 * D_MODEL

TM_PROJ = 512
TQ = 512
TK = 512
FAR_UNROLL = 4
TM_POST = 1024
TM_RANK = 1024
TM_DISPATCH = 8192
TM_COMBINE = 2048
ROW_UNROLL = 16
FFN_BLK = 1024
FFN_SUB = 256
ROW_SLABS = D_MODEL // 2 // LANES
VMEM_LIMIT = 56 * 1024 * 1024


def _cparams(sem):
    return pltpu.CompilerParams(dimension_semantics=sem, vmem_limit_bytes=VMEM_LIMIT)


def _pack_bf16_pairs(x):
    bits = lax.bitcast_convert_type(x.astype(jnp.bfloat16).astype(jnp.float32), jnp.uint32)
    half = x.shape[1] // 2
    return (bits[:, :half] & jnp.uint32(0xFFFF0000)) | lax.shift_right_logical(bits[:, half:], jnp.uint32(16))


def _unpack_bf16_pairs(pk):
    return (lax.bitcast_convert_type(pk & jnp.uint32(0xFFFF0000), jnp.float32),
            lax.bitcast_convert_type(lax.shift_left(pk, jnp.uint32(16)), jnp.float32))


def _proj_kernel(x_ref, g1_ref, wqk_ref, w_ref, gn_ref, gmat_ref,
                 qa_ref, ka_ref, va_ref, qb_ref, kb_ref, vb_ref, ga_ref, gb_ref):
    x = x_ref[...]
    xn = x * lax.rsqrt(jnp.mean(x * x, axis=-1, keepdims=True) + EPS) * g1_ref[...]
    xn = xn.astype(jnp.bfloat16)
    n_qk = wqk_ref.shape[1]

    def slab(c0, width):
        w = wqk_ref[:, c0:c0 + width] if c0 < n_qk else w_ref[:, c0:c0 + width]
        return jnp.dot(xn, w, preferred_element_type=jnp.float32)

    def headnorm(y, gi):
        sq = (y * y).astype(jnp.bfloat16)
        half = gmat_ref.shape[0]
        ss = jnp.concatenate([jnp.dot(sq[:, c:c + half], gmat_ref[...], preferred_element_type=jnp.float32)
                              for c in range(0, y.shape[1], half)], axis=1)
        return y * lax.rsqrt(ss * (1.0 / A_HEAD_DIM) + EPS) * gn_ref[gi:gi + 1, :]

    qa_ref[...] = headnorm(slab(0, 512), 0).astype(jnp.bfloat16)
    ka_ref[...] = headnorm(slab(512, 512), 1).astype(jnp.bfloat16)
    va_ref[...] = slab(1024, 512).astype(jnp.bfloat16)
    qb_ref[...] = headnorm(slab(1536, 512), 2).astype(jnp.bfloat16)
    kb_ref[...] = headnorm(slab(2048, 512), 3).astype(jnp.bfloat16)
    vb_ref[...] = slab(2560, 512).astype(jnp.bfloat16)
    for j in range(2):
        ga_ref[:, j * 512:(j + 1) * 512] = jax.nn.sigmoid(slab(3072 + j * 512, 512)).astype(jnp.bfloat16)
        gb_ref[:, j * 512:(j + 1) * 512] = jax.nn.sigmoid(slab(4096 + j * 512, 512)).astype(jnp.bfloat16)


def _proj(x2, g1, w_qk, w_all, gn, gmat):
    t = x2.shape[0]
    n = t // TM_PROJ
    row = lambda w: pl.BlockSpec((TM_PROJ, w), lambda i: (i, 0))
    full = lambda a: pl.BlockSpec(a.shape, lambda i: (0,) * a.ndim)
    outs = [jax.ShapeDtypeStruct((t, 512), jnp.bfloat16)] * 6 + [jax.ShapeDtypeStruct((t, D_MODEL), jnp.bfloat16)] * 2
    return pl.pallas_call(
        _proj_kernel,
        grid=(n,),
        in_specs=[row(D_MODEL), full(g1), full(w_qk), full(w_all), full(gn), full(gmat)],
        out_specs=[row(512)] * 6 + [row(D_MODEL)] * 2,
        out_shape=outs,
        compiler_params=_cparams(("arbitrary",)),
        name="proj",
    )(x2, g1, w_qk, w_all, gn, gmat)


def _diff_kernel(scal_ref, q_ref, k_ref, v_ref, vec_ref, vect_ref, gsub_ref, o_ref,
                 bias_sc, biast_sc, vt_sc, kmaxt_sc, qz_sc, qzt_sc, refn_sc, reff_sc, l_sc, acc_sc,
                 m_sc, acc2_sc, mf_sc, accf_sc, *, n_near):
    f32 = jnp.float32
    h = pl.program_id(1)
    qi = pl.program_id(2)
    nc = TK // LANES
    r2 = 2 * TQ
    n_heads = pl.num_programs(1)

    def near_span(j):
        return slice((n_near - 1 - j) * TK, (n_near - j) * TK)

    @pl.when(qi == 0)
    def _():
        row = lax.broadcasted_iota(jnp.int32, (TK, TQ), 0)
        col = lax.broadcasted_iota(jnp.int32, (TK, TQ), 1)
        for j in range(n_near):
            ok = lax.shift_right_arithmetic(col, CHUNK_SHIFT) >= lax.shift_right_arithmetic(row - j * TK, CHUNK_SHIFT)
            vect = jnp.broadcast_to(vect_ref[0, j:j + 1, :], (TK, TQ + TK))
            tilet = jnp.where(ok, pltpu.roll(vect, 0, 1, stride=1, stride_axis=0)[:, :TQ], NEG)
            for half in range(2):
                biast_sc[near_span(j), half * TQ:(half + 1) * TQ] = tilet
        vt_sc[...] = v_ref[...].T
        kf = k_ref[...].astype(f32)
        ksq = kf * kf
        klane = lax.broadcasted_iota(jnp.int32, ksq.shape, 1)
        for half in range(2):
            sel = (klane < 64) if half == 0 else (klane >= 64)
            norm2 = jnp.max(jnp.sum(jnp.where(sel, ksq, 0.0), axis=-1, keepdims=True), axis=0, keepdims=True)
            kmaxt_sc[:, half * TQ:(half + 1) * TQ] = jnp.broadcast_to(jnp.sqrt(norm2), (8, TQ))

    q = q_ref[...]
    lane = lax.broadcasted_iota(jnp.int32, q.shape, 1)
    zero = jnp.zeros_like(q)
    q1 = jnp.where(lane < 64, q, zero)
    q2 = jnp.where(lane >= 64, q, zero)
    qzt_sc[:, 0:TQ] = q1.T
    qzt_sc[:, TQ:r2] = q2.T
    cfar = scal_ref[1 + h]
    bias_max = scal_ref[1 + n_heads + h]
    n_far = jnp.maximum(qi - (n_near - 1), 0)

    def key_block(kblk):
        return k_ref[pl.ds(pl.multiple_of(kblk * TK, TK), TK), :]

    def finish(od):
        od = od * lax.rsqrt(jnp.mean(od * od, axis=-1, keepdims=True) + EPS) * gsub_ref[...]
        o_ref[...] = od.astype(o_ref.dtype)

    def fixed_reference_pass():
        qf = qzt_sc[...].astype(f32)
        qnorm = jnp.sqrt(jnp.sum(qf * qf, axis=0, keepdims=True))
        refn_sc[...] = qnorm * kmaxt_sc[...] + bias_max
        l_sc[...] = jnp.zeros(l_sc.shape, f32)
        acc_sc[...] = jnp.zeros(acc_sc.shape, f32)

        def accumulate(st, kblk, ref_ref):
            p = jnp.exp2(st.reshape(TK // 8, 8, r2) - ref_ref[...][None])
            l_sc[...] += jnp.sum(p, axis=0)
            vt = vt_sc[:, pl.ds(pl.multiple_of(kblk * TK, TK), TK)]
            acc_sc[...] += jnp.dot(vt, p.reshape(TK, r2).astype(jnp.bfloat16), preferred_element_type=f32)

        def scores_t(kblk):
            return jnp.dot(key_block(kblk), qzt_sc[...], preferred_element_type=f32)

        for j in range(n_near):
            def near(j=j):
                accumulate(scores_t(qi - j) + biast_sc[near_span(j), :], qi - j, refn_sc)
            if j == 0:
                near()
            else:
                pl.when(qi >= j)(near)

        reff_sc[...] = refn_sc[...] - cfar

        def far(kblk, carry):
            accumulate(scores_t(kblk), kblk, reff_sc)
            return carry

        def far_run(first, count, width):
            def body(i, carry):
                for u in range(width):
                    far(first + width * i + u, carry)
                return carry
            lax.fori_loop(0, count, body, 0)
            return first + count * width

        nxt = far_run(0, n_far // FAR_UNROLL, FAR_UNROLL)
        nxt = far_run(nxt, (n_far - nxt) // 2, 2)
        far_run(nxt, n_far - nxt, 1)

        denom = jnp.sum(l_sc[...], axis=0, keepdims=True)
        ot = acc_sc[...] / denom
        finish((ot[:, 0:TQ] - scal_ref[0] * ot[:, TQ:r2]).T)
        return denom

    def exact_pass():
        qz_sc[0:TQ, :] = q1
        qz_sc[TQ:r2, :] = q2
        row = lax.broadcasted_iota(jnp.int32, (TQ, TK), 0)
        col = lax.broadcasted_iota(jnp.int32, (TQ, TK), 1)
        for j in range(n_near):
            ok = lax.shift_right_arithmetic(row, CHUNK_SHIFT) >= lax.shift_right_arithmetic(col - j * TK, CHUNK_SHIFT)
            vec = jnp.broadcast_to(vec_ref[0, j:j + 1, :], (TQ, TQ + TK))
            tile = jnp.where(ok, pltpu.roll(vec, 0, 1, stride=1, stride_axis=0)[:, :TK], NEG)
            for half in range(2):
                bias_sc[half * TQ:(half + 1) * TQ, near_span(j)] = tile

        def scores(kblk):
            return lax.dot_general(qz_sc[...], key_block(kblk), (((1,), (1,)), ((), ())), preferred_element_type=f32)

        def update(s, kblk, m_ref, a_ref):
            cols = [s[:, c * LANES:(c + 1) * LANES] for c in range(nc)]
            m_old = m_ref[...]
            m_new = jnp.maximum(m_old, jnp.max(functools.reduce(jnp.maximum, cols), axis=-1, keepdims=True))
            alpha = jnp.exp2(m_old - m_new)
            ps = [jnp.exp2(c - m_new) for c in cols]
            p = jnp.concatenate([x.astype(jnp.bfloat16) for x in ps], axis=1)
            v = v_ref[pl.ds(pl.multiple_of(kblk * TK, TK), TK), :]
            a_ref[:, 0:LANES] = alpha * a_ref[:, 0:LANES] + jnp.dot(p, v, preferred_element_type=f32)
            a_ref[:, LANES:2 * LANES] = alpha * a_ref[:, LANES:2 * LANES] + functools.reduce(jnp.add, ps)
            m_ref[...] = m_new

        m_sc[...] = jnp.full(m_sc.shape, NEG, f32)
        acc2_sc[...] = jnp.zeros(acc2_sc.shape, f32)
        for j in range(n_near):
            def near(j=j):
                update(scores(qi - j) + bias_sc[:, near_span(j)], qi - j, m_sc, acc2_sc)
            if j == 0:
                near()
            else:
                pl.when(qi >= j)(near)

        mf_sc[...] = jnp.full(mf_sc.shape, NEG, f32)
        accf_sc[...] = jnp.zeros(accf_sc.shape, f32)

        def far(kblk, carry):
            update(scores(kblk), kblk, mf_sc, accf_sc)
            return carry

        lax.fori_loop(0, n_far, far, 0)
        mf = mf_sc[...] + cfar
        mn = m_sc[...]
        m = jnp.maximum(mf, mn)
        wf = jnp.exp2(mf - m)
        wn = jnp.exp2(mn - m)
        tot = (jnp.concatenate([wf, wf], axis=1) * accf_sc[...]
               + jnp.concatenate([wn, wn], axis=1) * acc2_sc[...])
        o = tot[:, 0:LANES] / jnp.sum(tot[:, LANES:2 * LANES], axis=-1, keepdims=True)
        finish(o[:TQ] - scal_ref[0] * o[TQ:])

    denom = fixed_reference_pass()
    pl.when(jnp.logical_not(jnp.min(denom) >= DENOM_FLOOR))(exact_pass)


def _diff_attention(scal, q, k, v, vecs, vecs_t, gsub, *, bsz, s_len):
    n_blk = q.shape[1] // LANES
    nq = s_len // TQ
    n_near = vecs.shape[1]
    assert vecs.shape == vecs_t.shape == (n_blk, n_near, TQ + TK)
    kern = functools.partial(_diff_kernel, n_near=n_near)
    f32, bf16 = jnp.float32, jnp.bfloat16
    vm = pltpu.VMEM
    vec_spec = pl.BlockSpec((1, n_near, TQ + TK), lambda b, h, i: (h, 0, 0))
    return pl.pallas_call(
        kern,
        grid=(bsz, n_blk, nq),
        in_specs=[
            pl.BlockSpec(memory_space=pltpu.SMEM),
            pl.BlockSpec((TQ, LANES), lambda b, h, i: (b * nq + i, h)),
            pl.BlockSpec((s_len, LANES), lambda b, h, i: (b, h)),
            pl.BlockSpec((s_len, LANES), lambda b, h, i: (b, h)),
            vec_spec, vec_spec,
            pl.BlockSpec((1, LANES), lambda b, h, i: (0, 0)),
        ],
        out_specs=pl.BlockSpec((TQ, LANES), lambda b, h, i: (b * nq + i, h)),
        out_shape=jax.ShapeDtypeStruct(q.shape, bf16),
        scratch_shapes=[vm((2 * TQ, n_near * TK), f32), vm((n_near * TK, 2 * TQ), f32),
                        vm((LANES, s_len), bf16), vm((8, 2 * TQ), f32),
                        vm((2 * TQ, LANES), bf16), vm((LANES, 2 * TQ), bf16),
                        vm((8, 2 * TQ), f32), vm((8, 2 * TQ), f32), vm((8, 2 * TQ), f32),
                        vm((LANES, 2 * TQ), f32),
                        vm((2 * TQ, LANES), f32), vm((2 * TQ, 2 * LANES), f32),
                        vm((2 * TQ, LANES), f32), vm((2 * TQ, 2 * LANES), f32)],
        compiler_params=_cparams(("arbitrary", "arbitrary", "arbitrary")),
        name="attn_diff",
    )(scal, q, k, v, vecs, vecs_t, gsub)


BAND_TQ = 1024
BAND_HQ = 256
BAND_SUBS = BAND_TQ // BAND_HQ
BAND_W = B_LEFT_CHUNKS * CHUNK + BAND_HQ


def _band_kernel(q_ref, k_ref, v_ref, vec_ref, o_ref, bias_sc, qz_sc):
    f32 = jnp.float32
    qi = pl.program_id(2)
    hq, w = BAND_HQ, BAND_W
    back = w - hq

    @pl.when(qi == 0)
    def _():
        qchunk = lax.shift_right_arithmetic(lax.broadcasted_iota(jnp.int32, (hq, w), 0), CHUNK_SHIFT)
        kchunk = lax.shift_right_arithmetic(lax.broadcasted_iota(jnp.int32, (hq, w), 1) - back, CHUNK_SHIFT)
        dchunk = qchunk - kchunk
        allowed = (dchunk >= 0) & (dchunk <= B_LEFT_CHUNKS)
        for half in range(2):
            vec = jnp.broadcast_to(vec_ref[0, half:half + 1, :], (hq, hq + w))
            tile = pltpu.roll(vec, 0, 1, stride=1, stride_axis=0)[:, :w]
            bias_sc[half * hq:(half + 1) * hq, :] = jnp.where(allowed, tile, NEG)

    q = q_ref[...]
    lane = lax.broadcasted_iota(jnp.int32, (hq, LANES), 1)
    zero = jnp.zeros((hq, LANES), q.dtype)
    for u in range(BAND_SUBS):
        qu = q[u * hq:(u + 1) * hq, :]
        qz_sc[(2 * u) * hq:(2 * u + 1) * hq, :] = jnp.where(lane < 64, qu, zero)
        qz_sc[(2 * u + 1) * hq:(2 * u + 2) * hq, :] = jnp.where(lane >= 64, qu, zero)

    def scores(u, k0, n_keys):
        k = k_ref[pl.ds(k0, n_keys), :]
        return lax.dot_general(qz_sc[2 * u * hq:(2 * u + 2) * hq, :], k, (((1,), (1,)), ((), ())),
                               preferred_element_type=f32) + bias_sc[:, w - n_keys:w]

    def softmax(s):
        cols = [s[:, c * LANES:(c + 1) * LANES] for c in range(s.shape[1] // LANES)]
        m = jnp.max(functools.reduce(jnp.maximum, cols), axis=-1, keepdims=True)
        ps = [jnp.exp2(c - m) for c in cols]
        denom = jnp.sum(functools.reduce(jnp.add, ps), axis=-1, keepdims=True)
        return jnp.concatenate([x.astype(jnp.bfloat16) for x in ps], axis=1), denom

    def finish(u, p, denom, k0):
        o = jnp.dot(p, v_ref[pl.ds(k0, p.shape[1]), :], preferred_element_type=f32) / denom
        o_ref[u * hq:(u + 1) * hq, :] = jnp.where(lane < 64, o[:hq], o[hq:]).astype(o_ref.dtype)

    def step(k0s, n_keys):
        ss = [scores(u, k0s[u], n_keys[u]) for u in range(BAND_SUBS)]
        pd = [softmax(x) for x in ss]
        for u in range(BAND_SUBS):
            finish(u, pd[u][0], pd[u][1], k0s[u])

    first = tuple(min((u + 1) * hq, w) for u in range(BAND_SUBS))

    @pl.when(qi == 0)
    def _():
        step(tuple(max((u + 1) * hq - w, 0) for u in range(BAND_SUBS)), first)

    @pl.when(qi > 0)
    def _():
        start = pl.multiple_of(qi * BAND_TQ - back, hq)
        step(tuple(pl.multiple_of(start + u * hq, hq) for u in range(BAND_SUBS)), (w,) * BAND_SUBS)


def _band_attention(q, k, v, vecs, *, bsz, s_len):
    n_blk = q.shape[1] // LANES
    nq = s_len // BAND_TQ
    assert vecs.shape == (n_blk, 2, BAND_HQ + BAND_W)
    return pl.pallas_call(
        _band_kernel,
        grid=(bsz, n_blk, nq),
        in_specs=[
            pl.BlockSpec((BAND_TQ, LANES), lambda b, h, i: (b * nq + i, h)),
            pl.BlockSpec((s_len, LANES), lambda b, h, i: (b, h)),
            pl.BlockSpec((s_len, LANES), lambda b, h, i: (b, h)),
            pl.BlockSpec((1, 2, BAND_HQ + BAND_W), lambda b, h, i: (h, 0, 0)),
        ],
        out_specs=pl.BlockSpec((BAND_TQ, LANES), lambda b, h, i: (b * nq + i, h)),
        out_shape=jax.ShapeDtypeStruct(q.shape, jnp.bfloat16),
        scratch_shapes=[pltpu.VMEM((2 * BAND_HQ, BAND_W), jnp.float32),
                        pltpu.VMEM((2 * BAND_TQ, LANES), jnp.bfloat16)],
        compiler_params=_cparams(("arbitrary", "arbitrary", "arbitrary")),
        name="attn_band",
    )(q, k, v, vecs)


def _post_kernel(oa_ref, ob_ref, ga_ref, gb_ref, x_ref, wa_ref, wb_ref, wo_ref, g2_ref,
                 wr2_ref, br_ref, x1_ref, hn_ref, route_ref, cnt_ref):
    f32 = jnp.float32
    ya = jnp.dot(oa_ref[...], wa_ref[...], preferred_element_type=f32)
    yb = jnp.dot(ob_ref[...], wb_ref[...], preferred_element_type=f32)
    mixed = ga_ref[...].astype(f32) * ya + gb_ref[...].astype(f32) * yb
    x1 = x_ref[...] + jnp.dot(mixed.astype(jnp.bfloat16), wo_ref[...], preferred_element_type=f32)
    x1_ref[...] = x1
    hn = x1 * lax.rsqrt(jnp.mean(x1 * x1, axis=-1, keepdims=True) + EPS) * g2_ref[...]
    hh = hn.astype(jnp.bfloat16)
    packed = _pack_bf16_pairs(hn)
    for c in range(ROW_SLABS):
        hn_ref[pl.ds(c, TM_POST, stride=ROW_SLABS), :] = packed[:, c * LANES:(c + 1) * LANES]

    hl = (hn - hh.astype(f32)).astype(jnp.bfloat16)
    hw = jnp.dot(hh, wr2_ref[...], preferred_element_type=f32)
    lg = (hw[:, 0:LANES] + hw[:, LANES:2 * LANES]
          + jnp.dot(hl, wr2_ref[:, 0:LANES], preferred_element_type=f32)) + br_ref[...]

    lanei = lax.broadcasted_iota(jnp.int32, lg.shape, 1)
    lanef = lanei.astype(f32)
    big = 999.0
    gmask = lanei < N_GROUPS
    gl = jnp.where(gmask, lg, NEG)
    gm = jnp.max(gl, axis=-1, keepdims=True)
    ge = jnp.where(gmask, jnp.exp(gl - gm), 0.0)
    gp = ge / jnp.sum(ge, axis=-1, keepdims=True)
    p_g = jnp.max(gp, axis=-1, keepdims=True)
    gidx = jnp.min(jnp.where(gmask & (gp == p_g), lanef, big), axis=-1, keepdims=True)
    egrp = lax.shift_right_arithmetic(lanei - N_GROUPS, 3).astype(f32)
    emask = (lanei >= N_GROUPS) & (lanei < N_GROUPS + N_EXPERTS) & (egrp == gidx)
    el = jnp.where(emask, lg, NEG)
    v1 = jnp.max(el, axis=-1, keepdims=True)
    i1 = jnp.min(jnp.where(emask & (el == v1), lanef, big), axis=-1, keepdims=True)
    emask2 = emask & (lanef != i1)
    el2 = jnp.where(emask2, lg, NEG)
    v2 = jnp.max(el2, axis=-1, keepdims=True)
    i2 = jnp.min(jnp.where(emask2 & (el2 == v2), lanef, big), axis=-1, keepdims=True)
    t = jnp.exp(v2 - v1)
    den = 1.0 + t
    w1 = p_g * (1.0 / den)
    w2 = p_g * (t / den)
    route = jnp.where(lanei == 0, i1 - N_GROUPS,
                      jnp.where(lanei == 1, i2 - N_GROUPS,
                                jnp.where(lanei == 2, w1, jnp.where(lanei == 3, w2, 0.0))))
    route_ref[...] = route

    @pl.when(pl.program_id(0) == 0)
    def _():
        cnt_ref[...] = jnp.zeros(cnt_ref.shape, f32)

    chosen = ((lanef == i1 - N_GROUPS) | (lanef == i2 - N_GROUPS)).astype(f32)
    cnt_ref[...] += jnp.broadcast_to(jnp.sum(chosen, axis=0, keepdims=True), cnt_ref.shape)


def _post(oa, ob, ga, gb, x2, wa, wb, wo, g2, wr2, br):
    t = x2.shape[0]
    n = t // TM_POST
    row = lambda w: pl.BlockSpec((TM_POST, w), lambda i: (i, 0))
    full = lambda a: pl.BlockSpec(a.shape, lambda i: (0,) * a.ndim)
    return pl.pallas_call(
        _post_kernel,
        grid=(n,),
        in_specs=[row(512), row(512), row(D_MODEL), row(D_MODEL), row(D_MODEL),
                  full(wa), full(wb), full(wo), full(g2), full(wr2), full(br)],
        out_specs=[row(D_MODEL), pl.BlockSpec((TM_POST * ROW_SLABS, LANES), lambda i: (i, 0)), row(LANES),
                   pl.BlockSpec((8, LANES), lambda i: (0, 0))],
        out_shape=[jax.ShapeDtypeStruct((t, D_MODEL), jnp.float32),
                   jax.ShapeDtypeStruct((t * ROW_SLABS, LANES), jnp.uint32),
                   jax.ShapeDtypeStruct((t, LANES), jnp.float32),
                   jax.ShapeDtypeStruct((8, LANES), jnp.float32)],
        compiler_params=_cparams(("arbitrary",)),
        name="post",
    )(oa, ob, ga, gb, x2, wa, wb, wo, g2, wr2, br)


def _rank_kernel(route_ref, cnt_ref, ltri_ref, utri_ref, dest_ref, pstart_sc, base_sc):
    f32 = jnp.float32
    i = pl.program_id(0)
    route = route_ref[...]
    lanef = lax.broadcasted_iota(jnp.int32, route.shape, 1).astype(f32)
    oh1 = (lanef == route[:, 0:1]).astype(f32)
    oh2 = (lanef == route[:, 1:2]).astype(f32)
    both = oh1 + oh2

    @pl.when(i == 0)
    def _():
        cnt = cnt_ref[0:1, :]
        chi = jnp.floor(cnt * (1.0 / 256.0))
        clo = cnt - chi * 256.0
        split = jnp.concatenate([jnp.broadcast_to(chi, (8, LANES)), jnp.broadcast_to(clo, (8, LANES))], axis=0)
        excl = jnp.dot(split.astype(jnp.bfloat16), utri_ref[...], preferred_element_type=f32)
        pstart_sc[...] = excl[0:1] * 256.0 + excl[8:9]
        base_sc[...] = jnp.zeros(base_sc.shape, f32)

    prior = jnp.dot(ltri_ref[...], both.astype(jnp.bfloat16), preferred_element_type=f32)
    slot = prior + base_sc[...] + pstart_sc[...]
    d1 = jnp.sum(oh1 * slot, axis=-1, keepdims=True)
    d2 = jnp.sum(oh2 * slot, axis=-1, keepdims=True)
    dest_ref[...] = jnp.where(lanef == 0.0, d1, jnp.where(lanef == 1.0, d2, 0.0))
    base_sc[...] += jnp.sum(both, axis=0, keepdims=True)


def _rank(route, cnt, ltri, utri):
    t = route.shape[0]
    n = t // TM_RANK
    full = lambda a: pl.BlockSpec(a.shape, lambda i: (0,) * a.ndim)
    row1 = lambda: pltpu.VMEM((1, LANES), jnp.float32)
    return pl.pallas_call(
        _rank_kernel,
        grid=(n,),
        in_specs=[pl.BlockSpec((TM_RANK, LANES), lambda i: (i, 0)), full(cnt), full(ltri), full(utri)],
        out_specs=pl.BlockSpec((TM_RANK, LANES), lambda i: (i, 0)),
        out_shape=jax.ShapeDtypeStruct((t, LANES), jnp.float32),
        scratch_shapes=[row1(), row1()],
        compiler_params=_cparams(("arbitrary",)),
        name="rank",
    )(route, cnt, ltri, utri)


def _dispatch_kernel(dest_ref, hn_ref, xs_ref, sems):
    def row_copy(r, k):
        d = dest_ref[0, 0, 2 * r + k]
        src = hn_ref.at[pl.ds(pl.multiple_of(r * ROW_SLABS, ROW_SLABS), ROW_SLABS), :]
        return pltpu.make_async_copy(src, xs_ref.at[d], sems.at[k])

    def issue(r, c):
        row_copy(r, 0).start(priority=0)
        row_copy(r, 1).start(priority=1)
        return c

    lax.fori_loop(0, TM_DISPATCH, issue, 0, unroll=ROW_UNROLL)
    for k in range(TOP_K):
        pltpu.make_async_copy(hn_ref, hn_ref, sems.at[k]).wait()


def _dispatch(dest3, hn, n_rows):
    t = hn.shape[0] // ROW_SLABS
    n = t // TM_DISPATCH
    return pl.pallas_call(
        _dispatch_kernel,
        grid=(n,),
        in_specs=[pl.BlockSpec((1, 1, 2 * TM_DISPATCH), lambda i: (i, 0, 0), memory_space=pltpu.SMEM),
                  pl.BlockSpec((TM_DISPATCH * ROW_SLABS, LANES), lambda i: (i, 0))],
        out_specs=pl.BlockSpec(memory_space=pl.ANY),
        out_shape=jax.ShapeDtypeStruct((n_rows, ROW_SLABS, LANES), hn.dtype),
        scratch_shapes=[pltpu.SemaphoreType.DMA((2,))],
        compiler_params=_cparams(("arbitrary",)),
        name="dispatch",
    )(dest3, hn)


def _ffn_kernel(tile_ref, exp_ref, lo_ref, hi_ref, cast_ref, init_ref,
                xs_ref, wg_ref, wu_ref, wd_ref, ys_ref, wg_sc, wu_sc, wd_sc):
    v = pl.program_id(0)
    lo = lo_ref[v]
    hi = hi_ref[v]

    @pl.when(init_ref[v] == 1)
    def _():
        ys_ref[...] = jnp.zeros(ys_ref.shape, ys_ref.dtype)

    @pl.when((hi > lo) & (cast_ref[v] == 1))
    def _():
        wg_sc[...] = wg_ref[0].astype(jnp.bfloat16)
        wu_sc[...] = wu_ref[0].astype(jnp.bfloat16)
        wd_sc[...] = wd_ref[0].astype(jnp.bfloat16)

    for piece in range(FFN_BLK // FFN_SUB):
        row0 = tile_ref[v] * FFN_BLK + piece * FFN_SUB

        @pl.when((hi > row0) & (lo < row0 + FFN_SUB))
        def _(piece=piece, row0=row0):
            def slab(ref, c):
                return ref.at[pl.ds(piece * FFN_SUB * ROW_SLABS + c, FFN_SUB, stride=ROW_SLABS), :]

            pk = jnp.concatenate([slab(xs_ref, c)[...] for c in range(ROW_SLABS)], axis=1)
            x = jnp.concatenate(_unpack_bf16_pairs(pk), axis=1).astype(jnp.bfloat16)
            g = jnp.dot(x, wg_sc[...], preferred_element_type=jnp.float32)
            u = jnp.dot(x, wu_sc[...], preferred_element_type=jnp.float32)
            hb = (g * jax.nn.sigmoid(g) * u).astype(jnp.bfloat16)
            packed = _pack_bf16_pairs(jnp.dot(hb, wd_sc[...], preferred_element_type=jnp.float32))
            rows = row0 + lax.broadcasted_iota(jnp.int32, (FFN_SUB, LANES), 0)
            mine = (rows >= lo) & (rows < hi)
            for c in range(ROW_SLABS):
                out = slab(ys_ref, c)
                out[...] = jnp.where(mine, packed[:, c * LANES:(c + 1) * LANES], out[...])


def _ffn(seg, xs, w_gate, w_up, w_down):
    n_seg = seg[0].shape[0]
    grid_spec = pltpu.PrefetchScalarGridSpec(
        num_scalar_prefetch=6,
        grid=(n_seg,),
        in_specs=[
            pl.BlockSpec((FFN_BLK * ROW_SLABS, LANES), lambda v, t, e, *_: (t[v], 0)),
            pl.BlockSpec((1, D_MODEL, D_EXPERT), lambda v, t, e, *_: (e[v], 0, 0)),
            pl.BlockSpec((1, D_MODEL, D_EXPERT), lambda v, t, e, *_: (e[v], 0, 0)),
            pl.BlockSpec((1, D_EXPERT, D_MODEL), lambda v, t, e, *_: (e[v], 0, 0)),
        ],
        out_specs=pl.BlockSpec((FFN_BLK * ROW_SLABS, LANES), lambda v, t, e, *_: (t[v], 0)),
        scratch_shapes=[pltpu.VMEM((D_MODEL, D_EXPERT), jnp.bfloat16),
                        pltpu.VMEM((D_MODEL, D_EXPERT), jnp.bfloat16),
                        pltpu.VMEM((D_EXPERT, D_MODEL), jnp.bfloat16)],
    )
    return pl.pallas_call(
        _ffn_kernel,
        grid_spec=grid_spec,
        out_shape=jax.ShapeDtypeStruct(xs.shape, jnp.uint32),
        compiler_params=_cparams(("arbitrary",)),
        name="ffn",
    )(*seg, xs, w_gate, w_up, w_down)


def _segments(counts, n_rows):
    i32 = jnp.int32
    n_tiles = n_rows // FFN_BLK
    n_seg = n_tiles + N_EXPERTS
    tri = jnp.tril(jnp.ones((N_EXPERTS, N_EXPERTS), i32))
    ends = jnp.sum(tri * counts[None, :], axis=1)
    starts = ends - counts
    edges = jnp.arange(n_tiles, dtype=i32) * FFN_BLK
    rank_e = jnp.arange(n_tiles, dtype=i32) + jnp.sum(starts[None, :] <= edges[:, None], axis=1)
    rank_s = jnp.arange(N_EXPERTS, dtype=i32) + jnp.sum(edges[None, :] < starts[:, None], axis=1)
    seg = jnp.arange(n_seg, dtype=i32)
    lo = (jnp.sum(jnp.where(rank_e[None, :] == seg[:, None], edges[None, :], 0), axis=1)
          + jnp.sum(jnp.where(rank_s[None, :] == seg[:, None], starts[None, :], 0), axis=1))
    hi = jnp.concatenate([lo[1:], jnp.array([n_rows], i32)])
    valid = hi > lo
    tile = jnp.minimum(lo // FFN_BLK, n_tiles - 1)
    expert = jnp.minimum(jnp.sum(ends[None, :] <= lo[:, None], axis=1), N_EXPERTS - 1).astype(i32)
    upto = seg[None, :] <= seg[:, None]
    expert = jnp.max(jnp.where(upto & valid[None, :], expert[None, :], 0), axis=1)
    prev_expert = jnp.concatenate([jnp.array([-1], i32), expert[:-1]])
    first_valid = valid & (jnp.sum(jnp.where(upto & valid[None, :], 1, 0), axis=1) == 1)
    cast = valid & ((expert != prev_expert) | first_valid)
    prev_tile = jnp.concatenate([jnp.array([-1], i32), tile[:-1]])
    init = tile != prev_tile
    return (tile.astype(i32), expert, lo.astype(i32), hi.astype(i32), cast.astype(i32), init.astype(i32))


def _combine_kernel(dest_ref, route_ref, x1_ref, ys_ref, out_ref, y0_sc, y1_sc, sems):
    def row_copy(r, k):
        d = dest_ref[0, 0, 2 * r + k]
        dst = y0_sc if k == 0 else y1_sc
        return pltpu.make_async_copy(ys_ref.at[d], dst.at[pl.ds(pl.multiple_of(r * ROW_SLABS, ROW_SLABS), ROW_SLABS), :],
                                     sems.at[k])

    def issue(r, c):
        row_copy(r, 0).start(priority=0)
        row_copy(r, 1).start(priority=1)
        return c

    lax.fori_loop(0, TM_COMBINE, issue, 0, unroll=ROW_UNROLL)
    pltpu.make_async_copy(y0_sc, y0_sc, sems.at[0]).wait()
    pltpu.make_async_copy(y1_sc, y1_sc, sems.at[1]).wait()
    route = route_ref[...]
    w0 = route[:, 2:3]
    w1 = route[:, 3:4]
    half = D_MODEL // 2
    for c in range(ROW_SLABS):
        hi0, lo0 = _unpack_bf16_pairs(y0_sc[pl.ds(c, TM_COMBINE, stride=ROW_SLABS), :])
        hi1, lo1 = _unpack_bf16_pairs(y1_sc[pl.ds(c, TM_COMBINE, stride=ROW_SLABS), :])
        ch = slice(c * LANES, (c + 1) * LANES)
        cl = slice(half + c * LANES, half + (c + 1) * LANES)
        out_ref[:, ch] = x1_ref[:, ch] + (w0 * hi0 + w1 * hi1)
        out_ref[:, cl] = x1_ref[:, cl] + (w0 * lo0 + w1 * lo1)


def _combine(dest3, route, x1, ys):
    t = x1.shape[0]
    n = t // TM_COMBINE
    return pl.pallas_call(
        _combine_kernel,
        grid=(n,),
        in_specs=[pl.BlockSpec((1, 1, 2 * TM_COMBINE), lambda i: (i, 0, 0), memory_space=pltpu.SMEM),
                  pl.BlockSpec((TM_COMBINE, LANES), lambda i: (i, 0)),
                  pl.BlockSpec((TM_COMBINE, D_MODEL), lambda i: (i, 0)),
                  pl.BlockSpec(memory_space=pl.ANY)],
        out_specs=pl.BlockSpec((TM_COMBINE, D_MODEL), lambda i: (i, 0)),
        out_shape=jax.ShapeDtypeStruct((t, D_MODEL), jnp.float32),
        scratch_shapes=[pltpu.VMEM((TM_COMBINE * ROW_SLABS, LANES), jnp.uint32),
                        pltpu.VMEM((TM_COMBINE * ROW_SLABS, LANES), jnp.uint32),
                        pltpu.SemaphoreType.DMA((2,))],
        compiler_params=_cparams(("arbitrary",)),
        name="combine",
    )(dest3, route, x1, ys)


def _t5_bucket(rel):
    nb = T5_BUCKETS // 2
    max_exact = nb // 2
    side = jnp.where(rel > 0, nb, 0)
    n = jnp.abs(rel)
    nf = jnp.maximum(n, 1).astype(jnp.float32)
    large = max_exact + (jnp.log(nf / max_exact) / math.log(T5_MAX_DIST / max_exact)
                         * (nb - max_exact)).astype(jnp.int32)
    large = jnp.minimum(large, nb - 1)
    return side + jnp.where(n < max_exact, n, large)


def _rel_offsets(j, transposed=False):
    i = jnp.arange(TQ + TK)
    if transposed:
        return jnp.where(i < TQ, -i, (TQ + TK) - i) - j * TK
    return jnp.where(i < TK, i, i - (TQ + TK)) - j * TK


def _diff_bias_vecs(t5_table):
    offs = jnp.stack([_rel_offsets(j, transposed) for transposed in (False, True) for j in range(2)])
    vals = jnp.transpose(t5_table[_t5_bucket(offs)].astype(jnp.float32), (2, 0, 1))
    far = t5_table[_t5_bucket(jnp.array(-(TK + 1)))].astype(jnp.float32)
    return vals[:, 0:2], vals[:, 2:4], far


def _band_bias_vecs(rel_table):
    i = jnp.arange(BAND_HQ + BAND_W)
    rel = jnp.where(i < BAND_W, i, i - (BAND_HQ + BAND_W)) - (BAND_W - BAND_HQ)
    vecs = rel_table[jnp.clip(rel, -B_MAX_REL, B_MAX_REL) + B_MAX_REL].astype(jnp.float32).T
    return vecs.reshape(B_HEADS // 2, 2, BAND_HQ + BAND_W)


def kernel(x, norm1_g, w_in, a_qnorm_g, a_knorm_g, a_lambda, a_subln_g, t5_table, b_qnorm_g, b_knorm_g,
           b_rel_table, w_branch_a, w_branch_b, w_out, norm2_g, w_router_group, b_router_group,
           w_router_expert, b_router_expert, w_gate, w_up, w_down):
    bsz, s_len, _ = x.shape
    n_tok = bsz * s_len
    f32, bf16 = jnp.float32, jnp.bfloat16
    assert s_len % TQ == 0 and TQ == TK and TQ % CHUNK == 0 and n_tok % TM_PROJ == 0
    assert TK >= T5_MAX_DIST and BAND_HQ % CHUNK == 0 and BAND_W % LANES == 0
    assert s_len % BAND_TQ == 0 and BAND_W - BAND_HQ <= BAND_TQ
    l = 0
    x2 = x.reshape(n_tok, D_MODEL)

    w_all = w_in[l].astype(bf16)
    qk = w_all[:, :1024].reshape(D_MODEL, 2, 2, A_HEADS, A_HEAD_DIM)
    w_qk = qk.transpose(0, 1, 3, 2, 4).reshape(D_MODEL, 1024)
    gn = jnp.stack([jnp.tile(a_qnorm_g[l] * (A_HEAD_DIM ** -0.5 * LOG2E), 8), jnp.tile(a_knorm_g[l], 8),
                    jnp.tile(b_qnorm_g[l] * (B_HEAD_DIM ** -0.5 * LOG2E), 8), jnp.tile(b_knorm_g[l], 8)]).astype(f32)
    gmat = jnp.asarray(np.kron(np.eye(4), np.ones((A_HEAD_DIM, A_HEAD_DIM))), dtype=bf16)

    qa, ka, va, qb, kb, vb, ga, gb = _proj(x2, norm1_g[l][None].astype(f32), w_qk, w_all, gn, gmat)

    lam_init = 0.8 - 0.6 * math.exp(-0.3 * l)
    lp = a_lambda[l].astype(f32)
    lam = jnp.exp(jnp.sum(lp[0] * lp[1])) - jnp.exp(jnp.sum(lp[2] * lp[3])) + lam_init
    bias_a, bias_at, far_a = _diff_bias_vecs(t5_table)
    bmax_a = jnp.maximum(jnp.max(bias_a, axis=(1, 2)), far_a)
    scal_a = (jnp.concatenate([lam[None], far_a, bmax_a]) * jnp.array([1.0] + [LOG2E] * (2 * A_HEADS))).astype(f32)
    gsub = (a_subln_g[l] * (1.0 - lam_init))[None].astype(f32)
    oa = _diff_attention(scal_a, qa, ka, va, bias_a * LOG2E, bias_at * LOG2E, gsub, bsz=bsz, s_len=s_len)

    bias_b = _band_bias_vecs(b_rel_table[l])
    ob = _band_attention(qb, kb, vb, bias_b * LOG2E, bsz=bsz, s_len=s_len)

    wr = jnp.zeros((D_MODEL, LANES), f32)
    wr = wr.at[:, :N_GROUPS].set(w_router_group[l]).at[:, N_GROUPS:N_GROUPS + N_EXPERTS].set(w_router_expert[l])
    wrh = wr.astype(bf16)
    wr2 = jnp.concatenate([wrh, (wr - wrh.astype(f32)).astype(bf16)], axis=1)
    br = jnp.zeros((1, LANES), f32)
    br = br.at[0, :N_GROUPS].set(b_router_group[l]).at[0, N_GROUPS:N_GROUPS + N_EXPERTS].set(b_router_expert[l])
    x1, hn, route, cnt = _post(oa, ob, ga, gb, x2, w_branch_a[l].astype(bf16), w_branch_b[l].astype(bf16),
                               w_out[l].astype(bf16), norm2_g[l][None].astype(f32), wr2, br)

    ltri = jnp.asarray(np.tril(np.ones((TM_RANK, TM_RANK)), -1), dtype=bf16)
    utri = jnp.asarray(np.triu(np.ones((LANES, LANES)), 1), dtype=bf16)
    dest = _rank(route, cnt, ltri, utri)

    counts = cnt[0, :N_EXPERTS].astype(jnp.int32)
    n_rows = n_tok * TOP_K
    seg = _segments(counts, n_rows)

    dest2 = dest[:, :TOP_K].astype(jnp.int32)
    xs = _dispatch(dest2.reshape(n_tok // TM_DISPATCH, 1, TOP_K * TM_DISPATCH), hn, n_rows)
    ys = _ffn(seg, xs.reshape(n_rows * ROW_SLABS, LANES), w_gate[l], w_up[l], w_down[l])
    out = _combine(dest2.reshape(n_tok // TM_COMBINE, 1, TOP_K * TM_COMBINE), route, x1,
                   ys.reshape(n_rows, ROW_SLABS, LANES))
    return out.reshape(bsz, s_len, D_MODEL)
```
